```python
import jax
import jax.numpy as jnp
from jax import lax
import numpy as np

D_MODEL = 1024
BATCH = 8
SEQ = 8192
DEPTH = 4

D_MIX = D_MODEL
HEAD_DIM = 64
ATTN_WIDTH = D_MIX // 2
ATTN_HEADS = ATTN_WIDTH // HEAD_DIM
DILATED_BRANCHES = ((128, 1), (512, 4), (2048, 16))
BAND = 128
ROPE_THETA = 10000.0
SGU_WIDTH = D_MIX // 4
SGU_GROUPS = 4
SGU_GROUP_DIM = SGU_WIDTH // SGU_GROUPS
SGU_CHUNK = 128
CONV_WIDTH = D_MIX - ATTN_WIDTH - SGU_WIDTH
CONV_TAPS = 3
PROJ_SIZES = (ATTN_WIDTH, ATTN_WIDTH, ATTN_WIDTH, SGU_WIDTH, SGU_WIDTH, CONV_WIDTH, CONV_WIDTH, CONV_WIDTH)
PROJ_WIDTH = 3 * ATTN_WIDTH + 2 * SGU_WIDTH + 3 * CONV_WIDTH
MOE_GROUPS = 4
EXPERTS_PER_GROUP = 8
N_EXPERTS = MOE_GROUPS * EXPERTS_PER_GROUP
TOP_K_INNER = 2
D_EXPERT = D_MODEL // 2
MOE_BLOCK = 256
ALPHA = (2.0 * DEPTH) ** 0.25
BETA = (8.0 * DEPTH) ** -0.25
EPS = 1e-5

kernel_name = 'hybrid_dilated_sgu_conv_hmoe'


def layer_norm(t, g, b):
    t32 = t.astype(jnp.float32)
    mu = jnp.mean(t32, -1, keepdims=True)
    var = jnp.mean(jnp.square(t32 - mu), -1, keepdims=True)
    return ((t32 - mu) * lax.rsqrt(var + EPS) * g + b).astype(t.dtype)


def rms_norm(t, g):
    t32 = t.astype(jnp.float32)
    return (t32 * lax.rsqrt(jnp.mean(jnp.square(t32), -1, keepdims=True) + EPS) * g).astype(t.dtype)


def rope(t):
    seq, hd = t.shape[1], t.shape[-1]
    half = hd // 2
    inv_freq = ROPE_THETA ** (-jnp.arange(half, dtype=jnp.float32) / half)
    ang = jnp.arange(seq, dtype=jnp.float32)[:, None] * inv_freq[None, :]
    cos = jnp.cos(ang)[None, :, None, :]
    sin = jnp.sin(ang)[None, :, None, :]
    t32 = t.astype(jnp.float32)
    t1, t2 = t32[..., :half], t32[..., half:]
    return jnp.concatenate([t1 * cos - t2 * sin, t2 * cos + t1 * sin], -1).astype(t.dtype)


def _to_strided(t, d, nb):
    b, s, h, hd = t.shape
    m = s // d
    t = t.reshape(b, m, d, h, hd).transpose(0, 2, 3, 1, 4)
    t = jnp.pad(t, ((0, 0), (0, 0), (0, 0), (0, nb * BAND - m), (0, 0)))
    return t.reshape(b, d, h, nb, BAND, hd)


def _with_prev(t):
    prev = jnp.pad(t[:, :, :, :-1], ((0, 0), (0, 0), (0, 0), (1, 0), (0, 0), (0, 0)))
    return jnp.concatenate([prev, t], axis=4)


def dilated_branch(q, k, v, d, n_back):
    b, s, h, hd = q.shape
    m = s // d
    nb = -(-m // BAND)
    qs = _to_strided(q, d, nb)
    ks = _with_prev(_to_strided(k, d, nb))
    vs = _with_prev(_to_strided(v, d, nb))
    sc = jnp.einsum('bdhnqc,bdhnkc->bdhnqk', qs, ks).astype(jnp.float32)
    qi = jnp.arange(BAND)[:, None] + BAND
    ki = jnp.arange(2 * BAND)[None, :]
    dist = qi - ki
    in_band = (dist >= 0) & (dist <= n_back)
    has_prev = (jnp.arange(nb)[:, None, None] > 0) | (ki[None] >= BAND)
    mask = in_band[None] & has_prev
    sc = jnp.where(mask, sc, -jnp.inf)
    mx = jnp.max(sc, -1, keepdims=True)
    p = jnp.exp(sc - mx)
    den = jnp.sum(p, -1, keepdims=True)
    o = jnp.einsum('bdhnqk,bdhnkc->bdhnqc', p, vs.astype(jnp.float32)) / den
    lse = (mx + jnp.log(den))[..., 0]
    o = o.reshape(b, d, h, nb * BAND, hd)[:, :, :, :m].transpose(0, 3, 1, 2, 4).reshape(b, s, h, hd)
    lse = lse.reshape(b, d, h, nb * BAND)[:, :, :, :m].transpose(0, 3, 1, 2).reshape(b, s, h)
    return o, lse


def dilated_attention(q, k, v):
    b, s, _ = q.shape
    q = rope(q.reshape(b, s, ATTN_HEADS, HEAD_DIM)) * (HEAD_DIM ** -0.5)
    k = rope(k.reshape(b, s, ATTN_HEADS, HEAD_DIM))
    v = v.reshape(b, s, ATTN_HEADS, HEAD_DIM)
    outs, lses = [], []
    for window, dil in DILATED_BRANCHES:
        o, l = dilated_branch(q, k, v, dil, window // dil)
        outs.append(o)
        lses.append(l)
    wts = jax.nn.softmax(jnp.stack(lses), axis=0)
    out = jnp.sum(wts[..., None] * jnp.stack(outs), axis=0)
    return out.reshape(b, s, ATTN_WIDTH).astype(q.dtype)


def spatial_gating(u, z, gain, w_s, b_s):
    b, s, _ = u.shape
    nc = s // SGU_CHUNK
    u = jax.nn.gelu(u)
    z = jax.nn.gelu(z).reshape(b, s, SGU_GROUPS, SGU_GROUP_DIM).astype(jnp.float32)
    mu = jnp.mean(z, -1, keepdims=True)
    var = jnp.mean(jnp.square(z - mu), -1, keepdims=True)
    z = (z - mu) * lax.rsqrt(var + EPS) * gain.reshape(SGU_GROUPS, SGU_GROUP_DIM)
    z = z.reshape(b, nc, SGU_CHUNK, SGU_GROUPS, SGU_GROUP_DIM)
    causal = jnp.tril(jnp.ones((SGU_CHUNK, SGU_CHUNK), dtype=bool))
    w = jnp.where(causal[None], w_s, 0)
    sp = jnp.einsum('gts,bnsgc->bntgc', w, z) + b_s.T[None, None, :, :, None]
    return (u * sp.reshape(b, s, SGU_WIDTH)).astype(u.dtype)


def short_conv(gb, gc, h, conv_w):
    z = gc * h
    y = lax.conv_general_dilated(z, conv_w[:, None, :], window_strides=(1,), padding=((CONV_TAPS - 1, 0),), dimension_numbers=('NWC', 'WIO', 'NWC'), feature_group_count=CONV_WIDTH)
    return gb * y


def hierarchical_moe(x, wg, bg, we, be, w_gate, w_up, w_down):
    b, s, dm = x.shape
    n = b * s
    xt = x.reshape(n, dm)
    g_logit = (xt @ wg).astype(jnp.float32) + bg.astype(jnp.float32)
    g_prob = jax.nn.softmax(g_logit, -1)
    _, g_idx = lax.top_k(g_logit, 1)
    g_p = jnp.take_along_axis(g_prob, g_idx, axis=1)
    e_logit = ((xt @ we).astype(jnp.float32) + be.astype(jnp.float32)).reshape(n, MOE_GROUPS, EXPERTS_PER_GROUP)
    e_logit = jnp.take_along_axis(e_logit, jnp.broadcast_to(g_idx[:, :, None], (n, 1, EXPERTS_PER_GROUP)), axis=1)[:, 0]
    top_logit, top_local = lax.top_k(e_logit, TOP_K_INNER)
    gate = g_p * jax.nn.softmax(top_logit, -1)
    expert = g_idx * EXPERTS_PER_GROUP + top_local
    n_assign = n * TOP_K_INNER
    e_flat = expert.reshape(n_assign)
    tok_flat = jnp.repeat(jnp.arange(n, dtype=jnp.int32), TOP_K_INNER)
    order = jnp.argsort(e_flat, stable=True)
    e_sorted = e_flat[order]
    tok_sorted = tok_flat[order]
    gate_sorted = gate.reshape(n_assign)[order]
    counts = jnp.bincount(e_flat, length=N_EXPERTS)
    padded = (counts + MOE_BLOCK - 1) // MOE_BLOCK * MOE_BLOCK
    pad_end = jnp.cumsum(padded)
    pad_start = pad_end - padded
    cnt_start = jnp.cumsum(counts) - counts
    dest = pad_start[e_sorted] + jnp.arange(n_assign, dtype=jnp.int32) - cnt_start[e_sorted]
    n_blocks = (n_assign + N_EXPERTS * (MOE_BLOCK - 1) + MOE_BLOCK - 1) // MOE_BLOCK
    row_tok = jnp.full((n_blocks * MOE_BLOCK,), n, jnp.int32).at[dest].set(tok_sorted)
    block_expert = jnp.minimum(jnp.searchsorted(pad_end, jnp.arange(n_blocks, dtype=jnp.int32) * MOE_BLOCK, side='right'), N_EXPERTS - 1)
    x_pad = jnp.concatenate([xt, jnp.zeros((1, dm), xt.dtype)], 0)

    def run_block(args):
        rows, e = args
        xb = x_pad[rows]
        hdn = jax.nn.silu(xb @ w_gate[e]) * (xb @ w_up[e])
        return hdn @ w_down[e]

    y_rows = lax.map(run_block, (row_tok.reshape(n_blocks, MOE_BLOCK), block_expert)).reshape(n_blocks * MOE_BLOCK, dm)
    y_assign = y_rows[dest] * gate_sorted[:, None].astype(y_rows.dtype)
    out = jax.ops.segment_sum(y_assign, tok_sorted, num_segments=n)
    return out.reshape(b, s, dm)


def setup_inputs(seed: int = 0) -> dict:
    key = jax.random.key(seed)
    ks = jax.random.split(key, 17)

    def nrm(k, shape, scale):
        return jax.random.normal(k, shape, jnp.float32) * scale

    return {
        'x': nrm(ks[0], (BATCH, SEQ, D_MODEL), 1.0),
        'w_in': nrm(ks[1], (DEPTH, D_MODEL, PROJ_WIDTH), D_MODEL ** -0.5),
        'w_out': nrm(ks[2], (DEPTH, D_MIX, D_MODEL), BETA * D_MIX ** -0.5),
        'branch_gain': 1.0 + nrm(ks[3], (DEPTH, D_MIX), 0.02),
        'sgu_gain': 1.0 + nrm(ks[4], (DEPTH, SGU_WIDTH), 0.02),
        'sgu_w': nrm(ks[5], (DEPTH, SGU_GROUPS, SGU_CHUNK, SGU_CHUNK), SGU_CHUNK ** -0.5),
        'sgu_b': 1.0 + nrm(ks[6], (DEPTH, SGU_GROUPS, SGU_CHUNK), 0.1),
        'conv_w': nrm(ks[7], (DEPTH, CONV_TAPS, CONV_WIDTH), CONV_TAPS ** -0.5),
        'ln_gain': 1.0 + nrm(ks[8], (DEPTH, 2, D_MODEL), 0.02),
        'ln_bias': nrm(ks[9], (DEPTH, 2, D_MODEL), 0.02),
        'router_group_w': nrm(ks[10], (DEPTH, D_MODEL, MOE_GROUPS), D_MODEL ** -0.5),
        'router_group_b': nrm(ks[11], (DEPTH, MOE_GROUPS), 0.01),
        'router_expert_w': nrm(ks[12], (DEPTH, D_MODEL, N_EXPERTS), D_MODEL ** -0.5),
        'router_expert_b': nrm(ks[13], (DEPTH, N_EXPERTS), 0.01),
        'expert_w_gate': nrm(ks[14], (DEPTH, N_EXPERTS, D_MODEL, D_EXPERT), D_MODEL ** -0.5),
        'expert_w_up': nrm(ks[15], (DEPTH, N_EXPERTS, D_MODEL, D_EXPERT), D_MODEL ** -0.5),
        'expert_w_down': nrm(ks[16], (DEPTH, N_EXPERTS, D_EXPERT, D_MODEL), BETA * D_EXPERT ** -0.5),
    }


def reference(x, w_in, w_out, branch_gain, sgu_gain, sgu_w, sgu_b, conv_w, ln_gain, ln_bias, router_group_w, router_group_b, router_expert_w, router_expert_b, expert_w_gate, expert_w_up, expert_w_down):
    a_end = ATTN_WIDTH
    b_end = ATTN_WIDTH + SGU_WIDTH
    split_at = []
    acc = 0
    for size in PROJ_SIZES[:-1]:
        acc += size
        split_at.append(acc)
    for l in range(DEPTH):
        proj = x @ w_in[l]
        q, k, v, u, z, gb, gc, h = jnp.split(proj, split_at, axis=-1)
        y_a = dilated_attention(q, k, v)
        y_b = spatial_gating(u, z, sgu_gain[l], sgu_w[l], sgu_b[l])
        y_c = short_conv(gb, gc, h, conv_w[l])
        g = branch_gain[l]
        mixed = jnp.concatenate([rms_norm(y_a, g[:a_end]), rms_norm(y_b, g[a_end:b_end]), rms_norm(y_c, g[b_end:])], axis=-1)
        x = layer_norm(ALPHA * x + mixed @ w_out[l], ln_gain[l, 0], ln_bias[l, 0])
        ffn = hierarchical_moe(x, router_group_w[l], router_group_b[l], router_expert_w[l], router_expert_b[l], expert_w_gate[l], expert_w_up[l], expert_w_down[l])
        x = layer_norm(ALPHA * x + ffn, ln_gain[l, 1], ln_bias[l, 1])
    return x
```

```python
import functools
import math

import jax
import jax.numpy as jnp
import numpy as np
from jax import lax
from jax.experimental import pallas as pl
from jax.experimental.pallas import tpu as pltpu

F32 = jnp.float32
BF16 = jnp.bfloat16
U32 = jnp.uint32
I32 = jnp.int32

D_MODEL = 1024
HEAD_DIM = 64
ATTN_W = 512
N_HEADS = 8
SGU_W = 256
SGU_GROUPS = 4
SGU_CHUNK = 128
CONV_W = 256
REST_W = 2 * SGU_W + 3 * CONV_W
PROJ_W = 3 * ATTN_W + REST_W
DILATIONS = (16, 4, 1)
BAND = 128
ROPE_THETA = 10000.0
MOE_GROUPS = 4
EXPERTS_PER_GROUP = 8
N_EXPERTS = MOE_GROUPS * EXPERTS_PER_GROUP
D_EXPERT = 512
EPS = 1e-5
NEG = -1e30

LANES = 128
VMEM_LIMIT = 56 * 1024 * 1024

PROJ_TM = 512
ATTN_TQ = 512
MIX_TM = 512
OUT_TM = 512
MOE_TB = 512
ROW_TM = 512


def _params(*sem):
    return pltpu.CompilerParams(dimension_semantics=sem, vmem_limit_bytes=VMEM_LIMIT)


def _pack_pairs(x):
    w = x.shape[1] // 2
    lo = lax.bitcast_convert_type(x[:, :w].astype(BF16).astype(F32), U32)
    hi = lax.bitcast_convert_type(x[:, w:].astype(BF16).astype(F32), U32)
    return (lo >> 16) | (hi & jnp.uint32(0xFFFF0000))


def _unpack_pairs(p):
    lo = lax.bitcast_convert_type(p << 16, F32)
    hi = lax.bitcast_convert_type(p & jnp.uint32(0xFFFF0000), F32)
    return lo, hi


def _proj_kernel(x_ref, w_ref, cos_ref, sin_ref, q_ref, k_ref, v_ref, r_ref):
    xb = x_ref[...].astype(BF16)
    cos = cos_ref[...]
    sin = sin_ref[...]
    lane = lax.broadcasted_iota(I32, cos.shape, 1)
    first_half = (lane % HEAD_DIM) < (HEAD_DIM // 2)

    def rope_store(col0, out_ref, scale):
        t = jnp.dot(xb, w_ref[:, col0:col0 + ATTN_W], preferred_element_type=F32)
        for c in range(ATTN_W // LANES):
            tc = t[:, c * LANES:(c + 1) * LANES]
            partner = jnp.where(first_half, pltpu.roll(tc, LANES - 32, 1), pltpu.roll(tc, 32, 1))
            o = tc * cos + partner * sin
            out_ref[:, c * LANES:(c + 1) * LANES] = (o * scale).astype(BF16)

    rope_store(0, q_ref, HEAD_DIM ** -0.5)
    rope_store(ATTN_W, k_ref, 1.0)
    v_ref[...] = jnp.dot(xb, w_ref[:, 2 * ATTN_W:3 * ATTN_W], preferred_element_type=F32).astype(BF16)
    r_ref[...] = jnp.dot(xb, w_ref[:, 3 * ATTN_W:], preferred_element_type=F32).astype(BF16)


def _proj(x, w_bf, cos_t, sin_t):
    b, s, _ = x.shape
    tm = PROJ_TM
    out3 = jax.ShapeDtypeStruct((b, s, ATTN_W), BF16)
    tok = lambda width: pl.BlockSpec((None, tm, width), lambda si, bi: (bi, si, 0))
    return pl.pallas_call(
        _proj_kernel,
        grid=(s // tm, b),
        in_specs=[tok(D_MODEL),
                  pl.BlockSpec((D_MODEL, PROJ_W), lambda si, bi: (0, 0)),
                  pl.BlockSpec((tm, LANES), lambda si, bi: (si, 0)),
                  pl.BlockSpec((tm, LANES), lambda si, bi: (si, 0))],
        out_specs=[tok(ATTN_W), tok(ATTN_W), tok(ATTN_W), tok(REST_W)],
        out_shape=[out3, out3, out3, jax.ShapeDtypeStruct((b, s, REST_W), BF16)],
        compiler_params=_params("arbitrary", "arbitrary"),
        name="proj",
    )(x, w_bf, cos_t, sin_t)


def _attn_kernel(*refs, tq, final):
    if final:
        (q_ref, k_ref, v_ref, o4_ref, s4_ref, o16_ref, s16_ref, g_ref,
         out_ref, kbuf, vbuf, acc) = refs
    else:
        q_ref, k_ref, v_ref, o_ref, st_ref, kbuf, vbuf = refs
    i = pl.program_id(2)

    @pl.when(i == 0)
    def _():
        kbuf[0:BAND, :] = jnp.zeros((BAND, ATTN_W), BF16)
        vbuf[0:BAND, :] = jnp.zeros((BAND, ATTN_W), BF16)

    kbuf[BAND:BAND + tq, :] = k_ref[...]
    vbuf[BAND:BAND + tq, :] = v_ref[...]

    row = lax.broadcasted_iota(I32, (BAND, 2 * BAND), 0)
    col = lax.broadcasted_iota(I32, (BAND, 2 * BAND), 1)
    lo_key = jnp.where(col < BAND, row, BAND)
    hi_key = jnp.where(col < BAND, BAND - 1, row + BAND)
    band_bias = jnp.where(jnp.logical_and(col >= lo_key, col <= hi_key), 0.0, NEG).astype(F32)
    def block(j, carry):
        lane = lax.broadcasted_iota(I32, (BAND, LANES), 1)
        lane_lo = lane < HEAD_DIM
        r0 = pl.multiple_of(j * BAND, BAND)
        first_col = jnp.where(jnp.logical_or(j > 0, i > 0), 0, BAND)
        bias = jnp.where(col >= first_col, band_bias, NEG)
        if final:
            s4 = s4_ref[pl.ds(r0, BAND), :]
            s16 = s16_ref[pl.ds(r0, BAND), :]
            ssq = jnp.zeros((BAND, 1), F32)
        else:
            st = jnp.zeros((BAND, LANES), F32)
        for p in range(ATTN_W // LANES):
            ql = q_ref[pl.ds(r0, BAND), p * LANES:(p + 1) * LANES]
            kk = kbuf[pl.ds(r0, 2 * BAND), p * LANES:(p + 1) * LANES]
            vv = vbuf[pl.ds(r0, 2 * BAND), p * LANES:(p + 1) * LANES]
            o_h, l_h = [], []
            for hh in range(2):
                qm = jnp.where(lane_lo if hh == 0 else jnp.logical_not(lane_lo), ql, jnp.zeros_like(ql))
                sc = lax.dot_general(qm, kk, (((1,), (1,)), ((), ())), preferred_element_type=F32)
                sc = sc + bias
                mx = jnp.max(sc, axis=1, keepdims=True)
                pe = jnp.exp(sc - mx)
                den = jnp.sum(pe, axis=1, keepdims=True)
                o = jnp.dot(pe.astype(BF16), vv, preferred_element_type=F32) / den
                o_h.append(o)
                l_h.append(mx + jnp.log(den))
            o_pair = jnp.where(lane_lo, o_h[0], o_h[1])
            if final:
                l1 = jnp.where(lane_lo, l_h[0], l_h[1])
                l4 = jnp.where(lane_lo, s4[:, 2 * p:2 * p + 1], s4[:, 2 * p + 1:2 * p + 2])
                l16 = jnp.where(lane_lo, s16[:, 2 * p:2 * p + 1], s16[:, 2 * p + 1:2 * p + 2])
                top = jnp.maximum(l1, jnp.maximum(l4, l16))
                e1 = jnp.exp(l1 - top)
                e4 = jnp.exp(l4 - top)
                e16 = jnp.exp(l16 - top)
                o4 = o4_ref[pl.ds(r0, BAND), p * LANES:(p + 1) * LANES].astype(F32)
                o16 = o16_ref[pl.ds(r0, BAND), p * LANES:(p + 1) * LANES].astype(F32)
                y = (e1 * o_pair + e4 * o4 + e16 * o16) / (e1 + e4 + e16)
                acc[:, p * LANES:(p + 1) * LANES] = y
                ssq = ssq + jnp.sum(y * y, axis=1, keepdims=True)
            else:
                o_ref[pl.ds(r0, BAND), p * LANES:(p + 1) * LANES] = o_pair.astype(BF16)
                st = jnp.where(lane == 2 * p, l_h[0], st)
                st = jnp.where(lane == 2 * p + 1, l_h[1], st)
        if final:
            inv = lax.rsqrt(ssq * (1.0 / ATTN_W) + EPS)
            out_ref[pl.ds(r0, BAND), :] = (acc[...] * inv * g_ref[...]).astype(BF16)
        else:
            st_ref[pl.ds(r0, BAND), :] = st
        return carry

    lax.fori_loop(0, tq // BAND, block, 0)
    kbuf[0:BAND, :] = kbuf[tq:tq + BAND, :]
    vbuf[0:BAND, :] = vbuf[tq:tq + BAND, :]


def _attn_branch(q, k, v, d, extra=None):
    b, s, _ = q.shape
    m = s // d
    tq = min(ATTN_TQ, m)
    view = lambda a, w: a.reshape(b, m, d * w)
    blk = lambda w: pl.BlockSpec((None, tq, w), lambda bi, ri, ii: (bi, ii, ri))
    final = extra is not None
    ins = [view(q, ATTN_W), view(k, ATTN_W), view(v, ATTN_W)]
    in_specs = [blk(ATTN_W)] * 3
    scratch = [pltpu.VMEM((tq + BAND, ATTN_W), BF16), pltpu.VMEM((tq + BAND, ATTN_W), BF16)]
    if final:
        o4, s4, o16, s16, gain = extra
        ins += [o4, s4, o16, s16, gain]
        in_specs += [blk(ATTN_W), blk(LANES), blk(ATTN_W), blk(LANES),
                     pl.BlockSpec((1, ATTN_W), lambda bi, ri, ii: (0, 0))]
        out_shape = jax.ShapeDtypeStruct((b, m, d * ATTN_W), BF16)
        out_specs = blk(ATTN_W)
        scratch.append(pltpu.VMEM((BAND, ATTN_W), F32))
    else:
        out_shape = [jax.ShapeDtypeStruct((b, m, d * ATTN_W), BF16),
                     jax.ShapeDtypeStruct((b, m, d * LANES), F32)]
        out_specs = [blk(ATTN_W), blk(LANES)]
    out = pl.pallas_call(
        functools.partial(_attn_kernel, tq=tq, final=final),
        grid=(b, d, m // tq),
        in_specs=in_specs, out_specs=out_specs, out_shape=out_shape,
        scratch_shapes=scratch,
        compiler_params=_params("arbitrary", "arbitrary", "arbitrary"),
        name=f"attn_d{d}",
    )(*ins)
    if final:
        return out.reshape(b, s, ATTN_W)
    return out[0].reshape(b, s, ATTN_W), out[1].reshape(b, s, LANES)


def _gelu_tanh(x):
    c = math.sqrt(2.0 / math.pi)
    return x * (0.5 * (1.0 + jnp.tanh(c * (x + 0.044715 * (x * x * x)))))


def _split_dot(x, m_bf):
    hi = x.astype(BF16)
    lo = (x - hi.astype(F32)).astype(BF16)
    return (jnp.dot(hi, m_bf, preferred_element_type=F32)
            + jnp.dot(lo, m_bf, preferred_element_type=F32))


def _mixbc_kernel(r_ref, halo_ref, sg_ref, sw_ref, sb_ref, cw_ref, bg_ref, out_ref, *, tm):
    si = pl.program_id(1)
    u = r_ref[:, 0:SGU_W].astype(F32)
    z = r_ref[:, SGU_W:2 * SGU_W].astype(F32)
    gb = r_ref[:, 2 * SGU_W:2 * SGU_W + CONV_W].astype(F32)
    gc = r_ref[:, 2 * SGU_W + CONV_W:2 * SGU_W + 2 * CONV_W].astype(F32)
    hh = r_ref[:, 2 * SGU_W + 2 * CONV_W:].astype(F32)

    gdim = SGU_W // SGU_GROUPS
    ri = lax.broadcasted_iota(I32, (SGU_W, SGU_W), 0) // gdim
    ci = lax.broadcasted_iota(I32, (SGU_W, SGU_W), 1) // gdim
    avg = jnp.where(ri == ci, 1.0 / gdim, 0.0).astype(BF16)
    z = _gelu_tanh(z)
    zc = z - _split_dot(z, avg)
    var = _split_dot(zc * zc, avg)
    zn = (zc * lax.rsqrt(var + EPS) * sg_ref[...]).astype(BF16)

    tr = lax.broadcasted_iota(I32, (SGU_CHUNK, SGU_CHUNK), 0)
    tc = lax.broadcasted_iota(I32, (SGU_CHUNK, SGU_CHUNK), 1)
    w_cat = jnp.concatenate(
        [jnp.where(tc <= tr, sw_ref[g], 0.0).astype(BF16) for g in range(SGU_GROUPS)], axis=1)
    lane_g = lax.broadcasted_iota(I32, (SGU_CHUNK, SGU_W), 1) // gdim
    gu = _gelu_tanh(u)
    bias = sb_ref[...]
    yb = []
    for c in range(tm // SGU_CHUNK):
        zch = zn[c * SGU_CHUNK:(c + 1) * SGU_CHUNK, :]
        stack = jnp.concatenate(
            [jnp.where(lane_g == g, zch, jnp.zeros_like(zch)) for g in range(SGU_GROUPS)], axis=0)
        sp = jnp.dot(w_cat, stack, preferred_element_type=F32) + bias
        yb.append(gu[c * SGU_CHUNK:(c + 1) * SGU_CHUNK, :] * sp)
    yb = jnp.concatenate(yb, axis=0)

    zz = gc * hh
    hrows = halo_ref.shape[0]
    prev = (halo_ref[:, 2 * SGU_W + CONV_W:2 * SGU_W + 2 * CONV_W].astype(F32)
            * halo_ref[:, 2 * SGU_W + 2 * CONV_W:].astype(F32))
    prev = prev * (si > 0).astype(F32)
    ext = jnp.concatenate([prev, zz], axis=0)
    z1 = ext[hrows - 1:hrows - 1 + tm, :]
    z2 = ext[hrows - 2:hrows - 2 + tm, :]
    yc = gb * (cw_ref[0:1, :] * z2 + cw_ref[1:2, :] * z1 + cw_ref[2:3, :] * zz)

    def rms(t, g):
        return t * lax.rsqrt(jnp.mean(t * t, axis=1, keepdims=True) + EPS) * g

    out_ref[:, 0:SGU_W] = rms(yb, bg_ref[:, 0:SGU_W]).astype(BF16)
    out_ref[:, SGU_W:] = rms(yc, bg_ref[:, SGU_W:]).astype(BF16)


def _mixbc(rest, sgu_gain, sgu_w, sgu_bias_tile, conv_w, gain_bc):
    b, s, _ = rest.shape
    tm = MIX_TM
    hrows = 16
    full = lambda shape: pl.BlockSpec(shape, lambda bi, si: (0,) * len(shape))
    return pl.pallas_call(
        functools.partial(_mixbc_kernel, tm=tm),
        grid=(b, s // tm),
        in_specs=[pl.BlockSpec((None, tm, REST_W), lambda bi, si: (bi, si, 0)),
                  pl.BlockSpec((None, hrows, REST_W),
                               lambda bi, si: (bi, jnp.maximum(si * (tm // hrows) - 1, 0), 0)),
                  full((1, SGU_W)), full((SGU_GROUPS, SGU_CHUNK, SGU_CHUNK)),
                  full((SGU_CHUNK, SGU_W)), full((3, CONV_W)), full((1, SGU_W + CONV_W))],
        out_specs=pl.BlockSpec((None, tm, SGU_W + CONV_W), lambda bi, si: (bi, si, 0)),
        out_shape=jax.ShapeDtypeStruct((b, s, SGU_W + CONV_W), BF16),
        compiler_params=_params("arbitrary", "arbitrary"),
        name="mixbc",
    )(rest, rest, sgu_gain, sgu_w, sgu_bias_tile, conv_w, gain_bc)


def _layer_norm(t, g, b):
    mu = jnp.mean(t, axis=1, keepdims=True)
    tc = t - mu
    var = jnp.mean(tc * tc, axis=1, keepdims=True)
    return tc * lax.rsqrt(var + EPS) * g + b


def _outproj_kernel(ma_ref, mb_ref, x_ref, wa_ref, wb_ref, lg_ref, lb_ref, rwh_ref, rwl_ref, rb_ref,
                    x1_ref, xp_ref, info_ref, cnt_ref, run_ref, *, tm, alpha):
    i = pl.program_id(0)

    @pl.when(i == 0)
    def _():
        run_ref[...] = jnp.zeros_like(run_ref)

    y = (jnp.dot(ma_ref[...], wa_ref[...], preferred_element_type=F32)
         + jnp.dot(mb_ref[...], wb_ref[...], preferred_element_type=F32))
    x1 = _layer_norm(alpha * x_ref[...] + y, lg_ref[...], lb_ref[...])
    x1_ref[...] = x1
    xp_ref[...] = _pack_pairs(x1)

    hi = x1.astype(BF16)
    lo = (x1 - hi.astype(F32)).astype(BF16)
    logit = (jnp.dot(hi, rwh_ref[...], preferred_element_type=F32)
             + jnp.dot(lo, rwh_ref[...], preferred_element_type=F32)
             + jnp.dot(hi, rwl_ref[...], preferred_element_type=F32)) + rb_ref[...]
    lane = lax.broadcasted_iota(I32, (tm, LANES), 1)
    lane_f = lane.astype(F32)

    def top(mask):
        v = jnp.max(jnp.where(mask, logit, NEG), axis=1, keepdims=True)
        first = jnp.min(jnp.where(jnp.logical_and(mask, logit == v), lane_f, float(LANES)),
                        axis=1, keepdims=True)
        return v, first.astype(I32)

    is_g = lane < MOE_GROUPS
    gmax, gidx = top(is_g)
    g_p = 1.0 / jnp.sum(jnp.where(is_g, jnp.exp(logit - gmax), 0.0), axis=1, keepdims=True)
    in_grp = jnp.logical_and(lane >= MOE_GROUPS + gidx * EXPERTS_PER_GROUP,
                             lane < MOE_GROUPS + (gidx + 1) * EXPERTS_PER_GROUP)
    v1, i1 = top(in_grp)
    v2, i2 = top(jnp.logical_and(in_grp, lane != i1))
    e21 = jnp.exp(v2 - v1)
    gate1 = g_p / (1.0 + e21)
    gate2 = g_p * e21 / (1.0 + e21)
    ex1 = i1 - MOE_GROUPS
    ex2 = i2 - MOE_GROUPS

    oh1 = lane == ex1
    oh2 = lane == ex2
    oh = (oh1.astype(F32) + oh2.astype(F32))
    tr = lax.broadcasted_iota(I32, (tm, tm), 0)
    tc = lax.broadcasted_iota(I32, (tm, tm), 1)
    lower = jnp.where(tc < tr, 1.0, 0.0).astype(BF16)
    before = jnp.dot(lower, oh.astype(BF16), preferred_element_type=F32) + run_ref[0:1, :]
    rank1 = jnp.sum(jnp.where(oh1, before, 0.0), axis=1, keepdims=True)
    rank2 = jnp.sum(jnp.where(oh2, before, 0.0), axis=1, keepdims=True)
    run_new = run_ref[0:1, :] + jnp.sum(oh, axis=0, keepdims=True)
    run_ref[...] = jnp.broadcast_to(run_new, run_ref.shape)
    cnt_ref[...] = jnp.broadcast_to(run_new, cnt_ref.shape)

    info = jnp.where(lane == 0, ex1.astype(F32), 0.0)
    info = jnp.where(lane == 1, ex2.astype(F32), info)
    info = jnp.where(lane == 2, gate1, info)
    info = jnp.where(lane == 3, gate2, info)
    info = jnp.where(lane == 4, rank1, info)
    info = jnp.where(lane == 5, rank2, info)
    info_ref[...] = info


def _outproj(ma, mbc, x, wo_a, wo_b, ln_g, ln_b, rw_hi, rw_lo, rbias, alpha):
    n = x.shape[0]
    tm = OUT_TM
    tok = lambda w: pl.BlockSpec((tm, w), lambda i: (i, 0))
    full = lambda shape: pl.BlockSpec(shape, lambda i: (0,) * len(shape))
    return pl.pallas_call(
        functools.partial(_outproj_kernel, tm=tm, alpha=alpha),
        grid=(n // tm,),
        in_specs=[tok(ATTN_W), tok(SGU_W + CONV_W), tok(D_MODEL),
                  full((ATTN_W, D_MODEL)), full((SGU_W + CONV_W, D_MODEL)),
                  full((1, D_MODEL)), full((1, D_MODEL)),
                  full((D_MODEL, LANES)), full((D_MODEL, LANES)), full((1, LANES))],
        out_specs=[tok(D_MODEL), tok(D_MODEL // 2), tok(LANES), full((8, LANES))],
        out_shape=[jax.ShapeDtypeStruct((n, D_MODEL), F32),
                   jax.ShapeDtypeStruct((n, D_MODEL // 2), U32),
                   jax.ShapeDtypeStruct((n, LANES), F32),
                   jax.ShapeDtypeStruct((8, LANES), F32)],
        scratch_shapes=[pltpu.VMEM((8, LANES), F32)],
        compiler_params=_params("arbitrary"),
        name="outproj",
    )(ma, mbc, x, wo_a, wo_b, ln_g, ln_b, rw_hi, rw_lo, rbias)


def _row_copy(src_ref, src_row, dst_ref, dst_row, sem):
    return pltpu.make_async_copy(src_ref.at[pl.ds(src_row, 1)], dst_ref.at[pl.ds(dst_row, 1)], sem)


def _dispatch_kernel(dest_hbm, xp_ref, xs_in, xs_out, idx, isem, sem, *, tm):
    del xs_in
    i = pl.program_id(0)
    cp = pltpu.make_async_copy(dest_hbm.at[i], idx, isem)
    cp.start()
    cp.wait()

    def issue(t, carry):
        _row_copy(xp_ref, t, xs_out, idx[t], sem).start()
        _row_copy(xp_ref, t, xs_out, idx[tm + t], sem).start()
        return carry

    lax.fori_loop(0, tm, issue, 0)

    def drain(t, carry):
        _row_copy(xp_ref, 0, xs_out, 0, sem).wait()
        _row_copy(xp_ref, 0, xs_out, 0, sem).wait()
        return carry

    lax.fori_loop(0, tm, drain, 0)


def _dispatch(dest_tiles, xp, xs_zero):
    n, w = xp.shape
    tm = ROW_TM
    return pl.pallas_call(
        functools.partial(_dispatch_kernel, tm=tm),
        grid=(n // tm,),
        in_specs=[pl.BlockSpec(memory_space=pl.ANY),
                  pl.BlockSpec((tm, w), lambda i: (i, 0)),
                  pl.BlockSpec(memory_space=pl.ANY)],
        out_specs=pl.BlockSpec(memory_space=pl.ANY),
        out_shape=jax.ShapeDtypeStruct(xs_zero.shape, U32),
        scratch_shapes=[pltpu.SMEM((2 * tm,), I32), pltpu.SemaphoreType.DMA, pltpu.SemaphoreType.DMA],
        input_output_aliases={2: 0},
        compiler_params=_params("arbitrary"),
        name="dispatch",
    )(dest_tiles, xp, xs_zero)


def _expert_kernel(be_ref, nu_ref, xs_ref, wg_ref, wu_ref, wd_ref, y_ref):
    i = pl.program_id(0)

    @pl.when(i < nu_ref[0])
    def _():
        lo, hi = _unpack_pairs(xs_ref[...])
        xb = jnp.concatenate([lo, hi], axis=1).astype(BF16)
        g = jnp.dot(xb, wg_ref[...], preferred_element_type=F32)
        u = jnp.dot(xb, wu_ref[...], preferred_element_type=F32)
        hdn = (g * (1.0 / (1.0 + jnp.exp(-g))) * u).astype(BF16)
        y_ref[...] = _pack_pairs(jnp.dot(hdn, wd_ref[...], preferred_element_type=F32))

    @pl.when(i >= nu_ref[0])
    def _():
        y_ref[...] = jnp.zeros_like(y_ref)


def _experts(block_expert, n_used, xs, wg, wu, wd):
    rows, w = xs.shape
    tb = MOE_TB
    blk = lambda i, be, nu: (jnp.minimum(i, nu[0] - 1), 0)
    oblk = lambda i, be, nu: (i, 0)
    wsel = lambda i, be, nu: (be[jnp.minimum(i, nu[0] - 1)], 0, 0)
    return pl.pallas_call(
        _expert_kernel,
        grid_spec=pltpu.PrefetchScalarGridSpec(
            num_scalar_prefetch=2,
            grid=(rows // tb,),
            in_specs=[pl.BlockSpec((tb, w), blk),
                      pl.BlockSpec((None, D_MODEL, D_EXPERT), wsel),
                      pl.BlockSpec((None, D_MODEL, D_EXPERT), wsel),
                      pl.BlockSpec((None, D_EXPERT, D_MODEL), wsel)],
            out_specs=pl.BlockSpec((tb, w), oblk)),
        out_shape=jax.ShapeDtypeStruct((rows, w), U32),
        compiler_params=_params("arbitrary"),
        name="experts",
    )(block_expert, n_used, xs, wg, wu, wd)


def _combine_kernel(dest_hbm, y_hbm, x_ref, info_ref, lg_ref, lb_ref, out_ref, idx, ybuf, isem, sem,
                    *, tm, alpha):
    i = pl.program_id(0)
    cp = pltpu.make_async_copy(dest_hbm.at[i], idx, isem)
    cp.start()
    cp.wait()

    def issue(t, carry):
        _row_copy(y_hbm, idx[t], ybuf, t, sem).start()
        _row_copy(y_hbm, idx[tm + t], ybuf, tm + t, sem).start()
        return carry

    lax.fori_loop(0, tm, issue, 0)

    def drain(t, carry):
        _row_copy(y_hbm, 0, ybuf, 0, sem).wait()
        _row_copy(y_hbm, 0, ybuf, 0, sem).wait()
        return carry

    lax.fori_loop(0, tm, drain, 0)

    info = info_ref[...]
    g1 = info[:, 2:3]
    g2 = info[:, 3:4]
    a_lo, a_hi = _unpack_pairs(ybuf[0:tm, :])
    b_lo, b_hi = _unpack_pairs(ybuf[tm:2 * tm, :])
    half = D_MODEL // 2
    t_lo = alpha * x_ref[:, 0:half] + g1 * a_lo + g2 * b_lo
    t_hi = alpha * x_ref[:, half:] + g1 * a_hi + g2 * b_hi
    t = jnp.concatenate([t_lo, t_hi], axis=1)
    out_ref[...] = _layer_norm(t, lg_ref[...], lb_ref[...])


def _combine(dest_tiles, y, x1, info, ln_g, ln_b, alpha):
    n = x1.shape[0]
    tm = ROW_TM
    tok = lambda w: pl.BlockSpec((tm, w), lambda i: (i, 0))
    full = lambda shape: pl.BlockSpec(shape, lambda i: (0,) * len(shape))
    return pl.pallas_call(
        functools.partial(_combine_kernel, tm=tm, alpha=alpha),
        grid=(n // tm,),
        in_specs=[pl.BlockSpec(memory_space=pl.ANY), pl.BlockSpec(memory_space=pl.ANY),
                  tok(D_MODEL), tok(LANES), full((1, D_MODEL)), full((1, D_MODEL))],
        out_specs=tok(D_MODEL),
        out_shape=jax.ShapeDtypeStruct((n, D_MODEL), F32),
        scratch_shapes=[pltpu.SMEM((2 * tm,), I32), pltpu.VMEM((2 * tm, D_MODEL // 2), U32),
                        pltpu.SemaphoreType.DMA, pltpu.SemaphoreType.DMA],
        compiler_params=_params("arbitrary"),
        name="combine",
    )(dest_tiles, y, x1, info, ln_g, ln_b)


def _rope_tables(s):
    half = HEAD_DIM // 2
    inv_freq = ROPE_THETA ** (-jnp.arange(half, dtype=F32) / half)
    ang = jnp.arange(s, dtype=F32)[:, None] * inv_freq[None, :]
    cos = jnp.cos(ang)
    sin = jnp.sin(ang)
    cos_t = jnp.tile(cos, (1, LANES // half))
    sin_t = jnp.tile(jnp.concatenate([-sin, sin], axis=1), (1, LANES // HEAD_DIM))
    return cos_t, sin_t


def _forward(x, w_in, w_out, branch_gain, sgu_gain, sgu_w, sgu_b, conv_w, ln_gain, ln_bias,
             router_group_w, router_group_b, router_expert_w, router_expert_b,
             expert_w_gate, expert_w_up, expert_w_down):
    b, s, _ = x.shape
    depth = w_in.shape[0]
    n = b * s
    alpha = (2.0 * depth) ** 0.25
    cos_t, sin_t = _rope_tables(s)
    tb = MOE_TB
    n_blocks = (2 * n + N_EXPERTS * (tb - 1) + tb - 1) // tb
    rows = n_blocks * tb

    for l in range(depth):
        q, k, v, rest = _proj(x, w_in[l].astype(BF16), cos_t, sin_t)
        o16, s16 = _attn_branch(q, k, v, 16)
        o4, s4 = _attn_branch(q, k, v, 4)
        g = branch_gain[l]
        ma = _attn_branch(q, k, v, 1, extra=(o4, s4, o16, s16, g[None, :ATTN_W]))
        bias_tile = jnp.repeat(sgu_b[l].T, SGU_W // SGU_GROUPS, axis=1)
        mbc = _mixbc(rest, sgu_gain[l][None, :], sgu_w[l], bias_tile, conv_w[l], g[None, ATTN_W:])

        rw = jnp.zeros((D_MODEL, LANES), F32)
        rw = rw.at[:, :MOE_GROUPS].set(router_group_w[l])
        rw = rw.at[:, MOE_GROUPS:MOE_GROUPS + N_EXPERTS].set(router_expert_w[l])
        rw_hi = rw.astype(BF16)
        rw_lo = (rw - rw_hi.astype(F32)).astype(BF16)
        rbias = jnp.zeros((1, LANES), F32)
        rbias = rbias.at[0, :MOE_GROUPS].set(router_group_b[l])
        rbias = rbias.at[0, MOE_GROUPS:MOE_GROUPS + N_EXPERTS].set(router_expert_b[l])
        wo = w_out[l].astype(BF16)
        x1, xp, info, cnt = _outproj(
            ma.reshape(n, ATTN_W), mbc.reshape(n, SGU_W + CONV_W), x.reshape(n, D_MODEL),
            wo[:ATTN_W], wo[ATTN_W:], ln_gain[l, 0][None], ln_bias[l, 0][None], rw_hi, rw_lo, rbias, alpha)

        counts = cnt[0, :N_EXPERTS].astype(I32)
        padded = (counts + tb - 1) // tb * tb
        pad_end = jnp.cumsum(padded)
        pad_start = pad_end - padded
        ex = info[:, 0:2].astype(I32)
        dest = pad_start[ex] + info[:, 4:6].astype(I32)
        dest_tiles = dest.reshape(n // ROW_TM, ROW_TM, 2).transpose(0, 2, 1).reshape(n // ROW_TM, 2 * ROW_TM)
        block_expert = jnp.minimum(
            jnp.searchsorted(pad_end, jnp.arange(n_blocks, dtype=I32) * tb, side='right'),
            N_EXPERTS - 1).astype(I32)
        n_used = (pad_end[-1:] // tb).astype(I32)

        xs = _dispatch(dest_tiles, xp, jnp.zeros((rows, D_MODEL // 2), U32))
        ys = _experts(block_expert, n_used, xs, expert_w_gate[l].astype(BF16),
                      expert_w_up[l].astype(BF16), expert_w_down[l].astype(BF16))
        x = _combine(dest_tiles, ys, x1, info, ln_gain[l, 1][None], ln_bias[l, 1][None], alpha).reshape(b, s, D_MODEL)
    return x


def kernel(x, w_in, w_out, branch_gain, sgu_gain, sgu_w, sgu_b, conv_w, ln_gain, ln_bias, router_group_w, router_group_b, router_expert_w, router_expert_b, expert_w_gate, expert_w_up, expert_w_down):
    return _forward(x, w_in, w_out, branch_gain, sgu_gain, sgu_w, sgu_b, conv_w, ln_gain, ln_bias,
                    router_group_w, router_group_b, router_expert_w, router_expert_b,
                    expert_w_gate, expert_w_up, expert_w_down)
```

```python
import functools
import math

import jax
import jax.numpy as jnp
from jax import lax
from jax.experimental import pallas as pl
from jax.experimental.pallas import tpu as pltpu
from jax.experimental.pallas import tpu_sc as plsc

F32 = jnp.float32
BF16 = jnp.bfloat16
U32 = jnp.uint32
I32 = jnp.int32

D_MODEL = 1024
HEAD_DIM = 64
ATTN_W = 512
N_HEADS = 8
SGU_W = 256
SGU_GROUPS = 4
SGU_CHUNK = 128
CONV_W = 256
REST_W = 2 * SGU_W + 3 * CONV_W
PROJ_W = 3 * ATTN_W + REST_W
DILATIONS = (16, 4, 1)
BAND = 128
ROPE_THETA = 10000.0
MOE_GROUPS = 4
EXPERTS_PER_GROUP = 8
N_EXPERTS = MOE_GROUPS * EXPERTS_PER_GROUP
D_EXPERT = 512
EPS = 1e-5
NEG = -1e30

LANES = 128
VMEM_LIMIT = 56 * 1024 * 1024

PROJ_TM = 512
ATTN_TQ = 512
MIX_TM = 512
OUT_TM = 512
MOE_TB = 512
ROW_TM = 512
SC_CORES = 2
SC_SUBCORES = 16
SC_WORKERS = SC_CORES * SC_SUBCORES
SC_CHUNK = 64


def _params(*sem):
    return pltpu.CompilerParams(dimension_semantics=sem, vmem_limit_bytes=VMEM_LIMIT)


def _pack_pairs(x):
    w = x.shape[1] // 2
    lo = lax.bitcast_convert_type(x[:, :w].astype(BF16).astype(F32), U32)
    hi = lax.bitcast_convert_type(x[:, w:].astype(BF16).astype(F32), U32)
    return (lo >> 16) | (hi & jnp.uint32(0xFFFF0000))


def _unpack_pairs(p):
    lo = lax.bitcast_convert_type(p << 16, F32)
    hi = lax.bitcast_convert_type(p & jnp.uint32(0xFFFF0000), F32)
    return lo, hi


def _proj_kernel(x_ref, w_ref, cos_ref, sin_ref, q_ref, k_ref, v_ref, r_ref):
    xb = x_ref[...].astype(BF16)
    cos = cos_ref[...]
    sin = sin_ref[...]
    lane = lax.broadcasted_iota(I32, cos.shape, 1)
    first_half = (lane % HEAD_DIM) < (HEAD_DIM // 2)

    def rope_store(col0, out_ref, scale):
        t = jnp.dot(xb, w_ref[:, col0:col0 + ATTN_W], preferred_element_type=F32)
        for c in range(ATTN_W // LANES):
            tc = t[:, c * LANES:(c + 1) * LANES]
            partner = jnp.where(first_half, pltpu.roll(tc, LANES - 32, 1), pltpu.roll(tc, 32, 1))
            o = tc * cos + partner * sin
            out_ref[:, c * LANES:(c + 1) * LANES] = (o * scale).astype(BF16)

    rope_store(0, q_ref, HEAD_DIM ** -0.5)
    rope_store(ATTN_W, k_ref, 1.0)
    v_ref[...] = jnp.dot(xb, w_ref[:, 2 * ATTN_W:3 * ATTN_W], preferred_element_type=F32).astype(BF16)
    r_ref[...] = jnp.dot(xb, w_ref[:, 3 * ATTN_W:], preferred_element_type=F32).astype(BF16)


def _proj(x, w_bf, cos_t, sin_t):
    b, s, _ = x.shape
    tm = PROJ_TM
    out3 = jax.ShapeDtypeStruct((b, s, ATTN_W), BF16)
    tok = lambda width: pl.BlockSpec((None, tm, width), lambda si, bi: (bi, si, 0))
    return pl.pallas_call(
        _proj_kernel,
        grid=(s // tm, b),
        in_specs=[tok(D_MODEL),
                  pl.BlockSpec((D_MODEL, PROJ_W), lambda si, bi: (0, 0)),
                  pl.BlockSpec((tm, LANES), lambda si, bi: (si, 0)),
                  pl.BlockSpec((tm, LANES), lambda si, bi: (si, 0))],
        out_specs=[tok(ATTN_W), tok(ATTN_W), tok(ATTN_W), tok(REST_W)],
        out_shape=[out3, out3, out3, jax.ShapeDtypeStruct((b, s, REST_W), BF16)],
        compiler_params=_params("arbitrary", "arbitrary"),
        name="proj",
    )(x, w_bf, cos_t, sin_t)


def _attn_kernel(*refs, tq, final):
    if final:
        (q_ref, k_ref, v_ref, o4_ref, s4_ref, o16_ref, s16_ref, g_ref,
         out_ref, kbuf, vbuf, acc) = refs
    else:
        q_ref, k_ref, v_ref, o_ref, st_ref, kbuf, vbuf = refs
    i = pl.program_id(2)

    @pl.when(i == 0)
    def _():
        kbuf[0:BAND, :] = jnp.zeros((BAND, ATTN_W), BF16)
        vbuf[0:BAND, :] = jnp.zeros((BAND, ATTN_W), BF16)

    kbuf[BAND:BAND + tq, :] = k_ref[...]
    vbuf[BAND:BAND + tq, :] = v_ref[...]

    row = lax.broadcasted_iota(I32, (BAND, 2 * BAND), 0)
    col = lax.broadcasted_iota(I32, (BAND, 2 * BAND), 1)
    lo_key = jnp.where(col < BAND, row, BAND)
    hi_key = jnp.where(col < BAND, BAND - 1, row + BAND)
    band_bias = jnp.where(jnp.logical_and(col >= lo_key, col <= hi_key), 0.0, NEG).astype(F32)

    def block(j, carry):
        lane = lax.broadcasted_iota(I32, (BAND, LANES), 1)
        lane_lo = lane < HEAD_DIM
        r0 = pl.multiple_of(j * BAND, BAND)
        first_col = jnp.where(jnp.logical_or(j > 0, i > 0), 0, BAND)
        bias = jnp.where(col >= first_col, band_bias, NEG)
        if final:
            s4 = s4_ref[pl.ds(r0, BAND), :]
            s16 = s16_ref[pl.ds(r0, BAND), :]
            ssq = jnp.zeros((BAND, 1), F32)
        else:
            st = jnp.zeros((BAND, LANES), F32)
        for p in range(ATTN_W // LANES):
            ql = q_ref[pl.ds(r0, BAND), p * LANES:(p + 1) * LANES]
            kk = kbuf[pl.ds(r0, 2 * BAND), p * LANES:(p + 1) * LANES]
            vv = vbuf[pl.ds(r0, 2 * BAND), p * LANES:(p + 1) * LANES]
            o_h, l_h = [], []
            for hh in range(2):
                qm = jnp.where(lane_lo if hh == 0 else jnp.logical_not(lane_lo), ql, jnp.zeros_like(ql))
                sc = lax.dot_general(qm, kk, (((1,), (1,)), ((), ())), preferred_element_type=F32)
                sc = sc + bias
                mx = jnp.max(sc, axis=1, keepdims=True)
                pe = jnp.exp(sc - mx)
                den = jnp.sum(pe, axis=1, keepdims=True)
                o = jnp.dot(pe.astype(BF16), vv, preferred_element_type=F32) / den
                o_h.append(o)
                l_h.append(mx + jnp.log(den))
            o_pair = jnp.where(lane_lo, o_h[0], o_h[1])
            if final:
                l1 = jnp.where(lane_lo, l_h[0], l_h[1])
                l4 = jnp.where(lane_lo, s4[:, 2 * p:2 * p + 1], s4[:, 2 * p + 1:2 * p + 2])
                l16 = jnp.where(lane_lo, s16[:, 2 * p:2 * p + 1], s16[:, 2 * p + 1:2 * p + 2])
                top = jnp.maximum(l1, jnp.maximum(l4, l16))
                e1 = jnp.exp(l1 - top)
                e4 = jnp.exp(l4 - top)
                e16 = jnp.exp(l16 - top)
                o4 = o4_ref[pl.ds(r0, BAND), p * LANES:(p + 1) * LANES].astype(F32)
                o16 = o16_ref[pl.ds(r0, BAND), p * LANES:(p + 1) * LANES].astype(F32)
                y = (e1 * o_pair + e4 * o4 + e16 * o16) / (e1 + e4 + e16)
                acc[:, p * LANES:(p + 1) * LANES] = y
                ssq = ssq + jnp.sum(y * y, axis=1, keepdims=True)
            else:
                o_ref[pl.ds(r0, BAND), p * LANES:(p + 1) * LANES] = o_pair.astype(BF16)
                st = jnp.where(lane == 2 * p, l_h[0], st)
                st = jnp.where(lane == 2 * p + 1, l_h[1], st)
        if final:
            inv = lax.rsqrt(ssq * (1.0 / ATTN_W) + EPS)
            out_ref[pl.ds(r0, BAND), :] = (acc[...] * inv * g_ref[...]).astype(BF16)
        else:
            st_ref[pl.ds(r0, BAND), :] = st
        return carry

    lax.fori_loop(0, tq // BAND, block, 0)
    kbuf[0:BAND, :] = kbuf[tq:tq + BAND, :]
    vbuf[0:BAND, :] = vbuf[tq:tq + BAND, :]


def _attn_branch(q, k, v, d, extra=None):
    b, s, _ = q.shape
    m = s // d
    tq = min(ATTN_TQ, m)
    view = lambda a, w: a.reshape(b, m, d * w)
    blk = lambda w: pl.BlockSpec((None, tq, w), lambda bi, ri, ii: (bi, ii, ri))
    final = extra is not None
    ins = [view(q, ATTN_W), view(k, ATTN_W), view(v, ATTN_W)]
    in_specs = [blk(ATTN_W)] * 3
    scratch = [pltpu.VMEM((tq + BAND, ATTN_W), BF16), pltpu.VMEM((tq + BAND, ATTN_W), BF16)]
    if final:
        o4, s4, o16, s16, gain = extra
        ins += [o4, s4, o16, s16, gain]
        in_specs += [blk(ATTN_W), blk(LANES), blk(ATTN_W), blk(LANES),
                     pl.BlockSpec((1, ATTN_W), lambda bi, ri, ii: (0, 0))]
        out_shape = jax.ShapeDtypeStruct((b, m, d * ATTN_W), BF16)
        out_specs = blk(ATTN_W)
        scratch.append(pltpu.VMEM((BAND, ATTN_W), F32))
    else:
        out_shape = [jax.ShapeDtypeStruct((b, m, d * ATTN_W), BF16),
                     jax.ShapeDtypeStruct((b, m, d * LANES), F32)]
        out_specs = [blk(ATTN_W), blk(LANES)]
    out = pl.pallas_call(
        functools.partial(_attn_kernel, tq=tq, final=final),
        grid=(b, d, m // tq),
        in_specs=in_specs, out_specs=out_specs, out_shape=out_shape,
        scratch_shapes=scratch,
        compiler_params=_params("arbitrary", "arbitrary", "arbitrary"),
        name=f"attn_d{d}",
    )(*ins)
    if final:
        return out.reshape(b, s, ATTN_W)
    return out[0].reshape(b, s, ATTN_W), out[1].reshape(b, s, LANES)


def _gelu_tanh(x):
    c = math.sqrt(2.0 / math.pi)
    return x * (0.5 * (1.0 + jnp.tanh(c * (x + 0.044715 * (x * x * x)))))


def _split_dot(x, m_bf):
    hi = x.astype(BF16)
    lo = (x - hi.astype(F32)).astype(BF16)
    return (jnp.dot(hi, m_bf, preferred_element_type=F32)
            + jnp.dot(lo, m_bf, preferred_element_type=F32))


def _mixbc_kernel(r_ref, halo_ref, sg_ref, sw_ref, sb_ref, cw_ref, bg_ref, out_ref, *, tm):
    si = pl.program_id(1)
    u = r_ref[:, 0:SGU_W].astype(F32)
    z = r_ref[:, SGU_W:2 * SGU_W].astype(F32)
    gb = r_ref[:, 2 * SGU_W:2 * SGU_W + CONV_W].astype(F32)
    gc = r_ref[:, 2 * SGU_W + CONV_W:2 * SGU_W + 2 * CONV_W].astype(F32)
    hh = r_ref[:, 2 * SGU_W + 2 * CONV_W:].astype(F32)

    gdim = SGU_W // SGU_GROUPS
    ri = lax.broadcasted_iota(I32, (SGU_W, SGU_W), 0) // gdim
    ci = lax.broadcasted_iota(I32, (SGU_W, SGU_W), 1) // gdim
    avg = jnp.where(ri == ci, 1.0 / gdim, 0.0).astype(BF16)
    z = _gelu_tanh(z)
    zc = z - _split_dot(z, avg)
    var = _split_dot(zc * zc, avg)
    zn = (zc * lax.rsqrt(var + EPS) * sg_ref[...]).astype(BF16)

    tr = lax.broadcasted_iota(I32, (SGU_CHUNK, SGU_CHUNK), 0)
    tc = lax.broadcasted_iota(I32, (SGU_CHUNK, SGU_CHUNK), 1)
    w_cat = jnp.concatenate(
        [jnp.where(tc <= tr, sw_ref[g], 0.0).astype(BF16) for g in range(SGU_GROUPS)], axis=1)
    lane_g = lax.broadcasted_iota(I32, (SGU_CHUNK, SGU_W), 1) // gdim
    gu = _gelu_tanh(u)
    bias = sb_ref[...]
    yb = []
    for c in range(tm // SGU_CHUNK):
        zch = zn[c * SGU_CHUNK:(c + 1) * SGU_CHUNK, :]
        stack = jnp.concatenate(
            [jnp.where(lane_g == g, zch, jnp.zeros_like(zch)) for g in range(SGU_GROUPS)], axis=0)
        sp = jnp.dot(w_cat, stack, preferred_element_type=F32) + bias
        yb.append(gu[c * SGU_CHUNK:(c + 1) * SGU_CHUNK, :] * sp)
    yb = jnp.concatenate(yb, axis=0)

    zz = gc * hh
    hrows = halo_ref.shape[0]
    prev = (halo_ref[:, 2 * SGU_W + CONV_W:2 * SGU_W + 2 * CONV_W].astype(F32)
            * halo_ref[:, 2 * SGU_W + 2 * CONV_W:].astype(F32))
    prev = prev * (si > 0).astype(F32)
    ext = jnp.concatenate([prev, zz], axis=0)
    z1 = ext[hrows - 1:hrows - 1 + tm, :]
    z2 = ext[hrows - 2:hrows - 2 + tm, :]
    yc = gb * (cw_ref[0:1, :] * z2 + cw_ref[1:2, :] * z1 + cw_ref[2:3, :] * zz)

    def rms(t, g):
        return t * lax.rsqrt(jnp.mean(t * t, axis=1, keepdims=True) + EPS) * g

    out_ref[:, 0:SGU_W] = rms(yb, bg_ref[:, 0:SGU_W]).astype(BF16)
    out_ref[:, SGU_W:] = rms(yc, bg_ref[:, SGU_W:]).astype(BF16)


def _mixbc(rest, sgu_gain, sgu_w, sgu_bias_tile, conv_w, gain_bc):
    b, s, _ = rest.shape
    tm = MIX_TM
    hrows = 16
    full = lambda shape: pl.BlockSpec(shape, lambda bi, si: (0,) * len(shape))
    return pl.pallas_call(
        functools.partial(_mixbc_kernel, tm=tm),
        grid=(b, s // tm),
        in_specs=[pl.BlockSpec((None, tm, REST_W), lambda bi, si: (bi, si, 0)),
                  pl.BlockSpec((None, hrows, REST_W),
                               lambda bi, si: (bi, jnp.maximum(si * (tm // hrows) - 1, 0), 0)),
                  full((1, SGU_W)), full((SGU_GROUPS, SGU_CHUNK, SGU_CHUNK)),
                  full((SGU_CHUNK, SGU_W)), full((3, CONV_W)), full((1, SGU_W + CONV_W))],
        out_specs=pl.BlockSpec((None, tm, SGU_W + CONV_W), lambda bi, si: (bi, si, 0)),
        out_shape=jax.ShapeDtypeStruct((b, s, SGU_W + CONV_W), BF16),
        compiler_params=_params("arbitrary", "arbitrary"),
        name="mixbc",
    )(rest, rest, sgu_gain, sgu_w, sgu_bias_tile, conv_w, gain_bc)


def _layer_norm(t, g, b):
    mu = jnp.mean(t, axis=1, keepdims=True)
    tc = t - mu
    var = jnp.mean(tc * tc, axis=1, keepdims=True)
    return tc * lax.rsqrt(var + EPS) * g + b


def _outproj_kernel(ma_ref, mb_ref, x_ref, wa_ref, wb_ref, lg_ref, lb_ref, rwh_ref, rwl_ref, rb_ref,
                    x1_ref, xp_ref, info_ref, cnt_ref, run_ref, *, tm, alpha):
    i = pl.program_id(0)

    @pl.when(i == 0)
    def _():
        run_ref[...] = jnp.zeros_like(run_ref)

    y = (jnp.dot(ma_ref[...], wa_ref[...], preferred_element_type=F32)
         + jnp.dot(mb_ref[...], wb_ref[...], preferred_element_type=F32))
    x1 = _layer_norm(alpha * x_ref[...] + y, lg_ref[...], lb_ref[...])
    x1_ref[...] = x1
    xp_ref[...] = _pack_pairs(x1)

    hi = x1.astype(BF16)
    lo = (x1 - hi.astype(F32)).astype(BF16)
    logit = (jnp.dot(hi, rwh_ref[...], preferred_element_type=F32)
             + jnp.dot(lo, rwh_ref[...], preferred_element_type=F32)
             + jnp.dot(hi, rwl_ref[...], preferred_element_type=F32)) + rb_ref[...]
    lane = lax.broadcasted_iota(I32, (tm, LANES), 1)
    lane_f = lane.astype(F32)

    def top(mask):
        v = jnp.max(jnp.where(mask, logit, NEG), axis=1, keepdims=True)
        first = jnp.min(jnp.where(jnp.logical_and(mask, logit == v), lane_f, float(LANES)),
                        axis=1, keepdims=True)
        return v, first.astype(I32)

    is_g = lane < MOE_GROUPS
    gmax, gidx = top(is_g)
    g_p = 1.0 / jnp.sum(jnp.where(is_g, jnp.exp(logit - gmax), 0.0), axis=1, keepdims=True)
    in_grp = jnp.logical_and(lane >= MOE_GROUPS + gidx * EXPERTS_PER_GROUP,
                             lane < MOE_GROUPS + (gidx + 1) * EXPERTS_PER_GROUP)
    v1, i1 = top(in_grp)
    v2, i2 = top(jnp.logical_and(in_grp, lane != i1))
    e21 = jnp.exp(v2 - v1)
    gate1 = g_p / (1.0 + e21)
    gate2 = g_p * e21 / (1.0 + e21)
    ex1 = i1 - MOE_GROUPS
    ex2 = i2 - MOE_GROUPS

    oh1 = lane == ex1
    oh2 = lane == ex2
    oh = (oh1.astype(F32) + oh2.astype(F32))
    tr = lax.broadcasted_iota(I32, (tm, tm), 0)
    tc = lax.broadcasted_iota(I32, (tm, tm), 1)
    lower = jnp.where(tc < tr, 1.0, 0.0).astype(BF16)
    before = jnp.dot(lower, oh.astype(BF16), preferred_element_type=F32) + run_ref[0:1, :]
    rank1 = jnp.sum(jnp.where(oh1, before, 0.0), axis=1, keepdims=True)
    rank2 = jnp.sum(jnp.where(oh2, before, 0.0), axis=1, keepdims=True)
    run_new = run_ref[0:1, :] + jnp.sum(oh, axis=0, keepdims=True)
    run_ref[...] = jnp.broadcast_to(run_new, run_ref.shape)
    cnt_ref[...] = jnp.broadcast_to(run_new, cnt_ref.shape)

    info = jnp.where(lane == 0, ex1.astype(F32), 0.0)
    info = jnp.where(lane == 1, ex2.astype(F32), info)
    info = jnp.where(lane == 2, gate1, info)
    info = jnp.where(lane == 3, gate2, info)
    info = jnp.where(lane == 4, rank1, info)
    info = jnp.where(lane == 5, rank2, info)
    info_ref[...] = info


def _outproj(ma, mbc, x, wo_a, wo_b, ln_g, ln_b, rw_hi, rw_lo, rbias, alpha):
    n = x.shape[0]
    tm = OUT_TM
    tok = lambda w: pl.BlockSpec((tm, w), lambda i: (i, 0))
    full = lambda shape: pl.BlockSpec(shape, lambda i: (0,) * len(shape))
    return pl.pallas_call(
        functools.partial(_outproj_kernel, tm=tm, alpha=alpha),
        grid=(n // tm,),
        in_specs=[tok(ATTN_W), tok(SGU_W + CONV_W), tok(D_MODEL),
                  full((ATTN_W, D_MODEL)), full((SGU_W + CONV_W, D_MODEL)),
                  full((1, D_MODEL)), full((1, D_MODEL)),
                  full((D_MODEL, LANES)), full((D_MODEL, LANES)), full((1, LANES))],
        out_specs=[tok(D_MODEL), tok(D_MODEL // 2), tok(LANES), full((8, LANES))],
        out_shape=[jax.ShapeDtypeStruct((n, D_MODEL), F32),
                   jax.ShapeDtypeStruct((n, D_MODEL // 2), U32),
                   jax.ShapeDtypeStruct((n, LANES), F32),
                   jax.ShapeDtypeStruct((8, LANES), F32)],
        scratch_shapes=[pltpu.VMEM((8, LANES), F32)],
        compiler_params=_params("arbitrary"),
        name="outproj",
    )(ma, mbc, x, wo_a, wo_b, ln_g, ln_b, rw_hi, rw_lo, rbias)


def _sc_mesh():
    return plsc.VectorSubcoreMesh(core_axis_name="c", subcore_axis_name="s",
                                  num_cores=SC_CORES, num_subcores=SC_SUBCORES)


def _sc_chunk(rows_per_worker):
    return min(SC_CHUNK, rows_per_worker // 2)


def _sc_dispatch(xp, dest_kn, rows):
    n, w = xp.shape
    t_per_w = n // SC_WORKERS
    chunk = _sc_chunk(t_per_w)
    nchunk = t_per_w // chunk

    def body(src_hbm, dest_hbm, out_hbm, idx_v, rows_v, lsem, ssem):
        wid = lax.axis_index("s") * SC_CORES + lax.axis_index("c")
        base = wid * t_per_w
        pltpu.sync_copy(dest_hbm.at[0, wid], idx_v.at[0])
        pltpu.sync_copy(dest_hbm.at[1, wid], idx_v.at[1])

        def load(c, slot):
            return pltpu.make_async_copy(src_hbm.at[pl.ds(base + c * chunk, chunk)], rows_v.at[slot],
                                         lsem.at[slot])

        def put(c, slot, kk):
            return pltpu.make_async_copy(rows_v.at[slot], out_hbm.at[idx_v.at[kk, c]], ssem.at[slot])

        load(0, 0).start()

        @pl.loop(0, nchunk, step=2)
        def _(c):
            for b in range(2):
                cc = c + b
                load(cc, b).wait()

                @pl.when(cc + 1 < nchunk)
                def _():
                    @pl.when(cc >= 1)
                    def _():
                        put(cc - 1, 1 - b, 0).wait()
                        put(cc - 1, 1 - b, 1).wait()
                    load(cc + 1, 1 - b).start()

                put(cc, b, 0).start()
                put(cc, b, 1).start()

        for b in range(2):
            put(nchunk - 2 + b, b, 0).wait()
            put(nchunk - 2 + b, b, 1).wait()

    call = pl.kernel(
        body, mesh=_sc_mesh(),
        out_type=jax.ShapeDtypeStruct((rows, w), U32),
        scratch_types=[pltpu.VMEM((2, nchunk, chunk), I32), pltpu.VMEM((2, chunk, w), U32),
                       pltpu.SemaphoreType.DMA((2,)), pltpu.SemaphoreType.DMA((2,))],
        name="sc_dispatch")
    return call(xp, dest_kn.reshape(2, SC_WORKERS, nchunk, chunk))


def _sc_gather(table, idx):
    b = idx.shape[0]
    w = table.shape[1]
    b_per_w = b // SC_WORKERS
    chunk = _sc_chunk(b_per_w)
    nchunk = b_per_w // chunk

    def body(table_hbm, idx_hbm, out_hbm, idx_v, rows_v, gsem, osem):
        wid = lax.axis_index("s") * SC_CORES + lax.axis_index("c")
        base = wid * b_per_w
        pltpu.sync_copy(idx_hbm.at[wid], idx_v)

        def gather(c, slot):
            return pltpu.make_async_copy(table_hbm.at[idx_v.at[c]], rows_v.at[slot], gsem.at[slot])

        def put(c, slot):
            return pltpu.make_async_copy(rows_v.at[slot], out_hbm.at[pl.ds(base + c * chunk, chunk)],
                                         osem.at[slot])

        gather(0, 0).start()

        @pl.loop(0, nchunk, step=2)
        def _(c):
            for b in range(2):
                cc = c + b
                gather(cc, b).wait()

                @pl.when(cc + 1 < nchunk)
                def _():
                    @pl.when(cc >= 1)
                    def _():
                        put(cc - 1, 1 - b).wait()
                    gather(cc + 1, 1 - b).start()

                put(cc, b).start()

        put(nchunk - 2, 0).wait()
        put(nchunk - 1, 1).wait()

    call = pl.kernel(
        body, mesh=_sc_mesh(),
        out_type=jax.ShapeDtypeStruct((b, w), table.dtype),
        scratch_types=[pltpu.VMEM((nchunk, chunk), I32), pltpu.VMEM((2, chunk, w), table.dtype),
                       pltpu.SemaphoreType.DMA((2,)), pltpu.SemaphoreType.DMA((2,))],
        name="sc_gather")
    return call(table, idx.reshape(SC_WORKERS, nchunk, chunk))


def _expert_kernel(be_ref, nv_ref, nu_ref, xs_ref, wg_ref, wu_ref, wd_ref, y_ref):
    i = pl.program_id(0)

    @pl.when(i < nu_ref[0])
    def _():
        row = lax.broadcasted_iota(I32, xs_ref.shape, 0)
        lo, hi = _unpack_pairs(jnp.where(row < nv_ref[i], xs_ref[...], jnp.uint32(0)))
        xb = jnp.concatenate([lo, hi], axis=1).astype(BF16)
        g = jnp.dot(xb, wg_ref[...], preferred_element_type=F32)
        u = jnp.dot(xb, wu_ref[...], preferred_element_type=F32)
        hdn = (g * (1.0 / (1.0 + jnp.exp(-g))) * u).astype(BF16)
        y_ref[...] = _pack_pairs(jnp.dot(hdn, wd_ref[...], preferred_element_type=F32))

    @pl.when(i >= nu_ref[0])
    def _():
        y_ref[...] = jnp.zeros_like(y_ref)


def _experts(block_expert, block_valid, n_used, xs, wg, wu, wd):
    rows, w = xs.shape
    tb = MOE_TB
    blk = lambda i, be, nv, nu: (jnp.minimum(i, nu[0] - 1), 0)
    oblk = lambda i, be, nv, nu: (i, 0)
    wsel = lambda i, be, nv, nu: (be[jnp.minimum(i, nu[0] - 1)], 0, 0)
    return pl.pallas_call(
        _expert_kernel,
        grid_spec=pltpu.PrefetchScalarGridSpec(
            num_scalar_prefetch=3,
            grid=(rows // tb,),
            in_specs=[pl.BlockSpec((tb, w), blk),
                      pl.BlockSpec((None, D_MODEL, D_EXPERT), wsel),
                      pl.BlockSpec((None, D_MODEL, D_EXPERT), wsel),
                      pl.BlockSpec((None, D_EXPERT, D_MODEL), wsel)],
            out_specs=pl.BlockSpec((tb, w), oblk)),
        out_shape=jax.ShapeDtypeStruct((rows, w), U32),
        compiler_params=_params("arbitrary"),
        name="experts",
    )(block_expert, block_valid, n_used, xs, wg, wu, wd)


def _combine_kernel(ya_ref, yb_ref, x_ref, info_ref, lg_ref, lb_ref, out_ref, *, alpha):
    info = info_ref[...]
    g1 = info[:, 2:3]
    g2 = info[:, 3:4]
    a_lo, a_hi = _unpack_pairs(ya_ref[...])
    b_lo, b_hi = _unpack_pairs(yb_ref[...])
    half = D_MODEL // 2
    t_lo = alpha * x_ref[:, 0:half] + g1 * a_lo + g2 * b_lo
    t_hi = alpha * x_ref[:, half:] + g1 * a_hi + g2 * b_hi
    t = jnp.concatenate([t_lo, t_hi], axis=1)
    out_ref[...] = _layer_norm(t, lg_ref[...], lb_ref[...])


def _combine(y_tok, x1, info, ln_g, ln_b, alpha):
    n = x1.shape[0]
    tm = ROW_TM
    y2 = y_tok.reshape(2, n, D_MODEL // 2)
    tok = lambda w: pl.BlockSpec((tm, w), lambda i: (i, 0))
    slot = lambda k: pl.BlockSpec((None, tm, D_MODEL // 2), lambda i: (k, i, 0))
    full = lambda shape: pl.BlockSpec(shape, lambda i: (0,) * len(shape))
    return pl.pallas_call(
        functools.partial(_combine_kernel, alpha=alpha),
        grid=(n // tm,),
        in_specs=[slot(0), slot(1), tok(D_MODEL), tok(LANES), full((1, D_MODEL)), full((1, D_MODEL))],
        out_specs=tok(D_MODEL),
        out_shape=jax.ShapeDtypeStruct((n, D_MODEL), F32),
        compiler_params=_params("arbitrary"),
        name="combine",
    )(y2, y2, x1, info, ln_g, ln_b)


def _rope_tables(s):
    half = HEAD_DIM // 2
    inv_freq = ROPE_THETA ** (-jnp.arange(half, dtype=F32) / half)
    ang = jnp.arange(s, dtype=F32)[:, None] * inv_freq[None, :]
    cos = jnp.cos(ang)
    sin = jnp.sin(ang)
    cos_t = jnp.tile(cos, (1, LANES // half))
    sin_t = jnp.tile(jnp.concatenate([-sin, sin], axis=1), (1, LANES // HEAD_DIM))
    return cos_t, sin_t


def _forward(x, w_in, w_out, branch_gain, sgu_gain, sgu_w, sgu_b, conv_w, ln_gain, ln_bias,
             router_group_w, router_group_b, router_expert_w, router_expert_b,
             expert_w_gate, expert_w_up, expert_w_down):
    b, s, _ = x.shape
    depth = w_in.shape[0]
    n = b * s
    alpha = (2.0 * depth) ** 0.25
    cos_t, sin_t = _rope_tables(s)
    tb = MOE_TB
    n_blocks = (2 * n + N_EXPERTS * (tb - 1) + tb - 1) // tb
    rows = n_blocks * tb

    for l in range(depth):
        q, k, v, rest = _proj(x, w_in[l].astype(BF16), cos_t, sin_t)
        o16, s16 = _attn_branch(q, k, v, 16)
        o4, s4 = _attn_branch(q, k, v, 4)
        g = branch_gain[l]
        ma = _attn_branch(q, k, v, 1, extra=(o4, s4, o16, s16, g[None, :ATTN_W]))
        bias_tile = jnp.repeat(sgu_b[l].T, SGU_W // SGU_GROUPS, axis=1)
        mbc = _mixbc(rest, sgu_gain[l][None, :], sgu_w[l], bias_tile, conv_w[l], g[None, ATTN_W:])

        rw = jnp.zeros((D_MODEL, LANES), F32)
        rw = rw.at[:, :MOE_GROUPS].set(router_group_w[l])
        rw = rw.at[:, MOE_GROUPS:MOE_GROUPS + N_EXPERTS].set(router_expert_w[l])
        rw_hi = rw.astype(BF16)
        rw_lo = (rw - rw_hi.astype(F32)).astype(BF16)
        rbias = jnp.zeros((1, LANES), F32)
        rbias = rbias.at[0, :MOE_GROUPS].set(router_group_b[l])
        rbias = rbias.at[0, MOE_GROUPS:MOE_GROUPS + N_EXPERTS].set(router_expert_b[l])
        wo = w_out[l].astype(BF16)
        x1, xp, info, cnt = _outproj(
            ma.reshape(n, ATTN_W), mbc.reshape(n, SGU_W + CONV_W), x.reshape(n, D_MODEL),
            wo[:ATTN_W], wo[ATTN_W:], ln_gain[l, 0][None], ln_bias[l, 0][None], rw_hi, rw_lo, rbias, alpha)

        counts = cnt[0, :N_EXPERTS].astype(I32)
        padded = (counts + tb - 1) // tb * tb
        pad_end = jnp.cumsum(padded)
        pad_start = pad_end - padded
        ex = info[:, 0:2].astype(I32)
        dest_kn = (pad_start[ex] + info[:, 4:6].astype(I32)).T
        blk_row0 = jnp.arange(n_blocks, dtype=I32) * tb
        block_expert = jnp.minimum(jnp.sum((pad_end[None, :] <= blk_row0[:, None]).astype(I32), axis=1),
                                   N_EXPERTS - 1)
        block_valid = jnp.clip(pad_start[block_expert] + counts[block_expert] - blk_row0, 0, tb)
        n_used = (pad_end[-1:] // tb).astype(I32)

        xs = _sc_dispatch(xp, dest_kn, rows)
        ys = _experts(block_expert, block_valid, n_used, xs, expert_w_gate[l].astype(BF16),
                      expert_w_up[l].astype(BF16), expert_w_down[l].astype(BF16))
        y_tok = _sc_gather(ys, dest_kn.reshape(2 * n))
        x = _combine(y_tok, x1, info, ln_gain[l, 1][None], ln_bias[l, 1][None], alpha).reshape(b, s, D_MODEL)
    return x


def kernel(x, w_in, w_out, branch_gain, sgu_gain, sgu_w, sgu_b, conv_w, ln_gain, ln_bias, router_group_w, router_group_b, router_expert_w, router_expert_b, expert_w_gate, expert_w_up, expert_w_down):
    return _forward(x, w_in, w_out, branch_gain, sgu_gain, sgu_w, sgu_b, conv_w, ln_gain, ln_bias,
                    router_group_w, router_group_b, router_expert_w, router_expert_b,
                    expert_w_gate, expert_w_up, expert_w_down)
```

```python
import functools
import math

import jax
import jax.numpy as jnp
from jax import lax
from jax.experimental import pallas as pl
from jax.experimental.pallas import tpu as pltpu
from jax.experimental.pallas import tpu_sc as plsc

F32 = jnp.float32
BF16 = jnp.bfloat16
U32 = jnp.uint32
I32 = jnp.int32

D_MODEL = 1024
HEAD_DIM = 64
ATTN_W = 512
N_HEADS = 8
SGU_W = 256
SGU_GROUPS = 4
SGU_CHUNK = 128
CONV_W = 256
REST_W = 2 * SGU_W + 3 * CONV_W
PROJ_W = 3 * ATTN_W + REST_W
DILATIONS = (16, 4, 1)
BAND = 128
ROPE_THETA = 10000.0
MOE_GROUPS = 4
EXPERTS_PER_GROUP = 8
N_EXPERTS = MOE_GROUPS * EXPERTS_PER_GROUP
D_EXPERT = 512
EPS = 1e-5
NEG = -1e30

LANES = 128
VMEM_LIMIT = 56 * 1024 * 1024

TILE = 512
PERM_D = 16
ATTN_TQ = 512
MIX_TM = 512
OUT_TM = 512
MOE_TB = 512
ROW_TM = 512
SC_CORES = 2
SC_SUBCORES = 16
SC_WORKERS = SC_CORES * SC_SUBCORES
SC_CHUNK = 64


def _params(*sem):
    return pltpu.CompilerParams(dimension_semantics=sem, vmem_limit_bytes=VMEM_LIMIT)


def _pack_pairs(x):
    w = x.shape[1] // 2
    lo = lax.bitcast_convert_type(x[:, :w].astype(BF16).astype(F32), U32)
    hi = lax.bitcast_convert_type(x[:, w:].astype(BF16).astype(F32), U32)
    return (lo >> 16) | (hi & jnp.uint32(0xFFFF0000))


def _unpack_pairs(p):
    lo = lax.bitcast_convert_type(p << 16, F32)
    hi = lax.bitcast_convert_type(p & jnp.uint32(0xFFFF0000), F32)
    return lo, hi


def _tile_perm():
    i = jnp.arange(TILE, dtype=I32)
    tok = PERM_D * (i % (TILE // PERM_D)) + i // (TILE // PERM_D)
    p = (jnp.arange(TILE, dtype=I32)[None, :] == tok[:, None]).astype(BF16)
    return p, p.T


def _proj_kernel(x_ref, w_ref, cos_ref, sin_ref, p_ref, q_ref, k_ref, v_ref, r_ref, q16_ref, k16_ref, v16_ref):
    xb = x_ref[...].astype(BF16)
    cos = cos_ref[...]
    sin = sin_ref[...]
    lane = lax.broadcasted_iota(I32, cos.shape, 1)
    first_half = (lane % HEAD_DIM) < (HEAD_DIM // 2)
    rows = TILE // PERM_D

    def store_both(val_bf, out_ref, out16_ref):
        out_ref[...] = val_bf
        perm = jnp.dot(p_ref[...], val_bf, preferred_element_type=F32).astype(BF16)
        for r in range(PERM_D):
            out16_ref[:, r * ATTN_W:(r + 1) * ATTN_W] = perm[r * rows:(r + 1) * rows, :]

    def rope(col0, scale):
        t = jnp.dot(xb, w_ref[:, col0:col0 + ATTN_W], preferred_element_type=F32)
        out = []
        for c in range(ATTN_W // LANES):
            tc = t[:, c * LANES:(c + 1) * LANES]
            partner = jnp.where(first_half, pltpu.roll(tc, LANES - 32, 1), pltpu.roll(tc, 32, 1))
            out.append(((tc * cos + partner * sin) * scale).astype(BF16))
        return jnp.concatenate(out, axis=1)

    store_both(rope(0, HEAD_DIM ** -0.5), q_ref, q16_ref)
    store_both(rope(ATTN_W, 1.0), k_ref, k16_ref)
    vb = jnp.dot(xb, w_ref[:, 2 * ATTN_W:3 * ATTN_W], preferred_element_type=F32).astype(BF16)
    store_both(vb, v_ref, v16_ref)
    r_ref[...] = jnp.dot(xb, w_ref[:, 3 * ATTN_W:], preferred_element_type=F32).astype(BF16)


def _proj(x, w_bf, cos_t, sin_t, perm):
    b, s, _ = x.shape
    tm = TILE
    out3 = jax.ShapeDtypeStruct((b, s, ATTN_W), BF16)
    out16 = jax.ShapeDtypeStruct((b, s // PERM_D, PERM_D * ATTN_W), BF16)
    tok = lambda width: pl.BlockSpec((None, tm, width), lambda si, bi: (bi, si, 0))
    tok16 = pl.BlockSpec((None, tm // PERM_D, PERM_D * ATTN_W), lambda si, bi: (bi, si, 0))
    return pl.pallas_call(
        _proj_kernel,
        grid=(s // tm, b),
        in_specs=[tok(D_MODEL),
                  pl.BlockSpec((D_MODEL, PROJ_W), lambda si, bi: (0, 0)),
                  pl.BlockSpec((tm, LANES), lambda si, bi: (si, 0)),
                  pl.BlockSpec((tm, LANES), lambda si, bi: (si, 0)),
                  pl.BlockSpec((tm, tm), lambda si, bi: (0, 0))],
        out_specs=[tok(ATTN_W), tok(ATTN_W), tok(ATTN_W), tok(REST_W), tok16, tok16, tok16],
        out_shape=[out3, out3, out3, jax.ShapeDtypeStruct((b, s, REST_W), BF16), out16, out16, out16],
        compiler_params=_params("arbitrary", "arbitrary"),
        name="proj",
    )(x, w_bf, cos_t, sin_t, perm)


def _band_bias(permuted):
    row = lax.broadcasted_iota(I32, (BAND, 2 * BAND), 0)
    col = lax.broadcasted_iota(I32, (BAND, 2 * BAND), 1)
    key = col % BAND
    if permuted:
        row = 4 * (row % 32) + row // 32
        key = 4 * (key % 32) + key // 32
    lo_key = jnp.where(col < BAND, row, 0)
    hi_key = jnp.where(col < BAND, BAND - 1, row)
    return jnp.where(jnp.logical_and(key >= lo_key, key <= hi_key), 0.0, NEG).astype(F32), col


def _attn_heads(get_q, get_k, get_v, bias, emit):
    lane = lax.broadcasted_iota(I32, (BAND, LANES), 1)
    lane_lo = lane < HEAD_DIM
    for p in range(ATTN_W // LANES):
        ql, kk, vv = get_q(p), get_k(p), get_v(p)
        o_h, l_h = [], []
        for hh in range(2):
            qm = jnp.where(lane_lo if hh == 0 else jnp.logical_not(lane_lo), ql, jnp.zeros_like(ql))
            sc = lax.dot_general(qm, kk, (((1,), (1,)), ((), ())), preferred_element_type=F32)
            sc = sc + bias
            mx = jnp.max(sc, axis=1, keepdims=True)
            pe = jnp.exp(sc - mx)
            den = jnp.sum(pe, axis=1, keepdims=True)
            o_h.append(jnp.dot(pe.astype(BF16), vv, preferred_element_type=F32) / den)
            l_h.append(mx + jnp.log(den))
        emit(p, jnp.where(lane_lo, o_h[0], o_h[1]), l_h[0], l_h[1], lane, lane_lo)


def _attn16_kernel(q_ref, k_ref, v_ref, o_ref, st_ref, kbuf, vbuf, *, tq):
    i = pl.program_id(2)

    @pl.when(i == 0)
    def _():
        kbuf[0:BAND, :] = jnp.zeros((BAND, ATTN_W), BF16)
        vbuf[0:BAND, :] = jnp.zeros((BAND, ATTN_W), BF16)

    kbuf[BAND:BAND + tq, :] = k_ref[...]
    vbuf[BAND:BAND + tq, :] = v_ref[...]
    band_bias, col = _band_bias(False)

    def block(j, carry):
        r0 = pl.multiple_of(j * BAND, BAND)
        first_col = jnp.where(jnp.logical_or(j > 0, i > 0), 0, BAND)
        bias = jnp.where(col >= first_col, band_bias, NEG)
        st = [jnp.zeros((BAND, LANES), F32)]

        def emit(p, o_pair, l0, l1, lane, lane_lo):
            o_ref[pl.ds(r0, BAND), p * LANES:(p + 1) * LANES] = o_pair.astype(BF16)
            st[0] = jnp.where(lane == 2 * p, l0, jnp.where(lane == 2 * p + 1, l1, st[0]))

        _attn_heads(lambda p: q_ref[pl.ds(r0, BAND), p * LANES:(p + 1) * LANES],
                    lambda p: kbuf[pl.ds(r0, 2 * BAND), p * LANES:(p + 1) * LANES],
                    lambda p: vbuf[pl.ds(r0, 2 * BAND), p * LANES:(p + 1) * LANES],
                    bias, emit)
        st_ref[pl.ds(r0, BAND), :] = st[0]
        return carry

    lax.fori_loop(0, tq // BAND, block, 0)
    kbuf[0:BAND, :] = kbuf[tq:tq + BAND, :]
    vbuf[0:BAND, :] = vbuf[tq:tq + BAND, :]


def _attn4_kernel(q_ref, k_ref, v_ref, o_ref, st_ref, qbuf, kbuf, vbuf, obuf, sbuf):
    i = pl.program_id(1)
    rows = TILE // PERM_D
    nres = 4

    @pl.when(i == 0)
    def _():
        kbuf[:, 0:BAND, :] = jnp.zeros((nres, BAND, ATTN_W), BF16)
        vbuf[:, 0:BAND, :] = jnp.zeros((nres, BAND, ATTN_W), BF16)

    @pl.when(i > 0)
    def _():
        kbuf[:, 0:BAND, :] = kbuf[:, BAND:2 * BAND, :]
        vbuf[:, 0:BAND, :] = vbuf[:, BAND:2 * BAND, :]

    for r4 in range(nres):
        for g in range(PERM_D // nres):
            lanes = slice((r4 + nres * g) * ATTN_W, (r4 + nres * g + 1) * ATTN_W)
            qbuf[r4, g * rows:(g + 1) * rows, :] = q_ref[:, lanes]
            kbuf[r4, BAND + g * rows:BAND + (g + 1) * rows, :] = k_ref[:, lanes]
            vbuf[r4, BAND + g * rows:BAND + (g + 1) * rows, :] = v_ref[:, lanes]

    band_bias, col = _band_bias(True)
    first_col = jnp.where(i > 0, 0, BAND)
    bias = jnp.where(col >= first_col, band_bias, NEG)

    def block(j, carry):
        st = [jnp.zeros((BAND, LANES), F32)]

        def emit(p, o_pair, l0, l1, lane, lane_lo):
            obuf[j, :, p * LANES:(p + 1) * LANES] = o_pair.astype(BF16)
            st[0] = jnp.where(lane == 2 * p, l0, jnp.where(lane == 2 * p + 1, l1, st[0]))

        _attn_heads(lambda p: qbuf[j, :, p * LANES:(p + 1) * LANES],
                    lambda p: kbuf[j, :, p * LANES:(p + 1) * LANES],
                    lambda p: vbuf[j, :, p * LANES:(p + 1) * LANES],
                    bias, emit)
        sbuf[j] = st[0]
        return carry

    lax.fori_loop(0, nres, block, 0)

    for r4 in range(nres):
        for g in range(PERM_D // nres):
            grp = r4 + nres * g
            o_ref[:, grp * ATTN_W:(grp + 1) * ATTN_W] = obuf[r4, g * rows:(g + 1) * rows, :]
            st_ref[:, grp * LANES:(grp + 1) * LANES] = sbuf[r4, g * rows:(g + 1) * rows, :]


def _attn1_kernel(q_ref, k_ref, v_ref, o4_ref, s4_ref, o16_ref, s16_ref, pt_ref, g_ref, out_ref,
                  kbuf, vbuf, acc, o4t, s4t, o16t, s16t, *, tq):
    i = pl.program_id(1)

    @pl.when(i == 0)
    def _():
        kbuf[0:BAND, :] = jnp.zeros((BAND, ATTN_W), BF16)
        vbuf[0:BAND, :] = jnp.zeros((BAND, ATTN_W), BF16)

    kbuf[BAND:BAND + tq, :] = k_ref[...]
    vbuf[BAND:BAND + tq, :] = v_ref[...]

    def to_token_order(o_ref, s_ref, ot, st):
        operm = jnp.concatenate([o_ref[:, r * ATTN_W:(r + 1) * ATTN_W] for r in range(PERM_D)], axis=0)
        ot[...] = jnp.dot(pt_ref[...], operm, preferred_element_type=F32).astype(BF16)
        sperm = jnp.concatenate([s_ref[:, r * LANES:(r + 1) * LANES] for r in range(PERM_D)], axis=0)
        tot = jnp.zeros((tq, LANES), F32)
        for _ in range(3):
            part = sperm.astype(BF16)
            tot = tot + jnp.dot(pt_ref[...], part, preferred_element_type=F32)
            sperm = sperm - part.astype(F32)
        st[...] = tot

    to_token_order(o4_ref, s4_ref, o4t, s4t)
    to_token_order(o16_ref, s16_ref, o16t, s16t)
    band_bias, col = _band_bias(False)

    def block(j, carry):
        r0 = pl.multiple_of(j * BAND, BAND)
        first_col = jnp.where(jnp.logical_or(j > 0, i > 0), 0, BAND)
        bias = jnp.where(col >= first_col, band_bias, NEG)
        s4 = s4t[pl.ds(r0, BAND), :]
        s16 = s16t[pl.ds(r0, BAND), :]
        ssq = [jnp.zeros((BAND, 1), F32)]

        def emit(p, o_pair, l0, l1, lane, lane_lo):
            l1_ = jnp.where(lane_lo, l0, l1)
            l4 = jnp.where(lane_lo, s4[:, 2 * p:2 * p + 1], s4[:, 2 * p + 1:2 * p + 2])
            l16 = jnp.where(lane_lo, s16[:, 2 * p:2 * p + 1], s16[:, 2 * p + 1:2 * p + 2])
            top = jnp.maximum(l1_, jnp.maximum(l4, l16))
            e1 = jnp.exp(l1_ - top)
            e4 = jnp.exp(l4 - top)
            e16 = jnp.exp(l16 - top)
            o4 = o4t[pl.ds(r0, BAND), p * LANES:(p + 1) * LANES].astype(F32)
            o16 = o16t[pl.ds(r0, BAND), p * LANES:(p + 1) * LANES].astype(F32)
            y = (e1 * o_pair + e4 * o4 + e16 * o16) / (e1 + e4 + e16)
            acc[:, p * LANES:(p + 1) * LANES] = y
            ssq[0] = ssq[0] + jnp.sum(y * y, axis=1, keepdims=True)

        _attn_heads(lambda p: q_ref[pl.ds(r0, BAND), p * LANES:(p + 1) * LANES],
                    lambda p: kbuf[pl.ds(r0, 2 * BAND), p * LANES:(p + 1) * LANES],
                    lambda p: vbuf[pl.ds(r0, 2 * BAND), p * LANES:(p + 1) * LANES],
                    bias, emit)
        inv = lax.rsqrt(ssq[0] * (1.0 / ATTN_W) + EPS)
        out_ref[pl.ds(r0, BAND), :] = (acc[...] * inv * g_ref[...]).astype(BF16)
        return carry

    lax.fori_loop(0, tq // BAND, block, 0)
    kbuf[0:BAND, :] = kbuf[tq:tq + BAND, :]
    vbuf[0:BAND, :] = vbuf[tq:tq + BAND, :]


def _attention(q, k, v, q16, k16, v16, perm_t, gain):
    b, s, _ = q.shape
    m16 = s // PERM_D
    rows = TILE // PERM_D
    o16_shape = [jax.ShapeDtypeStruct((b, m16, PERM_D * ATTN_W), BF16),
                 jax.ShapeDtypeStruct((b, m16, PERM_D * LANES), F32)]

    tq16 = min(ATTN_TQ, m16)
    blk16 = lambda w: pl.BlockSpec((None, tq16, w), lambda bi, ri, ii: (bi, ii, ri))
    o16, s16 = pl.pallas_call(
        functools.partial(_attn16_kernel, tq=tq16),
        grid=(b, PERM_D, m16 // tq16),
        in_specs=[blk16(ATTN_W)] * 3, out_specs=[blk16(ATTN_W), blk16(LANES)], out_shape=o16_shape,
        scratch_shapes=[pltpu.VMEM((tq16 + BAND, ATTN_W), BF16)] * 2,
        compiler_params=_params("arbitrary", "arbitrary", "arbitrary"),
        name="attn_d16",
    )(q16, k16, v16)

    tile = lambda w: pl.BlockSpec((None, rows, PERM_D * w), lambda bi, ii: (bi, ii, 0))
    o4, s4 = pl.pallas_call(
        _attn4_kernel,
        grid=(b, s // TILE),
        in_specs=[tile(ATTN_W)] * 3, out_specs=[tile(ATTN_W), tile(LANES)], out_shape=o16_shape,
        scratch_shapes=[pltpu.VMEM((4, BAND, ATTN_W), BF16), pltpu.VMEM((4, 2 * BAND, ATTN_W), BF16),
                        pltpu.VMEM((4, 2 * BAND, ATTN_W), BF16), pltpu.VMEM((4, BAND, ATTN_W), BF16),
                        pltpu.VMEM((4, BAND, LANES), F32)],
        compiler_params=_params("arbitrary", "arbitrary"),
        name="attn_d4",
    )(q16, k16, v16)

    tq = TILE
    tok = lambda w: pl.BlockSpec((None, tq, w), lambda bi, ii: (bi, ii, 0))
    full = lambda shape: pl.BlockSpec(shape, lambda bi, ii: (0,) * len(shape))
    return pl.pallas_call(
        functools.partial(_attn1_kernel, tq=tq),
        grid=(b, s // tq),
        in_specs=[tok(ATTN_W)] * 3 + [tile(ATTN_W), tile(LANES), tile(ATTN_W), tile(LANES),
                                      full((TILE, TILE)), full((1, ATTN_W))],
        out_specs=tok(ATTN_W),
        out_shape=jax.ShapeDtypeStruct((b, s, ATTN_W), BF16),
        scratch_shapes=[pltpu.VMEM((tq + BAND, ATTN_W), BF16), pltpu.VMEM((tq + BAND, ATTN_W), BF16),
                        pltpu.VMEM((BAND, ATTN_W), F32),
                        pltpu.VMEM((tq, ATTN_W), BF16), pltpu.VMEM((tq, LANES), F32),
                        pltpu.VMEM((tq, ATTN_W), BF16), pltpu.VMEM((tq, LANES), F32)],
        compiler_params=_params("arbitrary", "arbitrary"),
        name="attn_d1",
    )(q, k, v, o4, s4, o16, s16, perm_t, gain)


def _gelu_tanh(x):
    c = math.sqrt(2.0 / math.pi)
    return x * (0.5 * (1.0 + jnp.tanh(c * (x + 0.044715 * (x * x * x)))))


def _split_dot(x, m_bf):
    hi = x.astype(BF16)
    lo = (x - hi.astype(F32)).astype(BF16)
    return (jnp.dot(hi, m_bf, preferred_element_type=F32)
            + jnp.dot(lo, m_bf, preferred_element_type=F32))


def _mixbc_kernel(r_ref, halo_ref, sg_ref, sw_ref, sb_ref, cw_ref, bg_ref, out_ref, *, tm):
    si = pl.program_id(1)
    u = r_ref[:, 0:SGU_W].astype(F32)
    z = r_ref[:, SGU_W:2 * SGU_W].astype(F32)
    gb = r_ref[:, 2 * SGU_W:2 * SGU_W + CONV_W].astype(F32)
    gc = r_ref[:, 2 * SGU_W + CONV_W:2 * SGU_W + 2 * CONV_W].astype(F32)
    hh = r_ref[:, 2 * SGU_W + 2 * CONV_W:].astype(F32)

    gdim = SGU_W // SGU_GROUPS
    ri = lax.broadcasted_iota(I32, (SGU_W, SGU_W), 0) // gdim
    ci = lax.broadcasted_iota(I32, (SGU_W, SGU_W), 1) // gdim
    avg = jnp.where(ri == ci, 1.0 / gdim, 0.0).astype(BF16)
    z = _gelu_tanh(z)
    zc = z - _split_dot(z, avg)
    var = _split_dot(zc * zc, avg)
    zn = (zc * lax.rsqrt(var + EPS) * sg_ref[...]).astype(BF16)

    tr = lax.broadcasted_iota(I32, (SGU_CHUNK, SGU_CHUNK), 0)
    tc = lax.broadcasted_iota(I32, (SGU_CHUNK, SGU_CHUNK), 1)
    w_cat = jnp.concatenate(
        [jnp.where(tc <= tr, sw_ref[g], 0.0).astype(BF16) for g in range(SGU_GROUPS)], axis=1)
    lane_g = lax.broadcasted_iota(I32, (SGU_CHUNK, SGU_W), 1) // gdim
    gu = _gelu_tanh(u)
    bias = sb_ref[...]
    yb = []
    for c in range(tm // SGU_CHUNK):
        zch = zn[c * SGU_CHUNK:(c + 1) * SGU_CHUNK, :]
        stack = jnp.concatenate(
            [jnp.where(lane_g == g, zch, jnp.zeros_like(zch)) for g in range(SGU_GROUPS)], axis=0)
        sp = jnp.dot(w_cat, stack, preferred_element_type=F32) + bias
        yb.append(gu[c * SGU_CHUNK:(c + 1) * SGU_CHUNK, :] * sp)
    yb = jnp.concatenate(yb, axis=0)

    zz = gc * hh
    hrows = halo_ref.shape[0]
    prev = (halo_ref[:, 2 * SGU_W + CONV_W:2 * SGU_W + 2 * CONV_W].astype(F32)
            * halo_ref[:, 2 * SGU_W + 2 * CONV_W:].astype(F32))
    prev = prev * (si > 0).astype(F32)
    ext = jnp.concatenate([prev, zz], axis=0)
    z1 = ext[hrows - 1:hrows - 1 + tm, :]
    z2 = ext[hrows - 2:hrows - 2 + tm, :]
    yc = gb * (cw_ref[0:1, :] * z2 + cw_ref[1:2, :] * z1 + cw_ref[2:3, :] * zz)

    def rms(t, g):
        return t * lax.rsqrt(jnp.mean(t * t, axis=1, keepdims=True) + EPS) * g

    out_ref[:, 0:SGU_W] = rms(yb, bg_ref[:, 0:SGU_W]).astype(BF16)
    out_ref[:, SGU_W:] = rms(yc, bg_ref[:, SGU_W:]).astype(BF16)


def _mixbc(rest, sgu_gain, sgu_w, sgu_bias_tile, conv_w, gain_bc):
    b, s, _ = rest.shape
    tm = MIX_TM
    hrows = 16
    full = lambda shape: pl.BlockSpec(shape, lambda bi, si: (0,) * len(shape))
    return pl.pallas_call(
        functools.partial(_mixbc_kernel, tm=tm),
        grid=(b, s // tm),
        in_specs=[pl.BlockSpec((None, tm, REST_W), lambda bi, si: (bi, si, 0)),
                  pl.BlockSpec((None, hrows, REST_W),
                               lambda bi, si: (bi, jnp.maximum(si * (tm // hrows) - 1, 0), 0)),
                  full((1, SGU_W)), full((SGU_GROUPS, SGU_CHUNK, SGU_CHUNK)),
                  full((SGU_CHUNK, SGU_W)), full((3, CONV_W)), full((1, SGU_W + CONV_W))],
        out_specs=pl.BlockSpec((None, tm, SGU_W + CONV_W), lambda bi, si: (bi, si, 0)),
        out_shape=jax.ShapeDtypeStruct((b, s, SGU_W + CONV_W), BF16),
        compiler_params=_params("arbitrary", "arbitrary"),
        name="mixbc",
    )(rest, rest, sgu_gain, sgu_w, sgu_bias_tile, conv_w, gain_bc)


def _layer_norm(t, g, b):
    mu = jnp.mean(t, axis=1, keepdims=True)
    tc = t - mu
    var = jnp.mean(tc * tc, axis=1, keepdims=True)
    return tc * lax.rsqrt(var + EPS) * g + b


def _outproj_kernel(ma_ref, mb_ref, x_ref, wa_ref, wb_ref, lg_ref, lb_ref, rwh_ref, rwl_ref, rb_ref,
                    x1_ref, xp_ref, info_ref, infot_ref, cnt_ref, run_ref, *, tm, alpha):
    i = pl.program_id(0)

    @pl.when(i == 0)
    def _():
        run_ref[...] = jnp.zeros_like(run_ref)

    y = (jnp.dot(ma_ref[...], wa_ref[...], preferred_element_type=F32)
         + jnp.dot(mb_ref[...], wb_ref[...], preferred_element_type=F32))
    x1 = _layer_norm(alpha * x_ref[...] + y, lg_ref[...], lb_ref[...])
    x1_ref[...] = x1
    xp_ref[...] = _pack_pairs(x1)

    hi = x1.astype(BF16)
    lo = (x1 - hi.astype(F32)).astype(BF16)
    logit = (jnp.dot(hi, rwh_ref[...], preferred_element_type=F32)
             + jnp.dot(lo, rwh_ref[...], preferred_element_type=F32)
             + jnp.dot(hi, rwl_ref[...], preferred_element_type=F32)) + rb_ref[...]
    lane = lax.broadcasted_iota(I32, (tm, LANES), 1)
    lane_f = lane.astype(F32)

    def top(mask):
        v = jnp.max(jnp.where(mask, logit, NEG), axis=1, keepdims=True)
        first = jnp.min(jnp.where(jnp.logical_and(mask, logit == v), lane_f, float(LANES)),
                        axis=1, keepdims=True)
        return v, first.astype(I32)

    is_g = lane < MOE_GROUPS
    gmax, gidx = top(is_g)
    g_p = 1.0 / jnp.sum(jnp.where(is_g, jnp.exp(logit - gmax), 0.0), axis=1, keepdims=True)
    in_grp = jnp.logical_and(lane >= MOE_GROUPS + gidx * EXPERTS_PER_GROUP,
                             lane < MOE_GROUPS + (gidx + 1) * EXPERTS_PER_GROUP)
    v1, i1 = top(in_grp)
    v2, i2 = top(jnp.logical_and(in_grp, lane != i1))
    e21 = jnp.exp(v2 - v1)
    gate1 = g_p / (1.0 + e21)
    gate2 = g_p * e21 / (1.0 + e21)
    ex1 = i1 - MOE_GROUPS
    ex2 = i2 - MOE_GROUPS

    oh1 = lane == ex1
    oh2 = lane == ex2
    oh = (oh1.astype(F32) + oh2.astype(F32))
    tr = lax.broadcasted_iota(I32, (tm, tm), 0)
    tc = lax.broadcasted_iota(I32, (tm, tm), 1)
    lower = jnp.where(tc < tr, 1.0, 0.0).astype(BF16)
    before = jnp.dot(lower, oh.astype(BF16), preferred_element_type=F32) + run_ref[0:1, :]
    rank1 = jnp.sum(jnp.where(oh1, before, 0.0), axis=1, keepdims=True)
    rank2 = jnp.sum(jnp.where(oh2, before, 0.0), axis=1, keepdims=True)
    run_new = run_ref[0:1, :] + jnp.sum(oh, axis=0, keepdims=True)
    run_ref[...] = jnp.broadcast_to(run_new, run_ref.shape)
    cnt_ref[...] = jnp.broadcast_to(run_new, cnt_ref.shape)

    info = jnp.where(lane == 0, ex1.astype(F32), 0.0)
    info = jnp.where(lane == 1, ex2.astype(F32), info)
    info = jnp.where(lane == 2, gate1, info)
    info = jnp.where(lane == 3, gate2, info)
    info = jnp.where(lane == 4, rank1, info)
    info = jnp.where(lane == 5, rank2, info)
    info_ref[...] = info
    infot_ref[...] = info.T[0:8, :]


def _outproj(ma, mbc, x, wo_a, wo_b, ln_g, ln_b, rw_hi, rw_lo, rbias, alpha):
    n = x.shape[0]
    tm = OUT_TM
    tok = lambda w: pl.BlockSpec((tm, w), lambda i: (i, 0))
    full = lambda shape: pl.BlockSpec(shape, lambda i: (0,) * len(shape))
    return pl.pallas_call(
        functools.partial(_outproj_kernel, tm=tm, alpha=alpha),
        grid=(n // tm,),
        in_specs=[tok(ATTN_W), tok(SGU_W + CONV_W), tok(D_MODEL),
                  full((ATTN_W, D_MODEL)), full((SGU_W + CONV_W, D_MODEL)),
                  full((1, D_MODEL)), full((1, D_MODEL)),
                  full((D_MODEL, LANES)), full((D_MODEL, LANES)), full((1, LANES))],
        out_specs=[tok(D_MODEL), tok(D_MODEL // 2), tok(LANES), pl.BlockSpec((8, tm), lambda i: (0, i)),
                   full((8, LANES))],
        out_shape=[jax.ShapeDtypeStruct((n, D_MODEL), F32),
                   jax.ShapeDtypeStruct((n, D_MODEL // 2), U32),
                   jax.ShapeDtypeStruct((n, LANES), F32),
                   jax.ShapeDtypeStruct((8, n), F32),
                   jax.ShapeDtypeStruct((8, LANES), F32)],
        scratch_shapes=[pltpu.VMEM((8, LANES), F32)],
        compiler_params=_params("arbitrary"),
        name="outproj",
    )(ma, mbc, x, wo_a, wo_b, ln_g, ln_b, rw_hi, rw_lo, rbias)


def _sc_mesh():
    return plsc.VectorSubcoreMesh(core_axis_name="c", subcore_axis_name="s",
                                  num_cores=SC_CORES, num_subcores=SC_SUBCORES)


def _sc_chunk(rows_per_worker):
    return min(SC_CHUNK, rows_per_worker // 2)


def _sc_dispatch(xp, dest_kn, rows):
    n, w = xp.shape
    t_per_w = n // SC_WORKERS
    chunk = _sc_chunk(t_per_w)
    nchunk = t_per_w // chunk

    def body(src_hbm, dest_hbm, out_hbm, idx_v, rows_v, lsem, ssem):
        wid = lax.axis_index("s") * SC_CORES + lax.axis_index("c")
        base = wid * t_per_w
        pltpu.sync_copy(dest_hbm.at[0, wid], idx_v.at[0])
        pltpu.sync_copy(dest_hbm.at[1, wid], idx_v.at[1])

        def load(c, slot):
            return pltpu.make_async_copy(src_hbm.at[pl.ds(base + c * chunk, chunk)], rows_v.at[slot],
                                         lsem.at[slot])

        def put(c, slot, kk):
            return pltpu.make_async_copy(rows_v.at[slot], out_hbm.at[idx_v.at[kk, c]], ssem.at[slot])

        load(0, 0).start()

        @pl.loop(0, nchunk, step=2)
        def _(c):
            for b in range(2):
                cc = c + b
                load(cc, b).wait()

                @pl.when(cc + 1 < nchunk)
                def _():
                    @pl.when(cc >= 1)
                    def _():
                        put(cc - 1, 1 - b, 0).wait()
                        put(cc - 1, 1 - b, 1).wait()
                    load(cc + 1, 1 - b).start()

                put(cc, b, 0).start()
                put(cc, b, 1).start()

        for b in range(2):
            put(nchunk - 2 + b, b, 0).wait()
            put(nchunk - 2 + b, b, 1).wait()

    call = pl.kernel(
        body, mesh=_sc_mesh(),
        out_type=jax.ShapeDtypeStruct((rows, w), U32),
        scratch_types=[pltpu.VMEM((2, nchunk, chunk), I32), pltpu.VMEM((2, chunk, w), U32),
                       pltpu.SemaphoreType.DMA((2,)), pltpu.SemaphoreType.DMA((2,))],
        name="sc_dispatch")
    return call(xp, dest_kn.reshape(2, SC_WORKERS, nchunk, chunk))


def _sc_gather(table, idx):
    b = idx.shape[0]
    w = table.shape[1]
    b_per_w = b // SC_WORKERS
    chunk = _sc_chunk(b_per_w)
    nchunk = b_per_w // chunk

    def body(table_hbm, idx_hbm, out_hbm, idx_v, rows_v, gsem, osem):
        wid = lax.axis_index("s") * SC_CORES + lax.axis_index("c")
        base = wid * b_per_w
        pltpu.sync_copy(idx_hbm.at[wid], idx_v)

        def gather(c, slot):
            return pltpu.make_async_copy(table_hbm.at[idx_v.at[c]], rows_v.at[slot], gsem.at[slot])

        def put(c, slot):
            return pltpu.make_async_copy(rows_v.at[slot], out_hbm.at[pl.ds(base + c * chunk, chunk)],
                                         osem.at[slot])

        gather(0, 0).start()

        @pl.loop(0, nchunk, step=2)
        def _(c):
            for b in range(2):
                cc = c + b
                gather(cc, b).wait()

                @pl.when(cc + 1 < nchunk)
                def _():
                    @pl.when(cc >= 1)
                    def _():
                        put(cc - 1, 1 - b).wait()
                    gather(cc + 1, 1 - b).start()

                put(cc, b).start()

        put(nchunk - 2, 0).wait()
        put(nchunk - 1, 1).wait()

    call = pl.kernel(
        body, mesh=_sc_mesh(),
        out_type=jax.ShapeDtypeStruct((b, w), table.dtype),
        scratch_types=[pltpu.VMEM((nchunk, chunk), I32), pltpu.VMEM((2, chunk, w), table.dtype),
                       pltpu.SemaphoreType.DMA((2,)), pltpu.SemaphoreType.DMA((2,))],
        name="sc_gather")
    return call(table, idx.reshape(SC_WORKERS, nchunk, chunk))


def _expert_kernel(be_ref, nv_ref, nu_ref, xs_ref, wg_ref, wu_ref, wd_ref, y_ref):
    i = pl.program_id(0)

    @pl.when(i < nu_ref[0])
    def _():
        row = lax.broadcasted_iota(I32, xs_ref.shape, 0)
        lo, hi = _unpack_pairs(jnp.where(row < nv_ref[i], xs_ref[...], jnp.uint32(0)))
        xb = jnp.concatenate([lo, hi], axis=1).astype(BF16)
        g = jnp.dot(xb, wg_ref[...].astype(BF16), preferred_element_type=F32)
        u = jnp.dot(xb, wu_ref[...].astype(BF16), preferred_element_type=F32)
        hdn = (g * (1.0 / (1.0 + jnp.exp(-g))) * u).astype(BF16)
        y_ref[...] = _pack_pairs(jnp.dot(hdn, wd_ref[...].astype(BF16), preferred_element_type=F32))

    @pl.when(i >= nu_ref[0])
    def _():
        y_ref[...] = jnp.zeros_like(y_ref)


def _experts(block_expert, block_valid, n_used, xs, wg, wu, wd):
    rows, w = xs.shape
    tb = MOE_TB
    blk = lambda i, be, nv, nu: (jnp.minimum(i, nu[0] - 1), 0)
    oblk = lambda i, be, nv, nu: (i, 0)
    wsel = lambda i, be, nv, nu: (be[jnp.minimum(i, nu[0] - 1)], 0, 0)
    return pl.pallas_call(
        _expert_kernel,
        grid_spec=pltpu.PrefetchScalarGridSpec(
            num_scalar_prefetch=3,
            grid=(rows // tb,),
            in_specs=[pl.BlockSpec((tb, w), blk),
                      pl.BlockSpec((None, D_MODEL, D_EXPERT), wsel),
                      pl.BlockSpec((None, D_MODEL, D_EXPERT), wsel),
                      pl.BlockSpec((None, D_EXPERT, D_MODEL), wsel)],
            out_specs=pl.BlockSpec((tb, w), oblk)),
        out_shape=jax.ShapeDtypeStruct((rows, w), U32),
        compiler_params=_params("arbitrary"),
        name="experts",
    )(block_expert, block_valid, n_used, xs, wg, wu, wd)


def _combine_kernel(ya_ref, yb_ref, x_ref, info_ref, lg_ref, lb_ref, out_ref, *, alpha):
    info = info_ref[...]
    g1 = info[:, 2:3]
    g2 = info[:, 3:4]
    a_lo, a_hi = _unpack_pairs(ya_ref[...])
    b_lo, b_hi = _unpack_pairs(yb_ref[...])
    half = D_MODEL // 2
    t_lo = alpha * x_ref[:, 0:half] + g1 * a_lo + g2 * b_lo
    t_hi = alpha * x_ref[:, half:] + g1 * a_hi + g2 * b_hi
    t = jnp.concatenate([t_lo, t_hi], axis=1)
    out_ref[...] = _layer_norm(t, lg_ref[...], lb_ref[...])


def _combine(y_tok, x1, info, ln_g, ln_b, alpha):
    n = x1.shape[0]
    tm = ROW_TM
    y2 = y_tok.reshape(2, n, D_MODEL // 2)
    tok = lambda w: pl.BlockSpec((tm, w), lambda i: (i, 0))
    slot = lambda k: pl.BlockSpec((None, tm, D_MODEL // 2), lambda i: (k, i, 0))
    full = lambda shape: pl.BlockSpec(shape, lambda i: (0,) * len(shape))
    return pl.pallas_call(
        functools.partial(_combine_kernel, alpha=alpha),
        grid=(n // tm,),
        in_specs=[slot(0), slot(1), tok(D_MODEL), tok(LANES), full((1, D_MODEL)), full((1, D_MODEL))],
        out_specs=tok(D_MODEL),
        out_shape=jax.ShapeDtypeStruct((n, D_MODEL), F32),
        compiler_params=_params("arbitrary"),
        name="combine",
    )(y2, y2, x1, info, ln_g, ln_b)


def _rope_tables(s):
    half = HEAD_DIM // 2
    inv_freq = ROPE_THETA ** (-jnp.arange(half, dtype=F32) / half)
    ang = jnp.arange(s, dtype=F32)[:, None] * inv_freq[None, :]
    cos = jnp.cos(ang)
    sin = jnp.sin(ang)
    cos_t = jnp.tile(cos, (1, LANES // half))
    sin_t = jnp.tile(jnp.concatenate([-sin, sin], axis=1), (1, LANES // HEAD_DIM))
    return cos_t, sin_t


def _forward(x, w_in, w_out, branch_gain, sgu_gain, sgu_w, sgu_b, conv_w, ln_gain, ln_bias,
             router_group_w, router_group_b, router_expert_w, router_expert_b,
             expert_w_gate, expert_w_up, expert_w_down):
    b, s, _ = x.shape
    depth = w_in.shape[0]
    n = b * s
    alpha = (2.0 * depth) ** 0.25
    cos_t, sin_t = _rope_tables(s)
    perm, perm_t = _tile_perm()
    tb = MOE_TB
    n_blocks = (2 * n + N_EXPERTS * (tb - 1) + tb - 1) // tb
    rows = n_blocks * tb

    for l in range(depth):
        q, k, v, rest, q16, k16, v16 = _proj(x, w_in[l].astype(BF16), cos_t, sin_t, perm)
        g = branch_gain[l]
        ma = _attention(q, k, v, q16, k16, v16, perm_t, g[None, :ATTN_W])
        bias_tile = jnp.repeat(sgu_b[l].T, SGU_W // SGU_GROUPS, axis=1)
        mbc = _mixbc(rest, sgu_gain[l][None, :], sgu_w[l], bias_tile, conv_w[l], g[None, ATTN_W:])

        rw = jnp.zeros((D_MODEL, LANES), F32)
        rw = rw.at[:, :MOE_GROUPS].set(router_group_w[l])
        rw = rw.at[:, MOE_GROUPS:MOE_GROUPS + N_EXPERTS].set(router_expert_w[l])
        rw_hi = rw.astype(BF16)
        rw_lo = (rw - rw_hi.astype(F32)).astype(BF16)
        rbias = jnp.zeros((1, LANES), F32)
        rbias = rbias.at[0, :MOE_GROUPS].set(router_group_b[l])
        rbias = rbias.at[0, MOE_GROUPS:MOE_GROUPS + N_EXPERTS].set(router_expert_b[l])
        wo = w_out[l].astype(BF16)
        x1, xp, info, info_t, cnt = _outproj(
            ma.reshape(n, ATTN_W), mbc.reshape(n, SGU_W + CONV_W), x.reshape(n, D_MODEL),
            wo[:ATTN_W], wo[ATTN_W:], ln_gain[l, 0][None], ln_bias[l, 0][None], rw_hi, rw_lo, rbias, alpha)

        counts = cnt[0, :N_EXPERTS].astype(I32)
        padded = (counts + tb - 1) // tb * tb
        pad_end = jnp.cumsum(padded)
        pad_start = pad_end - padded
        ex = info_t[0:2].astype(I32)
        dest_kn = pad_start[ex] + info_t[4:6].astype(I32)
        blk_row0 = jnp.arange(n_blocks, dtype=I32) * tb
        block_expert = jnp.minimum(jnp.sum((pad_end[None, :] <= blk_row0[:, None]).astype(I32), axis=1),
                                   N_EXPERTS - 1)
        block_valid = jnp.clip(pad_start[block_expert] + counts[block_expert] - blk_row0, 0, tb)
        n_used = (pad_end[-1:] // tb).astype(I32)

        xs = _sc_dispatch(xp, dest_kn, rows)
        ys = _experts(block_expert, block_valid, n_used, xs, expert_w_gate[l], expert_w_up[l], expert_w_down[l])
        y_tok = _sc_gather(ys, dest_kn.reshape(2 * n))
        x = _combine(y_tok, x1, info, ln_gain[l, 1][None], ln_bias[l, 1][None], alpha).reshape(b, s, D_MODEL)
    return x


def kernel(x, w_in, w_out, branch_gain, sgu_gain, sgu_w, sgu_b, conv_w, ln_gain, ln_bias, router_group_w, router_group_b, router_expert_w, router_expert_b, expert_w_gate, expert_w_up, expert_w_down):
    return _forward(x, w_in, w_out, branch_gain, sgu_gain, sgu_w, sgu_b, conv_w, ln_gain, ln_bias,
                    router_group_w, router_group_b, router_expert_w, router_expert_b,
                    expert_w_gate, expert_w_up, expert_w_down)
```

```python
import functools
import math

import jax
import jax.numpy as jnp
from jax import lax
from jax.experimental import pallas as pl
from jax.experimental.pallas import tpu as pltpu
from jax.experimental.pallas import tpu_sc as plsc

F32 = jnp.float32
BF16 = jnp.bfloat16
U32 = jnp.uint32
I32 = jnp.int32

D_MODEL = 1024
HEAD_DIM = 64
ATTN_W = 512
N_HEADS = 8
SGU_W = 256
SGU_GROUPS = 4
SGU_CHUNK = 128
CONV_W = 256
REST_W = 2 * SGU_W + 3 * CONV_W
PROJ_W = 3 * ATTN_W + REST_W
DILATIONS = (16, 4, 1)
BAND = 128
ROPE_THETA = 10000.0
MOE_GROUPS = 4
EXPERTS_PER_GROUP = 8
N_EXPERTS = MOE_GROUPS * EXPERTS_PER_GROUP
D_EXPERT = 512
EPS = 1e-5
NEG = -1e30

LANES = 128
VMEM_LIMIT = 56 * 1024 * 1024

TILE = 512
PERM_D = 16
ATTN_TQ = 512
MIX_TM = 512
OUT_TM = 512
MOE_TB = 512
ROW_TM = 512
SC_CORES = 2
SC_SUBCORES = 16
SC_WORKERS = SC_CORES * SC_SUBCORES
SC_CHUNK = 64


def _params(*sem):
    return pltpu.CompilerParams(dimension_semantics=sem, vmem_limit_bytes=VMEM_LIMIT)


def _pack_pairs(x):
    w = x.shape[1] // 2
    lo = lax.bitcast_convert_type(x[:, :w].astype(BF16).astype(F32), U32)
    hi = lax.bitcast_convert_type(x[:, w:].astype(BF16).astype(F32), U32)
    return (lo >> 16) | (hi & jnp.uint32(0xFFFF0000))


def _unpack_pairs(p):
    lo = lax.bitcast_convert_type(p << 16, F32)
    hi = lax.bitcast_convert_type(p & jnp.uint32(0xFFFF0000), F32)
    return lo, hi


def _tile_perm():
    i = jnp.arange(TILE, dtype=I32)
    tok = PERM_D * (i % (TILE // PERM_D)) + i // (TILE // PERM_D)
    p = (jnp.arange(TILE, dtype=I32)[None, :] == tok[:, None]).astype(BF16)
    return p, p.T


def _proj_kernel(x_ref, w_ref, cos_ref, sin_ref, p_ref, q_ref, k_ref, v_ref, r_ref, q16_ref, k16_ref, v16_ref):
    xb = x_ref[...].astype(BF16)
    cos = cos_ref[...]
    sin = sin_ref[...]
    lane = lax.broadcasted_iota(I32, cos.shape, 1)
    first_half = (lane % HEAD_DIM) < (HEAD_DIM // 2)
    rows = TILE // PERM_D

    def store_both(val_bf, out_ref, out16_ref):
        out_ref[...] = val_bf
        perm = jnp.dot(p_ref[...], val_bf, preferred_element_type=F32).astype(BF16)
        for r in range(PERM_D):
            out16_ref[:, r * ATTN_W:(r + 1) * ATTN_W] = perm[r * rows:(r + 1) * rows, :]

    def rope(col0, scale):
        t = jnp.dot(xb, w_ref[:, col0:col0 + ATTN_W], preferred_element_type=F32)
        out = []
        for c in range(ATTN_W // LANES):
            tc = t[:, c * LANES:(c + 1) * LANES]
            partner = jnp.where(first_half, pltpu.roll(tc, LANES - 32, 1), pltpu.roll(tc, 32, 1))
            out.append(((tc * cos + partner * sin) * scale).astype(BF16))
        return jnp.concatenate(out, axis=1)

    store_both(rope(0, HEAD_DIM ** -0.5), q_ref, q16_ref)
    store_both(rope(ATTN_W, 1.0), k_ref, k16_ref)
    vb = jnp.dot(xb, w_ref[:, 2 * ATTN_W:3 * ATTN_W], preferred_element_type=F32).astype(BF16)
    store_both(vb, v_ref, v16_ref)
    r_ref[...] = jnp.dot(xb, w_ref[:, 3 * ATTN_W:], preferred_element_type=F32).astype(BF16)


def _proj(x, w_bf, cos_t, sin_t, perm):
    b, s, _ = x.shape
    tm = TILE
    out3 = jax.ShapeDtypeStruct((b, s, ATTN_W), BF16)
    out16 = jax.ShapeDtypeStruct((b, s // PERM_D, PERM_D * ATTN_W), BF16)
    tok = lambda width: pl.BlockSpec((None, tm, width), lambda si, bi: (bi, si, 0))
    tok16 = pl.BlockSpec((None, tm // PERM_D, PERM_D * ATTN_W), lambda si, bi: (bi, si, 0))
    return pl.pallas_call(
        _proj_kernel,
        grid=(s // tm, b),
        in_specs=[tok(D_MODEL),
                  pl.BlockSpec((D_MODEL, PROJ_W), lambda si, bi: (0, 0)),
                  pl.BlockSpec((tm, LANES), lambda si, bi: (si, 0)),
                  pl.BlockSpec((tm, LANES), lambda si, bi: (si, 0)),
                  pl.BlockSpec((tm, tm), lambda si, bi: (0, 0))],
        out_specs=[tok(ATTN_W), tok(ATTN_W), tok(ATTN_W), tok(REST_W), tok16, tok16, tok16],
        out_shape=[out3, out3, out3, jax.ShapeDtypeStruct((b, s, REST_W), BF16), out16, out16, out16],
        compiler_params=_params("arbitrary", "arbitrary"),
        name="proj",
    )(x, w_bf, cos_t, sin_t, perm)


def _band_bias(permuted):
    row = lax.broadcasted_iota(I32, (BAND, 2 * BAND), 0)
    col = lax.broadcasted_iota(I32, (BAND, 2 * BAND), 1)
    key = col % BAND
    if permuted:
        row = 4 * (row % 32) + row // 32
        key = 4 * (key % 32) + key // 32
    lo_key = jnp.where(col < BAND, row, 0)
    hi_key = jnp.where(col < BAND, BAND - 1, row)
    return jnp.where(jnp.logical_and(key >= lo_key, key <= hi_key), 0.0, NEG).astype(F32), col


def _attn_heads(get_q, get_k, get_v, bias, emit):
    lane = lax.broadcasted_iota(I32, (BAND, LANES), 1)
    lane_lo = lane < HEAD_DIM
    for p in range(ATTN_W // LANES):
        ql, kk, vv = get_q(p), get_k(p), get_v(p)
        o_h, l_h = [], []
        for hh in range(2):
            qm = jnp.where(lane_lo if hh == 0 else jnp.logical_not(lane_lo), ql, jnp.zeros_like(ql))
            sc = lax.dot_general(qm, kk, (((1,), (1,)), ((), ())), preferred_element_type=F32)
            sc = sc + bias
            mx = jnp.max(sc, axis=1, keepdims=True)
            pe = jnp.exp(sc - mx)
            den = jnp.sum(pe, axis=1, keepdims=True)
            o_h.append(jnp.dot(pe.astype(BF16), vv, preferred_element_type=F32) / den)
            l_h.append(mx + jnp.log(den))
        emit(p, jnp.where(lane_lo, o_h[0], o_h[1]), l_h[0], l_h[1], lane, lane_lo)


def _attn16_kernel(q_ref, k_ref, v_ref, o_ref, st_ref, kbuf, vbuf, *, tq):
    i = pl.program_id(2)

    @pl.when(i == 0)
    def _():
        kbuf[0:BAND, :] = jnp.zeros((BAND, ATTN_W), BF16)
        vbuf[0:BAND, :] = jnp.zeros((BAND, ATTN_W), BF16)

    kbuf[BAND:BAND + tq, :] = k_ref[...]
    vbuf[BAND:BAND + tq, :] = v_ref[...]
    band_bias, col = _band_bias(False)

    def block(j, carry):
        r0 = pl.multiple_of(j * BAND, BAND)
        first_col = jnp.where(jnp.logical_or(j > 0, i > 0), 0, BAND)
        bias = jnp.where(col >= first_col, band_bias, NEG)
        st = [jnp.zeros((BAND, LANES), F32)]

        def emit(p, o_pair, l0, l1, lane, lane_lo):
            o_ref[pl.ds(r0, BAND), p * LANES:(p + 1) * LANES] = o_pair.astype(BF16)
            st[0] = jnp.where(lane == 2 * p, l0, jnp.where(lane == 2 * p + 1, l1, st[0]))

        _attn_heads(lambda p: q_ref[pl.ds(r0, BAND), p * LANES:(p + 1) * LANES],
                    lambda p: kbuf[pl.ds(r0, 2 * BAND), p * LANES:(p + 1) * LANES],
                    lambda p: vbuf[pl.ds(r0, 2 * BAND), p * LANES:(p + 1) * LANES],
                    bias, emit)
        st_ref[pl.ds(r0, BAND), :] = st[0]
        return carry

    lax.fori_loop(0, tq // BAND, block, 0)
    kbuf[0:BAND, :] = kbuf[tq:tq + BAND, :]
    vbuf[0:BAND, :] = vbuf[tq:tq + BAND, :]


def _attn4_kernel(q_ref, k_ref, v_ref, o_ref, st_ref, qbuf, kbuf, vbuf, obuf, sbuf):
    i = pl.program_id(1)
    rows = TILE // PERM_D
    nres = 4

    @pl.when(i == 0)
    def _():
        kbuf[:, 0:BAND, :] = jnp.zeros((nres, BAND, ATTN_W), BF16)
        vbuf[:, 0:BAND, :] = jnp.zeros((nres, BAND, ATTN_W), BF16)

    @pl.when(i > 0)
    def _():
        kbuf[:, 0:BAND, :] = kbuf[:, BAND:2 * BAND, :]
        vbuf[:, 0:BAND, :] = vbuf[:, BAND:2 * BAND, :]

    for r4 in range(nres):
        for g in range(PERM_D // nres):
            lanes = slice((r4 + nres * g) * ATTN_W, (r4 + nres * g + 1) * ATTN_W)
            qbuf[r4, g * rows:(g + 1) * rows, :] = q_ref[:, lanes]
            kbuf[r4, BAND + g * rows:BAND + (g + 1) * rows, :] = k_ref[:, lanes]
            vbuf[r4, BAND + g * rows:BAND + (g + 1) * rows, :] = v_ref[:, lanes]

    band_bias, col = _band_bias(True)
    first_col = jnp.where(i > 0, 0, BAND)
    bias = jnp.where(col >= first_col, band_bias, NEG)

    def block(j, carry):
        st = [jnp.zeros((BAND, LANES), F32)]

        def emit(p, o_pair, l0, l1, lane, lane_lo):
            obuf[j, :, p * LANES:(p + 1) * LANES] = o_pair.astype(BF16)
            st[0] = jnp.where(lane == 2 * p, l0, jnp.where(lane == 2 * p + 1, l1, st[0]))

        _attn_heads(lambda p: qbuf[j, :, p * LANES:(p + 1) * LANES],
                    lambda p: kbuf[j, :, p * LANES:(p + 1) * LANES],
                    lambda p: vbuf[j, :, p * LANES:(p + 1) * LANES],
                    bias, emit)
        sbuf[j] = st[0]
        return carry

    lax.fori_loop(0, nres, block, 0)

    for r4 in range(nres):
        for g in range(PERM_D // nres):
            grp = r4 + nres * g
            o_ref[:, grp * ATTN_W:(grp + 1) * ATTN_W] = obuf[r4, g * rows:(g + 1) * rows, :]
            st_ref[:, grp * LANES:(grp + 1) * LANES] = sbuf[r4, g * rows:(g + 1) * rows, :]


def _attn1_kernel(q_ref, k_ref, v_ref, o4_ref, s4_ref, o16_ref, s16_ref, pt_ref, g_ref, out_ref,
                  kbuf, vbuf, acc, o4t, s4t, o16t, s16t, *, tq):
    i = pl.program_id(1)

    @pl.when(i == 0)
    def _():
        kbuf[0:BAND, :] = jnp.zeros((BAND, ATTN_W), BF16)
        vbuf[0:BAND, :] = jnp.zeros((BAND, ATTN_W), BF16)

    kbuf[BAND:BAND + tq, :] = k_ref[...]
    vbuf[BAND:BAND + tq, :] = v_ref[...]

    def to_token_order(o_ref, s_ref, ot, st):
        operm = jnp.concatenate([o_ref[:, r * ATTN_W:(r + 1) * ATTN_W] for r in range(PERM_D)], axis=0)
        ot[...] = jnp.dot(pt_ref[...], operm, preferred_element_type=F32).astype(BF16)
        sperm = jnp.concatenate([s_ref[:, r * LANES:(r + 1) * LANES] for r in range(PERM_D)], axis=0)
        tot = jnp.zeros((tq, LANES), F32)
        for _ in range(3):
            part = sperm.astype(BF16)
            tot = tot + jnp.dot(pt_ref[...], part, preferred_element_type=F32)
            sperm = sperm - part.astype(F32)
        st[...] = tot

    to_token_order(o4_ref, s4_ref, o4t, s4t)
    to_token_order(o16_ref, s16_ref, o16t, s16t)
    band_bias, col = _band_bias(False)

    def block(j, carry):
        r0 = pl.multiple_of(j * BAND, BAND)
        first_col = jnp.where(jnp.logical_or(j > 0, i > 0), 0, BAND)
        bias = jnp.where(col >= first_col, band_bias, NEG)
        s4 = s4t[pl.ds(r0, BAND), :]
        s16 = s16t[pl.ds(r0, BAND), :]
        ssq = [jnp.zeros((BAND, 1), F32)]

        def emit(p, o_pair, l0, l1, lane, lane_lo):
            l1_ = jnp.where(lane_lo, l0, l1)
            l4 = jnp.where(lane_lo, s4[:, 2 * p:2 * p + 1], s4[:, 2 * p + 1:2 * p + 2])
            l16 = jnp.where(lane_lo, s16[:, 2 * p:2 * p + 1], s16[:, 2 * p + 1:2 * p + 2])
            top = jnp.maximum(l1_, jnp.maximum(l4, l16))
            e1 = jnp.exp(l1_ - top)
            e4 = jnp.exp(l4 - top)
            e16 = jnp.exp(l16 - top)
            o4 = o4t[pl.ds(r0, BAND), p * LANES:(p + 1) * LANES].astype(F32)
            o16 = o16t[pl.ds(r0, BAND), p * LANES:(p + 1) * LANES].astype(F32)
            y = (e1 * o_pair + e4 * o4 + e16 * o16) / (e1 + e4 + e16)
            acc[:, p * LANES:(p + 1) * LANES] = y
            ssq[0] = ssq[0] + jnp.sum(y * y, axis=1, keepdims=True)

        _attn_heads(lambda p: q_ref[pl.ds(r0, BAND), p * LANES:(p + 1) * LANES],
                    lambda p: kbuf[pl.ds(r0, 2 * BAND), p * LANES:(p + 1) * LANES],
                    lambda p: vbuf[pl.ds(r0, 2 * BAND), p * LANES:(p + 1) * LANES],
                    bias, emit)
        inv = lax.rsqrt(ssq[0] * (1.0 / ATTN_W) + EPS)
        out_ref[pl.ds(r0, BAND), :] = (acc[...] * inv * g_ref[...]).astype(BF16)
        return carry

    lax.fori_loop(0, tq // BAND, block, 0)
    kbuf[0:BAND, :] = kbuf[tq:tq + BAND, :]
    vbuf[0:BAND, :] = vbuf[tq:tq + BAND, :]


def _attention(q, k, v, q16, k16, v16, perm_t, gain):
    b, s, _ = q.shape
    m16 = s // PERM_D
    rows = TILE // PERM_D
    o16_shape = [jax.ShapeDtypeStruct((b, m16, PERM_D * ATTN_W), BF16),
                 jax.ShapeDtypeStruct((b, m16, PERM_D * LANES), F32)]

    tq16 = min(ATTN_TQ, m16)
    blk16 = lambda w: pl.BlockSpec((None, tq16, w), lambda bi, ri, ii: (bi, ii, ri))
    o16, s16 = pl.pallas_call(
        functools.partial(_attn16_kernel, tq=tq16),
        grid=(b, PERM_D, m16 // tq16),
        in_specs=[blk16(ATTN_W)] * 3, out_specs=[blk16(ATTN_W), blk16(LANES)], out_shape=o16_shape,
        scratch_shapes=[pltpu.VMEM((tq16 + BAND, ATTN_W), BF16)] * 2,
        compiler_params=_params("arbitrary", "arbitrary", "arbitrary"),
        name="attn_d16",
    )(q16, k16, v16)

    tile = lambda w: pl.BlockSpec((None, rows, PERM_D * w), lambda bi, ii: (bi, ii, 0))
    o4, s4 = pl.pallas_call(
        _attn4_kernel,
        grid=(b, s // TILE),
        in_specs=[tile(ATTN_W)] * 3, out_specs=[tile(ATTN_W), tile(LANES)], out_shape=o16_shape,
        scratch_shapes=[pltpu.VMEM((4, BAND, ATTN_W), BF16), pltpu.VMEM((4, 2 * BAND, ATTN_W), BF16),
                        pltpu.VMEM((4, 2 * BAND, ATTN_W), BF16), pltpu.VMEM((4, BAND, ATTN_W), BF16),
                        pltpu.VMEM((4, BAND, LANES), F32)],
        compiler_params=_params("arbitrary", "arbitrary"),
        name="attn_d4",
    )(q16, k16, v16)

    tq = TILE
    tok = lambda w: pl.BlockSpec((None, tq, w), lambda bi, ii: (bi, ii, 0))
    full = lambda shape: pl.BlockSpec(shape, lambda bi, ii: (0,) * len(shape))
    return pl.pallas_call(
        functools.partial(_attn1_kernel, tq=tq),
        grid=(b, s // tq),
        in_specs=[tok(ATTN_W)] * 3 + [tile(ATTN_W), tile(LANES), tile(ATTN_W), tile(LANES),
                                      full((TILE, TILE)), full((1, ATTN_W))],
        out_specs=tok(ATTN_W),
        out_shape=jax.ShapeDtypeStruct((b, s, ATTN_W), BF16),
        scratch_shapes=[pltpu.VMEM((tq + BAND, ATTN_W), BF16), pltpu.VMEM((tq + BAND, ATTN_W), BF16),
                        pltpu.VMEM((BAND, ATTN_W), F32),
                        pltpu.VMEM((tq, ATTN_W), BF16), pltpu.VMEM((tq, LANES), F32),
                        pltpu.VMEM((tq, ATTN_W), BF16), pltpu.VMEM((tq, LANES), F32)],
        compiler_params=_params("arbitrary", "arbitrary"),
        name="attn_d1",
    )(q, k, v, o4, s4, o16, s16, perm_t, gain)


def _gelu_tanh(x):
    c = math.sqrt(2.0 / math.pi)
    return x * (0.5 * (1.0 + jnp.tanh(c * (x + 0.044715 * (x * x * x)))))


def _split_dot(x, m_bf):
    hi = x.astype(BF16)
    lo = (x - hi.astype(F32)).astype(BF16)
    return (jnp.dot(hi, m_bf, preferred_element_type=F32)
            + jnp.dot(lo, m_bf, preferred_element_type=F32))


def _mixbc_kernel(r_ref, halo_ref, sg_ref, sw_ref, sb_ref, cw_ref, bg_ref, out_ref, *, tm):
    si = pl.program_id(1)
    u = r_ref[:, 0:SGU_W].astype(F32)
    z = r_ref[:, SGU_W:2 * SGU_W].astype(F32)
    gb = r_ref[:, 2 * SGU_W:2 * SGU_W + CONV_W].astype(F32)
    gc = r_ref[:, 2 * SGU_W + CONV_W:2 * SGU_W + 2 * CONV_W].astype(F32)
    hh = r_ref[:, 2 * SGU_W + 2 * CONV_W:].astype(F32)

    gdim = SGU_W // SGU_GROUPS
    ri = lax.broadcasted_iota(I32, (SGU_W, SGU_W), 0) // gdim
    ci = lax.broadcasted_iota(I32, (SGU_W, SGU_W), 1) // gdim
    avg = jnp.where(ri == ci, 1.0 / gdim, 0.0).astype(BF16)
    z = _gelu_tanh(z)
    zc = z - _split_dot(z, avg)
    var = _split_dot(zc * zc, avg)
    zn = (zc * lax.rsqrt(var + EPS) * sg_ref[...]).astype(BF16)

    tr = lax.broadcasted_iota(I32, (SGU_CHUNK, SGU_CHUNK), 0)
    tc = lax.broadcasted_iota(I32, (SGU_CHUNK, SGU_CHUNK), 1)
    w_cat = jnp.concatenate(
        [jnp.where(tc <= tr, sw_ref[g], 0.0).astype(BF16) for g in range(SGU_GROUPS)], axis=1)
    lane_g = lax.broadcasted_iota(I32, (SGU_CHUNK, SGU_W), 1) // gdim
    gu = _gelu_tanh(u)
    bias = sb_ref[...]
    yb = []
    for c in range(tm // SGU_CHUNK):
        zch = zn[c * SGU_CHUNK:(c + 1) * SGU_CHUNK, :]
        stack = jnp.concatenate(
            [jnp.where(lane_g == g, zch, jnp.zeros_like(zch)) for g in range(SGU_GROUPS)], axis=0)
        sp = jnp.dot(w_cat, stack, preferred_element_type=F32) + bias
        yb.append(gu[c * SGU_CHUNK:(c + 1) * SGU_CHUNK, :] * sp)
    yb = jnp.concatenate(yb, axis=0)

    zz = gc * hh
    hrows = halo_ref.shape[0]
    prev = (halo_ref[:, 2 * SGU_W + CONV_W:2 * SGU_W + 2 * CONV_W].astype(F32)
            * halo_ref[:, 2 * SGU_W + 2 * CONV_W:].astype(F32))
    prev = prev * (si > 0).astype(F32)
    ext = jnp.concatenate([prev, zz], axis=0)
    z1 = ext[hrows - 1:hrows - 1 + tm, :]
    z2 = ext[hrows - 2:hrows - 2 + tm, :]
    yc = gb * (cw_ref[0:1, :] * z2 + cw_ref[1:2, :] * z1 + cw_ref[2:3, :] * zz)

    def rms(t, g):
        return t * lax.rsqrt(jnp.mean(t * t, axis=1, keepdims=True) + EPS) * g

    out_ref[:, 0:SGU_W] = rms(yb, bg_ref[:, 0:SGU_W]).astype(BF16)
    out_ref[:, SGU_W:] = rms(yc, bg_ref[:, SGU_W:]).astype(BF16)


def _mixbc(rest, sgu_gain, sgu_w, sgu_bias_tile, conv_w, gain_bc):
    b, s, _ = rest.shape
    tm = MIX_TM
    hrows = 16
    full = lambda shape: pl.BlockSpec(shape, lambda bi, si: (0,) * len(shape))
    return pl.pallas_call(
        functools.partial(_mixbc_kernel, tm=tm),
        grid=(b, s // tm),
        in_specs=[pl.BlockSpec((None, tm, REST_W), lambda bi, si: (bi, si, 0)),
                  pl.BlockSpec((None, hrows, REST_W),
                               lambda bi, si: (bi, jnp.maximum(si * (tm // hrows) - 1, 0), 0)),
                  full((1, SGU_W)), full((SGU_GROUPS, SGU_CHUNK, SGU_CHUNK)),
                  full((SGU_CHUNK, SGU_W)), full((3, CONV_W)), full((1, SGU_W + CONV_W))],
        out_specs=pl.BlockSpec((None, tm, SGU_W + CONV_W), lambda bi, si: (bi, si, 0)),
        out_shape=jax.ShapeDtypeStruct((b, s, SGU_W + CONV_W), BF16),
        compiler_params=_params("arbitrary", "arbitrary"),
        name="mixbc",
    )(rest, rest, sgu_gain, sgu_w, sgu_bias_tile, conv_w, gain_bc)


def _layer_norm(t, g, b):
    mu = jnp.mean(t, axis=1, keepdims=True)
    tc = t - mu
    var = jnp.mean(tc * tc, axis=1, keepdims=True)
    return tc * lax.rsqrt(var + EPS) * g + b


def _outproj_kernel(ma_ref, mb_ref, x_ref, wa_ref, wb_ref, lg_ref, lb_ref, rwh_ref, rwl_ref, rb_ref,
                    x1_ref, xp_ref, info_ref, infot_ref, cnt_ref, run_ref, *, tm, alpha):
    i = pl.program_id(0)

    @pl.when(i == 0)
    def _():
        run_ref[...] = jnp.zeros_like(run_ref)

    y = (jnp.dot(ma_ref[...], wa_ref[...], preferred_element_type=F32)
         + jnp.dot(mb_ref[...], wb_ref[...], preferred_element_type=F32))
    x1 = _layer_norm(alpha * x_ref[...] + y, lg_ref[...], lb_ref[...])
    x1_ref[...] = x1
    xp_ref[...] = _pack_pairs(x1)

    hi = x1.astype(BF16)
    lo = (x1 - hi.astype(F32)).astype(BF16)
    logit = (jnp.dot(hi, rwh_ref[...], preferred_element_type=F32)
             + jnp.dot(lo, rwh_ref[...], preferred_element_type=F32)
             + jnp.dot(hi, rwl_ref[...], preferred_element_type=F32)) + rb_ref[...]
    lane = lax.broadcasted_iota(I32, (tm, LANES), 1)
    lane_f = lane.astype(F32)

    def top(mask):
        v = jnp.max(jnp.where(mask, logit, NEG), axis=1, keepdims=True)
        first = jnp.min(jnp.where(jnp.logical_and(mask, logit == v), lane_f, float(LANES)),
                        axis=1, keepdims=True)
        return v, first.astype(I32)

    is_g = lane < MOE_GROUPS
    gmax, gidx = top(is_g)
    g_p = 1.0 / jnp.sum(jnp.where(is_g, jnp.exp(logit - gmax), 0.0), axis=1, keepdims=True)
    in_grp = jnp.logical_and(lane >= MOE_GROUPS + gidx * EXPERTS_PER_GROUP,
                             lane < MOE_GROUPS + (gidx + 1) * EXPERTS_PER_GROUP)
    v1, i1 = top(in_grp)
    v2, i2 = top(jnp.logical_and(in_grp, lane != i1))
    e21 = jnp.exp(v2 - v1)
    gate1 = g_p / (1.0 + e21)
    gate2 = g_p * e21 / (1.0 + e21)
    ex1 = i1 - MOE_GROUPS
    ex2 = i2 - MOE_GROUPS

    oh1 = lane == ex1
    oh2 = lane == ex2
    oh = (oh1.astype(F32) + oh2.astype(F32))
    tr = lax.broadcasted_iota(I32, (tm, tm), 0)
    tc = lax.broadcasted_iota(I32, (tm, tm), 1)
    lower = jnp.where(tc < tr, 1.0, 0.0).astype(BF16)
    before = jnp.dot(lower, oh.astype(BF16), preferred_element_type=F32) + run_ref[0:1, :]
    rank1 = jnp.sum(jnp.where(oh1, before, 0.0), axis=1, keepdims=True)
    rank2 = jnp.sum(jnp.where(oh2, before, 0.0), axis=1, keepdims=True)
    run_new = run_ref[0:1, :] + jnp.sum(oh, axis=0, keepdims=True)
    run_ref[...] = jnp.broadcast_to(run_new, run_ref.shape)
    cnt_ref[...] = jnp.broadcast_to(run_new, cnt_ref.shape)

    info = jnp.where(lane == 0, ex1.astype(F32), 0.0)
    info = jnp.where(lane == 1, ex2.astype(F32), info)
    info = jnp.where(lane == 2, gate1, info)
    info = jnp.where(lane == 3, gate2, info)
    info = jnp.where(lane == 4, rank1, info)
    info = jnp.where(lane == 5, rank2, info)
    info_ref[...] = info
    infot_ref[...] = info.T[0:8, :]


def _outproj(ma, mbc, x, wo_a, wo_b, ln_g, ln_b, rw_hi, rw_lo, rbias, alpha):
    n = x.shape[0]
    tm = OUT_TM
    tok = lambda w: pl.BlockSpec((tm, w), lambda i: (i, 0))
    full = lambda shape: pl.BlockSpec(shape, lambda i: (0,) * len(shape))
    return pl.pallas_call(
        functools.partial(_outproj_kernel, tm=tm, alpha=alpha),
        grid=(n // tm,),
        in_specs=[tok(ATTN_W), tok(SGU_W + CONV_W), tok(D_MODEL),
                  full((ATTN_W, D_MODEL)), full((SGU_W + CONV_W, D_MODEL)),
                  full((1, D_MODEL)), full((1, D_MODEL)),
                  full((D_MODEL, LANES)), full((D_MODEL, LANES)), full((1, LANES))],
        out_specs=[tok(D_MODEL), tok(D_MODEL // 2), tok(LANES), pl.BlockSpec((8, tm), lambda i: (0, i)),
                   full((8, LANES))],
        out_shape=[jax.ShapeDtypeStruct((n, D_MODEL), F32),
                   jax.ShapeDtypeStruct((n, D_MODEL // 2), U32),
                   jax.ShapeDtypeStruct((n, LANES), F32),
                   jax.ShapeDtypeStruct((8, n), F32),
                   jax.ShapeDtypeStruct((8, LANES), F32)],
        scratch_shapes=[pltpu.VMEM((8, LANES), F32)],
        compiler_params=_params("arbitrary"),
        name="outproj",
    )(ma, mbc, x, wo_a, wo_b, ln_g, ln_b, rw_hi, rw_lo, rbias)


def _sc_mesh():
    return plsc.VectorSubcoreMesh(core_axis_name="c", subcore_axis_name="s",
                                  num_cores=SC_CORES, num_subcores=SC_SUBCORES)


def _sc_chunk(rows_per_worker):
    return min(SC_CHUNK, rows_per_worker // 2)


def _sc_dispatch(xp, dest_kn, rows):
    n, w = xp.shape
    t_per_w = n // SC_WORKERS
    chunk = _sc_chunk(t_per_w)
    nchunk = t_per_w // chunk

    def body(src_hbm, dest_hbm, out_hbm, idx_v, rows_v, lsem, ssem):
        wid = lax.axis_index("s") * SC_CORES + lax.axis_index("c")
        base = wid * t_per_w
        pltpu.sync_copy(dest_hbm.at[0, wid], idx_v.at[0])
        pltpu.sync_copy(dest_hbm.at[1, wid], idx_v.at[1])

        def load(c, slot):
            return pltpu.make_async_copy(src_hbm.at[pl.ds(base + c * chunk, chunk)], rows_v.at[slot],
                                         lsem.at[slot])

        def put(c, slot, kk):
            return pltpu.make_async_copy(rows_v.at[slot], out_hbm.at[idx_v.at[kk, c]], ssem.at[slot])

        load(0, 0).start()

        @pl.loop(0, nchunk, step=2)
        def _(c):
            for b in range(2):
                cc = c + b
                load(cc, b).wait()

                @pl.when(cc + 1 < nchunk)
                def _():
                    @pl.when(cc >= 1)
                    def _():
                        put(cc - 1, 1 - b, 0).wait()
                        put(cc - 1, 1 - b, 1).wait()
                    load(cc + 1, 1 - b).start()

                put(cc, b, 0).start()
                put(cc, b, 1).start()

        for b in range(2):
            put(nchunk - 2 + b, b, 0).wait()
            put(nchunk - 2 + b, b, 1).wait()

    call = pl.kernel(
        body, mesh=_sc_mesh(),
        out_type=jax.ShapeDtypeStruct((rows, w), U32),
        scratch_types=[pltpu.VMEM((2, nchunk, chunk), I32), pltpu.VMEM((2, chunk, w), U32),
                       pltpu.SemaphoreType.DMA((2,)), pltpu.SemaphoreType.DMA((2,))],
        name="sc_dispatch")
    return call(xp, dest_kn.reshape(2, SC_WORKERS, nchunk, chunk))


def _sc_gather(table, idx):
    b = idx.shape[0]
    w = table.shape[1]
    b_per_w = b // SC_WORKERS
    chunk = _sc_chunk(b_per_w)
    nchunk = b_per_w // chunk

    def body(table_hbm, idx_hbm, out_hbm, idx_v, rows_v, gsem, osem):
        wid = lax.axis_index("s") * SC_CORES + lax.axis_index("c")
        base = wid * b_per_w
        pltpu.sync_copy(idx_hbm.at[wid], idx_v)

        def gather(c, slot):
            return pltpu.make_async_copy(table_hbm.at[idx_v.at[c]], rows_v.at[slot], gsem.at[slot])

        def put(c, slot):
            return pltpu.make_async_copy(rows_v.at[slot], out_hbm.at[pl.ds(base + c * chunk, chunk)],
                                         osem.at[slot])

        gather(0, 0).start()

        @pl.loop(0, nchunk, step=2)
        def _(c):
            for b in range(2):
                cc = c + b
                gather(cc, b).wait()

                @pl.when(cc + 1 < nchunk)
                def _():
                    @pl.when(cc >= 1)
                    def _():
                        put(cc - 1, 1 - b).wait()
                    gather(cc + 1, 1 - b).start()

                put(cc, b).start()

        put(nchunk - 2, 0).wait()
        put(nchunk - 1, 1).wait()

    call = pl.kernel(
        body, mesh=_sc_mesh(),
        out_type=jax.ShapeDtypeStruct((b, w), table.dtype),
        scratch_types=[pltpu.VMEM((nchunk, chunk), I32), pltpu.VMEM((2, chunk, w), table.dtype),
                       pltpu.SemaphoreType.DMA((2,)), pltpu.SemaphoreType.DMA((2,))],
        name="sc_gather")
    return call(table, idx.reshape(SC_WORKERS, nchunk, chunk))


def _expert_kernel(be_ref, nv_ref, nu_ref, xs_ref, wg_ref, wu_ref, wd_ref, y_ref):
    i = pl.program_id(0)

    @pl.when(i < nu_ref[0])
    def _():
        row = lax.broadcasted_iota(I32, xs_ref.shape, 0)
        lo, hi = _unpack_pairs(jnp.where(row < nv_ref[i], xs_ref[...], jnp.uint32(0)))
        xb = jnp.concatenate([lo, hi], axis=1).astype(BF16)
        g = jnp.dot(xb, wg_ref[...].astype(BF16), preferred_element_type=F32)
        u = jnp.dot(xb, wu_ref[...].astype(BF16), preferred_element_type=F32)
        hdn = (g * (1.0 / (1.0 + jnp.exp(-g))) * u).astype(BF16)
        y_ref[...] = _pack_pairs(jnp.dot(hdn, wd_ref[...].astype(BF16), preferred_element_type=F32))

    @pl.when(i >= nu_ref[0])
    def _():
        y_ref[...] = jnp.zeros_like(y_ref)


def _experts(block_expert, block_valid, n_used, xs, wg, wu, wd):
    rows, w = xs.shape
    tb = MOE_TB
    blk = lambda i, be, nv, nu: (jnp.minimum(i, nu[0] - 1), 0)
    oblk = lambda i, be, nv, nu: (i, 0)
    wsel = lambda i, be, nv, nu: (be[jnp.minimum(i, nu[0] - 1)], 0, 0)
    return pl.pallas_call(
        _expert_kernel,
        grid_spec=pltpu.PrefetchScalarGridSpec(
            num_scalar_prefetch=3,
            grid=(rows // tb,),
            in_specs=[pl.BlockSpec((tb, w), blk),
                      pl.BlockSpec((None, D_MODEL, D_EXPERT), wsel),
                      pl.BlockSpec((None, D_MODEL, D_EXPERT), wsel),
                      pl.BlockSpec((None, D_EXPERT, D_MODEL), wsel)],
            out_specs=pl.BlockSpec((tb, w), oblk)),
        out_shape=jax.ShapeDtypeStruct((rows, w), U32),
        compiler_params=_params("arbitrary"),
        name="experts",
    )(block_expert, block_valid, n_used, xs, wg, wu, wd)


def _combine_kernel(ya_ref, yb_ref, x_ref, info_ref, lg_ref, lb_ref, out_ref, *, alpha):
    info = info_ref[...]
    g1 = info[:, 2:3]
    g2 = info[:, 3:4]
    a_lo, a_hi = _unpack_pairs(ya_ref[...])
    b_lo, b_hi = _unpack_pairs(yb_ref[...])
    half = D_MODEL // 2
    t_lo = alpha * x_ref[:, 0:half] + g1 * a_lo + g2 * b_lo
    t_hi = alpha * x_ref[:, half:] + g1 * a_hi + g2 * b_hi
    t = jnp.concatenate([t_lo, t_hi], axis=1)
    out_ref[...] = _layer_norm(t, lg_ref[...], lb_ref[...])


def _combine(y_tok, x1, info, ln_g, ln_b, alpha):
    n = x1.shape[0]
    tm = ROW_TM
    tok = lambda w: pl.BlockSpec((tm, w), lambda i: (i, 0))
    slot = lambda k: pl.BlockSpec((tm, D_MODEL // 2), lambda i: (i + k * (n // tm), 0))
    full = lambda shape: pl.BlockSpec(shape, lambda i: (0,) * len(shape))
    return pl.pallas_call(
        functools.partial(_combine_kernel, alpha=alpha),
        grid=(n // tm,),
        in_specs=[slot(0), slot(1), tok(D_MODEL), tok(LANES), full((1, D_MODEL)), full((1, D_MODEL))],
        out_specs=tok(D_MODEL),
        out_shape=jax.ShapeDtypeStruct((n, D_MODEL), F32),
        compiler_params=_params("arbitrary"),
        name="combine",
    )(y_tok, y_tok, x1, info, ln_g, ln_b)


def _rope_tables(s):
    half = HEAD_DIM // 2
    inv_freq = ROPE_THETA ** (-jnp.arange(half, dtype=F32) / half)
    ang = jnp.arange(s, dtype=F32)[:, None] * inv_freq[None, :]
    cos = jnp.cos(ang)
    sin = jnp.sin(ang)
    cos_t = jnp.tile(cos, (1, LANES // half))
    sin_t = jnp.tile(jnp.concatenate([-sin, sin], axis=1), (1, LANES // HEAD_DIM))
    return cos_t, sin_t


def _forward(x, w_in, w_out, branch_gain, sgu_gain, sgu_w, sgu_b, conv_w, ln_gain, ln_bias,
             router_group_w, router_group_b, router_expert_w, router_expert_b,
             expert_w_gate, expert_w_up, expert_w_down):
    b, s, _ = x.shape
    depth = w_in.shape[0]
    n = b * s
    alpha = (2.0 * depth) ** 0.25
    cos_t, sin_t = _rope_tables(s)
    perm, perm_t = _tile_perm()
    tb = MOE_TB
    n_blocks = (2 * n + N_EXPERTS * (tb - 1) + tb - 1) // tb
    rows = n_blocks * tb

    for l in range(depth):
        q, k, v, rest, q16, k16, v16 = _proj(x, w_in[l].astype(BF16), cos_t, sin_t, perm)
        g = branch_gain[l]
        ma = _attention(q, k, v, q16, k16, v16, perm_t, g[None, :ATTN_W])
        bias_tile = jnp.repeat(sgu_b[l].T, SGU_W // SGU_GROUPS, axis=1)
        mbc = _mixbc(rest, sgu_gain[l][None, :], sgu_w[l], bias_tile, conv_w[l], g[None, ATTN_W:])

        rw = jnp.zeros((D_MODEL, LANES), F32)
        rw = rw.at[:, :MOE_GROUPS].set(router_group_w[l])
        rw = rw.at[:, MOE_GROUPS:MOE_GROUPS + N_EXPERTS].set(router_expert_w[l])
        rw_hi = rw.astype(BF16)
        rw_lo = (rw - rw_hi.astype(F32)).astype(BF16)
        rbias = jnp.zeros((1, LANES), F32)
        rbias = rbias.at[0, :MOE_GROUPS].set(router_group_b[l])
        rbias = rbias.at[0, MOE_GROUPS:MOE_GROUPS + N_EXPERTS].set(router_expert_b[l])
        wo = w_out[l].astype(BF16)
        x1, xp, info, info_t, cnt = _outproj(
            ma.reshape(n, ATTN_W), mbc.reshape(n, SGU_W + CONV_W), x.reshape(n, D_MODEL),
            wo[:ATTN_W], wo[ATTN_W:], ln_gain[l, 0][None], ln_bias[l, 0][None], rw_hi, rw_lo, rbias, alpha)

        counts = cnt[0, :N_EXPERTS].astype(I32)
        padded = (counts + tb - 1) // tb * tb
        pad_end = jnp.cumsum(padded)
        pad_start = pad_end - padded
        ex = info_t[0:2].astype(I32)
        start_of = jnp.zeros_like(ex)
        for e in range(N_EXPERTS):
            start_of = jnp.where(ex == e, pad_start[e], start_of)
        dest_kn = start_of + info_t[4:6].astype(I32)
        blk_row0 = jnp.arange(n_blocks, dtype=I32) * tb
        block_expert = jnp.minimum(jnp.sum((pad_end[None, :] <= blk_row0[:, None]).astype(I32), axis=1),
                                   N_EXPERTS - 1)
        block_valid = jnp.clip(pad_start[block_expert] + counts[block_expert] - blk_row0, 0, tb)
        n_used = (pad_end[-1:] // tb).astype(I32)

        xs = _sc_dispatch(xp, dest_kn, rows)
        ys = _experts(block_expert, block_valid, n_used, xs, expert_w_gate[l], expert_w_up[l], expert_w_down[l])
        y_tok = _sc_gather(ys, dest_kn.reshape(2 * n))
        x = _combine(y_tok, x1, info, ln_gain[l, 1][None], ln_bias[l, 1][None], alpha).reshape(b, s, D_MODEL)
    return x


def kernel(x, w_in, w_out, branch_gain, sgu_gain, sgu_w, sgu_b, conv_w, ln_gain, ln_bias, router_group_w, router_group_b, router_expert_w, router_expert_b, expert_w_gate, expert_w_up, expert_w_down):
    return _forward(x, w_in, w_out, branch_gain, sgu_gain, sgu_w, sgu_b, conv_w, ln_gain, ln_bias,
                    router_group_w, router_group_b, router_expert_w, router_expert_b,
                    expert_w_gate, expert_w_up, expert_w_down)
```

```python
import functools
import math

import jax
import jax.numpy as jnp
from jax import lax
from jax.experimental import pallas as pl
from jax.experimental.pallas import tpu as pltpu
from jax.experimental.pallas import tpu_sc as plsc

F32 = jnp.float32
BF16 = jnp.bfloat16
U32 = jnp.uint32
I32 = jnp.int32

D_MODEL = 1024
HEAD_DIM = 64
ATTN_W = 512
N_HEADS = 8
SGU_W = 256
SGU_GROUPS = 4
SGU_CHUNK = 128
CONV_W = 256
REST_W = 2 * SGU_W + 3 * CONV_W
PROJ_W = 3 * ATTN_W + REST_W
DILATIONS = (16, 4, 1)
BAND = 128
ROPE_THETA = 10000.0
MOE_GROUPS = 4
EXPERTS_PER_GROUP = 8
N_EXPERTS = MOE_GROUPS * EXPERTS_PER_GROUP
D_EXPERT = 512
EPS = 1e-5
NEG = -1e30

LANES = 128
VMEM_LIMIT = 56 * 1024 * 1024

TILE = 512
PERM_D = 16
ATTN_TQ = 512
MIX_TM = 512
OUT_TM = 512
MOE_TB = 512
ROW_TM = 512
SC_CORES = 2
SC_SUBCORES = 16
SC_WORKERS = SC_CORES * SC_SUBCORES
SC_CHUNK = 64


def _params(*sem):
    return pltpu.CompilerParams(dimension_semantics=sem, vmem_limit_bytes=VMEM_LIMIT)


def _pack_pairs(x):
    w = x.shape[1] // 2
    lo = lax.bitcast_convert_type(x[:, :w].astype(BF16).astype(F32), U32)
    hi = lax.bitcast_convert_type(x[:, w:].astype(BF16).astype(F32), U32)
    return (lo >> 16) | (hi & jnp.uint32(0xFFFF0000))


def _unpack_pairs(p):
    lo = lax.bitcast_convert_type(p << 16, F32)
    hi = lax.bitcast_convert_type(p & jnp.uint32(0xFFFF0000), F32)
    return lo, hi


def _tile_perm():
    i = jnp.arange(TILE, dtype=I32)
    tok = PERM_D * (i % (TILE // PERM_D)) + i // (TILE // PERM_D)
    p = (jnp.arange(TILE, dtype=I32)[None, :] == tok[:, None]).astype(BF16)
    return p, p.T


def _proj_kernel(x_ref, w_ref, cos_ref, sin_ref, p_ref, q_ref, k_ref, v_ref, r_ref, q16_ref, k16_ref, v16_ref):
    xb = x_ref[...].astype(BF16)
    cos = cos_ref[...]
    sin = sin_ref[...]
    lane = lax.broadcasted_iota(I32, cos.shape, 1)
    first_half = (lane % HEAD_DIM) < (HEAD_DIM // 2)
    rows = TILE // PERM_D

    def store_both(val_bf, out_ref, out16_ref):
        out_ref[...] = val_bf
        perm = jnp.dot(p_ref[...], val_bf, preferred_element_type=F32).astype(BF16)
        for r in range(PERM_D):
            out16_ref[:, r * ATTN_W:(r + 1) * ATTN_W] = perm[r * rows:(r + 1) * rows, :]

    def rope(col0, scale):
        t = jnp.dot(xb, w_ref[:, col0:col0 + ATTN_W], preferred_element_type=F32)
        out = []
        for c in range(ATTN_W // LANES):
            tc = t[:, c * LANES:(c + 1) * LANES]
            partner = jnp.where(first_half, pltpu.roll(tc, LANES - 32, 1), pltpu.roll(tc, 32, 1))
            out.append(((tc * cos + partner * sin) * scale).astype(BF16))
        return jnp.concatenate(out, axis=1)

    store_both(rope(0, HEAD_DIM ** -0.5), q_ref, q16_ref)
    store_both(rope(ATTN_W, 1.0), k_ref, k16_ref)
    vb = jnp.dot(xb, w_ref[:, 2 * ATTN_W:3 * ATTN_W], preferred_element_type=F32).astype(BF16)
    store_both(vb, v_ref, v16_ref)
    r_ref[...] = jnp.dot(xb, w_ref[:, 3 * ATTN_W:], preferred_element_type=F32).astype(BF16)


def _proj(x, w_bf, cos_t, sin_t, perm):
    b, s, _ = x.shape
    tm = TILE
    out3 = jax.ShapeDtypeStruct((b, s, ATTN_W), BF16)
    out16 = jax.ShapeDtypeStruct((b, s // PERM_D, PERM_D * ATTN_W), BF16)
    tok = lambda width: pl.BlockSpec((None, tm, width), lambda si, bi: (bi, si, 0))
    tok16 = pl.BlockSpec((None, tm // PERM_D, PERM_D * ATTN_W), lambda si, bi: (bi, si, 0))
    return pl.pallas_call(
        _proj_kernel,
        grid=(s // tm, b),
        in_specs=[tok(D_MODEL),
                  pl.BlockSpec((D_MODEL, PROJ_W), lambda si, bi: (0, 0)),
                  pl.BlockSpec((tm, LANES), lambda si, bi: (si, 0)),
                  pl.BlockSpec((tm, LANES), lambda si, bi: (si, 0)),
                  pl.BlockSpec((tm, tm), lambda si, bi: (0, 0))],
        out_specs=[tok(ATTN_W), tok(ATTN_W), tok(ATTN_W), tok(REST_W), tok16, tok16, tok16],
        out_shape=[out3, out3, out3, jax.ShapeDtypeStruct((b, s, REST_W), BF16), out16, out16, out16],
        compiler_params=_params("arbitrary", "arbitrary"),
        name="proj",
    )(x, w_bf, cos_t, sin_t, perm)


def _band_bias(permuted):
    row = lax.broadcasted_iota(I32, (BAND, 2 * BAND), 0)
    col = lax.broadcasted_iota(I32, (BAND, 2 * BAND), 1)
    key = col % BAND
    if permuted:
        row = 4 * (row % 32) + row // 32
        key = 4 * (key % 32) + key // 32
    lo_key = jnp.where(col < BAND, row, 0)
    hi_key = jnp.where(col < BAND, BAND - 1, row)
    return jnp.where(jnp.logical_and(key >= lo_key, key <= hi_key), 0.0, NEG).astype(F32), col


def _attn_heads(get_q, get_k, get_v, bias, emit):
    lane = lax.broadcasted_iota(I32, (BAND, LANES), 1)
    lane_lo = lane < HEAD_DIM
    for p in range(ATTN_W // LANES):
        ql, kk, vv = get_q(p), get_k(p), get_v(p)
        o_h, l_h = [], []
        for hh in range(2):
            qm = jnp.where(lane_lo if hh == 0 else jnp.logical_not(lane_lo), ql, jnp.zeros_like(ql))
            sc = lax.dot_general(qm, kk, (((1,), (1,)), ((), ())), preferred_element_type=F32)
            sc = sc + bias
            mx = jnp.max(sc, axis=1, keepdims=True)
            pe = jnp.exp(sc - mx)
            den = jnp.sum(pe, axis=1, keepdims=True)
            o_h.append(jnp.dot(pe.astype(BF16), vv, preferred_element_type=F32) / den)
            l_h.append(mx + jnp.log(den))
        emit(p, jnp.where(lane_lo, o_h[0], o_h[1]), l_h[0], l_h[1], lane, lane_lo)


def _attn16_kernel(q_ref, k_ref, v_ref, o_ref, st_ref, kbuf, vbuf, *, tq):
    i = pl.program_id(2)

    @pl.when(i == 0)
    def _():
        kbuf[0:BAND, :] = jnp.zeros((BAND, ATTN_W), BF16)
        vbuf[0:BAND, :] = jnp.zeros((BAND, ATTN_W), BF16)

    kbuf[BAND:BAND + tq, :] = k_ref[...]
    vbuf[BAND:BAND + tq, :] = v_ref[...]
    band_bias, col = _band_bias(False)

    def block(j, carry):
        r0 = pl.multiple_of(j * BAND, BAND)
        first_col = jnp.where(jnp.logical_or(j > 0, i > 0), 0, BAND)
        bias = jnp.where(col >= first_col, band_bias, NEG)
        st = [jnp.zeros((BAND, LANES), F32)]

        def emit(p, o_pair, l0, l1, lane, lane_lo):
            o_ref[pl.ds(r0, BAND), p * LANES:(p + 1) * LANES] = o_pair.astype(BF16)
            st[0] = jnp.where(lane == 2 * p, l0, jnp.where(lane == 2 * p + 1, l1, st[0]))

        _attn_heads(lambda p: q_ref[pl.ds(r0, BAND), p * LANES:(p + 1) * LANES],
                    lambda p: kbuf[pl.ds(r0, 2 * BAND), p * LANES:(p + 1) * LANES],
                    lambda p: vbuf[pl.ds(r0, 2 * BAND), p * LANES:(p + 1) * LANES],
                    bias, emit)
        st_ref[pl.ds(r0, BAND), :] = st[0]
        return carry

    lax.fori_loop(0, tq // BAND, block, 0, unroll=True)
    kbuf[0:BAND, :] = kbuf[tq:tq + BAND, :]
    vbuf[0:BAND, :] = vbuf[tq:tq + BAND, :]


def _attn4_kernel(q_ref, k_ref, v_ref, o_ref, st_ref, qbuf, kbuf, vbuf, obuf, sbuf):
    i = pl.program_id(1)
    rows = TILE // PERM_D
    nres = 4

    @pl.when(i == 0)
    def _():
        kbuf[:, 0:BAND, :] = jnp.zeros((nres, BAND, ATTN_W), BF16)
        vbuf[:, 0:BAND, :] = jnp.zeros((nres, BAND, ATTN_W), BF16)

    @pl.when(i > 0)
    def _():
        kbuf[:, 0:BAND, :] = kbuf[:, BAND:2 * BAND, :]
        vbuf[:, 0:BAND, :] = vbuf[:, BAND:2 * BAND, :]

    for r4 in range(nres):
        for g in range(PERM_D // nres):
            lanes = slice((r4 + nres * g) * ATTN_W, (r4 + nres * g + 1) * ATTN_W)
            qbuf[r4, g * rows:(g + 1) * rows, :] = q_ref[:, lanes]
            kbuf[r4, BAND + g * rows:BAND + (g + 1) * rows, :] = k_ref[:, lanes]
            vbuf[r4, BAND + g * rows:BAND + (g + 1) * rows, :] = v_ref[:, lanes]

    band_bias, col = _band_bias(True)
    first_col = jnp.where(i > 0, 0, BAND)
    bias = jnp.where(col >= first_col, band_bias, NEG)

    def block(j, carry):
        st = [jnp.zeros((BAND, LANES), F32)]

        def emit(p, o_pair, l0, l1, lane, lane_lo):
            obuf[j, :, p * LANES:(p + 1) * LANES] = o_pair.astype(BF16)
            st[0] = jnp.where(lane == 2 * p, l0, jnp.where(lane == 2 * p + 1, l1, st[0]))

        _attn_heads(lambda p: qbuf[j, :, p * LANES:(p + 1) * LANES],
                    lambda p: kbuf[j, :, p * LANES:(p + 1) * LANES],
                    lambda p: vbuf[j, :, p * LANES:(p + 1) * LANES],
                    bias, emit)
        sbuf[j] = st[0]
        return carry

    lax.fori_loop(0, nres, block, 0, unroll=True)

    for r4 in range(nres):
        for g in range(PERM_D // nres):
            grp = r4 + nres * g
            o_ref[:, grp * ATTN_W:(grp + 1) * ATTN_W] = obuf[r4, g * rows:(g + 1) * rows, :]
            st_ref[:, grp * LANES:(grp + 1) * LANES] = sbuf[r4, g * rows:(g + 1) * rows, :]


def _attn1_kernel(q_ref, k_ref, v_ref, o4_ref, s4_ref, o16_ref, s16_ref, pt_ref, g_ref, out_ref,
                  kbuf, vbuf, acc, o4t, s4t, o16t, s16t, *, tq):
    i = pl.program_id(1)

    @pl.when(i == 0)
    def _():
        kbuf[0:BAND, :] = jnp.zeros((BAND, ATTN_W), BF16)
        vbuf[0:BAND, :] = jnp.zeros((BAND, ATTN_W), BF16)

    kbuf[BAND:BAND + tq, :] = k_ref[...]
    vbuf[BAND:BAND + tq, :] = v_ref[...]

    def to_token_order(o_ref, s_ref, ot, st):
        operm = jnp.concatenate([o_ref[:, r * ATTN_W:(r + 1) * ATTN_W] for r in range(PERM_D)], axis=0)
        ot[...] = jnp.dot(pt_ref[...], operm, preferred_element_type=F32).astype(BF16)
        sperm = jnp.concatenate([s_ref[:, r * LANES:(r + 1) * LANES] for r in range(PERM_D)], axis=0)
        tot = jnp.zeros((tq, LANES), F32)
        for _ in range(3):
            part = sperm.astype(BF16)
            tot = tot + jnp.dot(pt_ref[...], part, preferred_element_type=F32)
            sperm = sperm - part.astype(F32)
        st[...] = tot

    to_token_order(o4_ref, s4_ref, o4t, s4t)
    to_token_order(o16_ref, s16_ref, o16t, s16t)
    band_bias, col = _band_bias(False)

    def block(j, carry):
        r0 = pl.multiple_of(j * BAND, BAND)
        first_col = jnp.where(jnp.logical_or(j > 0, i > 0), 0, BAND)
        bias = jnp.where(col >= first_col, band_bias, NEG)
        s4 = s4t[pl.ds(r0, BAND), :]
        s16 = s16t[pl.ds(r0, BAND), :]
        ssq = [jnp.zeros((BAND, 1), F32)]

        def emit(p, o_pair, l0, l1, lane, lane_lo):
            l1_ = jnp.where(lane_lo, l0, l1)
            l4 = jnp.where(lane_lo, s4[:, 2 * p:2 * p + 1], s4[:, 2 * p + 1:2 * p + 2])
            l16 = jnp.where(lane_lo, s16[:, 2 * p:2 * p + 1], s16[:, 2 * p + 1:2 * p + 2])
            top = jnp.maximum(l1_, jnp.maximum(l4, l16))
            e1 = jnp.exp(l1_ - top)
            e4 = jnp.exp(l4 - top)
            e16 = jnp.exp(l16 - top)
            o4 = o4t[pl.ds(r0, BAND), p * LANES:(p + 1) * LANES].astype(F32)
            o16 = o16t[pl.ds(r0, BAND), p * LANES:(p + 1) * LANES].astype(F32)
            y = (e1 * o_pair + e4 * o4 + e16 * o16) / (e1 + e4 + e16)
            acc[:, p * LANES:(p + 1) * LANES] = y
            ssq[0] = ssq[0] + jnp.sum(y * y, axis=1, keepdims=True)

        _attn_heads(lambda p: q_ref[pl.ds(r0, BAND), p * LANES:(p + 1) * LANES],
                    lambda p: kbuf[pl.ds(r0, 2 * BAND), p * LANES:(p + 1) * LANES],
                    lambda p: vbuf[pl.ds(r0, 2 * BAND), p * LANES:(p + 1) * LANES],
                    bias, emit)
        inv = lax.rsqrt(ssq[0] * (1.0 / ATTN_W) + EPS)
        out_ref[pl.ds(r0, BAND), :] = (acc[...] * inv * g_ref[...]).astype(BF16)
        return carry

    lax.fori_loop(0, tq // BAND, block, 0, unroll=True)
    kbuf[0:BAND, :] = kbuf[tq:tq + BAND, :]
    vbuf[0:BAND, :] = vbuf[tq:tq + BAND, :]


def _attention(q, k, v, q16, k16, v16, perm_t, gain):
    b, s, _ = q.shape
    m16 = s // PERM_D
    rows = TILE // PERM_D
    o16_shape = [jax.ShapeDtypeStruct((b, m16, PERM_D * ATTN_W), BF16),
                 jax.ShapeDtypeStruct((b, m16, PERM_D * LANES), F32)]

    tq16 = min(ATTN_TQ, m16)
    blk16 = lambda w: pl.BlockSpec((None, tq16, w), lambda bi, ri, ii: (bi, ii, ri))
    o16, s16 = pl.pallas_call(
        functools.partial(_attn16_kernel, tq=tq16),
        grid=(b, PERM_D, m16 // tq16),
        in_specs=[blk16(ATTN_W)] * 3, out_specs=[blk16(ATTN_W), blk16(LANES)], out_shape=o16_shape,
        scratch_shapes=[pltpu.VMEM((tq16 + BAND, ATTN_W), BF16)] * 2,
        compiler_params=_params("arbitrary", "arbitrary", "arbitrary"),
        name="attn_d16",
    )(q16, k16, v16)

    tile = lambda w: pl.BlockSpec((None, rows, PERM_D * w), lambda bi, ii: (bi, ii, 0))
    o4, s4 = pl.pallas_call(
        _attn4_kernel,
        grid=(b, s // TILE),
        in_specs=[tile(ATTN_W)] * 3, out_specs=[tile(ATTN_W), tile(LANES)], out_shape=o16_shape,
        scratch_shapes=[pltpu.VMEM((4, BAND, ATTN_W), BF16), pltpu.VMEM((4, 2 * BAND, ATTN_W), BF16),
                        pltpu.VMEM((4, 2 * BAND, ATTN_W), BF16), pltpu.VMEM((4, BAND, ATTN_W), BF16),
                        pltpu.VMEM((4, BAND, LANES), F32)],
        compiler_params=_params("arbitrary", "arbitrary"),
        name="attn_d4",
    )(q16, k16, v16)

    tq = TILE
    tok = lambda w: pl.BlockSpec((None, tq, w), lambda bi, ii: (bi, ii, 0))
    full = lambda shape: pl.BlockSpec(shape, lambda bi, ii: (0,) * len(shape))
    return pl.pallas_call(
        functools.partial(_attn1_kernel, tq=tq),
        grid=(b, s // tq),
        in_specs=[tok(ATTN_W)] * 3 + [tile(ATTN_W), tile(LANES), tile(ATTN_W), tile(LANES),
                                      full((TILE, TILE)), full((1, ATTN_W))],
        out_specs=tok(ATTN_W),
        out_shape=jax.ShapeDtypeStruct((b, s, ATTN_W), BF16),
        scratch_shapes=[pltpu.VMEM((tq + BAND, ATTN_W), BF16), pltpu.VMEM((tq + BAND, ATTN_W), BF16),
                        pltpu.VMEM((BAND, ATTN_W), F32),
                        pltpu.VMEM((tq, ATTN_W), BF16), pltpu.VMEM((tq, LANES), F32),
                        pltpu.VMEM((tq, ATTN_W), BF16), pltpu.VMEM((tq, LANES), F32)],
        compiler_params=_params("arbitrary", "arbitrary"),
        name="attn_d1",
    )(q, k, v, o4, s4, o16, s16, perm_t, gain)


def _gelu_tanh(x):
    c = math.sqrt(2.0 / math.pi)
    return x * (0.5 * (1.0 + jnp.tanh(c * (x + 0.044715 * (x * x * x)))))


def _split_dot(x, m_bf):
    hi = x.astype(BF16)
    lo = (x - hi.astype(F32)).astype(BF16)
    return (jnp.dot(hi, m_bf, preferred_element_type=F32)
            + jnp.dot(lo, m_bf, preferred_element_type=F32))


def _mixbc_kernel(r_ref, halo_ref, sg_ref, sw_ref, sb_ref, cw_ref, bg_ref, out_ref, *, tm):
    si = pl.program_id(1)
    u = r_ref[:, 0:SGU_W].astype(F32)
    z = r_ref[:, SGU_W:2 * SGU_W].astype(F32)
    gb = r_ref[:, 2 * SGU_W:2 * SGU_W + CONV_W].astype(F32)
    gc = r_ref[:, 2 * SGU_W + CONV_W:2 * SGU_W + 2 * CONV_W].astype(F32)
    hh = r_ref[:, 2 * SGU_W + 2 * CONV_W:].astype(F32)

    gdim = SGU_W // SGU_GROUPS
    ri = lax.broadcasted_iota(I32, (SGU_W, SGU_W), 0) // gdim
    ci = lax.broadcasted_iota(I32, (SGU_W, SGU_W), 1) // gdim
    avg = jnp.where(ri == ci, 1.0 / gdim, 0.0).astype(BF16)
    z = _gelu_tanh(z)
    zc = z - _split_dot(z, avg)
    var = _split_dot(zc * zc, avg)
    zn = (zc * lax.rsqrt(var + EPS) * sg_ref[...]).astype(BF16)

    tr = lax.broadcasted_iota(I32, (SGU_CHUNK, SGU_CHUNK), 0)
    tc = lax.broadcasted_iota(I32, (SGU_CHUNK, SGU_CHUNK), 1)
    w_cat = jnp.concatenate(
        [jnp.where(tc <= tr, sw_ref[g], 0.0).astype(BF16) for g in range(SGU_GROUPS)], axis=1)
    lane_g = lax.broadcasted_iota(I32, (SGU_CHUNK, SGU_W), 1) // gdim
    gu = _gelu_tanh(u)
    bias = sb_ref[...]
    yb = []
    for c in range(tm // SGU_CHUNK):
        zch = zn[c * SGU_CHUNK:(c + 1) * SGU_CHUNK, :]
        stack = jnp.concatenate(
            [jnp.where(lane_g == g, zch, jnp.zeros_like(zch)) for g in range(SGU_GROUPS)], axis=0)
        sp = jnp.dot(w_cat, stack, preferred_element_type=F32) + bias
        yb.append(gu[c * SGU_CHUNK:(c + 1) * SGU_CHUNK, :] * sp)
    yb = jnp.concatenate(yb, axis=0)

    zz = gc * hh
    hrows = halo_ref.shape[0]
    prev = (halo_ref[:, 2 * SGU_W + CONV_W:2 * SGU_W + 2 * CONV_W].astype(F32)
            * halo_ref[:, 2 * SGU_W + 2 * CONV_W:].astype(F32))
    prev = prev * (si > 0).astype(F32)
    ext = jnp.concatenate([prev, zz], axis=0)
    z1 = ext[hrows - 1:hrows - 1 + tm, :]
    z2 = ext[hrows - 2:hrows - 2 + tm, :]
    yc = gb * (cw_ref[0:1, :] * z2 + cw_ref[1:2, :] * z1 + cw_ref[2:3, :] * zz)

    def rms(t, g):
        return t * lax.rsqrt(jnp.mean(t * t, axis=1, keepdims=True) + EPS) * g

    out_ref[:, 0:SGU_W] = rms(yb, bg_ref[:, 0:SGU_W]).astype(BF16)
    out_ref[:, SGU_W:] = rms(yc, bg_ref[:, SGU_W:]).astype(BF16)


def _mixbc(rest, sgu_gain, sgu_w, sgu_bias_tile, conv_w, gain_bc):
    b, s, _ = rest.shape
    tm = MIX_TM
    hrows = 16
    full = lambda shape: pl.BlockSpec(shape, lambda bi, si: (0,) * len(shape))
    return pl.pallas_call(
        functools.partial(_mixbc_kernel, tm=tm),
        grid=(b, s // tm),
        in_specs=[pl.BlockSpec((None, tm, REST_W), lambda bi, si: (bi, si, 0)),
                  pl.BlockSpec((None, hrows, REST_W),
                               lambda bi, si: (bi, jnp.maximum(si * (tm // hrows) - 1, 0), 0)),
                  full((1, SGU_W)), full((SGU_GROUPS, SGU_CHUNK, SGU_CHUNK)),
                  full((SGU_CHUNK, SGU_W)), full((3, CONV_W)), full((1, SGU_W + CONV_W))],
        out_specs=pl.BlockSpec((None, tm, SGU_W + CONV_W), lambda bi, si: (bi, si, 0)),
        out_shape=jax.ShapeDtypeStruct((b, s, SGU_W + CONV_W), BF16),
        compiler_params=_params("arbitrary", "arbitrary"),
        name="mixbc",
    )(rest, rest, sgu_gain, sgu_w, sgu_bias_tile, conv_w, gain_bc)


def _layer_norm(t, g, b):
    mu = jnp.mean(t, axis=1, keepdims=True)
    tc = t - mu
    var = jnp.mean(tc * tc, axis=1, keepdims=True)
    return tc * lax.rsqrt(var + EPS) * g + b


def _outproj_kernel(ma_ref, mb_ref, x_ref, wa_ref, wb_ref, lg_ref, lb_ref, rw_ref, rb_ref,
                    x1_ref, xp_ref, info_ref, infot_ref, cnt_ref, run_ref, *, tm, alpha):
    i = pl.program_id(0)

    @pl.when(i == 0)
    def _():
        run_ref[...] = jnp.zeros_like(run_ref)

    y = (jnp.dot(ma_ref[...], wa_ref[...], preferred_element_type=F32)
         + jnp.dot(mb_ref[...], wb_ref[...], preferred_element_type=F32))
    x1 = _layer_norm(alpha * x_ref[...] + y, lg_ref[...], lb_ref[...])
    x1_ref[...] = x1
    xp_ref[...] = _pack_pairs(x1)

    hi = x1.astype(BF16)
    lo = (x1 - hi.astype(F32)).astype(BF16)
    both = jnp.dot(hi, rw_ref[...], preferred_element_type=F32)
    logit = (both[:, :LANES] + both[:, LANES:]
             + jnp.dot(lo, rw_ref[:, :LANES], preferred_element_type=F32)) + rb_ref[...]
    lane = lax.broadcasted_iota(I32, (tm, LANES), 1)
    lane_f = lane.astype(F32)

    def top(mask):
        v = jnp.max(jnp.where(mask, logit, NEG), axis=1, keepdims=True)
        first = jnp.min(jnp.where(jnp.logical_and(mask, logit == v), lane_f, float(LANES)),
                        axis=1, keepdims=True)
        return v, first.astype(I32)

    is_g = lane < MOE_GROUPS
    gmax, gidx = top(is_g)
    g_p = 1.0 / jnp.sum(jnp.where(is_g, jnp.exp(logit - gmax), 0.0), axis=1, keepdims=True)
    in_grp = jnp.logical_and(lane >= MOE_GROUPS + gidx * EXPERTS_PER_GROUP,
                             lane < MOE_GROUPS + (gidx + 1) * EXPERTS_PER_GROUP)
    v1, i1 = top(in_grp)
    v2, i2 = top(jnp.logical_and(in_grp, lane != i1))
    e21 = jnp.exp(v2 - v1)
    gate1 = g_p / (1.0 + e21)
    gate2 = g_p * e21 / (1.0 + e21)
    ex1 = i1 - MOE_GROUPS
    ex2 = i2 - MOE_GROUPS

    oh1 = lane == ex1
    oh2 = lane == ex2
    oh = (oh1.astype(F32) + oh2.astype(F32))
    tr = lax.broadcasted_iota(I32, (tm, tm), 0)
    tc = lax.broadcasted_iota(I32, (tm, tm), 1)
    lower = jnp.where(tc < tr, 1.0, 0.0).astype(BF16)
    before = jnp.dot(lower, oh.astype(BF16), preferred_element_type=F32) + run_ref[0:1, :]
    rank1 = jnp.sum(jnp.where(oh1, before, 0.0), axis=1, keepdims=True)
    rank2 = jnp.sum(jnp.where(oh2, before, 0.0), axis=1, keepdims=True)
    run_new = run_ref[0:1, :] + jnp.sum(oh, axis=0, keepdims=True)
    run_ref[...] = jnp.broadcast_to(run_new, run_ref.shape)
    cnt_ref[...] = jnp.broadcast_to(run_new, cnt_ref.shape)

    info = jnp.where(lane == 0, ex1.astype(F32), 0.0)
    info = jnp.where(lane == 1, ex2.astype(F32), info)
    info = jnp.where(lane == 2, gate1, info)
    info = jnp.where(lane == 3, gate2, info)
    info = jnp.where(lane == 4, rank1, info)
    info = jnp.where(lane == 5, rank2, info)
    info_ref[...] = info
    infot_ref[...] = info.T[0:8, :]


def _outproj(ma, mbc, x, wo_a, wo_b, ln_g, ln_b, rw_hilo, rbias, alpha):
    n = x.shape[0]
    tm = OUT_TM
    tok = lambda w: pl.BlockSpec((tm, w), lambda i: (i, 0))
    full = lambda shape: pl.BlockSpec(shape, lambda i: (0,) * len(shape))
    return pl.pallas_call(
        functools.partial(_outproj_kernel, tm=tm, alpha=alpha),
        grid=(n // tm,),
        in_specs=[tok(ATTN_W), tok(SGU_W + CONV_W), tok(D_MODEL),
                  full((ATTN_W, D_MODEL)), full((SGU_W + CONV_W, D_MODEL)),
                  full((1, D_MODEL)), full((1, D_MODEL)),
                  full((D_MODEL, 2 * LANES)), full((1, LANES))],
        out_specs=[tok(D_MODEL), tok(D_MODEL // 2), tok(LANES), pl.BlockSpec((8, tm), lambda i: (0, i)),
                   full((8, LANES))],
        out_shape=[jax.ShapeDtypeStruct((n, D_MODEL), F32),
                   jax.ShapeDtypeStruct((n, D_MODEL // 2), U32),
                   jax.ShapeDtypeStruct((n, LANES), F32),
                   jax.ShapeDtypeStruct((8, n), F32),
                   jax.ShapeDtypeStruct((8, LANES), F32)],
        scratch_shapes=[pltpu.VMEM((8, LANES), F32)],
        compiler_params=_params("arbitrary"),
        name="outproj",
    )(ma, mbc, x, wo_a, wo_b, ln_g, ln_b, rw_hilo, rbias)


def _sc_mesh():
    return plsc.VectorSubcoreMesh(core_axis_name="c", subcore_axis_name="s",
                                  num_cores=SC_CORES, num_subcores=SC_SUBCORES)


def _sc_chunk(rows_per_worker):
    return min(SC_CHUNK, rows_per_worker // 2)


def _sc_dispatch(xp, dest_kn, rows):
    n, w = xp.shape
    t_per_w = n // SC_WORKERS
    chunk = _sc_chunk(t_per_w)
    nchunk = t_per_w // chunk

    def body(src_hbm, dest_hbm, out_hbm, idx_v, rows_v, lsem, ssem):
        wid = lax.axis_index("s") * SC_CORES + lax.axis_index("c")
        base = wid * t_per_w
        pltpu.sync_copy(dest_hbm.at[0, wid], idx_v.at[0])
        pltpu.sync_copy(dest_hbm.at[1, wid], idx_v.at[1])

        def load(c, slot):
            return pltpu.make_async_copy(src_hbm.at[pl.ds(base + c * chunk, chunk)], rows_v.at[slot],
                                         lsem.at[slot])

        def put(c, slot, kk):
            return pltpu.make_async_copy(rows_v.at[slot], out_hbm.at[idx_v.at[kk, c]], ssem.at[slot])

        load(0, 0).start()

        @pl.loop(0, nchunk, step=2)
        def _(c):
            for b in range(2):
                cc = c + b
                load(cc, b).wait()

                @pl.when(cc + 1 < nchunk)
                def _():
                    @pl.when(cc >= 1)
                    def _():
                        put(cc - 1, 1 - b, 0).wait()
                        put(cc - 1, 1 - b, 1).wait()
                    load(cc + 1, 1 - b).start()

                put(cc, b, 0).start()
                put(cc, b, 1).start()

        for b in range(2):
            put(nchunk - 2 + b, b, 0).wait()
            put(nchunk - 2 + b, b, 1).wait()

    call = pl.kernel(
        body, mesh=_sc_mesh(),
        out_type=jax.ShapeDtypeStruct((rows, w), U32),
        scratch_types=[pltpu.VMEM((2, nchunk, chunk), I32), pltpu.VMEM((2, chunk, w), U32),
                       pltpu.SemaphoreType.DMA((2,)), pltpu.SemaphoreType.DMA((2,))],
        name="sc_dispatch")
    return call(xp, dest_kn.reshape(2, SC_WORKERS, nchunk, chunk))


def _sc_gather(table, idx):
    b = idx.shape[0]
    w = table.shape[1]
    b_per_w = b // SC_WORKERS
    chunk = _sc_chunk(b_per_w)
    nchunk = b_per_w // chunk

    def body(table_hbm, idx_hbm, out_hbm, idx_v, rows_v, gsem, osem):
        wid = lax.axis_index("s") * SC_CORES + lax.axis_index("c")
        base = wid * b_per_w
        pltpu.sync_copy(idx_hbm.at[wid], idx_v)

        def gather(c, slot):
            return pltpu.make_async_copy(table_hbm.at[idx_v.at[c]], rows_v.at[slot], gsem.at[slot])

        def put(c, slot):
            return pltpu.make_async_copy(rows_v.at[slot], out_hbm.at[pl.ds(base + c * chunk, chunk)],
                                         osem.at[slot])

        gather(0, 0).start()

        @pl.loop(0, nchunk, step=2)
        def _(c):
            for b in range(2):
                cc = c + b
                gather(cc, b).wait()

                @pl.when(cc + 1 < nchunk)
                def _():
                    @pl.when(cc >= 1)
                    def _():
                        put(cc - 1, 1 - b).wait()
                    gather(cc + 1, 1 - b).start()

                put(cc, b).start()

        put(nchunk - 2, 0).wait()
        put(nchunk - 1, 1).wait()

    call = pl.kernel(
        body, mesh=_sc_mesh(),
        out_type=jax.ShapeDtypeStruct((b, w), table.dtype),
        scratch_types=[pltpu.VMEM((nchunk, chunk), I32), pltpu.VMEM((2, chunk, w), table.dtype),
                       pltpu.SemaphoreType.DMA((2,)), pltpu.SemaphoreType.DMA((2,))],
        name="sc_gather")
    return call(table, idx.reshape(SC_WORKERS, nchunk, chunk))


def _expert_kernel(be_ref, nv_ref, nu_ref, xs_ref, wg_ref, wu_ref, wd_ref, y_ref, wgb, wub, wdb):
    i = pl.program_id(0)
    cur = jnp.minimum(i, nu_ref[0] - 1)
    new_expert = jnp.logical_or(i == 0, be_ref[cur] != be_ref[jnp.maximum(cur - 1, 0)])

    @pl.when(jnp.logical_and(i < nu_ref[0], new_expert))
    def _():
        wgb[...] = wg_ref[...].astype(BF16)
        wub[...] = wu_ref[...].astype(BF16)
        wdb[...] = wd_ref[...].astype(BF16)

    @pl.when(i < nu_ref[0])
    def _():
        row = lax.broadcasted_iota(I32, xs_ref.shape, 0)
        lo, hi = _unpack_pairs(jnp.where(row < nv_ref[i], xs_ref[...], jnp.uint32(0)))
        xb = jnp.concatenate([lo, hi], axis=1).astype(BF16)
        g = jnp.dot(xb, wgb[...], preferred_element_type=F32)
        u = jnp.dot(xb, wub[...], preferred_element_type=F32)
        hdn = (g * (1.0 / (1.0 + jnp.exp(-g))) * u).astype(BF16)
        y_ref[...] = _pack_pairs(jnp.dot(hdn, wdb[...], preferred_element_type=F32))

    @pl.when(i >= nu_ref[0])
    def _():
        y_ref[...] = jnp.zeros_like(y_ref)


def _experts(block_expert, block_valid, n_used, xs, wg, wu, wd):
    rows, w = xs.shape
    tb = MOE_TB
    blk = lambda i, be, nv, nu: (jnp.minimum(i, nu[0] - 1), 0)
    oblk = lambda i, be, nv, nu: (i, 0)
    wsel = lambda i, be, nv, nu: (be[jnp.minimum(i, nu[0] - 1)], 0, 0)
    return pl.pallas_call(
        _expert_kernel,
        grid_spec=pltpu.PrefetchScalarGridSpec(
            num_scalar_prefetch=3,
            grid=(rows // tb,),
            in_specs=[pl.BlockSpec((tb, w), blk),
                      pl.BlockSpec((None, D_MODEL, D_EXPERT), wsel),
                      pl.BlockSpec((None, D_MODEL, D_EXPERT), wsel),
                      pl.BlockSpec((None, D_EXPERT, D_MODEL), wsel)],
            out_specs=pl.BlockSpec((tb, w), oblk),
            scratch_shapes=[pltpu.VMEM((D_MODEL, D_EXPERT), BF16), pltpu.VMEM((D_MODEL, D_EXPERT), BF16),
                            pltpu.VMEM((D_EXPERT, D_MODEL), BF16)]),
        out_shape=jax.ShapeDtypeStruct((rows, w), U32),
        compiler_params=_params("arbitrary"),
        name="experts",
    )(block_expert, block_valid, n_used, xs, wg, wu, wd)


def _combine_kernel(ya_ref, yb_ref, x_ref, info_ref, lg_ref, lb_ref, out_ref, *, alpha):
    info = info_ref[...]
    g1 = info[:, 2:3]
    g2 = info[:, 3:4]
    a_lo, a_hi = _unpack_pairs(ya_ref[...])
    b_lo, b_hi = _unpack_pairs(yb_ref[...])
    half = D_MODEL // 2
    t_lo = alpha * x_ref[:, 0:half] + g1 * a_lo + g2 * b_lo
    t_hi = alpha * x_ref[:, half:] + g1 * a_hi + g2 * b_hi
    t = jnp.concatenate([t_lo, t_hi], axis=1)
    out_ref[...] = _layer_norm(t, lg_ref[...], lb_ref[...])


def _combine(y_tok, x1, info, ln_g, ln_b, alpha):
    n = x1.shape[0]
    tm = ROW_TM
    tok = lambda w: pl.BlockSpec((tm, w), lambda i: (i, 0))
    slot = lambda k: pl.BlockSpec((tm, D_MODEL // 2), lambda i: (i + k * (n // tm), 0))
    full = lambda shape: pl.BlockSpec(shape, lambda i: (0,) * len(shape))
    return pl.pallas_call(
        functools.partial(_combine_kernel, alpha=alpha),
        grid=(n // tm,),
        in_specs=[slot(0), slot(1), tok(D_MODEL), tok(LANES), full((1, D_MODEL)), full((1, D_MODEL))],
        out_specs=tok(D_MODEL),
        out_shape=jax.ShapeDtypeStruct((n, D_MODEL), F32),
        compiler_params=_params("arbitrary"),
        name="combine",
    )(y_tok, y_tok, x1, info, ln_g, ln_b)


def _rope_tables(s):
    half = HEAD_DIM // 2
    inv_freq = ROPE_THETA ** (-jnp.arange(half, dtype=F32) / half)
    ang = jnp.arange(s, dtype=F32)[:, None] * inv_freq[None, :]
    cos = jnp.cos(ang)
    sin = jnp.sin(ang)
    cos_t = jnp.tile(cos, (1, LANES // half))
    sin_t = jnp.tile(jnp.concatenate([-sin, sin], axis=1), (1, LANES // HEAD_DIM))
    return cos_t, sin_t


def _forward(x, w_in, w_out, branch_gain, sgu_gain, sgu_w, sgu_b, conv_w, ln_gain, ln_bias,
             router_group_w, router_group_b, router_expert_w, router_expert_b,
             expert_w_gate, expert_w_up, expert_w_down):
    b, s, _ = x.shape
    depth = w_in.shape[0]
    n = b * s
    alpha = (2.0 * depth) ** 0.25
    cos_t, sin_t = _rope_tables(s)
    perm, perm_t = _tile_perm()
    tb = MOE_TB
    n_blocks = (2 * n + N_EXPERTS * (tb - 1) + tb - 1) // tb
    rows = n_blocks * tb

    for l in range(depth):
        q, k, v, rest, q16, k16, v16 = _proj(x, w_in[l].astype(BF16), cos_t, sin_t, perm)
        g = branch_gain[l]
        ma = _attention(q, k, v, q16, k16, v16, perm_t, g[None, :ATTN_W])
        bias_tile = jnp.repeat(sgu_b[l].T, SGU_W // SGU_GROUPS, axis=1)
        mbc = _mixbc(rest, sgu_gain[l][None, :], sgu_w[l], bias_tile, conv_w[l], g[None, ATTN_W:])

        rw = jnp.zeros((D_MODEL, LANES), F32)
        rw = rw.at[:, :MOE_GROUPS].set(router_group_w[l])
        rw = rw.at[:, MOE_GROUPS:MOE_GROUPS + N_EXPERTS].set(router_expert_w[l])
        rw_hi = rw.astype(BF16)
        rw_hilo = jnp.concatenate([rw_hi, (rw - rw_hi.astype(F32)).astype(BF16)], axis=1)
        rbias = jnp.zeros((1, LANES), F32)
        rbias = rbias.at[0, :MOE_GROUPS].set(router_group_b[l])
        rbias = rbias.at[0, MOE_GROUPS:MOE_GROUPS + N_EXPERTS].set(router_expert_b[l])
        wo = w_out[l].astype(BF16)
        x1, xp, info, info_t, cnt = _outproj(
            ma.reshape(n, ATTN_W), mbc.reshape(n, SGU_W + CONV_W), x.reshape(n, D_MODEL),
            wo[:ATTN_W], wo[ATTN_W:], ln_gain[l, 0][None], ln_bias[l, 0][None], rw_hilo, rbias, alpha)

        counts = cnt[0, :N_EXPERTS].astype(I32)
        padded = (counts + tb - 1) // tb * tb
        pad_end = jnp.cumsum(padded)
        pad_start = pad_end - padded
        ex = info_t[0:2].astype(I32)
        start_of = jnp.zeros_like(ex)
        for e in range(N_EXPERTS):
            start_of = jnp.where(ex == e, pad_start[e], start_of)
        dest_kn = start_of + info_t[4:6].astype(I32)
        blk_row0 = jnp.arange(n_blocks, dtype=I32) * tb
        block_expert = jnp.minimum(jnp.sum((pad_end[None, :] <= blk_row0[:, None]).astype(I32), axis=1),
                                   N_EXPERTS - 1)
        block_valid = jnp.clip(pad_start[block_expert] + counts[block_expert] - blk_row0, 0, tb)
        n_used = (pad_end[-1:] // tb).astype(I32)

        xs = _sc_dispatch(xp, dest_kn, rows)
        ys = _experts(block_expert, block_valid, n_used, xs, expert_w_gate[l], expert_w_up[l], expert_w_down[l])
        y_tok = _sc_gather(ys, dest_kn.reshape(2 * n))
        x = _combine(y_tok, x1, info, ln_gain[l, 1][None], ln_bias[l, 1][None], alpha).reshape(b, s, D_MODEL)
    return x


def kernel(x, w_in, w_out, branch_gain, sgu_gain, sgu_w, sgu_b, conv_w, ln_gain, ln_bias, router_group_w, router_group_b, router_expert_w, router_expert_b, expert_w_gate, expert_w_up, expert_w_down):
    return _forward(x, w_in, w_out, branch_gain, sgu_gain, sgu_w, sgu_b, conv_w, ln_gain, ln_bias,
                    router_group_w, router_group_b, router_expert_w, router_expert_b,
                    expert_w_gate, expert_w_up, expert_w_down)
```

```python
import functools
import math

import jax
import jax.numpy as jnp
from jax import lax
from jax.experimental import pallas as pl
from jax.experimental.pallas import tpu as pltpu
from jax.experimental.pallas import tpu_sc as plsc

F32 = jnp.float32
BF16 = jnp.bfloat16
U32 = jnp.uint32
I32 = jnp.int32

D_MODEL = 1024
HEAD_DIM = 64
ATTN_W = 512
N_HEADS = 8
SGU_W = 256
SGU_GROUPS = 4
SGU_CHUNK = 128
CONV_W = 256
REST_W = 2 * SGU_W + 3 * CONV_W
PROJ_W = 3 * ATTN_W + REST_W
DILATIONS = (16, 4, 1)
BAND = 128
ROPE_THETA = 10000.0
MOE_GROUPS = 4
EXPERTS_PER_GROUP = 8
N_EXPERTS = MOE_GROUPS * EXPERTS_PER_GROUP
D_EXPERT = 512
EPS = 1e-5
NEG = -1e30

LANES = 128
VMEM_LIMIT = 56 * 1024 * 1024

TILE = 512
PERM_D = 16
ATTN_TQ = 512
MIX_TM = 512
OUT_TM = 512
MOE_TB = 512
ROW_TM = 512
SC_CORES = 2
SC_SUBCORES = 16
SC_WORKERS = SC_CORES * SC_SUBCORES
SC_CHUNK = 64


def _params(*sem):
    return pltpu.CompilerParams(dimension_semantics=sem, vmem_limit_bytes=VMEM_LIMIT)


def _pack_pairs(x):
    w = x.shape[1] // 2
    lo = lax.bitcast_convert_type(x[:, :w].astype(BF16).astype(F32), U32)
    hi = lax.bitcast_convert_type(x[:, w:].astype(BF16).astype(F32), U32)
    return (lo >> 16) | (hi & jnp.uint32(0xFFFF0000))


def _unpack_pairs(p):
    lo = lax.bitcast_convert_type(p << 16, F32)
    hi = lax.bitcast_convert_type(p & jnp.uint32(0xFFFF0000), F32)
    return lo, hi


def _tile_perm():
    i = jnp.arange(TILE, dtype=I32)
    tok = PERM_D * (i % (TILE // PERM_D)) + i // (TILE // PERM_D)
    p = (jnp.arange(TILE, dtype=I32)[None, :] == tok[:, None]).astype(BF16)
    return p, p.T


def _proj_kernel(x_ref, w_ref, cos_ref, sin_ref, p_ref, q_ref, k_ref, v_ref, r_ref, q16_ref, k16_ref, v16_ref):
    xb = x_ref[...].astype(BF16)
    cos = cos_ref[...]
    sin = sin_ref[...]
    lane = lax.broadcasted_iota(I32, cos.shape, 1)
    first_half = (lane % HEAD_DIM) < (HEAD_DIM // 2)
    rows = TILE // PERM_D

    def store_both(val_bf, out_ref, out16_ref):
        out_ref[...] = val_bf
        perm = jnp.dot(p_ref[...], val_bf, preferred_element_type=F32).astype(BF16)
        for r in range(PERM_D):
            out16_ref[:, r * ATTN_W:(r + 1) * ATTN_W] = perm[r * rows:(r + 1) * rows, :]

    def rope(col0, scale):
        t = jnp.dot(xb, w_ref[:, col0:col0 + ATTN_W], preferred_element_type=F32)
        out = []
        for c in range(ATTN_W // LANES):
            tc = t[:, c * LANES:(c + 1) * LANES]
            partner = jnp.where(first_half, pltpu.roll(tc, LANES - 32, 1), pltpu.roll(tc, 32, 1))
            out.append(((tc * cos + partner * sin) * scale).astype(BF16))
        return jnp.concatenate(out, axis=1)

    store_both(rope(0, HEAD_DIM ** -0.5), q_ref, q16_ref)
    store_both(rope(ATTN_W, 1.0), k_ref, k16_ref)
    vb = jnp.dot(xb, w_ref[:, 2 * ATTN_W:3 * ATTN_W], preferred_element_type=F32).astype(BF16)
    store_both(vb, v_ref, v16_ref)
    r_ref[...] = jnp.dot(xb, w_ref[:, 3 * ATTN_W:], preferred_element_type=F32).astype(BF16)


def _proj(x, w_bf, cos_t, sin_t, perm):
    b, s, _ = x.shape
    tm = TILE
    out3 = jax.ShapeDtypeStruct((b, s, ATTN_W), BF16)
    out16 = jax.ShapeDtypeStruct((b, s // PERM_D, PERM_D * ATTN_W), BF16)
    tok = lambda width: pl.BlockSpec((None, tm, width), lambda si, bi: (bi, si, 0))
    tok16 = pl.BlockSpec((None, tm // PERM_D, PERM_D * ATTN_W), lambda si, bi: (bi, si, 0))
    return pl.pallas_call(
        _proj_kernel,
        grid=(s // tm, b),
        in_specs=[tok(D_MODEL),
                  pl.BlockSpec((D_MODEL, PROJ_W), lambda si, bi: (0, 0)),
                  pl.BlockSpec((tm, LANES), lambda si, bi: (si, 0)),
                  pl.BlockSpec((tm, LANES), lambda si, bi: (si, 0)),
                  pl.BlockSpec((tm, tm), lambda si, bi: (0, 0))],
        out_specs=[tok(ATTN_W), tok(ATTN_W), tok(ATTN_W), tok(REST_W), tok16, tok16, tok16],
        out_shape=[out3, out3, out3, jax.ShapeDtypeStruct((b, s, REST_W), BF16), out16, out16, out16],
        compiler_params=_params("arbitrary", "arbitrary"),
        name="proj",
    )(x, w_bf, cos_t, sin_t, perm)


def _band_bias(permuted):
    row = lax.broadcasted_iota(I32, (BAND, 2 * BAND), 0)
    col = lax.broadcasted_iota(I32, (BAND, 2 * BAND), 1)
    key = col % BAND
    if permuted:
        row = 4 * (row % 32) + row // 32
        key = 4 * (key % 32) + key // 32
    lo_key = jnp.where(col < BAND, row, 0)
    hi_key = jnp.where(col < BAND, BAND - 1, row)
    return jnp.where(jnp.logical_and(key >= lo_key, key <= hi_key), 0.0, NEG).astype(F32), col


def _attn_heads(get_q, get_k, get_v, bias, emit):
    lane = lax.broadcasted_iota(I32, (BAND, LANES), 1)
    lane_lo = lane < HEAD_DIM
    for p in range(ATTN_W // LANES):
        ql, kk, vv = get_q(p), get_k(p), get_v(p)
        o_h, l_h = [], []
        for hh in range(2):
            qm = jnp.where(lane_lo if hh == 0 else jnp.logical_not(lane_lo), ql, jnp.zeros_like(ql))
            sc = lax.dot_general(qm, kk, (((1,), (1,)), ((), ())), preferred_element_type=F32)
            sc = sc + bias
            mx = jnp.max(sc, axis=1, keepdims=True)
            pe = jnp.exp(sc - mx)
            den = jnp.sum(pe, axis=1, keepdims=True)
            o_h.append(jnp.dot(pe.astype(BF16), vv, preferred_element_type=F32) / den)
            l_h.append(mx + jnp.log(den))
        emit(p, jnp.where(lane_lo, o_h[0], o_h[1]), l_h[0], l_h[1], lane, lane_lo)


def _three_terms(st):
    hi = st.astype(BF16).astype(F32)
    rest = st - hi
    mid = rest.astype(BF16).astype(F32)
    lo = (rest - mid).astype(BF16).astype(F32)
    return hi + pltpu.roll(mid, 8, 1) + pltpu.roll(lo, 16, 1)


def _attn16_kernel(q_ref, k_ref, v_ref, o_ref, st_ref, kbuf, vbuf, *, tq):
    i = pl.program_id(2)

    @pl.when(i == 0)
    def _():
        kbuf[0:BAND, :] = jnp.zeros((BAND, ATTN_W), BF16)
        vbuf[0:BAND, :] = jnp.zeros((BAND, ATTN_W), BF16)

    kbuf[BAND:BAND + tq, :] = k_ref[...]
    vbuf[BAND:BAND + tq, :] = v_ref[...]
    band_bias, col = _band_bias(False)

    def block(j, carry):
        r0 = pl.multiple_of(j * BAND, BAND)
        first_col = jnp.where(jnp.logical_or(j > 0, i > 0), 0, BAND)
        bias = jnp.where(col >= first_col, band_bias, NEG)
        st = [jnp.zeros((BAND, LANES), F32)]

        def emit(p, o_pair, l0, l1, lane, lane_lo):
            o_ref[pl.ds(r0, BAND), p * LANES:(p + 1) * LANES] = o_pair.astype(BF16)
            st[0] = jnp.where(lane == 2 * p, l0, jnp.where(lane == 2 * p + 1, l1, st[0]))

        _attn_heads(lambda p: q_ref[pl.ds(r0, BAND), p * LANES:(p + 1) * LANES],
                    lambda p: kbuf[pl.ds(r0, 2 * BAND), p * LANES:(p + 1) * LANES],
                    lambda p: vbuf[pl.ds(r0, 2 * BAND), p * LANES:(p + 1) * LANES],
                    bias, emit)
        st_ref[pl.ds(r0, BAND), :] = _three_terms(st[0])
        return carry

    lax.fori_loop(0, tq // BAND, block, 0, unroll=True)
    kbuf[0:BAND, :] = kbuf[tq:tq + BAND, :]
    vbuf[0:BAND, :] = vbuf[tq:tq + BAND, :]


def _attn4_kernel(q_ref, k_ref, v_ref, o_ref, st_ref, qbuf, kbuf, vbuf, obuf, sbuf):
    i = pl.program_id(1)
    rows = TILE // PERM_D
    nres = 4

    @pl.when(i == 0)
    def _():
        kbuf[:, 0:BAND, :] = jnp.zeros((nres, BAND, ATTN_W), BF16)
        vbuf[:, 0:BAND, :] = jnp.zeros((nres, BAND, ATTN_W), BF16)

    @pl.when(i > 0)
    def _():
        kbuf[:, 0:BAND, :] = kbuf[:, BAND:2 * BAND, :]
        vbuf[:, 0:BAND, :] = vbuf[:, BAND:2 * BAND, :]

    for r4 in range(nres):
        for g in range(PERM_D // nres):
            lanes = slice((r4 + nres * g) * ATTN_W, (r4 + nres * g + 1) * ATTN_W)
            qbuf[r4, g * rows:(g + 1) * rows, :] = q_ref[:, lanes]
            kbuf[r4, BAND + g * rows:BAND + (g + 1) * rows, :] = k_ref[:, lanes]
            vbuf[r4, BAND + g * rows:BAND + (g + 1) * rows, :] = v_ref[:, lanes]

    band_bias, col = _band_bias(True)
    first_col = jnp.where(i > 0, 0, BAND)
    bias = jnp.where(col >= first_col, band_bias, NEG)

    def block(j, carry):
        st = [jnp.zeros((BAND, LANES), F32)]

        def emit(p, o_pair, l0, l1, lane, lane_lo):
            obuf[j, :, p * LANES:(p + 1) * LANES] = o_pair.astype(BF16)
            st[0] = jnp.where(lane == 2 * p, l0, jnp.where(lane == 2 * p + 1, l1, st[0]))

        _attn_heads(lambda p: qbuf[j, :, p * LANES:(p + 1) * LANES],
                    lambda p: kbuf[j, :, p * LANES:(p + 1) * LANES],
                    lambda p: vbuf[j, :, p * LANES:(p + 1) * LANES],
                    bias, emit)
        sbuf[j] = _three_terms(st[0])
        return carry

    lax.fori_loop(0, nres, block, 0, unroll=True)

    for r4 in range(nres):
        for g in range(PERM_D // nres):
            grp = r4 + nres * g
            o_ref[:, grp * ATTN_W:(grp + 1) * ATTN_W] = obuf[r4, g * rows:(g + 1) * rows, :]
            st_ref[:, grp * LANES:(grp + 1) * LANES] = sbuf[r4, g * rows:(g + 1) * rows, :]


def _attn1_kernel(q_ref, k_ref, v_ref, o4_ref, s4_ref, o16_ref, s16_ref, pt_ref, g_ref, out_ref,
                  kbuf, vbuf, acc, o4t, o16t, stt, s1t, *, tq):
    i = pl.program_id(1)

    @pl.when(i == 0)
    def _():
        kbuf[0:BAND, :] = jnp.zeros((BAND, ATTN_W), BF16)
        vbuf[0:BAND, :] = jnp.zeros((BAND, ATTN_W), BF16)

    kbuf[BAND:BAND + tq, :] = k_ref[...]
    vbuf[BAND:BAND + tq, :] = v_ref[...]

    def rows_of(ref, w):
        return jnp.concatenate([ref[:, r * w:(r + 1) * w] for r in range(PERM_D)], axis=0)

    o4t[...] = jnp.dot(pt_ref[...], rows_of(o4_ref, ATTN_W), preferred_element_type=F32).astype(BF16)
    o16t[...] = jnp.dot(pt_ref[...], rows_of(o16_ref, ATTN_W), preferred_element_type=F32).astype(BF16)
    terms = (rows_of(s4_ref, LANES) + pltpu.roll(rows_of(s16_ref, LANES), 32, 1)).astype(BF16)
    terms = jnp.dot(pt_ref[...], terms, preferred_element_type=F32)
    stt[...] = terms + pltpu.roll(terms, LANES - 8, 1) + pltpu.roll(terms, LANES - 16, 1)
    band_bias, col = _band_bias(False)

    def block(j, carry):
        r0 = pl.multiple_of(j * BAND, BAND)
        first_col = jnp.where(jnp.logical_or(j > 0, i > 0), 0, BAND)
        bias = jnp.where(col >= first_col, band_bias, NEG)
        st = [jnp.zeros((BAND, LANES), F32)]

        def emit(p, o_pair, l0, l1, lane, lane_lo):
            acc[pl.ds(r0, BAND), p * LANES:(p + 1) * LANES] = o_pair
            st[0] = jnp.where(lane == 2 * p, l0, jnp.where(lane == 2 * p + 1, l1, st[0]))

        _attn_heads(lambda p: q_ref[pl.ds(r0, BAND), p * LANES:(p + 1) * LANES],
                    lambda p: kbuf[pl.ds(r0, 2 * BAND), p * LANES:(p + 1) * LANES],
                    lambda p: vbuf[pl.ds(r0, 2 * BAND), p * LANES:(p + 1) * LANES],
                    bias, emit)
        s1t[pl.ds(r0, BAND), :] = st[0]
        return carry

    lax.fori_loop(0, tq // BAND, block, 0, unroll=True)
    kbuf[0:BAND, :] = kbuf[tq:tq + BAND, :]
    vbuf[0:BAND, :] = vbuf[tq:tq + BAND, :]

    head_of = lax.broadcasted_iota(I32, (LANES, ATTN_W), 1) // HEAD_DIM
    expand = jnp.where(lax.broadcasted_iota(I32, (LANES, ATTN_W), 0) == head_of, 1.0, 0.0).astype(BF16)

    def merge(j, carry):
        r0 = pl.multiple_of(j * BAND, BAND)
        is_head = lax.broadcasted_iota(I32, (BAND, LANES), 1) < N_HEADS
        l1 = s1t[pl.ds(r0, BAND), :]
        l4 = stt[pl.ds(r0, BAND), :]
        l16 = pltpu.roll(l4, LANES - 32, 1)
        top = jnp.maximum(l1, jnp.maximum(l4, l16))
        e1 = jnp.exp(l1 - top)
        e4 = jnp.exp(l4 - top)
        e16 = jnp.exp(l16 - top)
        inv = 1.0 / (e1 + e4 + e16)

        def spread(e):
            w = jnp.where(is_head, e * inv, 0.0)
            hi = w.astype(BF16)
            lo = (w - hi.astype(F32)).astype(BF16)
            return (jnp.dot(hi, expand, preferred_element_type=F32)
                    + jnp.dot(lo, expand, preferred_element_type=F32))

        y = (spread(e1) * acc[pl.ds(r0, BAND), :]
             + spread(e4) * o4t[pl.ds(r0, BAND), :].astype(F32)
             + spread(e16) * o16t[pl.ds(r0, BAND), :].astype(F32))
        scale = lax.rsqrt(jnp.mean(y * y, axis=1, keepdims=True) + EPS)
        out_ref[pl.ds(r0, BAND), :] = (y * scale * g_ref[...]).astype(BF16)
        return carry

    lax.fori_loop(0, tq // BAND, merge, 0, unroll=True)


def _attention(q, k, v, q16, k16, v16, perm_t, gain):
    b, s, _ = q.shape
    m16 = s // PERM_D
    rows = TILE // PERM_D
    o16_shape = [jax.ShapeDtypeStruct((b, m16, PERM_D * ATTN_W), BF16),
                 jax.ShapeDtypeStruct((b, m16, PERM_D * LANES), F32)]

    tq16 = min(ATTN_TQ, m16)
    blk16 = lambda w: pl.BlockSpec((None, tq16, w), lambda bi, ri, ii: (bi, ii, ri))
    o16, s16 = pl.pallas_call(
        functools.partial(_attn16_kernel, tq=tq16),
        grid=(b, PERM_D, m16 // tq16),
        in_specs=[blk16(ATTN_W)] * 3, out_specs=[blk16(ATTN_W), blk16(LANES)], out_shape=o16_shape,
        scratch_shapes=[pltpu.VMEM((tq16 + BAND, ATTN_W), BF16)] * 2,
        compiler_params=_params("arbitrary", "arbitrary", "arbitrary"),
        name="attn_d16",
    )(q16, k16, v16)

    tile = lambda w: pl.BlockSpec((None, rows, PERM_D * w), lambda bi, ii: (bi, ii, 0))
    o4, s4 = pl.pallas_call(
        _attn4_kernel,
        grid=(b, s // TILE),
        in_specs=[tile(ATTN_W)] * 3, out_specs=[tile(ATTN_W), tile(LANES)], out_shape=o16_shape,
        scratch_shapes=[pltpu.VMEM((4, BAND, ATTN_W), BF16), pltpu.VMEM((4, 2 * BAND, ATTN_W), BF16),
                        pltpu.VMEM((4, 2 * BAND, ATTN_W), BF16), pltpu.VMEM((4, BAND, ATTN_W), BF16),
                        pltpu.VMEM((4, BAND, LANES), F32)],
        compiler_params=_params("arbitrary", "arbitrary"),
        name="attn_d4",
    )(q16, k16, v16)

    tq = TILE
    tok = lambda w: pl.BlockSpec((None, tq, w), lambda bi, ii: (bi, ii, 0))
    full = lambda shape: pl.BlockSpec(shape, lambda bi, ii: (0,) * len(shape))
    return pl.pallas_call(
        functools.partial(_attn1_kernel, tq=tq),
        grid=(b, s // tq),
        in_specs=[tok(ATTN_W)] * 3 + [tile(ATTN_W), tile(LANES), tile(ATTN_W), tile(LANES),
                                      full((TILE, TILE)), full((1, ATTN_W))],
        out_specs=tok(ATTN_W),
        out_shape=jax.ShapeDtypeStruct((b, s, ATTN_W), BF16),
        scratch_shapes=[pltpu.VMEM((tq + BAND, ATTN_W), BF16), pltpu.VMEM((tq + BAND, ATTN_W), BF16),
                        pltpu.VMEM((tq, ATTN_W), F32),
                        pltpu.VMEM((tq, ATTN_W), BF16), pltpu.VMEM((tq, ATTN_W), BF16),
                        pltpu.VMEM((tq, LANES), F32), pltpu.VMEM((tq, LANES), F32)],
        compiler_params=_params("arbitrary", "arbitrary"),
        name="attn_d1",
    )(q, k, v, o4, s4, o16, s16, perm_t, gain)


def _gelu_tanh(x):
    c = math.sqrt(2.0 / math.pi)
    return x * (0.5 * (1.0 + jnp.tanh(c * (x + 0.044715 * (x * x * x)))))


def _split_dot(x, m_bf):
    hi = x.astype(BF16)
    lo = (x - hi.astype(F32)).astype(BF16)
    return (jnp.dot(hi, m_bf, preferred_element_type=F32)
            + jnp.dot(lo, m_bf, preferred_element_type=F32))


def _mixbc_kernel(r_ref, halo_ref, sg_ref, sw_ref, sb_ref, cw_ref, bg_ref, out_ref, *, tm):
    si = pl.program_id(1)
    u = r_ref[:, 0:SGU_W].astype(F32)
    z = r_ref[:, SGU_W:2 * SGU_W].astype(F32)
    gb = r_ref[:, 2 * SGU_W:2 * SGU_W + CONV_W].astype(F32)
    gc = r_ref[:, 2 * SGU_W + CONV_W:2 * SGU_W + 2 * CONV_W].astype(F32)
    hh = r_ref[:, 2 * SGU_W + 2 * CONV_W:].astype(F32)

    gdim = SGU_W // SGU_GROUPS
    ri = lax.broadcasted_iota(I32, (SGU_W, SGU_W), 0) // gdim
    ci = lax.broadcasted_iota(I32, (SGU_W, SGU_W), 1) // gdim
    avg = jnp.where(ri == ci, 1.0 / gdim, 0.0).astype(BF16)
    z = _gelu_tanh(z)
    zc = z - _split_dot(z, avg)
    var = _split_dot(zc * zc, avg)
    zn = (zc * lax.rsqrt(var + EPS) * sg_ref[...]).astype(BF16)

    tr = lax.broadcasted_iota(I32, (SGU_CHUNK, SGU_CHUNK), 0)
    tc = lax.broadcasted_iota(I32, (SGU_CHUNK, SGU_CHUNK), 1)
    w_cat = jnp.concatenate(
        [jnp.where(tc <= tr, sw_ref[g], 0.0).astype(BF16) for g in range(SGU_GROUPS)], axis=1)
    lane_g = lax.broadcasted_iota(I32, (SGU_CHUNK, SGU_W), 1) // gdim
    gu = _gelu_tanh(u)
    bias = sb_ref[...]
    yb = []
    for c in range(tm // SGU_CHUNK):
        zch = zn[c * SGU_CHUNK:(c + 1) * SGU_CHUNK, :]
        stack = jnp.concatenate(
            [jnp.where(lane_g == g, zch, jnp.zeros_like(zch)) for g in range(SGU_GROUPS)], axis=0)
        sp = jnp.dot(w_cat, stack, preferred_element_type=F32) + bias
        yb.append(gu[c * SGU_CHUNK:(c + 1) * SGU_CHUNK, :] * sp)
    yb = jnp.concatenate(yb, axis=0)

    zz = gc * hh
    hrows = halo_ref.shape[0]
    prev = (halo_ref[:, 2 * SGU_W + CONV_W:2 * SGU_W + 2 * CONV_W].astype(F32)
            * halo_ref[:, 2 * SGU_W + 2 * CONV_W:].astype(F32))
    prev = prev * (si > 0).astype(F32)
    ext = jnp.concatenate([prev, zz], axis=0)
    z1 = ext[hrows - 1:hrows - 1 + tm, :]
    z2 = ext[hrows - 2:hrows - 2 + tm, :]
    yc = gb * (cw_ref[0:1, :] * z2 + cw_ref[1:2, :] * z1 + cw_ref[2:3, :] * zz)

    def rms(t, g):
        return t * lax.rsqrt(jnp.mean(t * t, axis=1, keepdims=True) + EPS) * g

    out_ref[:, 0:SGU_W] = rms(yb, bg_ref[:, 0:SGU_W]).astype(BF16)
    out_ref[:, SGU_W:] = rms(yc, bg_ref[:, SGU_W:]).astype(BF16)


def _mixbc(rest, sgu_gain, sgu_w, sgu_bias_tile, conv_w, gain_bc):
    b, s, _ = rest.shape
    tm = MIX_TM
    hrows = 16
    full = lambda shape: pl.BlockSpec(shape, lambda bi, si: (0,) * len(shape))
    return pl.pallas_call(
        functools.partial(_mixbc_kernel, tm=tm),
        grid=(b, s // tm),
        in_specs=[pl.BlockSpec((None, tm, REST_W), lambda bi, si: (bi, si, 0)),
                  pl.BlockSpec((None, hrows, REST_W),
                               lambda bi, si: (bi, jnp.maximum(si * (tm // hrows) - 1, 0), 0)),
                  full((1, SGU_W)), full((SGU_GROUPS, SGU_CHUNK, SGU_CHUNK)),
                  full((SGU_CHUNK, SGU_W)), full((3, CONV_W)), full((1, SGU_W + CONV_W))],
        out_specs=pl.BlockSpec((None, tm, SGU_W + CONV_W), lambda bi, si: (bi, si, 0)),
        out_shape=jax.ShapeDtypeStruct((b, s, SGU_W + CONV_W), BF16),
        compiler_params=_params("arbitrary", "arbitrary"),
        name="mixbc",
    )(rest, rest, sgu_gain, sgu_w, sgu_bias_tile, conv_w, gain_bc)


def _layer_norm(t, g, b):
    mu = jnp.mean(t, axis=1, keepdims=True)
    tc = t - mu
    var = jnp.mean(tc * tc, axis=1, keepdims=True)
    return tc * lax.rsqrt(var + EPS) * g + b


def _outproj_kernel(ma_ref, mb_ref, x_ref, wa_ref, wb_ref, lg_ref, lb_ref, rw_ref, rb_ref,
                    x1_ref, xp_ref, info_ref, infot_ref, cnt_ref, run_ref, *, tm, alpha):
    i = pl.program_id(0)

    @pl.when(i == 0)
    def _():
        run_ref[...] = jnp.zeros_like(run_ref)

    y = (jnp.dot(ma_ref[...], wa_ref[...], preferred_element_type=F32)
         + jnp.dot(mb_ref[...], wb_ref[...], preferred_element_type=F32))
    x1 = _layer_norm(alpha * x_ref[...] + y, lg_ref[...], lb_ref[...])
    x1_ref[...] = x1
    xp_ref[...] = _pack_pairs(x1)

    hi = x1.astype(BF16)
    lo = (x1 - hi.astype(F32)).astype(BF16)
    both = jnp.dot(hi, rw_ref[...], preferred_element_type=F32)
    logit = (both[:, :LANES] + both[:, LANES:]
             + jnp.dot(lo, rw_ref[:, :LANES], preferred_element_type=F32)) + rb_ref[...]
    lane = lax.broadcasted_iota(I32, (tm, LANES), 1)
    lane_f = lane.astype(F32)

    def top(mask):
        v = jnp.max(jnp.where(mask, logit, NEG), axis=1, keepdims=True)
        first = jnp.min(jnp.where(jnp.logical_and(mask, logit == v), lane_f, float(LANES)),
                        axis=1, keepdims=True)
        return v, first.astype(I32)

    is_g = lane < MOE_GROUPS
    gmax, gidx = top(is_g)
    g_p = 1.0 / jnp.sum(jnp.where(is_g, jnp.exp(logit - gmax), 0.0), axis=1, keepdims=True)
    in_grp = jnp.logical_and(lane >= MOE_GROUPS + gidx * EXPERTS_PER_GROUP,
                             lane < MOE_GROUPS + (gidx + 1) * EXPERTS_PER_GROUP)
    v1, i1 = top(in_grp)
    v2, i2 = top(jnp.logical_and(in_grp, lane != i1))
    e21 = jnp.exp(v2 - v1)
    gate1 = g_p / (1.0 + e21)
    gate2 = g_p * e21 / (1.0 + e21)
    ex1 = i1 - MOE_GROUPS
    ex2 = i2 - MOE_GROUPS

    oh1 = lane == ex1
    oh2 = lane == ex2
    oh = (oh1.astype(F32) + oh2.astype(F32))
    tr = lax.broadcasted_iota(I32, (tm, tm), 0)
    tc = lax.broadcasted_iota(I32, (tm, tm), 1)
    lower = jnp.where(tc < tr, 1.0, 0.0).astype(BF16)
    before = jnp.dot(lower, oh.astype(BF16), preferred_element_type=F32) + run_ref[0:1, :]
    rank1 = jnp.sum(jnp.where(oh1, before, 0.0), axis=1, keepdims=True)
    rank2 = jnp.sum(jnp.where(oh2, before, 0.0), axis=1, keepdims=True)
    run_new = run_ref[0:1, :] + jnp.sum(oh, axis=0, keepdims=True)
    run_ref[...] = jnp.broadcast_to(run_new, run_ref.shape)
    cnt_ref[...] = jnp.broadcast_to(run_new, cnt_ref.shape)

    info = jnp.where(lane == 0, ex1.astype(F32), 0.0)
    info = jnp.where(lane == 1, ex2.astype(F32), info)
    info = jnp.where(lane == 2, gate1, info)
    info = jnp.where(lane == 3, gate2, info)
    info = jnp.where(lane == 4, rank1, info)
    info = jnp.where(lane == 5, rank2, info)
    info_ref[...] = info
    infot_ref[...] = info.T[0:8, :]


def _outproj(ma, mbc, x, wo_a, wo_b, ln_g, ln_b, rw_hilo, rbias, alpha):
    n = x.shape[0]
    tm = OUT_TM
    tok = lambda w: pl.BlockSpec((tm, w), lambda i: (i, 0))
    full = lambda shape: pl.BlockSpec(shape, lambda i: (0,) * len(shape))
    return pl.pallas_call(
        functools.partial(_outproj_kernel, tm=tm, alpha=alpha),
        grid=(n // tm,),
        in_specs=[tok(ATTN_W), tok(SGU_W + CONV_W), tok(D_MODEL),
                  full((ATTN_W, D_MODEL)), full((SGU_W + CONV_W, D_MODEL)),
                  full((1, D_MODEL)), full((1, D_MODEL)),
                  full((D_MODEL, 2 * LANES)), full((1, LANES))],
        out_specs=[tok(D_MODEL), tok(D_MODEL // 2), tok(LANES), pl.BlockSpec((8, tm), lambda i: (0, i)),
                   full((8, LANES))],
        out_shape=[jax.ShapeDtypeStruct((n, D_MODEL), F32),
                   jax.ShapeDtypeStruct((n, D_MODEL // 2), U32),
                   jax.ShapeDtypeStruct((n, LANES), F32),
                   jax.ShapeDtypeStruct((8, n), F32),
                   jax.ShapeDtypeStruct((8, LANES), F32)],
        scratch_shapes=[pltpu.VMEM((8, LANES), F32)],
        compiler_params=_params("arbitrary"),
        name="outproj",
    )(ma, mbc, x, wo_a, wo_b, ln_g, ln_b, rw_hilo, rbias)


def _sc_mesh():
    return plsc.VectorSubcoreMesh(core_axis_name="c", subcore_axis_name="s",
                                  num_cores=SC_CORES, num_subcores=SC_SUBCORES)


def _sc_chunk(rows_per_worker):
    return min(SC_CHUNK, rows_per_worker // 2)


def _sc_dispatch(xp, dest_kn, rows):
    n, w = xp.shape
    t_per_w = n // SC_WORKERS
    chunk = _sc_chunk(t_per_w)
    nchunk = t_per_w // chunk

    def body(src_hbm, dest_hbm, out_hbm, idx_v, rows_v, lsem, ssem):
        wid = lax.axis_index("s") * SC_CORES + lax.axis_index("c")
        base = wid * t_per_w
        pltpu.sync_copy(dest_hbm.at[0, wid], idx_v.at[0])
        pltpu.sync_copy(dest_hbm.at[1, wid], idx_v.at[1])

        def load(c, slot):
            return pltpu.make_async_copy(src_hbm.at[pl.ds(base + c * chunk, chunk)], rows_v.at[slot],
                                         lsem.at[slot])

        def put(c, slot, kk):
            return pltpu.make_async_copy(rows_v.at[slot], out_hbm.at[idx_v.at[kk, c]], ssem.at[slot])

        load(0, 0).start()

        @pl.loop(0, nchunk, step=2)
        def _(c):
            for b in range(2):
                cc = c + b
                load(cc, b).wait()

                @pl.when(cc + 1 < nchunk)
                def _():
                    @pl.when(cc >= 1)
                    def _():
                        put(cc - 1, 1 - b, 0).wait()
                        put(cc - 1, 1 - b, 1).wait()
                    load(cc + 1, 1 - b).start()

                put(cc, b, 0).start()
                put(cc, b, 1).start()

        for b in range(2):
            put(nchunk - 2 + b, b, 0).wait()
            put(nchunk - 2 + b, b, 1).wait()

    call = pl.kernel(
        body, mesh=_sc_mesh(),
        out_type=jax.ShapeDtypeStruct((rows, w), U32),
        scratch_types=[pltpu.VMEM((2, nchunk, chunk), I32), pltpu.VMEM((2, chunk, w), U32),
                       pltpu.SemaphoreType.DMA((2,)), pltpu.SemaphoreType.DMA((2,))],
        name="sc_dispatch")
    return call(xp, dest_kn.reshape(2, SC_WORKERS, nchunk, chunk))


def _sc_gather(table, idx):
    b = idx.shape[0]
    w = table.shape[1]
    b_per_w = b // SC_WORKERS
    chunk = _sc_chunk(b_per_w)
    nchunk = b_per_w // chunk

    def body(table_hbm, idx_hbm, out_hbm, idx_v, rows_v, gsem, osem):
        wid = lax.axis_index("s") * SC_CORES + lax.axis_index("c")
        base = wid * b_per_w
        pltpu.sync_copy(idx_hbm.at[wid], idx_v)

        def gather(c, slot):
            return pltpu.make_async_copy(table_hbm.at[idx_v.at[c]], rows_v.at[slot], gsem.at[slot])

        def put(c, slot):
            return pltpu.make_async_copy(rows_v.at[slot], out_hbm.at[pl.ds(base + c * chunk, chunk)],
                                         osem.at[slot])

        gather(0, 0).start()

        @pl.loop(0, nchunk, step=2)
        def _(c):
            for b in range(2):
                cc = c + b
                gather(cc, b).wait()

                @pl.when(cc + 1 < nchunk)
                def _():
                    @pl.when(cc >= 1)
                    def _():
                        put(cc - 1, 1 - b).wait()
                    gather(cc + 1, 1 - b).start()

                put(cc, b).start()

        put(nchunk - 2, 0).wait()
        put(nchunk - 1, 1).wait()

    call = pl.kernel(
        body, mesh=_sc_mesh(),
        out_type=jax.ShapeDtypeStruct((b, w), table.dtype),
        scratch_types=[pltpu.VMEM((nchunk, chunk), I32), pltpu.VMEM((2, chunk, w), table.dtype),
                       pltpu.SemaphoreType.DMA((2,)), pltpu.SemaphoreType.DMA((2,))],
        name="sc_gather")
    return call(table, idx.reshape(SC_WORKERS, nchunk, chunk))


def _expert_kernel(be_ref, nv_ref, nu_ref, xs_ref, wg_ref, wu_ref, wd_ref, y_ref, wgb, wub, wdb):
    i = pl.program_id(0)
    cur = jnp.minimum(i, nu_ref[0] - 1)
    new_expert = jnp.logical_or(i == 0, be_ref[cur] != be_ref[jnp.maximum(cur - 1, 0)])

    @pl.when(jnp.logical_and(i < nu_ref[0], new_expert))
    def _():
        wgb[...] = wg_ref[...].astype(BF16)
        wub[...] = wu_ref[...].astype(BF16)
        wdb[...] = wd_ref[...].astype(BF16)

    @pl.when(i < nu_ref[0])
    def _():
        row = lax.broadcasted_iota(I32, xs_ref.shape, 0)
        lo, hi = _unpack_pairs(jnp.where(row < nv_ref[i], xs_ref[...], jnp.uint32(0)))
        xb = jnp.concatenate([lo, hi], axis=1).astype(BF16)
        g = jnp.dot(xb, wgb[...], preferred_element_type=F32)
        u = jnp.dot(xb, wub[...], preferred_element_type=F32)
        hdn = (g * (1.0 / (1.0 + jnp.exp(-g))) * u).astype(BF16)
        y_ref[...] = _pack_pairs(jnp.dot(hdn, wdb[...], preferred_element_type=F32))

    @pl.when(i >= nu_ref[0])
    def _():
        y_ref[...] = jnp.zeros_like(y_ref)


def _experts(block_expert, block_valid, n_used, xs, wg, wu, wd, layer):
    rows, w = xs.shape
    tb = MOE_TB
    blk = lambda i, be, nv, nu: (jnp.minimum(i, nu[0] - 1), 0)
    oblk = lambda i, be, nv, nu: (i, 0)
    wsel = lambda i, be, nv, nu: (layer, be[jnp.minimum(i, nu[0] - 1)], 0, 0)
    return pl.pallas_call(
        _expert_kernel,
        grid_spec=pltpu.PrefetchScalarGridSpec(
            num_scalar_prefetch=3,
            grid=(rows // tb,),
            in_specs=[pl.BlockSpec((tb, w), blk),
                      pl.BlockSpec((None, None, D_MODEL, D_EXPERT), wsel),
                      pl.BlockSpec((None, None, D_MODEL, D_EXPERT), wsel),
                      pl.BlockSpec((None, None, D_EXPERT, D_MODEL), wsel)],
            out_specs=pl.BlockSpec((tb, w), oblk),
            scratch_shapes=[pltpu.VMEM((D_MODEL, D_EXPERT), BF16), pltpu.VMEM((D_MODEL, D_EXPERT), BF16),
                            pltpu.VMEM((D_EXPERT, D_MODEL), BF16)]),
        out_shape=jax.ShapeDtypeStruct((rows, w), U32),
        compiler_params=_params("arbitrary"),
        name="experts",
    )(block_expert, block_valid, n_used, xs, wg, wu, wd)


def _combine_kernel(ya_ref, yb_ref, x_ref, info_ref, lg_ref, lb_ref, out_ref, *, alpha):
    info = info_ref[...]
    g1 = info[:, 2:3]
    g2 = info[:, 3:4]
    a_lo, a_hi = _unpack_pairs(ya_ref[...])
    b_lo, b_hi = _unpack_pairs(yb_ref[...])
    half = D_MODEL // 2
    t_lo = alpha * x_ref[:, 0:half] + g1 * a_lo + g2 * b_lo
    t_hi = alpha * x_ref[:, half:] + g1 * a_hi + g2 * b_hi
    t = jnp.concatenate([t_lo, t_hi], axis=1)
    out_ref[...] = _layer_norm(t, lg_ref[...], lb_ref[...])


def _combine(y_tok, x1, info, ln_g, ln_b, alpha):
    n = x1.shape[0]
    tm = ROW_TM
    tok = lambda w: pl.BlockSpec((tm, w), lambda i: (i, 0))
    slot = lambda k: pl.BlockSpec((tm, D_MODEL // 2), lambda i: (i + k * (n // tm), 0))
    full = lambda shape: pl.BlockSpec(shape, lambda i: (0,) * len(shape))
    return pl.pallas_call(
        functools.partial(_combine_kernel, alpha=alpha),
        grid=(n // tm,),
        in_specs=[slot(0), slot(1), tok(D_MODEL), tok(LANES), full((1, D_MODEL)), full((1, D_MODEL))],
        out_specs=tok(D_MODEL),
        out_shape=jax.ShapeDtypeStruct((n, D_MODEL), F32),
        compiler_params=_params("arbitrary"),
        name="combine",
    )(y_tok, y_tok, x1, info, ln_g, ln_b)


def _rope_tables(s):
    half = HEAD_DIM // 2
    inv_freq = ROPE_THETA ** (-jnp.arange(half, dtype=F32) / half)
    ang = jnp.arange(s, dtype=F32)[:, None] * inv_freq[None, :]
    cos = jnp.cos(ang)
    sin = jnp.sin(ang)
    cos_t = jnp.tile(cos, (1, LANES // half))
    sin_t = jnp.tile(jnp.concatenate([-sin, sin], axis=1), (1, LANES // HEAD_DIM))
    return cos_t, sin_t


def _forward(x, w_in, w_out, branch_gain, sgu_gain, sgu_w, sgu_b, conv_w, ln_gain, ln_bias,
             router_group_w, router_group_b, router_expert_w, router_expert_b,
             expert_w_gate, expert_w_up, expert_w_down):
    b, s, _ = x.shape
    depth = w_in.shape[0]
    n = b * s
    alpha = (2.0 * depth) ** 0.25
    cos_t, sin_t = _rope_tables(s)
    perm, perm_t = _tile_perm()
    tb = MOE_TB
    n_blocks = (2 * n + N_EXPERTS * (tb - 1) + tb - 1) // tb
    rows = n_blocks * tb

    for l in range(depth):
        q, k, v, rest, q16, k16, v16 = _proj(x, w_in[l].astype(BF16), cos_t, sin_t, perm)
        g = branch_gain[l]
        ma = _attention(q, k, v, q16, k16, v16, perm_t, g[None, :ATTN_W])
        bias_tile = jnp.repeat(sgu_b[l].T, SGU_W // SGU_GROUPS, axis=1)
        mbc = _mixbc(rest, sgu_gain[l][None, :], sgu_w[l], bias_tile, conv_w[l], g[None, ATTN_W:])

        rw = jnp.zeros((D_MODEL, LANES), F32)
        rw = rw.at[:, :MOE_GROUPS].set(router_group_w[l])
        rw = rw.at[:, MOE_GROUPS:MOE_GROUPS + N_EXPERTS].set(router_expert_w[l])
        rw_hi = rw.astype(BF16)
        rw_hilo = jnp.concatenate([rw_hi, (rw - rw_hi.astype(F32)).astype(BF16)], axis=1)
        rbias = jnp.zeros((1, LANES), F32)
        rbias = rbias.at[0, :MOE_GROUPS].set(router_group_b[l])
        rbias = rbias.at[0, MOE_GROUPS:MOE_GROUPS + N_EXPERTS].set(router_expert_b[l])
        wo = w_out[l].astype(BF16)
        x1, xp, info, info_t, cnt = _outproj(
            ma.reshape(n, ATTN_W), mbc.reshape(n, SGU_W + CONV_W), x.reshape(n, D_MODEL),
            wo[:ATTN_W], wo[ATTN_W:], ln_gain[l, 0][None], ln_bias[l, 0][None], rw_hilo, rbias, alpha)

        counts = cnt[0, :N_EXPERTS].astype(I32)
        padded = (counts + tb - 1) // tb * tb
        pad_end = jnp.cumsum(padded)
        pad_start = pad_end - padded
        ex = info_t[0:2].astype(I32)
        start_of = jnp.zeros_like(ex)
        for e in range(N_EXPERTS):
            start_of = jnp.where(ex == e, pad_start[e], start_of)
        dest_kn = start_of + info_t[4:6].astype(I32)
        blk_row0 = jnp.arange(n_blocks, dtype=I32) * tb
        block_expert = jnp.minimum(jnp.sum((pad_end[None, :] <= blk_row0[:, None]).astype(I32), axis=1),
                                   N_EXPERTS - 1)
        block_valid = jnp.clip(pad_start[block_expert] + counts[block_expert] - blk_row0, 0, tb)
        n_used = (pad_end[-1:] // tb).astype(I32)

        xs = _sc_dispatch(xp, dest_kn, rows)
        ys = _experts(block_expert, block_valid, n_used, xs, expert_w_gate, expert_w_up, expert_w_down, l)
        y_tok = _sc_gather(ys, dest_kn.reshape(2 * n))
        x = _combine(y_tok, x1, info, ln_gain[l, 1][None], ln_bias[l, 1][None], alpha).reshape(b, s, D_MODEL)
    return x


def kernel(x, w_in, w_out, branch_gain, sgu_gain, sgu_w, sgu_b, conv_w, ln_gain, ln_bias, router_group_w, router_group_b, router_expert_w, router_expert_b, expert_w_gate, expert_w_up, expert_w_down):
    return _forward(x, w_in, w_out, branch_gain, sgu_gain, sgu_w, sgu_b, conv_w, ln_gain, ln_bias,
                    router_group_w, router_group_b, router_expert_w, router_expert_b,
                    expert_w_gate, expert_w_up, expert_w_down)
```

```python
import functools
import math

import jax
import jax.numpy as jnp
from jax import lax
from jax.experimental import pallas as pl
from jax.experimental.pallas import tpu as pltpu
from jax.experimental.pallas import tpu_sc as plsc

F32 = jnp.float32
BF16 = jnp.bfloat16
U32 = jnp.uint32
I32 = jnp.int32

D_MODEL = 1024
HEAD_DIM = 64
ATTN_W = 512
N_HEADS = 8
SGU_W = 256
SGU_GROUPS = 4
SGU_CHUNK = 128
CONV_W = 256
REST_W = 2 * SGU_W + 3 * CONV_W
PROJ_W = 3 * ATTN_W + REST_W
DILATIONS = (16, 4, 1)
BAND = 128
ROPE_THETA = 10000.0
MOE_GROUPS = 4
EXPERTS_PER_GROUP = 8
N_EXPERTS = MOE_GROUPS * EXPERTS_PER_GROUP
D_EXPERT = 512
EPS = 1e-5
NEG = -1e30

LANES = 128
VMEM_LIMIT = 56 * 1024 * 1024

TILE = 512
PERM_D = 16
ATTN_TQ = 512
OUT_TM = 512
MOE_TB = 512
ROW_TM = 512
SC_CORES = 2
SC_SUBCORES = 16
SC_WORKERS = SC_CORES * SC_SUBCORES
SC_CHUNK = 64


def _params(*sem):
    return pltpu.CompilerParams(dimension_semantics=sem, vmem_limit_bytes=VMEM_LIMIT)


def _pack_pairs(x):
    w = x.shape[1] // 2
    lo = lax.bitcast_convert_type(x[:, :w].astype(BF16).astype(F32), U32)
    hi = lax.bitcast_convert_type(x[:, w:].astype(BF16).astype(F32), U32)
    return (lo >> 16) | (hi & jnp.uint32(0xFFFF0000))


def _unpack_pairs(p):
    lo = lax.bitcast_convert_type(p << 16, F32)
    hi = lax.bitcast_convert_type(p & jnp.uint32(0xFFFF0000), F32)
    return lo, hi


def _tile_perm():
    i = jnp.arange(TILE, dtype=I32)
    tok = PERM_D * (i % (TILE // PERM_D)) + i // (TILE // PERM_D)
    p = (jnp.arange(TILE, dtype=I32)[None, :] == tok[:, None]).astype(BF16)
    return p, p.T


def _mixers(rest, prev_zz, sg, sw_ref, sb, cw_ref, bg):
    tm = rest.shape[0]
    u = rest[:, 0:SGU_W]
    z = rest[:, SGU_W:2 * SGU_W]
    gb = rest[:, 2 * SGU_W:2 * SGU_W + CONV_W]
    gc = rest[:, 2 * SGU_W + CONV_W:2 * SGU_W + 2 * CONV_W]
    hh = rest[:, 2 * SGU_W + 2 * CONV_W:]

    gdim = SGU_W // SGU_GROUPS
    ri = lax.broadcasted_iota(I32, (SGU_W, SGU_W), 0) // gdim
    ci = lax.broadcasted_iota(I32, (SGU_W, SGU_W), 1) // gdim
    avg = jnp.where(ri == ci, 1.0 / gdim, 0.0).astype(BF16)
    z = _gelu_tanh(z)
    zc = z - _split_dot(z, avg)
    var = _split_dot(zc * zc, avg)
    zn = (zc * lax.rsqrt(var + EPS) * sg).astype(BF16)

    tr = lax.broadcasted_iota(I32, (SGU_CHUNK, SGU_CHUNK), 0)
    tc = lax.broadcasted_iota(I32, (SGU_CHUNK, SGU_CHUNK), 1)
    w_cat = jnp.concatenate(
        [jnp.where(tc <= tr, sw_ref[g], 0.0).astype(BF16) for g in range(SGU_GROUPS)], axis=1)
    lane_g = lax.broadcasted_iota(I32, (SGU_CHUNK, SGU_W), 1) // gdim
    gu = _gelu_tanh(u)
    yb = []
    for c in range(tm // SGU_CHUNK):
        zch = zn[c * SGU_CHUNK:(c + 1) * SGU_CHUNK, :]
        stack = jnp.concatenate(
            [jnp.where(lane_g == g, zch, jnp.zeros_like(zch)) for g in range(SGU_GROUPS)], axis=0)
        sp = jnp.dot(w_cat, stack, preferred_element_type=F32) + sb
        yb.append(gu[c * SGU_CHUNK:(c + 1) * SGU_CHUNK, :] * sp)
    yb = jnp.concatenate(yb, axis=0)

    zz = gc * hh
    hrows = prev_zz.shape[0]
    ext = jnp.concatenate([prev_zz, zz], axis=0)
    z1 = ext[hrows - 1:hrows - 1 + tm, :]
    z2 = ext[hrows - 2:hrows - 2 + tm, :]
    yc = gb * (cw_ref[0:1, :] * z2 + cw_ref[1:2, :] * z1 + cw_ref[2:3, :] * zz)

    def rms(t, g):
        return t * lax.rsqrt(jnp.mean(t * t, axis=1, keepdims=True) + EPS) * g

    out = jnp.concatenate([rms(yb, bg[:, 0:SGU_W]).astype(BF16), rms(yc, bg[:, SGU_W:]).astype(BF16)], axis=1)
    return out, zz[tm - hrows:, :]


def _combine_tile(ya, yb, x1, info, lg, lb, alpha):
    g1 = info[:, 2:3]
    g2 = info[:, 3:4]
    a_lo, a_hi = _unpack_pairs(ya)
    b_lo, b_hi = _unpack_pairs(yb)
    half = D_MODEL // 2
    t_lo = alpha * x1[:, 0:half] + g1 * a_lo + g2 * b_lo
    t_hi = alpha * x1[:, half:] + g1 * a_hi + g2 * b_hi
    return _layer_norm(jnp.concatenate([t_lo, t_hi], axis=1), lg, lb)


def _proj_kernel(*refs, fused, alpha):
    if fused:
        (ya_ref, yb_ref, x1_ref, info_ref, lg_ref, lb_ref, w_ref, cos_ref, sin_ref, p_ref,
         sg_ref, sw_ref, sb_ref, cw_ref, bg_ref,
         x_out, q_ref, k_ref, v_ref, m_ref, q16_ref, k16_ref, v16_ref, zz_prev) = refs
        x = _combine_tile(ya_ref[...], yb_ref[...], x1_ref[...], info_ref[...], lg_ref[...], lb_ref[...], alpha)
        x_out[...] = x
    else:
        (x_ref, w_ref, cos_ref, sin_ref, p_ref, sg_ref, sw_ref, sb_ref, cw_ref, bg_ref,
         q_ref, k_ref, v_ref, m_ref, q16_ref, k16_ref, v16_ref, zz_prev) = refs
        x = x_ref[...]
    si = pl.program_id(1)

    @pl.when(si == 0)
    def _():
        zz_prev[...] = jnp.zeros_like(zz_prev)

    xb = x.astype(BF16)
    cos = cos_ref[...]
    sin = sin_ref[...]
    lane = lax.broadcasted_iota(I32, cos.shape, 1)
    first_half = (lane % HEAD_DIM) < (HEAD_DIM // 2)
    rows = TILE // PERM_D

    def store_both(val_bf, out_ref, out16_ref):
        out_ref[...] = val_bf
        perm = jnp.dot(p_ref[...], val_bf, preferred_element_type=F32).astype(BF16)
        for r in range(PERM_D):
            out16_ref[:, r * ATTN_W:(r + 1) * ATTN_W] = perm[r * rows:(r + 1) * rows, :]

    def rope(col0, scale):
        t = jnp.dot(xb, w_ref[:, col0:col0 + ATTN_W], preferred_element_type=F32)
        out = []
        for c in range(ATTN_W // LANES):
            tc = t[:, c * LANES:(c + 1) * LANES]
            partner = jnp.where(first_half, pltpu.roll(tc, LANES - 32, 1), pltpu.roll(tc, 32, 1))
            out.append(((tc * cos + partner * sin) * scale).astype(BF16))
        return jnp.concatenate(out, axis=1)

    rest = jnp.dot(xb, w_ref[:, 3 * ATTN_W:], preferred_element_type=F32)
    tail = [zz_prev[...]]

    def mix_chunk(c):
        rows_c = slice(c * SGU_CHUNK, (c + 1) * SGU_CHUNK)
        out, tail[0] = _mixers(rest[rows_c, :], tail[0], sg_ref[...], sw_ref, sb_ref[...], cw_ref, bg_ref[...])
        m_ref[rows_c, :] = out

    qb = rope(0, HEAD_DIM ** -0.5)
    mix_chunk(0)
    store_both(qb, q_ref, q16_ref)
    kb = rope(ATTN_W, 1.0)
    mix_chunk(1)
    store_both(kb, k_ref, k16_ref)
    vb = jnp.dot(xb, w_ref[:, 2 * ATTN_W:3 * ATTN_W], preferred_element_type=F32).astype(BF16)
    mix_chunk(2)
    store_both(vb, v_ref, v16_ref)
    mix_chunk(3)
    zz_prev[...] = tail[0]


def _proj(x_or_parts, w_bf, cos_t, sin_t, perm, mix_params, b, s, alpha):
    tm = TILE
    nt = s // tm
    fused = isinstance(x_or_parts, tuple)
    out3 = jax.ShapeDtypeStruct((b, s, ATTN_W), BF16)
    out16 = jax.ShapeDtypeStruct((b, s // PERM_D, PERM_D * ATTN_W), BF16)
    tok = lambda width: pl.BlockSpec((None, tm, width), lambda bi, si: (bi, si, 0))
    tok16 = pl.BlockSpec((None, tm // PERM_D, PERM_D * ATTN_W), lambda bi, si: (bi, si, 0))
    full = lambda shape: pl.BlockSpec(shape, lambda bi, si: (0,) * len(shape))
    flat = lambda width, off: pl.BlockSpec((tm, width), lambda bi, si: (bi * nt + si + off, 0))
    common_specs = [full((D_MODEL, PROJ_W)),
                    pl.BlockSpec((tm, LANES), lambda bi, si: (si, 0)),
                    pl.BlockSpec((tm, LANES), lambda bi, si: (si, 0)),
                    full((tm, tm)),
                    full((1, SGU_W)), full((SGU_GROUPS, SGU_CHUNK, SGU_CHUNK)),
                    full((SGU_CHUNK, SGU_W)), full((3, CONV_W)), full((1, SGU_W + CONV_W))]
    out_specs = [tok(ATTN_W), tok(ATTN_W), tok(ATTN_W), tok(SGU_W + CONV_W), tok16, tok16, tok16]
    out_shape = [out3, out3, out3, jax.ShapeDtypeStruct((b, s, SGU_W + CONV_W), BF16), out16, out16, out16]
    if fused:
        y_tok, x1, info, lg, lb = x_or_parts
        n = b * s
        ins = [y_tok, y_tok, x1, info, lg, lb]
        in_specs = [flat(D_MODEL // 2, 0), flat(D_MODEL // 2, n // tm), flat(D_MODEL, 0), flat(LANES, 0),
                    full((1, D_MODEL)), full((1, D_MODEL))]
        out_specs = [tok(D_MODEL)] + out_specs
        out_shape = [jax.ShapeDtypeStruct((b, s, D_MODEL), F32)] + out_shape
    else:
        ins = [x_or_parts]
        in_specs = [tok(D_MODEL)]
    return pl.pallas_call(
        functools.partial(_proj_kernel, fused=fused, alpha=alpha),
        grid=(b, nt),
        in_specs=in_specs + common_specs,
        out_specs=out_specs, out_shape=out_shape,
        scratch_shapes=[pltpu.VMEM((8, CONV_W), F32)],
        compiler_params=_params("arbitrary", "arbitrary"),
        name="proj",
    )(*ins, w_bf, cos_t, sin_t, perm, *mix_params)


def _band_bias(permuted):
    row = lax.broadcasted_iota(I32, (BAND, 2 * BAND), 0)
    col = lax.broadcasted_iota(I32, (BAND, 2 * BAND), 1)
    key = col % BAND
    if permuted:
        row = 4 * (row % 32) + row // 32
        key = 4 * (key % 32) + key // 32
    lo_key = jnp.where(col < BAND, row, 0)
    hi_key = jnp.where(col < BAND, BAND - 1, row)
    return jnp.where(jnp.logical_and(key >= lo_key, key <= hi_key), 0.0, NEG).astype(F32), col


def _attn_heads(get_q, get_k, get_v, bias, emit):
    lane = lax.broadcasted_iota(I32, (BAND, LANES), 1)
    lane_lo = lane < HEAD_DIM
    for p in range(ATTN_W // LANES):
        ql, kk, vv = get_q(p), get_k(p), get_v(p)
        o_h, l_h = [], []
        for hh in range(2):
            qm = jnp.where(lane_lo if hh == 0 else jnp.logical_not(lane_lo), ql, jnp.zeros_like(ql))
            sc = lax.dot_general(qm, kk, (((1,), (1,)), ((), ())), preferred_element_type=F32)
            sc = sc + bias
            mx = jnp.max(sc, axis=1, keepdims=True)
            pe = jnp.exp(sc - mx)
            den = jnp.sum(pe, axis=1, keepdims=True)
            o_h.append(jnp.dot(pe.astype(BF16), vv, preferred_element_type=F32) / den)
            l_h.append(mx + jnp.log(den))
        emit(p, jnp.where(lane_lo, o_h[0], o_h[1]), l_h[0], l_h[1], lane, lane_lo)


def _three_terms(st):
    hi = st.astype(BF16).astype(F32)
    rest = st - hi
    mid = rest.astype(BF16).astype(F32)
    lo = (rest - mid).astype(BF16).astype(F32)
    return hi + pltpu.roll(mid, 8, 1) + pltpu.roll(lo, 16, 1)


def _attn16_kernel(q_ref, k_ref, v_ref, o_ref, st_ref, kbuf, vbuf, *, tq):
    i = pl.program_id(2)

    @pl.when(i == 0)
    def _():
        kbuf[0:BAND, :] = jnp.zeros((BAND, ATTN_W), BF16)
        vbuf[0:BAND, :] = jnp.zeros((BAND, ATTN_W), BF16)

    kbuf[BAND:BAND + tq, :] = k_ref[...]
    vbuf[BAND:BAND + tq, :] = v_ref[...]
    band_bias, col = _band_bias(False)

    def block(j, carry):
        r0 = pl.multiple_of(j * BAND, BAND)
        first_col = jnp.where(jnp.logical_or(j > 0, i > 0), 0, BAND)
        bias = jnp.where(col >= first_col, band_bias, NEG)
        st = [jnp.zeros((BAND, LANES), F32)]

        def emit(p, o_pair, l0, l1, lane, lane_lo):
            o_ref[pl.ds(r0, BAND), p * LANES:(p + 1) * LANES] = o_pair.astype(BF16)
            st[0] = jnp.where(lane == 2 * p, l0, jnp.where(lane == 2 * p + 1, l1, st[0]))

        _attn_heads(lambda p: q_ref[pl.ds(r0, BAND), p * LANES:(p + 1) * LANES],
                    lambda p: kbuf[pl.ds(r0, 2 * BAND), p * LANES:(p + 1) * LANES],
                    lambda p: vbuf[pl.ds(r0, 2 * BAND), p * LANES:(p + 1) * LANES],
                    bias, emit)
        st_ref[pl.ds(r0, BAND), :] = _three_terms(st[0])
        return carry

    lax.fori_loop(0, tq // BAND, block, 0, unroll=True)
    kbuf[0:BAND, :] = kbuf[tq:tq + BAND, :]
    vbuf[0:BAND, :] = vbuf[tq:tq + BAND, :]


def _attn4_kernel(q_ref, k_ref, v_ref, o_ref, st_ref, qbuf, kbuf, vbuf, obuf, sbuf):
    i = pl.program_id(1)
    rows = TILE // PERM_D
    nres = 4

    @pl.when(i == 0)
    def _():
        kbuf[:, 0:BAND, :] = jnp.zeros((nres, BAND, ATTN_W), BF16)
        vbuf[:, 0:BAND, :] = jnp.zeros((nres, BAND, ATTN_W), BF16)

    @pl.when(i > 0)
    def _():
        kbuf[:, 0:BAND, :] = kbuf[:, BAND:2 * BAND, :]
        vbuf[:, 0:BAND, :] = vbuf[:, BAND:2 * BAND, :]

    for r4 in range(nres):
        for g in range(PERM_D // nres):
            lanes = slice((r4 + nres * g) * ATTN_W, (r4 + nres * g + 1) * ATTN_W)
            qbuf[r4, g * rows:(g + 1) * rows, :] = q_ref[:, lanes]
            kbuf[r4, BAND + g * rows:BAND + (g + 1) * rows, :] = k_ref[:, lanes]
            vbuf[r4, BAND + g * rows:BAND + (g + 1) * rows, :] = v_ref[:, lanes]

    band_bias, col = _band_bias(True)
    first_col = jnp.where(i > 0, 0, BAND)
    bias = jnp.where(col >= first_col, band_bias, NEG)

    def block(j, carry):
        st = [jnp.zeros((BAND, LANES), F32)]

        def emit(p, o_pair, l0, l1, lane, lane_lo):
            obuf[j, :, p * LANES:(p + 1) * LANES] = o_pair.astype(BF16)
            st[0] = jnp.where(lane == 2 * p, l0, jnp.where(lane == 2 * p + 1, l1, st[0]))

        _attn_heads(lambda p: qbuf[j, :, p * LANES:(p + 1) * LANES],
                    lambda p: kbuf[j, :, p * LANES:(p + 1) * LANES],
                    lambda p: vbuf[j, :, p * LANES:(p + 1) * LANES],
                    bias, emit)
        sbuf[j] = _three_terms(st[0])
        return carry

    lax.fori_loop(0, nres, block, 0, unroll=True)

    for r4 in range(nres):
        for g in range(PERM_D // nres):
            grp = r4 + nres * g
            o_ref[:, grp * ATTN_W:(grp + 1) * ATTN_W] = obuf[r4, g * rows:(g + 1) * rows, :]
            st_ref[:, grp * LANES:(grp + 1) * LANES] = sbuf[r4, g * rows:(g + 1) * rows, :]


def _attn1_kernel(q_ref, k_ref, v_ref, o4_ref, s4_ref, o16_ref, s16_ref, pt_ref, g_ref, out_ref,
                  kbuf, vbuf, acc, o4t, o16t, stt, s1t, *, tq):
    i = pl.program_id(1)

    @pl.when(i == 0)
    def _():
        kbuf[0:BAND, :] = jnp.zeros((BAND, ATTN_W), BF16)
        vbuf[0:BAND, :] = jnp.zeros((BAND, ATTN_W), BF16)

    kbuf[BAND:BAND + tq, :] = k_ref[...]
    vbuf[BAND:BAND + tq, :] = v_ref[...]

    def rows_of(ref, w):
        return jnp.concatenate([ref[:, r * w:(r + 1) * w] for r in range(PERM_D)], axis=0)

    o4t[...] = jnp.dot(pt_ref[...], rows_of(o4_ref, ATTN_W), preferred_element_type=F32).astype(BF16)
    o16t[...] = jnp.dot(pt_ref[...], rows_of(o16_ref, ATTN_W), preferred_element_type=F32).astype(BF16)
    terms = (rows_of(s4_ref, LANES) + pltpu.roll(rows_of(s16_ref, LANES), 32, 1)).astype(BF16)
    terms = jnp.dot(pt_ref[...], terms, preferred_element_type=F32)
    stt[...] = terms + pltpu.roll(terms, LANES - 8, 1) + pltpu.roll(terms, LANES - 16, 1)
    band_bias, col = _band_bias(False)

    def block(j, carry):
        r0 = pl.multiple_of(j * BAND, BAND)
        first_col = jnp.where(jnp.logical_or(j > 0, i > 0), 0, BAND)
        bias = jnp.where(col >= first_col, band_bias, NEG)
        st = [jnp.zeros((BAND, LANES), F32)]

        def emit(p, o_pair, l0, l1, lane, lane_lo):
            acc[pl.ds(r0, BAND), p * LANES:(p + 1) * LANES] = o_pair
            st[0] = jnp.where(lane == 2 * p, l0, jnp.where(lane == 2 * p + 1, l1, st[0]))

        _attn_heads(lambda p: q_ref[pl.ds(r0, BAND), p * LANES:(p + 1) * LANES],
                    lambda p: kbuf[pl.ds(r0, 2 * BAND), p * LANES:(p + 1) * LANES],
                    lambda p: vbuf[pl.ds(r0, 2 * BAND), p * LANES:(p + 1) * LANES],
                    bias, emit)
        s1t[pl.ds(r0, BAND), :] = st[0]
        return carry

    lax.fori_loop(0, tq // BAND, block, 0, unroll=True)
    kbuf[0:BAND, :] = kbuf[tq:tq + BAND, :]
    vbuf[0:BAND, :] = vbuf[tq:tq + BAND, :]

    head_of = lax.broadcasted_iota(I32, (LANES, ATTN_W), 1) // HEAD_DIM
    expand = jnp.where(lax.broadcasted_iota(I32, (LANES, ATTN_W), 0) == head_of, 1.0, 0.0).astype(BF16)

    def merge(j, carry):
        r0 = pl.multiple_of(j * BAND, BAND)
        is_head = lax.broadcasted_iota(I32, (BAND, LANES), 1) < N_HEADS
        l1 = s1t[pl.ds(r0, BAND), :]
        l4 = stt[pl.ds(r0, BAND), :]
        l16 = pltpu.roll(l4, LANES - 32, 1)
        top = jnp.maximum(l1, jnp.maximum(l4, l16))
        e1 = jnp.exp(l1 - top)
        e4 = jnp.exp(l4 - top)
        e16 = jnp.exp(l16 - top)
        inv = 1.0 / (e1 + e4 + e16)

        def spread(e):
            w = jnp.where(is_head, e * inv, 0.0)
            hi = w.astype(BF16)
            lo = (w - hi.astype(F32)).astype(BF16)
            return (jnp.dot(hi, expand, preferred_element_type=F32)
                    + jnp.dot(lo, expand, preferred_element_type=F32))

        y = (spread(e1) * acc[pl.ds(r0, BAND), :]
             + spread(e4) * o4t[pl.ds(r0, BAND), :].astype(F32)
             + spread(e16) * o16t[pl.ds(r0, BAND), :].astype(F32))
        scale = lax.rsqrt(jnp.mean(y * y, axis=1, keepdims=True) + EPS)
        out_ref[pl.ds(r0, BAND), :] = (y * scale * g_ref[...]).astype(BF16)
        return carry

    lax.fori_loop(0, tq // BAND, merge, 0, unroll=True)


def _attention(q, k, v, q16, k16, v16, perm_t, gain):
    b, s, _ = q.shape
    m16 = s // PERM_D
    rows = TILE // PERM_D
    o16_shape = [jax.ShapeDtypeStruct((b, m16, PERM_D * ATTN_W), BF16),
                 jax.ShapeDtypeStruct((b, m16, PERM_D * LANES), F32)]

    tq16 = min(ATTN_TQ, m16)
    blk16 = lambda w: pl.BlockSpec((None, tq16, w), lambda bi, ri, ii: (bi, ii, ri))
    o16, s16 = pl.pallas_call(
        functools.partial(_attn16_kernel, tq=tq16),
        grid=(b, PERM_D, m16 // tq16),
        in_specs=[blk16(ATTN_W)] * 3, out_specs=[blk16(ATTN_W), blk16(LANES)], out_shape=o16_shape,
        scratch_shapes=[pltpu.VMEM((tq16 + BAND, ATTN_W), BF16)] * 2,
        compiler_params=_params("arbitrary", "arbitrary", "arbitrary"),
        name="attn_d16",
    )(q16, k16, v16)

    tile = lambda w: pl.BlockSpec((None, rows, PERM_D * w), lambda bi, ii: (bi, ii, 0))
    o4, s4 = pl.pallas_call(
        _attn4_kernel,
        grid=(b, s // TILE),
        in_specs=[tile(ATTN_W)] * 3, out_specs=[tile(ATTN_W), tile(LANES)], out_shape=o16_shape,
        scratch_shapes=[pltpu.VMEM((4, BAND, ATTN_W), BF16), pltpu.VMEM((4, 2 * BAND, ATTN_W), BF16),
                        pltpu.VMEM((4, 2 * BAND, ATTN_W), BF16), pltpu.VMEM((4, BAND, ATTN_W), BF16),
                        pltpu.VMEM((4, BAND, LANES), F32)],
        compiler_params=_params("arbitrary", "arbitrary"),
        name="attn_d4",
    )(q16, k16, v16)

    tq = TILE
    tok = lambda w: pl.BlockSpec((None, tq, w), lambda bi, ii: (bi, ii, 0))
    full = lambda shape: pl.BlockSpec(shape, lambda bi, ii: (0,) * len(shape))
    return pl.pallas_call(
        functools.partial(_attn1_kernel, tq=tq),
        grid=(b, s // tq),
        in_specs=[tok(ATTN_W)] * 3 + [tile(ATTN_W), tile(LANES), tile(ATTN_W), tile(LANES),
                                      full((TILE, TILE)), full((1, ATTN_W))],
        out_specs=tok(ATTN_W),
        out_shape=jax.ShapeDtypeStruct((b, s, ATTN_W), BF16),
        scratch_shapes=[pltpu.VMEM((tq + BAND, ATTN_W), BF16), pltpu.VMEM((tq + BAND, ATTN_W), BF16),
                        pltpu.VMEM((tq, ATTN_W), F32),
                        pltpu.VMEM((tq, ATTN_W), BF16), pltpu.VMEM((tq, ATTN_W), BF16),
                        pltpu.VMEM((tq, LANES), F32), pltpu.VMEM((tq, LANES), F32)],
        compiler_params=_params("arbitrary", "arbitrary"),
        name="attn_d1",
    )(q, k, v, o4, s4, o16, s16, perm_t, gain)


def _gelu_tanh(x):
    c = math.sqrt(2.0 / math.pi)
    return x * (0.5 * (1.0 + jnp.tanh(c * (x + 0.044715 * (x * x * x)))))


def _split_dot(x, m_bf):
    hi = x.astype(BF16)
    lo = (x - hi.astype(F32)).astype(BF16)
    return (jnp.dot(hi, m_bf, preferred_element_type=F32)
            + jnp.dot(lo, m_bf, preferred_element_type=F32))


def _layer_norm(t, g, b):
    mu = jnp.mean(t, axis=1, keepdims=True)
    tc = t - mu
    var = jnp.mean(tc * tc, axis=1, keepdims=True)
    return tc * lax.rsqrt(var + EPS) * g + b


def _outproj_kernel(ma_ref, mb_ref, x_ref, wa_ref, wb_ref, lg_ref, lb_ref, rw_ref, rb_ref,
                    x1_ref, xp_ref, info_ref, infot_ref, cnt_ref, run_ref, *, tm, alpha):
    i = pl.program_id(0)

    @pl.when(i == 0)
    def _():
        run_ref[...] = jnp.zeros_like(run_ref)

    y = (jnp.dot(ma_ref[...], wa_ref[...], preferred_element_type=F32)
         + jnp.dot(mb_ref[...], wb_ref[...], preferred_element_type=F32))
    x1 = _layer_norm(alpha * x_ref[...] + y, lg_ref[...], lb_ref[...])
    x1_ref[...] = x1
    xp_ref[...] = _pack_pairs(x1)

    hi = x1.astype(BF16)
    lo = (x1 - hi.astype(F32)).astype(BF16)
    both = jnp.dot(hi, rw_ref[...], preferred_element_type=F32)
    logit = (both[:, :LANES] + both[:, LANES:]
             + jnp.dot(lo, rw_ref[:, :LANES], preferred_element_type=F32)) + rb_ref[...]
    lane = lax.broadcasted_iota(I32, (tm, LANES), 1)
    lane_f = lane.astype(F32)

    def top(mask):
        v = jnp.max(jnp.where(mask, logit, NEG), axis=1, keepdims=True)
        first = jnp.min(jnp.where(jnp.logical_and(mask, logit == v), lane_f, float(LANES)),
                        axis=1, keepdims=True)
        return v, first.astype(I32)

    is_g = lane < MOE_GROUPS
    gmax, gidx = top(is_g)
    g_p = 1.0 / jnp.sum(jnp.where(is_g, jnp.exp(logit - gmax), 0.0), axis=1, keepdims=True)
    in_grp = jnp.logical_and(lane >= MOE_GROUPS + gidx * EXPERTS_PER_GROUP,
                             lane < MOE_GROUPS + (gidx + 1) * EXPERTS_PER_GROUP)
    v1, i1 = top(in_grp)
    v2, i2 = top(jnp.logical_and(in_grp, lane != i1))
    e21 = jnp.exp(v2 - v1)
    gate1 = g_p / (1.0 + e21)
    gate2 = g_p * e21 / (1.0 + e21)
    ex1 = i1 - MOE_GROUPS
    ex2 = i2 - MOE_GROUPS

    oh1 = lane == ex1
    oh2 = lane == ex2
    oh = (oh1.astype(F32) + oh2.astype(F32))
    tr = lax.broadcasted_iota(I32, (tm, tm), 0)
    tc = lax.broadcasted_iota(I32, (tm, tm), 1)
    lower = jnp.where(tc < tr, 1.0, 0.0).astype(BF16)
    before = jnp.dot(lower, oh.astype(BF16), preferred_element_type=F32) + run_ref[0:1, :]
    rank1 = jnp.sum(jnp.where(oh1, before, 0.0), axis=1, keepdims=True)
    rank2 = jnp.sum(jnp.where(oh2, before, 0.0), axis=1, keepdims=True)
    run_new = run_ref[0:1, :] + jnp.sum(oh, axis=0, keepdims=True)
    run_ref[...] = jnp.broadcast_to(run_new, run_ref.shape)
    cnt_ref[...] = jnp.broadcast_to(run_new, cnt_ref.shape)

    info = jnp.where(lane == 0, ex1.astype(F32), 0.0)
    info = jnp.where(lane == 1, ex2.astype(F32), info)
    info = jnp.where(lane == 2, gate1, info)
    info = jnp.where(lane == 3, gate2, info)
    info = jnp.where(lane == 4, rank1, info)
    info = jnp.where(lane == 5, rank2, info)
    info_ref[...] = info
    infot_ref[...] = info.T[0:8, :]


def _outproj(ma, mbc, x, wo_a, wo_b, ln_g, ln_b, rw_hilo, rbias, alpha):
    n = x.shape[0]
    tm = OUT_TM
    tok = lambda w: pl.BlockSpec((tm, w), lambda i: (i, 0))
    full = lambda shape: pl.BlockSpec(shape, lambda i: (0,) * len(shape))
    return pl.pallas_call(
        functools.partial(_outproj_kernel, tm=tm, alpha=alpha),
        grid=(n // tm,),
        in_specs=[tok(ATTN_W), tok(SGU_W + CONV_W), tok(D_MODEL),
                  full((ATTN_W, D_MODEL)), full((SGU_W + CONV_W, D_MODEL)),
                  full((1, D_MODEL)), full((1, D_MODEL)),
                  full((D_MODEL, 2 * LANES)), full((1, LANES))],
        out_specs=[tok(D_MODEL), tok(D_MODEL // 2), tok(LANES), pl.BlockSpec((8, tm), lambda i: (0, i)),
                   full((8, LANES))],
        out_shape=[jax.ShapeDtypeStruct((n, D_MODEL), F32),
                   jax.ShapeDtypeStruct((n, D_MODEL // 2), U32),
                   jax.ShapeDtypeStruct((n, LANES), F32),
                   jax.ShapeDtypeStruct((8, n), F32),
                   jax.ShapeDtypeStruct((8, LANES), F32)],
        scratch_shapes=[pltpu.VMEM((8, LANES), F32)],
        compiler_params=_params("arbitrary"),
        name="outproj",
    )(ma, mbc, x, wo_a, wo_b, ln_g, ln_b, rw_hilo, rbias)


def _sc_mesh():
    return plsc.VectorSubcoreMesh(core_axis_name="c", subcore_axis_name="s",
                                  num_cores=SC_CORES, num_subcores=SC_SUBCORES)


def _sc_chunk(rows_per_worker):
    return min(SC_CHUNK, rows_per_worker // 2)


def _sc_dispatch(xp, dest_kn, rows):
    n, w = xp.shape
    t_per_w = n // SC_WORKERS
    chunk = _sc_chunk(t_per_w)
    nchunk = t_per_w // chunk

    def body(src_hbm, dest_hbm, out_hbm, idx_v, rows_v, lsem, ssem):
        wid = lax.axis_index("s") * SC_CORES + lax.axis_index("c")
        base = wid * t_per_w
        pltpu.sync_copy(dest_hbm.at[0, wid], idx_v.at[0])
        pltpu.sync_copy(dest_hbm.at[1, wid], idx_v.at[1])

        def load(c, slot):
            return pltpu.make_async_copy(src_hbm.at[pl.ds(base + c * chunk, chunk)], rows_v.at[slot],
                                         lsem.at[slot])

        def put(c, slot, kk):
            return pltpu.make_async_copy(rows_v.at[slot], out_hbm.at[idx_v.at[kk, c]], ssem.at[slot])

        load(0, 0).start()

        @pl.loop(0, nchunk, step=2)
        def _(c):
            for b in range(2):
                cc = c + b
                load(cc, b).wait()

                @pl.when(cc + 1 < nchunk)
                def _():
                    @pl.when(cc >= 1)
                    def _():
                        put(cc - 1, 1 - b, 0).wait()
                        put(cc - 1, 1 - b, 1).wait()
                    load(cc + 1, 1 - b).start()

                put(cc, b, 0).start()
                put(cc, b, 1).start()

        for b in range(2):
            put(nchunk - 2 + b, b, 0).wait()
            put(nchunk - 2 + b, b, 1).wait()

    call = pl.kernel(
        body, mesh=_sc_mesh(),
        out_type=jax.ShapeDtypeStruct((rows, w), U32),
        scratch_types=[pltpu.VMEM((2, nchunk, chunk), I32), pltpu.VMEM((2, chunk, w), U32),
                       pltpu.SemaphoreType.DMA((2,)), pltpu.SemaphoreType.DMA((2,))],
        name="sc_dispatch")
    return call(xp, dest_kn.reshape(2, SC_WORKERS, nchunk, chunk))


def _sc_gather(table, idx):
    b = idx.shape[0]
    w = table.shape[1]
    b_per_w = b // SC_WORKERS
    chunk = _sc_chunk(b_per_w)
    nchunk = b_per_w // chunk

    def body(table_hbm, idx_hbm, out_hbm, idx_v, rows_v, gsem, osem):
        wid = lax.axis_index("s") * SC_CORES + lax.axis_index("c")
        base = wid * b_per_w
        pltpu.sync_copy(idx_hbm.at[wid], idx_v)

        def gather(c, slot):
            return pltpu.make_async_copy(table_hbm.at[idx_v.at[c]], rows_v.at[slot], gsem.at[slot])

        def put(c, slot):
            return pltpu.make_async_copy(rows_v.at[slot], out_hbm.at[pl.ds(base + c * chunk, chunk)],
                                         osem.at[slot])

        gather(0, 0).start()

        @pl.loop(0, nchunk, step=2)
        def _(c):
            for b in range(2):
                cc = c + b
                gather(cc, b).wait()

                @pl.when(cc + 1 < nchunk)
                def _():
                    @pl.when(cc >= 1)
                    def _():
                        put(cc - 1, 1 - b).wait()
                    gather(cc + 1, 1 - b).start()

                put(cc, b).start()

        put(nchunk - 2, 0).wait()
        put(nchunk - 1, 1).wait()

    call = pl.kernel(
        body, mesh=_sc_mesh(),
        out_type=jax.ShapeDtypeStruct((b, w), table.dtype),
        scratch_types=[pltpu.VMEM((nchunk, chunk), I32), pltpu.VMEM((2, chunk, w), table.dtype),
                       pltpu.SemaphoreType.DMA((2,)), pltpu.SemaphoreType.DMA((2,))],
        name="sc_gather")
    return call(table, idx.reshape(SC_WORKERS, nchunk, chunk))


def _expert_kernel(be_ref, nv_ref, nu_ref, xs_ref, wg_ref, wu_ref, wd_ref, y_ref, wgb, wub, wdb):
    i = pl.program_id(0)
    cur = jnp.minimum(i, nu_ref[0] - 1)
    new_expert = jnp.logical_or(i == 0, be_ref[cur] != be_ref[jnp.maximum(cur - 1, 0)])

    @pl.when(jnp.logical_and(i < nu_ref[0], new_expert))
    def _():
        wgb[...] = wg_ref[...].astype(BF16)
        wub[...] = wu_ref[...].astype(BF16)
        wdb[...] = wd_ref[...].astype(BF16)

    @pl.when(i < nu_ref[0])
    def _():
        row = lax.broadcasted_iota(I32, xs_ref.shape, 0)
        lo, hi = _unpack_pairs(jnp.where(row < nv_ref[i], xs_ref[...], jnp.uint32(0)))
        xb = jnp.concatenate([lo, hi], axis=1).astype(BF16)
        g = jnp.dot(xb, wgb[...], preferred_element_type=F32)
        u = jnp.dot(xb, wub[...], preferred_element_type=F32)
        hdn = (g * (1.0 / (1.0 + jnp.exp(-g))) * u).astype(BF16)
        y_ref[...] = _pack_pairs(jnp.dot(hdn, wdb[...], preferred_element_type=F32))

    @pl.when(i >= nu_ref[0])
    def _():
        y_ref[...] = jnp.zeros_like(y_ref)


def _experts(block_expert, block_valid, n_used, xs, wg, wu, wd, layer):
    rows, w = xs.shape
    tb = MOE_TB
    blk = lambda i, be, nv, nu: (jnp.minimum(i, nu[0] - 1), 0)
    oblk = lambda i, be, nv, nu: (i, 0)
    wsel = lambda i, be, nv, nu: (layer, be[jnp.minimum(i, nu[0] - 1)], 0, 0)
    return pl.pallas_call(
        _expert_kernel,
        grid_spec=pltpu.PrefetchScalarGridSpec(
            num_scalar_prefetch=3,
            grid=(rows // tb,),
            in_specs=[pl.BlockSpec((tb, w), blk),
                      pl.BlockSpec((None, None, D_MODEL, D_EXPERT), wsel),
                      pl.BlockSpec((None, None, D_MODEL, D_EXPERT), wsel),
                      pl.BlockSpec((None, None, D_EXPERT, D_MODEL), wsel)],
            out_specs=pl.BlockSpec((tb, w), oblk),
            scratch_shapes=[pltpu.VMEM((D_MODEL, D_EXPERT), BF16), pltpu.VMEM((D_MODEL, D_EXPERT), BF16),
                            pltpu.VMEM((D_EXPERT, D_MODEL), BF16)]),
        out_shape=jax.ShapeDtypeStruct((rows, w), U32),
        compiler_params=_params("arbitrary"),
        name="experts",
    )(block_expert, block_valid, n_used, xs, wg, wu, wd)


def _combine_kernel(ya_ref, yb_ref, x_ref, info_ref, lg_ref, lb_ref, out_ref, *, alpha):
    out_ref[...] = _combine_tile(ya_ref[...], yb_ref[...], x_ref[...], info_ref[...], lg_ref[...], lb_ref[...],
                                 alpha)


def _combine(y_tok, x1, info, ln_g, ln_b, alpha):
    n = x1.shape[0]
    tm = ROW_TM
    tok = lambda w: pl.BlockSpec((tm, w), lambda i: (i, 0))
    slot = lambda k: pl.BlockSpec((tm, D_MODEL // 2), lambda i: (i + k * (n // tm), 0))
    full = lambda shape: pl.BlockSpec(shape, lambda i: (0,) * len(shape))
    return pl.pallas_call(
        functools.partial(_combine_kernel, alpha=alpha),
        grid=(n // tm,),
        in_specs=[slot(0), slot(1), tok(D_MODEL), tok(LANES), full((1, D_MODEL)), full((1, D_MODEL))],
        out_specs=tok(D_MODEL),
        out_shape=jax.ShapeDtypeStruct((n, D_MODEL), F32),
        compiler_params=_params("arbitrary"),
        name="combine",
    )(y_tok, y_tok, x1, info, ln_g, ln_b)


def _rope_tables(s):
    half = HEAD_DIM // 2
    inv_freq = ROPE_THETA ** (-jnp.arange(half, dtype=F32) / half)
    ang = jnp.arange(s, dtype=F32)[:, None] * inv_freq[None, :]
    cos = jnp.cos(ang)
    sin = jnp.sin(ang)
    cos_t = jnp.tile(cos, (1, LANES // half))
    sin_t = jnp.tile(jnp.concatenate([-sin, sin], axis=1), (1, LANES // HEAD_DIM))
    return cos_t, sin_t


def _forward(x, w_in, w_out, branch_gain, sgu_gain, sgu_w, sgu_b, conv_w, ln_gain, ln_bias,
             router_group_w, router_group_b, router_expert_w, router_expert_b,
             expert_w_gate, expert_w_up, expert_w_down):
    b, s, _ = x.shape
    depth = w_in.shape[0]
    n = b * s
    alpha = (2.0 * depth) ** 0.25
    cos_t, sin_t = _rope_tables(s)
    perm, perm_t = _tile_perm()
    tb = MOE_TB
    n_blocks = (2 * n + N_EXPERTS * (tb - 1) + tb - 1) // tb
    rows = n_blocks * tb

    pending = None
    for l in range(depth):
        g = branch_gain[l]
        bias_tile = jnp.repeat(sgu_b[l].T, SGU_W // SGU_GROUPS, axis=1)
        mix_params = (sgu_gain[l][None, :], sgu_w[l], bias_tile, conv_w[l], g[None, ATTN_W:])
        outs = _proj(x if pending is None else pending, w_in[l].astype(BF16), cos_t, sin_t, perm, mix_params,
                     b, s, alpha)
        if pending is not None:
            x, outs = outs[0], outs[1:]
        q, k, v, mbc, q16, k16, v16 = outs
        ma = _attention(q, k, v, q16, k16, v16, perm_t, g[None, :ATTN_W])

        rw = jnp.zeros((D_MODEL, LANES), F32)
        rw = rw.at[:, :MOE_GROUPS].set(router_group_w[l])
        rw = rw.at[:, MOE_GROUPS:MOE_GROUPS + N_EXPERTS].set(router_expert_w[l])
        rw_hi = rw.astype(BF16)
        rw_hilo = jnp.concatenate([rw_hi, (rw - rw_hi.astype(F32)).astype(BF16)], axis=1)
        rbias = jnp.zeros((1, LANES), F32)
        rbias = rbias.at[0, :MOE_GROUPS].set(router_group_b[l])
        rbias = rbias.at[0, MOE_GROUPS:MOE_GROUPS + N_EXPERTS].set(router_expert_b[l])
        wo = w_out[l].astype(BF16)
        x1, xp, info, info_t, cnt = _outproj(
            ma.reshape(n, ATTN_W), mbc.reshape(n, SGU_W + CONV_W), x.reshape(n, D_MODEL),
            wo[:ATTN_W], wo[ATTN_W:], ln_gain[l, 0][None], ln_bias[l, 0][None], rw_hilo, rbias, alpha)

        counts = cnt[0, :N_EXPERTS].astype(I32)
        padded = (counts + tb - 1) // tb * tb
        pad_end = jnp.cumsum(padded)
        pad_start = pad_end - padded
        ex = info_t[0:2].astype(I32)
        start_of = jnp.zeros_like(ex)
        for e in range(N_EXPERTS):
            start_of = jnp.where(ex == e, pad_start[e], start_of)
        dest_kn = start_of + info_t[4:6].astype(I32)
        blk_row0 = jnp.arange(n_blocks, dtype=I32) * tb
        block_expert = jnp.minimum(jnp.sum((pad_end[None, :] <= blk_row0[:, None]).astype(I32), axis=1),
                                   N_EXPERTS - 1)
        block_valid = jnp.clip(pad_start[block_expert] + counts[block_expert] - blk_row0, 0, tb)
        n_used = (pad_end[-1:] // tb).astype(I32)

        xs = _sc_dispatch(xp, dest_kn, rows)
        ys = _experts(block_expert, block_valid, n_used, xs, expert_w_gate, expert_w_up, expert_w_down, l)
        y_tok = _sc_gather(ys, dest_kn.reshape(2 * n))
        pending = (y_tok, x1, info, ln_gain[l, 1][None], ln_bias[l, 1][None])
    return _combine(*pending, alpha).reshape(b, s, D_MODEL)


def kernel(x, w_in, w_out, branch_gain, sgu_gain, sgu_w, sgu_b, conv_w, ln_gain, ln_bias, router_group_w, router_group_b, router_expert_w, router_expert_b, expert_w_gate, expert_w_up, expert_w_down):
    return _forward(x, w_in, w_out, branch_gain, sgu_gain, sgu_w, sgu_b, conv_w, ln_gain, ln_bias,
                    router_group_w, router_group_b, router_expert_w, router_expert_b,
                    expert_w_gate, expert_w_up, expert_w_down)
```

```python
import functools
import math

import jax
import jax.numpy as jnp
from jax import lax
from jax.experimental import pallas as pl
from jax.experimental.pallas import tpu as pltpu
from jax.experimental.pallas import tpu_sc as plsc

F32 = jnp.float32
BF16 = jnp.bfloat16
U32 = jnp.uint32
I32 = jnp.int32

D_MODEL = 1024
HEAD_DIM = 64
ATTN_W = 512
N_HEADS = 8
SGU_W = 256
SGU_GROUPS = 4
SGU_CHUNK = 128
CONV_W = 256
REST_W = 2 * SGU_W + 3 * CONV_W
PROJ_W = 3 * ATTN_W + REST_W
DILATIONS = (16, 4, 1)
BAND = 128
ROPE_THETA = 10000.0
MOE_GROUPS = 4
EXPERTS_PER_GROUP = 8
N_EXPERTS = MOE_GROUPS * EXPERTS_PER_GROUP
D_EXPERT = 512
EPS = 1e-5
NEG = -1e30

LANES = 128
VMEM_LIMIT = 56 * 1024 * 1024

TILE = 512
PERM_D = 16
ATTN_TQ = 512
OUT_TM = 512
MOE_TB = 512
ROW_TM = 512
SC_CORES = 2
SC_SUBCORES = 16
SC_WORKERS = SC_CORES * SC_SUBCORES
SC_CHUNK = 64


def _params(*sem):
    return pltpu.CompilerParams(dimension_semantics=sem, vmem_limit_bytes=VMEM_LIMIT)


def _pack_pairs(x):
    w = x.shape[1] // 2
    lo = lax.bitcast_convert_type(x[:, :w].astype(BF16).astype(F32), U32)
    hi = lax.bitcast_convert_type(x[:, w:].astype(BF16).astype(F32), U32)
    return (lo >> 16) | (hi & jnp.uint32(0xFFFF0000))


def _unpack_pairs(p):
    lo = lax.bitcast_convert_type(p << 16, F32)
    hi = lax.bitcast_convert_type(p & jnp.uint32(0xFFFF0000), F32)
    return lo, hi


def _tile_perm():
    i = jnp.arange(TILE, dtype=I32)
    tok = PERM_D * (i % (TILE // PERM_D)) + i // (TILE // PERM_D)
    p = (jnp.arange(TILE, dtype=I32)[None, :] == tok[:, None]).astype(BF16)
    return p, p.T


def _mixers(rest, prev_zz, sg, sw_ref, sb, cw_ref, bg):
    tm = rest.shape[0]
    u = rest[:, 0:SGU_W]
    z = rest[:, SGU_W:2 * SGU_W]
    gb = rest[:, 2 * SGU_W:2 * SGU_W + CONV_W]
    gc = rest[:, 2 * SGU_W + CONV_W:2 * SGU_W + 2 * CONV_W]
    hh = rest[:, 2 * SGU_W + 2 * CONV_W:]

    gdim = SGU_W // SGU_GROUPS
    ri = lax.broadcasted_iota(I32, (SGU_W, SGU_W), 0) // gdim
    ci = lax.broadcasted_iota(I32, (SGU_W, SGU_W), 1) // gdim
    avg = jnp.where(ri == ci, 1.0 / gdim, 0.0).astype(BF16)
    z = _gelu_tanh(z)
    zc = z - _split_dot(z, avg)
    var = _split_dot(zc * zc, avg)
    zn = (zc * lax.rsqrt(var + EPS) * sg).astype(BF16)

    tr = lax.broadcasted_iota(I32, (SGU_CHUNK, SGU_CHUNK), 0)
    tc = lax.broadcasted_iota(I32, (SGU_CHUNK, SGU_CHUNK), 1)
    w_cat = jnp.concatenate(
        [jnp.where(tc <= tr, sw_ref[g], 0.0).astype(BF16) for g in range(SGU_GROUPS)], axis=1)
    lane_g = lax.broadcasted_iota(I32, (SGU_CHUNK, SGU_W), 1) // gdim
    gu = _gelu_tanh(u)
    yb = []
    for c in range(tm // SGU_CHUNK):
        zch = zn[c * SGU_CHUNK:(c + 1) * SGU_CHUNK, :]
        stack = jnp.concatenate(
            [jnp.where(lane_g == g, zch, jnp.zeros_like(zch)) for g in range(SGU_GROUPS)], axis=0)
        sp = jnp.dot(w_cat, stack, preferred_element_type=F32) + sb
        yb.append(gu[c * SGU_CHUNK:(c + 1) * SGU_CHUNK, :] * sp)
    yb = jnp.concatenate(yb, axis=0)

    zz = gc * hh
    hrows = prev_zz.shape[0]
    ext = jnp.concatenate([prev_zz, zz], axis=0)
    z1 = ext[hrows - 1:hrows - 1 + tm, :]
    z2 = ext[hrows - 2:hrows - 2 + tm, :]
    yc = gb * (cw_ref[0:1, :] * z2 + cw_ref[1:2, :] * z1 + cw_ref[2:3, :] * zz)

    def rms(t, g):
        return t * lax.rsqrt(jnp.mean(t * t, axis=1, keepdims=True) + EPS) * g

    out = jnp.concatenate([rms(yb, bg[:, 0:SGU_W]).astype(BF16), rms(yc, bg[:, SGU_W:]).astype(BF16)], axis=1)
    return out, zz[tm - hrows:, :]


def _combine_tile(ya, yb, x1, info, lg, lb, alpha):
    g1 = info[:, 2:3]
    g2 = info[:, 3:4]
    a_lo, a_hi = _unpack_pairs(ya)
    b_lo, b_hi = _unpack_pairs(yb)
    half = D_MODEL // 2
    t_lo = alpha * x1[:, 0:half] + g1 * a_lo + g2 * b_lo
    t_hi = alpha * x1[:, half:] + g1 * a_hi + g2 * b_hi
    return _layer_norm(jnp.concatenate([t_lo, t_hi], axis=1), lg, lb)


def _proj_kernel(*refs, fused, alpha):
    if fused:
        (ya_ref, yb_ref, x1_ref, info_ref, lg_ref, lb_ref, w_ref, cos_ref, sin_ref, p_ref,
         sg_ref, sw_ref, sb_ref, cw_ref, bg_ref,
         x_out, q_ref, k_ref, v_ref, m_ref, q16_ref, k16_ref, v16_ref, zz_prev) = refs
        x = _combine_tile(ya_ref[...], yb_ref[...], x1_ref[...], info_ref[...], lg_ref[...], lb_ref[...], alpha)
        x_out[...] = x
    else:
        (x_ref, w_ref, cos_ref, sin_ref, p_ref, sg_ref, sw_ref, sb_ref, cw_ref, bg_ref,
         q_ref, k_ref, v_ref, m_ref, q16_ref, k16_ref, v16_ref, zz_prev) = refs
        x = x_ref[...]
    si = pl.program_id(1)

    @pl.when(si == 0)
    def _():
        zz_prev[...] = jnp.zeros_like(zz_prev)

    xb = x.astype(BF16)
    cos = cos_ref[...]
    sin = sin_ref[...]
    lane = lax.broadcasted_iota(I32, cos.shape, 1)
    first_half = (lane % HEAD_DIM) < (HEAD_DIM // 2)
    rows = TILE // PERM_D

    def store_both(val_bf, out_ref, out16_ref):
        out_ref[...] = val_bf
        perm = jnp.dot(p_ref[...], val_bf, preferred_element_type=F32).astype(BF16)
        for r in range(PERM_D):
            out16_ref[:, r * ATTN_W:(r + 1) * ATTN_W] = perm[r * rows:(r + 1) * rows, :]

    def rope(col0, scale):
        t = jnp.dot(xb, w_ref[:, col0:col0 + ATTN_W], preferred_element_type=F32)
        out = []
        for c in range(ATTN_W // LANES):
            tc = t[:, c * LANES:(c + 1) * LANES]
            partner = jnp.where(first_half, pltpu.roll(tc, LANES - 32, 1), pltpu.roll(tc, 32, 1))
            out.append(((tc * cos + partner * sin) * scale).astype(BF16))
        return jnp.concatenate(out, axis=1)

    rest = jnp.dot(xb, w_ref[:, 3 * ATTN_W:], preferred_element_type=F32)
    tail = [zz_prev[...]]

    def mix_chunk(c):
        rows_c = slice(c * SGU_CHUNK, (c + 1) * SGU_CHUNK)
        out, tail[0] = _mixers(rest[rows_c, :], tail[0], sg_ref[...], sw_ref, sb_ref[...], cw_ref, bg_ref[...])
        m_ref[rows_c, :] = out

    qb = rope(0, HEAD_DIM ** -0.5)
    mix_chunk(0)
    store_both(qb, q_ref, q16_ref)
    kb = rope(ATTN_W, 1.0)
    mix_chunk(1)
    store_both(kb, k_ref, k16_ref)
    vb = jnp.dot(xb, w_ref[:, 2 * ATTN_W:3 * ATTN_W], preferred_element_type=F32).astype(BF16)
    mix_chunk(2)
    store_both(vb, v_ref, v16_ref)
    mix_chunk(3)
    zz_prev[...] = tail[0]


def _proj(x_or_parts, w_bf, cos_t, sin_t, perm, mix_params, b, s, alpha):
    tm = TILE
    nt = s // tm
    fused = isinstance(x_or_parts, tuple)
    out3 = jax.ShapeDtypeStruct((b, s, ATTN_W), BF16)
    out16 = jax.ShapeDtypeStruct((b, s // PERM_D, PERM_D * ATTN_W), BF16)
    tok = lambda width: pl.BlockSpec((None, tm, width), lambda bi, si: (bi, si, 0))
    tok16 = pl.BlockSpec((None, tm // PERM_D, PERM_D * ATTN_W), lambda bi, si: (bi, si, 0))
    full = lambda shape: pl.BlockSpec(shape, lambda bi, si: (0,) * len(shape))
    flat = lambda width, off: pl.BlockSpec((tm, width), lambda bi, si: (bi * nt + si + off, 0))
    common_specs = [full((D_MODEL, PROJ_W)),
                    pl.BlockSpec((tm, LANES), lambda bi, si: (si, 0)),
                    pl.BlockSpec((tm, LANES), lambda bi, si: (si, 0)),
                    full((tm, tm)),
                    full((1, SGU_W)), full((SGU_GROUPS, SGU_CHUNK, SGU_CHUNK)),
                    full((SGU_CHUNK, SGU_W)), full((3, CONV_W)), full((1, SGU_W + CONV_W))]
    out_specs = [tok(ATTN_W), tok(ATTN_W), tok(ATTN_W), tok(SGU_W + CONV_W), tok16, tok16, tok16]
    out_shape = [out3, out3, out3, jax.ShapeDtypeStruct((b, s, SGU_W + CONV_W), BF16), out16, out16, out16]
    if fused:
        y_tok, x1, info, lg, lb = x_or_parts
        n = b * s
        ins = [y_tok, y_tok, x1, info, lg, lb]
        in_specs = [flat(D_MODEL // 2, 0), flat(D_MODEL // 2, n // tm), flat(D_MODEL, 0), flat(LANES, 0),
                    full((1, D_MODEL)), full((1, D_MODEL))]
        out_specs = [tok(D_MODEL)] + out_specs
        out_shape = [jax.ShapeDtypeStruct((b, s, D_MODEL), F32)] + out_shape
    else:
        ins = [x_or_parts]
        in_specs = [tok(D_MODEL)]
    return pl.pallas_call(
        functools.partial(_proj_kernel, fused=fused, alpha=alpha),
        grid=(b, nt),
        in_specs=in_specs + common_specs,
        out_specs=out_specs, out_shape=out_shape,
        scratch_shapes=[pltpu.VMEM((8, CONV_W), F32)],
        compiler_params=_params("arbitrary", "arbitrary"),
        name="proj",
    )(*ins, w_bf, cos_t, sin_t, perm, *mix_params)


def _band_bias(permuted):
    row = lax.broadcasted_iota(I32, (2 * BAND, BAND), 0)
    qry = lax.broadcasted_iota(I32, (2 * BAND, BAND), 1)
    key = row % BAND
    if permuted:
        qry = 4 * (qry % 32) + qry // 32
        key = 4 * (key % 32) + key // 32
    lo_key = jnp.where(row < BAND, qry, 0)
    hi_key = jnp.where(row < BAND, BAND - 1, qry)
    return jnp.where(jnp.logical_and(key >= lo_key, key <= hi_key), 0.0, NEG).astype(F32), row


def _attn_heads(get_q, get_k, get_v, bias, emit):
    lane_lo = lax.broadcasted_iota(I32, (BAND, LANES), 1) < HEAD_DIM
    lses = []
    for p in range(ATTN_W // LANES):
        ql, kk, vv = get_q(p), get_k(p), get_v(p)
        o_t = []
        for hh in range(2):
            qm = jnp.where(lane_lo if hh == 0 else jnp.logical_not(lane_lo), ql, jnp.zeros_like(ql))
            sc = lax.dot_general(kk, qm, (((1,), (1,)), ((), ())), preferred_element_type=F32)
            sc = sc + bias
            mx = jnp.max(sc, axis=0, keepdims=True)
            pe = jnp.exp(sc - mx)
            den = jnp.sum(pe, axis=0, keepdims=True)
            ot = lax.dot_general(vv, pe.astype(BF16), (((0,), (0,)), ((), ())),
                                 preferred_element_type=F32)
            o_t.append(ot / den)
            lses.append(mx + jnp.log(den))
        emit(p, jnp.concatenate([o_t[0][0:HEAD_DIM, :], o_t[1][HEAD_DIM:, :]], axis=0).T)
    return jnp.concatenate(lses + [jnp.zeros((BAND - N_HEADS, BAND), F32)], axis=0).T


def _three_terms(st):
    hi = st.astype(BF16).astype(F32)
    rest = st - hi
    mid = rest.astype(BF16).astype(F32)
    lo = (rest - mid).astype(BF16).astype(F32)
    return hi + pltpu.roll(mid, 8, 1) + pltpu.roll(lo, 16, 1)


def _attn16_kernel(q_ref, k_ref, v_ref, o_ref, st_ref, kbuf, vbuf, *, tq):
    i = pl.program_id(2)

    @pl.when(i == 0)
    def _():
        kbuf[0:BAND, :] = jnp.zeros((BAND, ATTN_W), BF16)
        vbuf[0:BAND, :] = jnp.zeros((BAND, ATTN_W), BF16)

    kbuf[BAND:BAND + tq, :] = k_ref[...]
    vbuf[BAND:BAND + tq, :] = v_ref[...]
    band_bias, col = _band_bias(False)

    def block(j, carry):
        r0 = pl.multiple_of(j * BAND, BAND)
        first_key = jnp.where(jnp.logical_or(j > 0, i > 0), 0, BAND)
        bias = jnp.where(col >= first_key, band_bias, NEG)

        def emit(p, o_pair):
            o_ref[pl.ds(r0, BAND), p * LANES:(p + 1) * LANES] = o_pair.astype(BF16)

        st = _attn_heads(lambda p: q_ref[pl.ds(r0, BAND), p * LANES:(p + 1) * LANES],
                         lambda p: kbuf[pl.ds(r0, 2 * BAND), p * LANES:(p + 1) * LANES],
                         lambda p: vbuf[pl.ds(r0, 2 * BAND), p * LANES:(p + 1) * LANES],
                         bias, emit)
        st_ref[pl.ds(r0, BAND), :] = _three_terms(st)
        return carry

    lax.fori_loop(0, tq // BAND, block, 0, unroll=True)
    kbuf[0:BAND, :] = kbuf[tq:tq + BAND, :]
    vbuf[0:BAND, :] = vbuf[tq:tq + BAND, :]


def _attn4_kernel(q_ref, k_ref, v_ref, o_ref, st_ref, qbuf, kbuf, vbuf, obuf, sbuf):
    i = pl.program_id(1)
    rows = TILE // PERM_D
    nres = 4

    @pl.when(i == 0)
    def _():
        kbuf[:, 0:BAND, :] = jnp.zeros((nres, BAND, ATTN_W), BF16)
        vbuf[:, 0:BAND, :] = jnp.zeros((nres, BAND, ATTN_W), BF16)

    @pl.when(i > 0)
    def _():
        kbuf[:, 0:BAND, :] = kbuf[:, BAND:2 * BAND, :]
        vbuf[:, 0:BAND, :] = vbuf[:, BAND:2 * BAND, :]

    for r4 in range(nres):
        for g in range(PERM_D // nres):
            lanes = slice((r4 + nres * g) * ATTN_W, (r4 + nres * g + 1) * ATTN_W)
            qbuf[r4, g * rows:(g + 1) * rows, :] = q_ref[:, lanes]
            kbuf[r4, BAND + g * rows:BAND + (g + 1) * rows, :] = k_ref[:, lanes]
            vbuf[r4, BAND + g * rows:BAND + (g + 1) * rows, :] = v_ref[:, lanes]

    band_bias, col = _band_bias(True)
    first_key = jnp.where(i > 0, 0, BAND)
    bias = jnp.where(col >= first_key, band_bias, NEG)

    def block(j, carry):
        def emit(p, o_pair):
            obuf[j, :, p * LANES:(p + 1) * LANES] = o_pair.astype(BF16)

        st = _attn_heads(lambda p: qbuf[j, :, p * LANES:(p + 1) * LANES],
                         lambda p: kbuf[j, :, p * LANES:(p + 1) * LANES],
                         lambda p: vbuf[j, :, p * LANES:(p + 1) * LANES],
                         bias, emit)
        sbuf[j] = _three_terms(st)
        return carry

    lax.fori_loop(0, nres, block, 0, unroll=True)

    for r4 in range(nres):
        for g in range(PERM_D // nres):
            grp = r4 + nres * g
            o_ref[:, grp * ATTN_W:(grp + 1) * ATTN_W] = obuf[r4, g * rows:(g + 1) * rows, :]
            st_ref[:, grp * LANES:(grp + 1) * LANES] = sbuf[r4, g * rows:(g + 1) * rows, :]


def _attn1_kernel(q_ref, k_ref, v_ref, o4_ref, s4_ref, o16_ref, s16_ref, pt_ref, g_ref, out_ref,
                  kbuf, vbuf, acc, o4t, o16t, stt, s1t, *, tq):
    i = pl.program_id(1)

    @pl.when(i == 0)
    def _():
        kbuf[0:BAND, :] = jnp.zeros((BAND, ATTN_W), BF16)
        vbuf[0:BAND, :] = jnp.zeros((BAND, ATTN_W), BF16)

    kbuf[BAND:BAND + tq, :] = k_ref[...]
    vbuf[BAND:BAND + tq, :] = v_ref[...]

    def rows_of(ref, w):
        return jnp.concatenate([ref[:, r * w:(r + 1) * w] for r in range(PERM_D)], axis=0)

    o4t[...] = jnp.dot(pt_ref[...], rows_of(o4_ref, ATTN_W), preferred_element_type=F32).astype(BF16)
    o16t[...] = jnp.dot(pt_ref[...], rows_of(o16_ref, ATTN_W), preferred_element_type=F32).astype(BF16)
    terms = (rows_of(s4_ref, LANES) + pltpu.roll(rows_of(s16_ref, LANES), 32, 1)).astype(BF16)
    terms = jnp.dot(pt_ref[...], terms, preferred_element_type=F32)
    stt[...] = terms + pltpu.roll(terms, LANES - 8, 1) + pltpu.roll(terms, LANES - 16, 1)
    band_bias, col = _band_bias(False)

    def block(j, carry):
        r0 = pl.multiple_of(j * BAND, BAND)
        first_key = jnp.where(jnp.logical_or(j > 0, i > 0), 0, BAND)
        bias = jnp.where(col >= first_key, band_bias, NEG)

        def emit(p, o_pair):
            acc[pl.ds(r0, BAND), p * LANES:(p + 1) * LANES] = o_pair

        s1t[pl.ds(r0, BAND), :] = _attn_heads(
            lambda p: q_ref[pl.ds(r0, BAND), p * LANES:(p + 1) * LANES],
            lambda p: kbuf[pl.ds(r0, 2 * BAND), p * LANES:(p + 1) * LANES],
            lambda p: vbuf[pl.ds(r0, 2 * BAND), p * LANES:(p + 1) * LANES],
            bias, emit)
        return carry

    lax.fori_loop(0, tq // BAND, block, 0, unroll=True)
    kbuf[0:BAND, :] = kbuf[tq:tq + BAND, :]
    vbuf[0:BAND, :] = vbuf[tq:tq + BAND, :]

    head_of = lax.broadcasted_iota(I32, (LANES, ATTN_W), 1) // HEAD_DIM
    expand = jnp.where(lax.broadcasted_iota(I32, (LANES, ATTN_W), 0) == head_of, 1.0, 0.0).astype(BF16)

    def merge(j, carry):
        r0 = pl.multiple_of(j * BAND, BAND)
        is_head = lax.broadcasted_iota(I32, (BAND, LANES), 1) < N_HEADS
        l1 = s1t[pl.ds(r0, BAND), :]
        l4 = stt[pl.ds(r0, BAND), :]
        l16 = pltpu.roll(l4, LANES - 32, 1)
        top = jnp.maximum(l1, jnp.maximum(l4, l16))
        e1 = jnp.exp(l1 - top)
        e4 = jnp.exp(l4 - top)
        e16 = jnp.exp(l16 - top)
        inv = 1.0 / (e1 + e4 + e16)

        def spread(e):
            w = jnp.where(is_head, e * inv, 0.0)
            hi = w.astype(BF16)
            lo = (w - hi.astype(F32)).astype(BF16)
            return (jnp.dot(hi, expand, preferred_element_type=F32)
                    + jnp.dot(lo, expand, preferred_element_type=F32))

        y = (spread(e1) * acc[pl.ds(r0, BAND), :]
             + spread(e4) * o4t[pl.ds(r0, BAND), :].astype(F32)
             + spread(e16) * o16t[pl.ds(r0, BAND), :].astype(F32))
        scale = lax.rsqrt(jnp.mean(y * y, axis=1, keepdims=True) + EPS)
        out_ref[pl.ds(r0, BAND), :] = (y * scale * g_ref[...]).astype(BF16)
        return carry

    lax.fori_loop(0, tq // BAND, merge, 0, unroll=True)


def _attention(q, k, v, q16, k16, v16, perm_t, gain):
    b, s, _ = q.shape
    m16 = s // PERM_D
    rows = TILE // PERM_D
    o16_shape = [jax.ShapeDtypeStruct((b, m16, PERM_D * ATTN_W), BF16),
                 jax.ShapeDtypeStruct((b, m16, PERM_D * LANES), F32)]

    tq16 = min(ATTN_TQ, m16)
    blk16 = lambda w: pl.BlockSpec((None, tq16, w), lambda bi, ri, ii: (bi, ii, ri))
    o16, s16 = pl.pallas_call(
        functools.partial(_attn16_kernel, tq=tq16),
        grid=(b, PERM_D, m16 // tq16),
        in_specs=[blk16(ATTN_W)] * 3, out_specs=[blk16(ATTN_W), blk16(LANES)], out_shape=o16_shape,
        scratch_shapes=[pltpu.VMEM((tq16 + BAND, ATTN_W), BF16)] * 2,
        compiler_params=_params("arbitrary", "arbitrary", "arbitrary"),
        name="attn_d16",
    )(q16, k16, v16)

    tile = lambda w: pl.BlockSpec((None, rows, PERM_D * w), lambda bi, ii: (bi, ii, 0))
    o4, s4 = pl.pallas_call(
        _attn4_kernel,
        grid=(b, s // TILE),
        in_specs=[tile(ATTN_W)] * 3, out_specs=[tile(ATTN_W), tile(LANES)], out_shape=o16_shape,
        scratch_shapes=[pltpu.VMEM((4, BAND, ATTN_W), BF16), pltpu.VMEM((4, 2 * BAND, ATTN_W), BF16),
                        pltpu.VMEM((4, 2 * BAND, ATTN_W), BF16), pltpu.VMEM((4, BAND, ATTN_W), BF16),
                        pltpu.VMEM((4, BAND, LANES), F32)],
        compiler_params=_params("arbitrary", "arbitrary"),
        name="attn_d4",
    )(q16, k16, v16)

    tq = TILE
    tok = lambda w: pl.BlockSpec((None, tq, w), lambda bi, ii: (bi, ii, 0))
    full = lambda shape: pl.BlockSpec(shape, lambda bi, ii: (0,) * len(shape))
    return pl.pallas_call(
        functools.partial(_attn1_kernel, tq=tq),
        grid=(b, s // tq),
        in_specs=[tok(ATTN_W)] * 3 + [tile(ATTN_W), tile(LANES), tile(ATTN_W), tile(LANES),
                                      full((TILE, TILE)), full((1, ATTN_W))],
        out_specs=tok(ATTN_W),
        out_shape=jax.ShapeDtypeStruct((b, s, ATTN_W), BF16),
        scratch_shapes=[pltpu.VMEM((tq + BAND, ATTN_W), BF16), pltpu.VMEM((tq + BAND, ATTN_W), BF16),
                        pltpu.VMEM((tq, ATTN_W), F32),
                        pltpu.VMEM((tq, ATTN_W), BF16), pltpu.VMEM((tq, ATTN_W), BF16),
                        pltpu.VMEM((tq, LANES), F32), pltpu.VMEM((tq, LANES), F32)],
        compiler_params=_params("arbitrary", "arbitrary"),
        name="attn_d1",
    )(q, k, v, o4, s4, o16, s16, perm_t, gain)


def _gelu_tanh(x):
    c = math.sqrt(2.0 / math.pi)
    return x * (0.5 * (1.0 + jnp.tanh(c * (x + 0.044715 * (x * x * x)))))


def _split_dot(x, m_bf):
    hi = x.astype(BF16)
    lo = (x - hi.astype(F32)).astype(BF16)
    return (jnp.dot(hi, m_bf, preferred_element_type=F32)
            + jnp.dot(lo, m_bf, preferred_element_type=F32))


def _layer_norm(t, g, b):
    mu = jnp.mean(t, axis=1, keepdims=True)
    tc = t - mu
    var = jnp.mean(tc * tc, axis=1, keepdims=True)
    return tc * lax.rsqrt(var + EPS) * g + b


def _outproj_kernel(ma_ref, mb_ref, x_ref, wa_ref, wb_ref, lg_ref, lb_ref, rw_ref, rb_ref,
                    x1_ref, xp_ref, info_ref, infot_ref, cnt_ref, run_ref, *, tm, alpha):
    i = pl.program_id(0)

    @pl.when(i == 0)
    def _():
        run_ref[...] = jnp.zeros_like(run_ref)

    y = (jnp.dot(ma_ref[...], wa_ref[...], preferred_element_type=F32)
         + jnp.dot(mb_ref[...], wb_ref[...], preferred_element_type=F32))
    x1 = _layer_norm(alpha * x_ref[...] + y, lg_ref[...], lb_ref[...])
    x1_ref[...] = x1
    xp_ref[...] = _pack_pairs(x1)

    hi = x1.astype(BF16)
    lo = (x1 - hi.astype(F32)).astype(BF16)
    both = jnp.dot(hi, rw_ref[...], preferred_element_type=F32)
    logit = (both[:, :LANES] + both[:, LANES:]
             + jnp.dot(lo, rw_ref[:, :LANES], preferred_element_type=F32)) + rb_ref[...]
    lane = lax.broadcasted_iota(I32, (tm, LANES), 1)
    lane_f = lane.astype(F32)

    def top(mask):
        v = jnp.max(jnp.where(mask, logit, NEG), axis=1, keepdims=True)
        first = jnp.min(jnp.where(jnp.logical_and(mask, logit == v), lane_f, float(LANES)),
                        axis=1, keepdims=True)
        return v, first.astype(I32)

    is_g = lane < MOE_GROUPS
    gmax, gidx = top(is_g)
    g_p = 1.0 / jnp.sum(jnp.where(is_g, jnp.exp(logit - gmax), 0.0), axis=1, keepdims=True)
    in_grp = jnp.logical_and(lane >= MOE_GROUPS + gidx * EXPERTS_PER_GROUP,
                             lane < MOE_GROUPS + (gidx + 1) * EXPERTS_PER_GROUP)
    v1, i1 = top(in_grp)
    v2, i2 = top(jnp.logical_and(in_grp, lane != i1))
    e21 = jnp.exp(v2 - v1)
    gate1 = g_p / (1.0 + e21)
    gate2 = g_p * e21 / (1.0 + e21)
    ex1 = i1 - MOE_GROUPS
    ex2 = i2 - MOE_GROUPS

    oh1 = lane == ex1
    oh2 = lane == ex2
    oh = (oh1.astype(F32) + oh2.astype(F32))
    tr = lax.broadcasted_iota(I32, (tm, tm), 0)
    tc = lax.broadcasted_iota(I32, (tm, tm), 1)
    lower = jnp.where(tc < tr, 1.0, 0.0).astype(BF16)
    before = jnp.dot(lower, oh.astype(BF16), preferred_element_type=F32) + run_ref[0:1, :]
    rank1 = jnp.sum(jnp.where(oh1, before, 0.0), axis=1, keepdims=True)
    rank2 = jnp.sum(jnp.where(oh2, before, 0.0), axis=1, keepdims=True)
    run_new = run_ref[0:1, :] + jnp.sum(oh, axis=0, keepdims=True)
    run_ref[...] = jnp.broadcast_to(run_new, run_ref.shape)
    cnt_ref[...] = jnp.broadcast_to(run_new, cnt_ref.shape)

    info = jnp.where(lane == 0, ex1.astype(F32), 0.0)
    info = jnp.where(lane == 1, ex2.astype(F32), info)
    info = jnp.where(lane == 2, gate1, info)
    info = jnp.where(lane == 3, gate2, info)
    info = jnp.where(lane == 4, rank1, info)
    info = jnp.where(lane == 5, rank2, info)
    info_ref[...] = info
    infot_ref[...] = info.T[0:8, :]


def _outproj(ma, mbc, x, wo_a, wo_b, ln_g, ln_b, rw_hilo, rbias, alpha):
    n = x.shape[0]
    tm = OUT_TM
    tok = lambda w: pl.BlockSpec((tm, w), lambda i: (i, 0))
    full = lambda shape: pl.BlockSpec(shape, lambda i: (0,) * len(shape))
    return pl.pallas_call(
        functools.partial(_outproj_kernel, tm=tm, alpha=alpha),
        grid=(n // tm,),
        in_specs=[tok(ATTN_W), tok(SGU_W + CONV_W), tok(D_MODEL),
                  full((ATTN_W, D_MODEL)), full((SGU_W + CONV_W, D_MODEL)),
                  full((1, D_MODEL)), full((1, D_MODEL)),
                  full((D_MODEL, 2 * LANES)), full((1, LANES))],
        out_specs=[tok(D_MODEL), tok(D_MODEL // 2), tok(LANES), pl.BlockSpec((8, tm), lambda i: (0, i)),
                   full((8, LANES))],
        out_shape=[jax.ShapeDtypeStruct((n, D_MODEL), F32),
                   jax.ShapeDtypeStruct((n, D_MODEL // 2), U32),
                   jax.ShapeDtypeStruct((n, LANES), F32),
                   jax.ShapeDtypeStruct((8, n), F32),
                   jax.ShapeDtypeStruct((8, LANES), F32)],
        scratch_shapes=[pltpu.VMEM((8, LANES), F32)],
        compiler_params=_params("arbitrary"),
        name="outproj",
    )(ma, mbc, x, wo_a, wo_b, ln_g, ln_b, rw_hilo, rbias)


def _sc_mesh():
    return plsc.VectorSubcoreMesh(core_axis_name="c", subcore_axis_name="s",
                                  num_cores=SC_CORES, num_subcores=SC_SUBCORES)


def _sc_chunk(rows_per_worker):
    return min(SC_CHUNK, rows_per_worker // 2)


def _sc_dispatch(xp, dest_kn, rows):
    n, w = xp.shape
    t_per_w = n // SC_WORKERS
    chunk = _sc_chunk(t_per_w)
    nchunk = t_per_w // chunk

    def body(src_hbm, dest_hbm, out_hbm, idx_v, rows_v, lsem, ssem):
        wid = lax.axis_index("s") * SC_CORES + lax.axis_index("c")
        base = wid * t_per_w
        pltpu.sync_copy(dest_hbm.at[0, wid], idx_v.at[0])
        pltpu.sync_copy(dest_hbm.at[1, wid], idx_v.at[1])

        def load(c, slot):
            return pltpu.make_async_copy(src_hbm.at[pl.ds(base + c * chunk, chunk)], rows_v.at[slot],
                                         lsem.at[slot])

        def put(c, slot, kk):
            return pltpu.make_async_copy(rows_v.at[slot], out_hbm.at[idx_v.at[kk, c]], ssem.at[slot])

        load(0, 0).start()

        @pl.loop(0, nchunk, step=2)
        def _(c):
            for b in range(2):
                cc = c + b
                load(cc, b).wait()

                @pl.when(cc + 1 < nchunk)
                def _():
                    @pl.when(cc >= 1)
                    def _():
                        put(cc - 1, 1 - b, 0).wait()
                        put(cc - 1, 1 - b, 1).wait()
                    load(cc + 1, 1 - b).start()

                put(cc, b, 0).start()
                put(cc, b, 1).start()

        for b in range(2):
            put(nchunk - 2 + b, b, 0).wait()
            put(nchunk - 2 + b, b, 1).wait()

    call = pl.kernel(
        body, mesh=_sc_mesh(),
        out_type=jax.ShapeDtypeStruct((rows, w), U32),
        scratch_types=[pltpu.VMEM((2, nchunk, chunk), I32), pltpu.VMEM((2, chunk, w), U32),
                       pltpu.SemaphoreType.DMA((2,)), pltpu.SemaphoreType.DMA((2,))],
        name="sc_dispatch")
    return call(xp, dest_kn.reshape(2, SC_WORKERS, nchunk, chunk))


def _sc_gather(table, idx):
    b = idx.shape[0]
    w = table.shape[1]
    b_per_w = b // SC_WORKERS
    chunk = _sc_chunk(b_per_w)
    nchunk = b_per_w // chunk

    def body(table_hbm, idx_hbm, out_hbm, idx_v, rows_v, gsem, osem):
        wid = lax.axis_index("s") * SC_CORES + lax.axis_index("c")
        base = wid * b_per_w
        pltpu.sync_copy(idx_hbm.at[wid], idx_v)

        def gather(c, slot):
            return pltpu.make_async_copy(table_hbm.at[idx_v.at[c]], rows_v.at[slot], gsem.at[slot])

        def put(c, slot):
            return pltpu.make_async_copy(rows_v.at[slot], out_hbm.at[pl.ds(base + c * chunk, chunk)],
                                         osem.at[slot])

        gather(0, 0).start()

        @pl.loop(0, nchunk, step=2)
        def _(c):
            for b in range(2):
                cc = c + b
                gather(cc, b).wait()

                @pl.when(cc + 1 < nchunk)
                def _():
                    @pl.when(cc >= 1)
                    def _():
                        put(cc - 1, 1 - b).wait()
                    gather(cc + 1, 1 - b).start()

                put(cc, b).start()

        put(nchunk - 2, 0).wait()
        put(nchunk - 1, 1).wait()

    call = pl.kernel(
        body, mesh=_sc_mesh(),
        out_type=jax.ShapeDtypeStruct((b, w), table.dtype),
        scratch_types=[pltpu.VMEM((nchunk, chunk), I32), pltpu.VMEM((2, chunk, w), table.dtype),
                       pltpu.SemaphoreType.DMA((2,)), pltpu.SemaphoreType.DMA((2,))],
        name="sc_gather")
    return call(table, idx.reshape(SC_WORKERS, nchunk, chunk))


def _expert_kernel(be_ref, nv_ref, nu_ref, xs_ref, wg_ref, wu_ref, wd_ref, y_ref, wgb, wub, wdb):
    i = pl.program_id(0)
    cur = jnp.minimum(i, nu_ref[0] - 1)
    new_expert = jnp.logical_or(i == 0, be_ref[cur] != be_ref[jnp.maximum(cur - 1, 0)])

    @pl.when(jnp.logical_and(i < nu_ref[0], new_expert))
    def _():
        wgb[...] = wg_ref[...].astype(BF16)
        wub[...] = wu_ref[...].astype(BF16)
        wdb[...] = wd_ref[...].astype(BF16)

    @pl.when(i < nu_ref[0])
    def _():
        row = lax.broadcasted_iota(I32, xs_ref.shape, 0)
        lo, hi = _unpack_pairs(jnp.where(row < nv_ref[i], xs_ref[...], jnp.uint32(0)))
        xb = jnp.concatenate([lo, hi], axis=1).astype(BF16)
        g = jnp.dot(xb, wgb[...], preferred_element_type=F32)
        u = jnp.dot(xb, wub[...], preferred_element_type=F32)
        hdn = (g * (1.0 / (1.0 + jnp.exp(-g))) * u).astype(BF16)
        y_ref[...] = _pack_pairs(jnp.dot(hdn, wdb[...], preferred_element_type=F32))

    @pl.when(i >= nu_ref[0])
    def _():
        y_ref[...] = jnp.zeros_like(y_ref)


def _experts(block_expert, block_valid, n_used, xs, wg, wu, wd, layer):
    rows, w = xs.shape
    tb = MOE_TB
    blk = lambda i, be, nv, nu: (jnp.minimum(i, nu[0] - 1), 0)
    oblk = lambda i, be, nv, nu: (i, 0)
    wsel = lambda i, be, nv, nu: (layer, be[jnp.minimum(i, nu[0] - 1)], 0, 0)
    return pl.pallas_call(
        _expert_kernel,
        grid_spec=pltpu.PrefetchScalarGridSpec(
            num_scalar_prefetch=3,
            grid=(rows // tb,),
            in_specs=[pl.BlockSpec((tb, w), blk),
                      pl.BlockSpec((None, None, D_MODEL, D_EXPERT), wsel),
                      pl.BlockSpec((None, None, D_MODEL, D_EXPERT), wsel),
                      pl.BlockSpec((None, None, D_EXPERT, D_MODEL), wsel)],
            out_specs=pl.BlockSpec((tb, w), oblk),
            scratch_shapes=[pltpu.VMEM((D_MODEL, D_EXPERT), BF16), pltpu.VMEM((D_MODEL, D_EXPERT), BF16),
                            pltpu.VMEM((D_EXPERT, D_MODEL), BF16)]),
        out_shape=jax.ShapeDtypeStruct((rows, w), U32),
        compiler_params=_params("arbitrary"),
        name="experts",
    )(block_expert, block_valid, n_used, xs, wg, wu, wd)


def _combine_kernel(ya_ref, yb_ref, x_ref, info_ref, lg_ref, lb_ref, out_ref, *, alpha):
    out_ref[...] = _combine_tile(ya_ref[...], yb_ref[...], x_ref[...], info_ref[...], lg_ref[...], lb_ref[...],
                                 alpha)


def _combine(y_tok, x1, info, ln_g, ln_b, alpha):
    n = x1.shape[0]
    tm = ROW_TM
    tok = lambda w: pl.BlockSpec((tm, w), lambda i: (i, 0))
    slot = lambda k: pl.BlockSpec((tm, D_MODEL // 2), lambda i: (i + k * (n // tm), 0))
    full = lambda shape: pl.BlockSpec(shape, lambda i: (0,) * len(shape))
    return pl.pallas_call(
        functools.partial(_combine_kernel, alpha=alpha),
        grid=(n // tm,),
        in_specs=[slot(0), slot(1), tok(D_MODEL), tok(LANES), full((1, D_MODEL)), full((1, D_MODEL))],
        out_specs=tok(D_MODEL),
        out_shape=jax.ShapeDtypeStruct((n, D_MODEL), F32),
        compiler_params=_params("arbitrary"),
        name="combine",
    )(y_tok, y_tok, x1, info, ln_g, ln_b)


def _rope_tables(s):
    half = HEAD_DIM // 2
    inv_freq = ROPE_THETA ** (-jnp.arange(half, dtype=F32) / half)
    ang = jnp.arange(s, dtype=F32)[:, None] * inv_freq[None, :]
    cos = jnp.cos(ang)
    sin = jnp.sin(ang)
    cos_t = jnp.tile(cos, (1, LANES // half))
    sin_t = jnp.tile(jnp.concatenate([-sin, sin], axis=1), (1, LANES // HEAD_DIM))
    return cos_t, sin_t


def _forward(x, w_in, w_out, branch_gain, sgu_gain, sgu_w, sgu_b, conv_w, ln_gain, ln_bias,
             router_group_w, router_group_b, router_expert_w, router_expert_b,
             expert_w_gate, expert_w_up, expert_w_down):
    b, s, _ = x.shape
    depth = w_in.shape[0]
    n = b * s
    alpha = (2.0 * depth) ** 0.25
    cos_t, sin_t = _rope_tables(s)
    perm, perm_t = _tile_perm()
    tb = MOE_TB
    n_blocks = (2 * n + N_EXPERTS * (tb - 1) + tb - 1) // tb
    rows = n_blocks * tb

    pending = None
    for l in range(depth):
        g = branch_gain[l]
        bias_tile = jnp.repeat(sgu_b[l].T, SGU_W // SGU_GROUPS, axis=1)
        mix_params = (sgu_gain[l][None, :], sgu_w[l], bias_tile, conv_w[l], g[None, ATTN_W:])
        outs = _proj(x if pending is None else pending, w_in[l].astype(BF16), cos_t, sin_t, perm, mix_params,
                     b, s, alpha)
        if pending is not None:
            x, outs = outs[0], outs[1:]
        q, k, v, mbc, q16, k16, v16 = outs
        ma = _attention(q, k, v, q16, k16, v16, perm_t, g[None, :ATTN_W])

        rw = jnp.zeros((D_MODEL, LANES), F32)
        rw = rw.at[:, :MOE_GROUPS].set(router_group_w[l])
        rw = rw.at[:, MOE_GROUPS:MOE_GROUPS + N_EXPERTS].set(router_expert_w[l])
        rw_hi = rw.astype(BF16)
        rw_hilo = jnp.concatenate([rw_hi, (rw - rw_hi.astype(F32)).astype(BF16)], axis=1)
        rbias = jnp.zeros((1, LANES), F32)
        rbias = rbias.at[0, :MOE_GROUPS].set(router_group_b[l])
        rbias = rbias.at[0, MOE_GROUPS:MOE_GROUPS + N_EXPERTS].set(router_expert_b[l])
        wo = w_out[l].astype(BF16)
        x1, xp, info, info_t, cnt = _outproj(
            ma.reshape(n, ATTN_W), mbc.reshape(n, SGU_W + CONV_W), x.reshape(n, D_MODEL),
            wo[:ATTN_W], wo[ATTN_W:], ln_gain[l, 0][None], ln_bias[l, 0][None], rw_hilo, rbias, alpha)

        counts = cnt[0, :N_EXPERTS].astype(I32)
        padded = (counts + tb - 1) // tb * tb
        pad_end = jnp.cumsum(padded)
        pad_start = pad_end - padded
        ex = info_t[0:2].astype(I32)
        start_of = jnp.zeros_like(ex)
        for e in range(N_EXPERTS):
            start_of = jnp.where(ex == e, pad_start[e], start_of)
        dest_kn = start_of + info_t[4:6].astype(I32)
        blk_row0 = jnp.arange(n_blocks, dtype=I32) * tb
        block_expert = jnp.minimum(jnp.sum((pad_end[None, :] <= blk_row0[:, None]).astype(I32), axis=1),
                                   N_EXPERTS - 1)
        block_valid = jnp.clip(pad_start[block_expert] + counts[block_expert] - blk_row0, 0, tb)
        n_used = (pad_end[-1:] // tb).astype(I32)

        xs = _sc_dispatch(xp, dest_kn, rows)
        ys = _experts(block_expert, block_valid, n_used, xs, expert_w_gate, expert_w_up, expert_w_down, l)
        y_tok = _sc_gather(ys, dest_kn.reshape(2 * n))
        pending = (y_tok, x1, info, ln_gain[l, 1][None], ln_bias[l, 1][None])
    return _combine(*pending, alpha).reshape(b, s, D_MODEL)


def kernel(x, w_in, w_out, branch_gain, sgu_gain, sgu_w, sgu_b, conv_w, ln_gain, ln_bias, router_group_w, router_group_b, router_expert_w, router_expert_b, expert_w_gate, expert_w_up, expert_w_down):
    return _forward(x, w_in, w_out, branch_gain, sgu_gain, sgu_w, sgu_b, conv_w, ln_gain, ln_bias,
                    router_group_w, router_group_b, router_expert_w, router_expert_b,
                    expert_w_gate, expert_w_up, expert_w_down)
```

```python
import functools
import math

import jax
import jax.numpy as jnp
from jax import lax
from jax.experimental import pallas as pl
from jax.experimental.pallas import tpu as pltpu
from jax.experimental.pallas import tpu_sc as plsc

F32 = jnp.float32
BF16 = jnp.bfloat16
U32 = jnp.uint32
I32 = jnp.int32

D_MODEL = 1024
HEAD_DIM = 64
ATTN_W = 512
N_HEADS = 8
SGU_W = 256
SGU_GROUPS = 4
SGU_CHUNK = 128
CONV_W = 256
REST_W = 2 * SGU_W + 3 * CONV_W
PROJ_W = 3 * ATTN_W + REST_W
DILATIONS = (16, 4, 1)
BAND = 128
ROPE_THETA = 10000.0
MOE_GROUPS = 4
EXPERTS_PER_GROUP = 8
N_EXPERTS = MOE_GROUPS * EXPERTS_PER_GROUP
D_EXPERT = 512
EPS = 1e-5
NEG = -1e30

LANES = 128
VMEM_LIMIT = 56 * 1024 * 1024

TILE = 512
PERM_D = 16
PERM_TILE = 256
ATTN_TQ = 512
OUT_TM = 512
MOE_TB = 512
ROW_TM = 512
SC_CORES = 2
SC_SUBCORES = 16
SC_WORKERS = SC_CORES * SC_SUBCORES
SC_CHUNK = 64


def _params(*sem):
    return pltpu.CompilerParams(dimension_semantics=sem, vmem_limit_bytes=VMEM_LIMIT)


def _pack_pairs(x):
    w = x.shape[1] // 2
    lo = lax.bitcast_convert_type(x[:, :w].astype(BF16).astype(F32), U32)
    hi = lax.bitcast_convert_type(x[:, w:].astype(BF16).astype(F32), U32)
    return (lo >> 16) | (hi & jnp.uint32(0xFFFF0000))


def _unpack_pairs(p):
    lo = lax.bitcast_convert_type(p << 16, F32)
    hi = lax.bitcast_convert_type(p & jnp.uint32(0xFFFF0000), F32)
    return lo, hi


def _tile_perm():
    i = jnp.arange(PERM_TILE, dtype=I32)
    tok = PERM_D * (i % (PERM_TILE // PERM_D)) + i // (PERM_TILE // PERM_D)
    p = (jnp.arange(PERM_TILE, dtype=I32)[None, :] == tok[:, None]).astype(BF16)
    return p, p.T


def _mixers(rest, prev_zz, sg, sw_ref, sb, cw_ref, bg):
    tm = rest.shape[0]
    u = rest[:, 0:SGU_W]
    z = rest[:, SGU_W:2 * SGU_W]
    gb = rest[:, 2 * SGU_W:2 * SGU_W + CONV_W]
    gc = rest[:, 2 * SGU_W + CONV_W:2 * SGU_W + 2 * CONV_W]
    hh = rest[:, 2 * SGU_W + 2 * CONV_W:]

    gdim = SGU_W // SGU_GROUPS
    ri = lax.broadcasted_iota(I32, (SGU_W, SGU_W), 0) // gdim
    ci = lax.broadcasted_iota(I32, (SGU_W, SGU_W), 1) // gdim
    avg = jnp.where(ri == ci, 1.0 / gdim, 0.0).astype(BF16)
    z = _gelu_tanh(z)
    zc = z - _split_dot(z, avg)
    var = _split_dot(zc * zc, avg)
    zn = (zc * lax.rsqrt(var + EPS) * sg).astype(BF16)

    tr = lax.broadcasted_iota(I32, (SGU_CHUNK, SGU_CHUNK), 0)
    tc = lax.broadcasted_iota(I32, (SGU_CHUNK, SGU_CHUNK), 1)
    w_cat = jnp.concatenate(
        [jnp.where(tc <= tr, sw_ref[g], 0.0).astype(BF16) for g in range(SGU_GROUPS)], axis=1)
    lane_g = lax.broadcasted_iota(I32, (SGU_CHUNK, SGU_W), 1) // gdim
    gu = _gelu_tanh(u)
    yb = []
    for c in range(tm // SGU_CHUNK):
        zch = zn[c * SGU_CHUNK:(c + 1) * SGU_CHUNK, :]
        stack = jnp.concatenate(
            [jnp.where(lane_g == g, zch, jnp.zeros_like(zch)) for g in range(SGU_GROUPS)], axis=0)
        sp = jnp.dot(w_cat, stack, preferred_element_type=F32) + sb
        yb.append(gu[c * SGU_CHUNK:(c + 1) * SGU_CHUNK, :] * sp)
    yb = jnp.concatenate(yb, axis=0)

    zz = gc * hh
    hrows = prev_zz.shape[0]
    ext = jnp.concatenate([prev_zz, zz], axis=0)
    z1 = ext[hrows - 1:hrows - 1 + tm, :]
    z2 = ext[hrows - 2:hrows - 2 + tm, :]
    yc = gb * (cw_ref[0:1, :] * z2 + cw_ref[1:2, :] * z1 + cw_ref[2:3, :] * zz)

    def rms(t, g):
        return t * lax.rsqrt(jnp.mean(t * t, axis=1, keepdims=True) + EPS) * g

    out = jnp.concatenate([rms(yb, bg[:, 0:SGU_W]).astype(BF16), rms(yc, bg[:, SGU_W:]).astype(BF16)], axis=1)
    return out, zz[tm - hrows:, :]


def _combine_tile(ya, yb, x1, info, lg, lb, alpha):
    g1 = info[:, 2:3]
    g2 = info[:, 3:4]
    a_lo, a_hi = _unpack_pairs(ya)
    b_lo, b_hi = _unpack_pairs(yb)
    half = D_MODEL // 2
    t_lo = alpha * x1[:, 0:half] + g1 * a_lo + g2 * b_lo
    t_hi = alpha * x1[:, half:] + g1 * a_hi + g2 * b_hi
    return _layer_norm(jnp.concatenate([t_lo, t_hi], axis=1), lg, lb)


def _proj_kernel(*refs, fused, alpha):
    if fused:
        (ya_ref, yb_ref, x1_ref, info_ref, lg_ref, lb_ref, w_ref, cos_ref, sin_ref, p_ref,
         sg_ref, sw_ref, sb_ref, cw_ref, bg_ref,
         x_out, q_ref, k_ref, v_ref, m_ref, q16_ref, k16_ref, v16_ref, zz_prev) = refs
        x = _combine_tile(ya_ref[...], yb_ref[...], x1_ref[...], info_ref[...], lg_ref[...], lb_ref[...], alpha)
        x_out[...] = x
    else:
        (x_ref, w_ref, cos_ref, sin_ref, p_ref, sg_ref, sw_ref, sb_ref, cw_ref, bg_ref,
         q_ref, k_ref, v_ref, m_ref, q16_ref, k16_ref, v16_ref, zz_prev) = refs
        x = x_ref[...]
    si = pl.program_id(1)

    @pl.when(si == 0)
    def _():
        zz_prev[...] = jnp.zeros_like(zz_prev)

    xb = x.astype(BF16)
    cos = cos_ref[...]
    sin = sin_ref[...]
    lane = lax.broadcasted_iota(I32, cos.shape, 1)
    first_half = (lane % HEAD_DIM) < (HEAD_DIM // 2)
    rows = PERM_TILE // PERM_D

    def store_both(val_bf, out_ref, out16_ref):
        out_ref[...] = val_bf
        for h in range(TILE // PERM_TILE):
            perm = jnp.dot(p_ref[...], val_bf[h * PERM_TILE:(h + 1) * PERM_TILE, :],
                           preferred_element_type=F32).astype(BF16)
            for r in range(PERM_D):
                out16_ref[h * rows:(h + 1) * rows, r * ATTN_W:(r + 1) * ATTN_W] = perm[r * rows:(r + 1) * rows, :]

    def rope(col0, scale):
        t = jnp.dot(xb, w_ref[:, col0:col0 + ATTN_W], preferred_element_type=F32)
        out = []
        for c in range(ATTN_W // LANES):
            tc = t[:, c * LANES:(c + 1) * LANES]
            partner = jnp.where(first_half, pltpu.roll(tc, LANES - 32, 1), pltpu.roll(tc, 32, 1))
            out.append(((tc * cos + partner * sin) * scale).astype(BF16))
        return jnp.concatenate(out, axis=1)

    rest = jnp.dot(xb, w_ref[:, 3 * ATTN_W:], preferred_element_type=F32)
    tail = [zz_prev[...]]

    def mix_chunk(c):
        rows_c = slice(c * SGU_CHUNK, (c + 1) * SGU_CHUNK)
        out, tail[0] = _mixers(rest[rows_c, :], tail[0], sg_ref[...], sw_ref, sb_ref[...], cw_ref, bg_ref[...])
        m_ref[rows_c, :] = out

    qb = rope(0, HEAD_DIM ** -0.5)
    mix_chunk(0)
    store_both(qb, q_ref, q16_ref)
    kb = rope(ATTN_W, 1.0)
    mix_chunk(1)
    store_both(kb, k_ref, k16_ref)
    vb = jnp.dot(xb, w_ref[:, 2 * ATTN_W:3 * ATTN_W], preferred_element_type=F32).astype(BF16)
    mix_chunk(2)
    store_both(vb, v_ref, v16_ref)
    mix_chunk(3)
    zz_prev[...] = tail[0]


def _proj(x_or_parts, w_bf, cos_t, sin_t, perm, mix_params, b, s, alpha):
    tm = TILE
    nt = s // tm
    fused = isinstance(x_or_parts, tuple)
    out3 = jax.ShapeDtypeStruct((b, s, ATTN_W), BF16)
    out16 = jax.ShapeDtypeStruct((b, s // PERM_D, PERM_D * ATTN_W), BF16)
    tok = lambda width: pl.BlockSpec((None, tm, width), lambda bi, si: (bi, si, 0))
    tok16 = pl.BlockSpec((None, tm // PERM_D, PERM_D * ATTN_W), lambda bi, si: (bi, si, 0))
    full = lambda shape: pl.BlockSpec(shape, lambda bi, si: (0,) * len(shape))
    flat = lambda width, off: pl.BlockSpec((tm, width), lambda bi, si: (bi * nt + si + off, 0))
    common_specs = [full((D_MODEL, PROJ_W)),
                    pl.BlockSpec((tm, LANES), lambda bi, si: (si, 0)),
                    pl.BlockSpec((tm, LANES), lambda bi, si: (si, 0)),
                    full((PERM_TILE, PERM_TILE)),
                    full((1, SGU_W)), full((SGU_GROUPS, SGU_CHUNK, SGU_CHUNK)),
                    full((SGU_CHUNK, SGU_W)), full((3, CONV_W)), full((1, SGU_W + CONV_W))]
    out_specs = [tok(ATTN_W), tok(ATTN_W), tok(ATTN_W), tok(SGU_W + CONV_W), tok16, tok16, tok16]
    out_shape = [out3, out3, out3, jax.ShapeDtypeStruct((b, s, SGU_W + CONV_W), BF16), out16, out16, out16]
    if fused:
        y_tok, x1, info, lg, lb = x_or_parts
        n = b * s
        ins = [y_tok, y_tok, x1, info, lg, lb]
        in_specs = [flat(D_MODEL // 2, 0), flat(D_MODEL // 2, n // tm), flat(D_MODEL, 0), flat(LANES, 0),
                    full((1, D_MODEL)), full((1, D_MODEL))]
        out_specs = [tok(D_MODEL)] + out_specs
        out_shape = [jax.ShapeDtypeStruct((b, s, D_MODEL), F32)] + out_shape
    else:
        ins = [x_or_parts]
        in_specs = [tok(D_MODEL)]
    return pl.pallas_call(
        functools.partial(_proj_kernel, fused=fused, alpha=alpha),
        grid=(b, nt),
        in_specs=in_specs + common_specs,
        out_specs=out_specs, out_shape=out_shape,
        scratch_shapes=[pltpu.VMEM((8, CONV_W), F32)],
        compiler_params=_params("arbitrary", "arbitrary"),
        name="proj",
    )(*ins, w_bf, cos_t, sin_t, perm, *mix_params)


def _band_bias(permuted):
    row = lax.broadcasted_iota(I32, (2 * BAND, BAND), 0)
    qry = lax.broadcasted_iota(I32, (2 * BAND, BAND), 1)
    key = row % BAND
    if permuted:
        qry = 4 * (qry % 32) + qry // 32
        key = 4 * (key % 32) + key // 32
    lo_key = jnp.where(row < BAND, qry, 0)
    hi_key = jnp.where(row < BAND, BAND - 1, qry)
    return jnp.where(jnp.logical_and(key >= lo_key, key <= hi_key), 0.0, NEG).astype(F32), row


def _attn_heads(get_q, get_k, get_v, bias, emit):
    lane_lo = lax.broadcasted_iota(I32, (BAND, LANES), 1) < HEAD_DIM
    lses = []
    for p in range(ATTN_W // LANES):
        ql, kk, vv = get_q(p), get_k(p), get_v(p)
        o_t = []
        for hh in range(2):
            qm = jnp.where(lane_lo if hh == 0 else jnp.logical_not(lane_lo), ql, jnp.zeros_like(ql))
            sc = lax.dot_general(kk, qm, (((1,), (1,)), ((), ())), preferred_element_type=F32)
            sc = sc + bias
            mx = jnp.max(sc, axis=0, keepdims=True)
            pe = jnp.exp(sc - mx)
            den = jnp.sum(pe, axis=0, keepdims=True)
            ot = lax.dot_general(vv, pe.astype(BF16), (((0,), (0,)), ((), ())),
                                 preferred_element_type=F32)
            o_t.append(ot / den)
            lses.append(mx + jnp.log(den))
        emit(p, jnp.concatenate([o_t[0][0:HEAD_DIM, :], o_t[1][HEAD_DIM:, :]], axis=0).T)
    return jnp.concatenate(lses + [jnp.zeros((BAND - N_HEADS, BAND), F32)], axis=0).T


def _three_terms(st):
    hi = st.astype(BF16).astype(F32)
    rest = st - hi
    mid = rest.astype(BF16).astype(F32)
    lo = (rest - mid).astype(BF16).astype(F32)
    return hi + pltpu.roll(mid, 8, 1) + pltpu.roll(lo, 16, 1)


def _attn16_kernel(q_ref, k_ref, v_ref, o_ref, st_ref, kbuf, vbuf, *, tq):
    i = pl.program_id(2)

    @pl.when(i == 0)
    def _():
        kbuf[0:BAND, :] = jnp.zeros((BAND, ATTN_W), BF16)
        vbuf[0:BAND, :] = jnp.zeros((BAND, ATTN_W), BF16)

    kbuf[BAND:BAND + tq, :] = k_ref[...]
    vbuf[BAND:BAND + tq, :] = v_ref[...]
    band_bias, col = _band_bias(False)

    def block(j, carry):
        r0 = pl.multiple_of(j * BAND, BAND)
        first_key = jnp.where(jnp.logical_or(j > 0, i > 0), 0, BAND)
        bias = jnp.where(col >= first_key, band_bias, NEG)

        def emit(p, o_pair):
            o_ref[pl.ds(r0, BAND), p * LANES:(p + 1) * LANES] = o_pair.astype(BF16)

        st = _attn_heads(lambda p: q_ref[pl.ds(r0, BAND), p * LANES:(p + 1) * LANES],
                         lambda p: kbuf[pl.ds(r0, 2 * BAND), p * LANES:(p + 1) * LANES],
                         lambda p: vbuf[pl.ds(r0, 2 * BAND), p * LANES:(p + 1) * LANES],
                         bias, emit)
        st_ref[pl.ds(r0, BAND), :] = _three_terms(st)
        return carry

    lax.fori_loop(0, tq // BAND, block, 0, unroll=True)
    kbuf[0:BAND, :] = kbuf[tq:tq + BAND, :]
    vbuf[0:BAND, :] = vbuf[tq:tq + BAND, :]


def _attn4_kernel(q_ref, k_ref, v_ref, o_ref, st_ref, qbuf, kbuf, vbuf, obuf, sbuf):
    i = pl.program_id(1)
    rows = TILE // PERM_D
    nres = 4

    @pl.when(i == 0)
    def _():
        kbuf[:, 0:BAND, :] = jnp.zeros((nres, BAND, ATTN_W), BF16)
        vbuf[:, 0:BAND, :] = jnp.zeros((nres, BAND, ATTN_W), BF16)

    @pl.when(i > 0)
    def _():
        kbuf[:, 0:BAND, :] = kbuf[:, BAND:2 * BAND, :]
        vbuf[:, 0:BAND, :] = vbuf[:, BAND:2 * BAND, :]

    for r4 in range(nres):
        for g in range(PERM_D // nres):
            lanes = slice((r4 + nres * g) * ATTN_W, (r4 + nres * g + 1) * ATTN_W)
            qbuf[r4, g * rows:(g + 1) * rows, :] = q_ref[:, lanes]
            kbuf[r4, BAND + g * rows:BAND + (g + 1) * rows, :] = k_ref[:, lanes]
            vbuf[r4, BAND + g * rows:BAND + (g + 1) * rows, :] = v_ref[:, lanes]

    band_bias, col = _band_bias(True)
    first_key = jnp.where(i > 0, 0, BAND)
    bias = jnp.where(col >= first_key, band_bias, NEG)

    def block(j, carry):
        def emit(p, o_pair):
            obuf[j, :, p * LANES:(p + 1) * LANES] = o_pair.astype(BF16)

        st = _attn_heads(lambda p: qbuf[j, :, p * LANES:(p + 1) * LANES],
                         lambda p: kbuf[j, :, p * LANES:(p + 1) * LANES],
                         lambda p: vbuf[j, :, p * LANES:(p + 1) * LANES],
                         bias, emit)
        sbuf[j] = _three_terms(st)
        return carry

    lax.fori_loop(0, nres, block, 0, unroll=True)

    for r4 in range(nres):
        for g in range(PERM_D // nres):
            grp = r4 + nres * g
            o_ref[:, grp * ATTN_W:(grp + 1) * ATTN_W] = obuf[r4, g * rows:(g + 1) * rows, :]
            st_ref[:, grp * LANES:(grp + 1) * LANES] = sbuf[r4, g * rows:(g + 1) * rows, :]


def _attn1_kernel(q_ref, k_ref, v_ref, o4_ref, s4_ref, o16_ref, s16_ref, pt_ref, g_ref, out_ref,
                  kbuf, vbuf, acc, o4t, o16t, stt, s1t, *, tq):
    i = pl.program_id(1)

    @pl.when(i == 0)
    def _():
        kbuf[0:BAND, :] = jnp.zeros((BAND, ATTN_W), BF16)
        vbuf[0:BAND, :] = jnp.zeros((BAND, ATTN_W), BF16)

    kbuf[BAND:BAND + tq, :] = k_ref[...]
    vbuf[BAND:BAND + tq, :] = v_ref[...]

    prow = PERM_TILE // PERM_D
    for h in range(tq // PERM_TILE):
        tok_rows = slice(h * PERM_TILE, (h + 1) * PERM_TILE)

        def rows_of(ref, w):
            return jnp.concatenate([ref[h * prow:(h + 1) * prow, r * w:(r + 1) * w] for r in range(PERM_D)], axis=0)

        o4t[tok_rows, :] = jnp.dot(pt_ref[...], rows_of(o4_ref, ATTN_W), preferred_element_type=F32).astype(BF16)
        o16t[tok_rows, :] = jnp.dot(pt_ref[...], rows_of(o16_ref, ATTN_W), preferred_element_type=F32).astype(BF16)
        terms = (rows_of(s4_ref, LANES) + pltpu.roll(rows_of(s16_ref, LANES), 32, 1)).astype(BF16)
        terms = jnp.dot(pt_ref[...], terms, preferred_element_type=F32)
        stt[tok_rows, :] = terms + pltpu.roll(terms, LANES - 8, 1) + pltpu.roll(terms, LANES - 16, 1)
    band_bias, col = _band_bias(False)

    def block(j, carry):
        r0 = pl.multiple_of(j * BAND, BAND)
        first_key = jnp.where(jnp.logical_or(j > 0, i > 0), 0, BAND)
        bias = jnp.where(col >= first_key, band_bias, NEG)

        def emit(p, o_pair):
            acc[pl.ds(r0, BAND), p * LANES:(p + 1) * LANES] = o_pair

        s1t[pl.ds(r0, BAND), :] = _attn_heads(
            lambda p: q_ref[pl.ds(r0, BAND), p * LANES:(p + 1) * LANES],
            lambda p: kbuf[pl.ds(r0, 2 * BAND), p * LANES:(p + 1) * LANES],
            lambda p: vbuf[pl.ds(r0, 2 * BAND), p * LANES:(p + 1) * LANES],
            bias, emit)
        return carry

    lax.fori_loop(0, tq // BAND, block, 0, unroll=True)
    kbuf[0:BAND, :] = kbuf[tq:tq + BAND, :]
    vbuf[0:BAND, :] = vbuf[tq:tq + BAND, :]

    head_of = lax.broadcasted_iota(I32, (LANES, ATTN_W), 1) // HEAD_DIM
    expand = jnp.where(lax.broadcasted_iota(I32, (LANES, ATTN_W), 0) == head_of, 1.0, 0.0).astype(BF16)

    def merge(j, carry):
        r0 = pl.multiple_of(j * BAND, BAND)
        is_head = lax.broadcasted_iota(I32, (BAND, LANES), 1) < N_HEADS
        l1 = s1t[pl.ds(r0, BAND), :]
        l4 = stt[pl.ds(r0, BAND), :]
        l16 = pltpu.roll(l4, LANES - 32, 1)
        top = jnp.maximum(l1, jnp.maximum(l4, l16))
        e1 = jnp.exp(l1 - top)
        e4 = jnp.exp(l4 - top)
        e16 = jnp.exp(l16 - top)
        inv = 1.0 / (e1 + e4 + e16)

        def spread(e):
            w = jnp.where(is_head, e * inv, 0.0)
            hi = w.astype(BF16)
            lo = (w - hi.astype(F32)).astype(BF16)
            return (jnp.dot(hi, expand, preferred_element_type=F32)
                    + jnp.dot(lo, expand, preferred_element_type=F32))

        y = (spread(e1) * acc[pl.ds(r0, BAND), :]
             + spread(e4) * o4t[pl.ds(r0, BAND), :].astype(F32)
             + spread(e16) * o16t[pl.ds(r0, BAND), :].astype(F32))
        scale = lax.rsqrt(jnp.mean(y * y, axis=1, keepdims=True) + EPS)
        out_ref[pl.ds(r0, BAND), :] = (y * scale * g_ref[...]).astype(BF16)
        return carry

    lax.fori_loop(0, tq // BAND, merge, 0, unroll=True)


def _attention(q, k, v, q16, k16, v16, perm_t, gain):
    b, s, _ = q.shape
    m16 = s // PERM_D
    rows = TILE // PERM_D
    o16_shape = [jax.ShapeDtypeStruct((b, m16, PERM_D * ATTN_W), BF16),
                 jax.ShapeDtypeStruct((b, m16, PERM_D * LANES), F32)]

    tq16 = min(ATTN_TQ, m16)
    blk16 = lambda w: pl.BlockSpec((None, tq16, w), lambda bi, ri, ii: (bi, ii, ri))
    o16, s16 = pl.pallas_call(
        functools.partial(_attn16_kernel, tq=tq16),
        grid=(b, PERM_D, m16 // tq16),
        in_specs=[blk16(ATTN_W)] * 3, out_specs=[blk16(ATTN_W), blk16(LANES)], out_shape=o16_shape,
        scratch_shapes=[pltpu.VMEM((tq16 + BAND, ATTN_W), BF16)] * 2,
        compiler_params=_params("arbitrary", "arbitrary", "arbitrary"),
        name="attn_d16",
    )(q16, k16, v16)

    tile = lambda w: pl.BlockSpec((None, rows, PERM_D * w), lambda bi, ii: (bi, ii, 0))
    o4, s4 = pl.pallas_call(
        _attn4_kernel,
        grid=(b, s // TILE),
        in_specs=[tile(ATTN_W)] * 3, out_specs=[tile(ATTN_W), tile(LANES)], out_shape=o16_shape,
        scratch_shapes=[pltpu.VMEM((4, BAND, ATTN_W), BF16), pltpu.VMEM((4, 2 * BAND, ATTN_W), BF16),
                        pltpu.VMEM((4, 2 * BAND, ATTN_W), BF16), pltpu.VMEM((4, BAND, ATTN_W), BF16),
                        pltpu.VMEM((4, BAND, LANES), F32)],
        compiler_params=_params("arbitrary", "arbitrary"),
        name="attn_d4",
    )(q16, k16, v16)

    tq = TILE
    tok = lambda w: pl.BlockSpec((None, tq, w), lambda bi, ii: (bi, ii, 0))
    full = lambda shape: pl.BlockSpec(shape, lambda bi, ii: (0,) * len(shape))
    return pl.pallas_call(
        functools.partial(_attn1_kernel, tq=tq),
        grid=(b, s // tq),
        in_specs=[tok(ATTN_W)] * 3 + [tile(ATTN_W), tile(LANES), tile(ATTN_W), tile(LANES),
                                      full((PERM_TILE, PERM_TILE)), full((1, ATTN_W))],
        out_specs=tok(ATTN_W),
        out_shape=jax.ShapeDtypeStruct((b, s, ATTN_W), BF16),
        scratch_shapes=[pltpu.VMEM((tq + BAND, ATTN_W), BF16), pltpu.VMEM((tq + BAND, ATTN_W), BF16),
                        pltpu.VMEM((tq, ATTN_W), F32),
                        pltpu.VMEM((tq, ATTN_W), BF16), pltpu.VMEM((tq, ATTN_W), BF16),
                        pltpu.VMEM((tq, LANES), F32), pltpu.VMEM((tq, LANES), F32)],
        compiler_params=_params("arbitrary", "arbitrary"),
        name="attn_d1",
    )(q, k, v, o4, s4, o16, s16, perm_t, gain)


def _gelu_tanh(x):
    c = math.sqrt(2.0 / math.pi)
    return x * (0.5 * (1.0 + jnp.tanh(c * (x + 0.044715 * (x * x * x)))))


def _split_dot(x, m_bf):
    hi = x.astype(BF16)
    lo = (x - hi.astype(F32)).astype(BF16)
    return (jnp.dot(hi, m_bf, preferred_element_type=F32)
            + jnp.dot(lo, m_bf, preferred_element_type=F32))


def _layer_norm(t, g, b):
    mu = jnp.mean(t, axis=1, keepdims=True)
    tc = t - mu
    var = jnp.mean(tc * tc, axis=1, keepdims=True)
    return tc * lax.rsqrt(var + EPS) * g + b


def _outproj_kernel(ma_ref, mb_ref, x_ref, wa_ref, wb_ref, lg_ref, lb_ref, rw_ref, rb_ref,
                    x1_ref, xp_ref, info_ref, infot_ref, cnt_ref, run_ref, *, tm, alpha):
    i = pl.program_id(0)

    @pl.when(i == 0)
    def _():
        run_ref[...] = jnp.zeros_like(run_ref)

    y = (jnp.dot(ma_ref[...], wa_ref[...], preferred_element_type=F32)
         + jnp.dot(mb_ref[...], wb_ref[...], preferred_element_type=F32))
    x1 = _layer_norm(alpha * x_ref[...] + y, lg_ref[...], lb_ref[...])
    x1_ref[...] = x1
    xp_ref[...] = _pack_pairs(x1)

    hi = x1.astype(BF16)
    lo = (x1 - hi.astype(F32)).astype(BF16)
    both = jnp.dot(hi, rw_ref[...], preferred_element_type=F32)
    logit = (both[:, :LANES] + both[:, LANES:]
             + jnp.dot(lo, rw_ref[:, :LANES], preferred_element_type=F32)) + rb_ref[...]
    lane = lax.broadcasted_iota(I32, (tm, LANES), 1)
    lane_f = lane.astype(F32)

    def top(mask):
        v = jnp.max(jnp.where(mask, logit, NEG), axis=1, keepdims=True)
        first = jnp.min(jnp.where(jnp.logical_and(mask, logit == v), lane_f, float(LANES)),
                        axis=1, keepdims=True)
        return v, first.astype(I32)

    is_g = lane < MOE_GROUPS
    gmax, gidx = top(is_g)
    g_p = 1.0 / jnp.sum(jnp.where(is_g, jnp.exp(logit - gmax), 0.0), axis=1, keepdims=True)
    in_grp = jnp.logical_and(lane >= MOE_GROUPS + gidx * EXPERTS_PER_GROUP,
                             lane < MOE_GROUPS + (gidx + 1) * EXPERTS_PER_GROUP)
    v1, i1 = top(in_grp)
    v2, i2 = top(jnp.logical_and(in_grp, lane != i1))
    e21 = jnp.exp(v2 - v1)
    gate1 = g_p / (1.0 + e21)
    gate2 = g_p * e21 / (1.0 + e21)
    ex1 = i1 - MOE_GROUPS
    ex2 = i2 - MOE_GROUPS

    oh1 = lane == ex1
    oh2 = lane == ex2
    oh = (oh1.astype(F32) + oh2.astype(F32))
    tr = lax.broadcasted_iota(I32, (tm, tm), 0)
    tc = lax.broadcasted_iota(I32, (tm, tm), 1)
    lower = jnp.where(tc < tr, 1.0, 0.0).astype(BF16)
    before = jnp.dot(lower, oh.astype(BF16), preferred_element_type=F32) + run_ref[0:1, :]
    rank1 = jnp.sum(jnp.where(oh1, before, 0.0), axis=1, keepdims=True)
    rank2 = jnp.sum(jnp.where(oh2, before, 0.0), axis=1, keepdims=True)
    run_new = run_ref[0:1, :] + jnp.sum(oh, axis=0, keepdims=True)
    run_ref[...] = jnp.broadcast_to(run_new, run_ref.shape)
    cnt_ref[...] = jnp.broadcast_to(run_new, cnt_ref.shape)

    info = jnp.where(lane == 0, ex1.astype(F32), 0.0)
    info = jnp.where(lane == 1, ex2.astype(F32), info)
    info = jnp.where(lane == 2, gate1, info)
    info = jnp.where(lane == 3, gate2, info)
    info = jnp.where(lane == 4, rank1, info)
    info = jnp.where(lane == 5, rank2, info)
    info_ref[...] = info
    infot_ref[...] = info.T[0:8, :]


def _outproj(ma, mbc, x, wo_a, wo_b, ln_g, ln_b, rw_hilo, rbias, alpha):
    n = x.shape[0]
    tm = OUT_TM
    tok = lambda w: pl.BlockSpec((tm, w), lambda i: (i, 0))
    full = lambda shape: pl.BlockSpec(shape, lambda i: (0,) * len(shape))
    return pl.pallas_call(
        functools.partial(_outproj_kernel, tm=tm, alpha=alpha),
        grid=(n // tm,),
        in_specs=[tok(ATTN_W), tok(SGU_W + CONV_W), tok(D_MODEL),
                  full((ATTN_W, D_MODEL)), full((SGU_W + CONV_W, D_MODEL)),
                  full((1, D_MODEL)), full((1, D_MODEL)),
                  full((D_MODEL, 2 * LANES)), full((1, LANES))],
        out_specs=[tok(D_MODEL), tok(D_MODEL // 2), tok(LANES), pl.BlockSpec((8, tm), lambda i: (0, i)),
                   full((8, LANES))],
        out_shape=[jax.ShapeDtypeStruct((n, D_MODEL), F32),
                   jax.ShapeDtypeStruct((n, D_MODEL // 2), U32),
                   jax.ShapeDtypeStruct((n, LANES), F32),
                   jax.ShapeDtypeStruct((8, n), F32),
                   jax.ShapeDtypeStruct((8, LANES), F32)],
        scratch_shapes=[pltpu.VMEM((8, LANES), F32)],
        compiler_params=_params("arbitrary"),
        name="outproj",
    )(ma, mbc, x, wo_a, wo_b, ln_g, ln_b, rw_hilo, rbias)


def _sc_mesh():
    return plsc.VectorSubcoreMesh(core_axis_name="c", subcore_axis_name="s",
                                  num_cores=SC_CORES, num_subcores=SC_SUBCORES)


def _sc_chunk(rows_per_worker):
    return min(SC_CHUNK, rows_per_worker // 2)


def _sc_dispatch(xp, dest_kn, rows):
    n, w = xp.shape
    t_per_w = n // SC_WORKERS
    chunk = _sc_chunk(t_per_w)
    nchunk = t_per_w // chunk

    def body(src_hbm, dest_hbm, out_hbm, idx_v, rows_v, lsem, ssem):
        wid = lax.axis_index("s") * SC_CORES + lax.axis_index("c")
        base = wid * t_per_w
        pltpu.sync_copy(dest_hbm.at[0, wid], idx_v.at[0])
        pltpu.sync_copy(dest_hbm.at[1, wid], idx_v.at[1])

        def load(c, slot):
            return pltpu.make_async_copy(src_hbm.at[pl.ds(base + c * chunk, chunk)], rows_v.at[slot],
                                         lsem.at[slot])

        def put(c, slot, kk):
            return pltpu.make_async_copy(rows_v.at[slot], out_hbm.at[idx_v.at[kk, c]], ssem.at[slot])

        load(0, 0).start()

        @pl.loop(0, nchunk, step=2)
        def _(c):
            for b in range(2):
                cc = c + b
                load(cc, b).wait()

                @pl.when(cc + 1 < nchunk)
                def _():
                    @pl.when(cc >= 1)
                    def _():
                        put(cc - 1, 1 - b, 0).wait()
                        put(cc - 1, 1 - b, 1).wait()
                    load(cc + 1, 1 - b).start()

                put(cc, b, 0).start()
                put(cc, b, 1).start()

        for b in range(2):
            put(nchunk - 2 + b, b, 0).wait()
            put(nchunk - 2 + b, b, 1).wait()

    call = pl.kernel(
        body, mesh=_sc_mesh(),
        out_type=jax.ShapeDtypeStruct((rows, w), U32),
        scratch_types=[pltpu.VMEM((2, nchunk, chunk), I32), pltpu.VMEM((2, chunk, w), U32),
                       pltpu.SemaphoreType.DMA((2,)), pltpu.SemaphoreType.DMA((2,))],
        name="sc_dispatch")
    return call(xp, dest_kn.reshape(2, SC_WORKERS, nchunk, chunk))


def _sc_gather(table, idx):
    b = idx.shape[0]
    w = table.shape[1]
    b_per_w = b // SC_WORKERS
    chunk = _sc_chunk(b_per_w)
    nchunk = b_per_w // chunk

    def body(table_hbm, idx_hbm, out_hbm, idx_v, rows_v, gsem, osem):
        wid = lax.axis_index("s") * SC_CORES + lax.axis_index("c")
        base = wid * b_per_w
        pltpu.sync_copy(idx_hbm.at[wid], idx_v)

        def gather(c, slot):
            return pltpu.make_async_copy(table_hbm.at[idx_v.at[c]], rows_v.at[slot], gsem.at[slot])

        def put(c, slot):
            return pltpu.make_async_copy(rows_v.at[slot], out_hbm.at[pl.ds(base + c * chunk, chunk)],
                                         osem.at[slot])

        gather(0, 0).start()

        @pl.loop(0, nchunk, step=2)
        def _(c):
            for b in range(2):
                cc = c + b
                gather(cc, b).wait()

                @pl.when(cc + 1 < nchunk)
                def _():
                    @pl.when(cc >= 1)
                    def _():
                        put(cc - 1, 1 - b).wait()
                    gather(cc + 1, 1 - b).start()

                put(cc, b).start()

        put(nchunk - 2, 0).wait()
        put(nchunk - 1, 1).wait()

    call = pl.kernel(
        body, mesh=_sc_mesh(),
        out_type=jax.ShapeDtypeStruct((b, w), table.dtype),
        scratch_types=[pltpu.VMEM((nchunk, chunk), I32), pltpu.VMEM((2, chunk, w), table.dtype),
                       pltpu.SemaphoreType.DMA((2,)), pltpu.SemaphoreType.DMA((2,))],
        name="sc_gather")
    return call(table, idx.reshape(SC_WORKERS, nchunk, chunk))


def _expert_kernel(be_ref, nv_ref, nu_ref, xs_ref, wg_ref, wu_ref, wd_ref, y_ref, wgb, wub, wdb):
    i = pl.program_id(0)
    cur = jnp.minimum(i, nu_ref[0] - 1)
    new_expert = jnp.logical_or(i == 0, be_ref[cur] != be_ref[jnp.maximum(cur - 1, 0)])

    @pl.when(jnp.logical_and(i < nu_ref[0], new_expert))
    def _():
        wgb[...] = wg_ref[...].astype(BF16)
        wub[...] = wu_ref[...].astype(BF16)
        wdb[...] = wd_ref[...].astype(BF16)

    @pl.when(i < nu_ref[0])
    def _():
        row = lax.broadcasted_iota(I32, xs_ref.shape, 0)
        lo, hi = _unpack_pairs(jnp.where(row < nv_ref[i], xs_ref[...], jnp.uint32(0)))
        xb = jnp.concatenate([lo, hi], axis=1).astype(BF16)
        g = jnp.dot(xb, wgb[...], preferred_element_type=F32)
        u = jnp.dot(xb, wub[...], preferred_element_type=F32)
        hdn = (g * (1.0 / (1.0 + jnp.exp(-g))) * u).astype(BF16)
        y_ref[...] = _pack_pairs(jnp.dot(hdn, wdb[...], preferred_element_type=F32))

    @pl.when(i >= nu_ref[0])
    def _():
        y_ref[...] = jnp.zeros_like(y_ref)


def _experts(block_expert, block_valid, n_used, xs, wg, wu, wd, layer):
    rows, w = xs.shape
    tb = MOE_TB
    blk = lambda i, be, nv, nu: (jnp.minimum(i, nu[0] - 1), 0)
    oblk = lambda i, be, nv, nu: (i, 0)
    wsel = lambda i, be, nv, nu: (layer, be[jnp.minimum(i, nu[0] - 1)], 0, 0)
    return pl.pallas_call(
        _expert_kernel,
        grid_spec=pltpu.PrefetchScalarGridSpec(
            num_scalar_prefetch=3,
            grid=(rows // tb,),
            in_specs=[pl.BlockSpec((tb, w), blk),
                      pl.BlockSpec((None, None, D_MODEL, D_EXPERT), wsel),
                      pl.BlockSpec((None, None, D_MODEL, D_EXPERT), wsel),
                      pl.BlockSpec((None, None, D_EXPERT, D_MODEL), wsel)],
            out_specs=pl.BlockSpec((tb, w), oblk),
            scratch_shapes=[pltpu.VMEM((D_MODEL, D_EXPERT), BF16), pltpu.VMEM((D_MODEL, D_EXPERT), BF16),
                            pltpu.VMEM((D_EXPERT, D_MODEL), BF16)]),
        out_shape=jax.ShapeDtypeStruct((rows, w), U32),
        compiler_params=_params("arbitrary"),
        name="experts",
    )(block_expert, block_valid, n_used, xs, wg, wu, wd)


def _combine_kernel(ya_ref, yb_ref, x_ref, info_ref, lg_ref, lb_ref, out_ref, *, alpha):
    out_ref[...] = _combine_tile(ya_ref[...], yb_ref[...], x_ref[...], info_ref[...], lg_ref[...], lb_ref[...],
                                 alpha)


def _combine(y_tok, x1, info, ln_g, ln_b, alpha):
    n = x1.shape[0]
    tm = ROW_TM
    tok = lambda w: pl.BlockSpec((tm, w), lambda i: (i, 0))
    slot = lambda k: pl.BlockSpec((tm, D_MODEL // 2), lambda i: (i + k * (n // tm), 0))
    full = lambda shape: pl.BlockSpec(shape, lambda i: (0,) * len(shape))
    return pl.pallas_call(
        functools.partial(_combine_kernel, alpha=alpha),
        grid=(n // tm,),
        in_specs=[slot(0), slot(1), tok(D_MODEL), tok(LANES), full((1, D_MODEL)), full((1, D_MODEL))],
        out_specs=tok(D_MODEL),
        out_shape=jax.ShapeDtypeStruct((n, D_MODEL), F32),
        compiler_params=_params("arbitrary"),
        name="combine",
    )(y_tok, y_tok, x1, info, ln_g, ln_b)


def _rope_tables(s):
    half = HEAD_DIM // 2
    inv_freq = ROPE_THETA ** (-jnp.arange(half, dtype=F32) / half)
    ang = jnp.arange(s, dtype=F32)[:, None] * inv_freq[None, :]
    cos = jnp.cos(ang)
    sin = jnp.sin(ang)
    cos_t = jnp.tile(cos, (1, LANES // half))
    sin_t = jnp.tile(jnp.concatenate([-sin, sin], axis=1), (1, LANES // HEAD_DIM))
    return cos_t, sin_t


def _forward(x, w_in, w_out, branch_gain, sgu_gain, sgu_w, sgu_b, conv_w, ln_gain, ln_bias,
             router_group_w, router_group_b, router_expert_w, router_expert_b,
             expert_w_gate, expert_w_up, expert_w_down):
    b, s, _ = x.shape
    depth = w_in.shape[0]
    n = b * s
    alpha = (2.0 * depth) ** 0.25
    cos_t, sin_t = _rope_tables(s)
    perm, perm_t = _tile_perm()
    tb = MOE_TB
    n_blocks = (2 * n + N_EXPERTS * (tb - 1) + tb - 1) // tb
    rows = n_blocks * tb

    pending = None
    for l in range(depth):
        g = branch_gain[l]
        bias_tile = jnp.repeat(sgu_b[l].T, SGU_W // SGU_GROUPS, axis=1)
        mix_params = (sgu_gain[l][None, :], sgu_w[l], bias_tile, conv_w[l], g[None, ATTN_W:])
        outs = _proj(x if pending is None else pending, w_in[l].astype(BF16), cos_t, sin_t, perm, mix_params,
                     b, s, alpha)
        if pending is not None:
            x, outs = outs[0], outs[1:]
        q, k, v, mbc, q16, k16, v16 = outs
        ma = _attention(q, k, v, q16, k16, v16, perm_t, g[None, :ATTN_W])

        rw = jnp.zeros((D_MODEL, LANES), F32)
        rw = rw.at[:, :MOE_GROUPS].set(router_group_w[l])
        rw = rw.at[:, MOE_GROUPS:MOE_GROUPS + N_EXPERTS].set(router_expert_w[l])
        rw_hi = rw.astype(BF16)
        rw_hilo = jnp.concatenate([rw_hi, (rw - rw_hi.astype(F32)).astype(BF16)], axis=1)
        rbias = jnp.zeros((1, LANES), F32)
        rbias = rbias.at[0, :MOE_GROUPS].set(router_group_b[l])
        rbias = rbias.at[0, MOE_GROUPS:MOE_GROUPS + N_EXPERTS].set(router_expert_b[l])
        wo = w_out[l].astype(BF16)
        x1, xp, info, info_t, cnt = _outproj(
            ma.reshape(n, ATTN_W), mbc.reshape(n, SGU_W + CONV_W), x.reshape(n, D_MODEL),
            wo[:ATTN_W], wo[ATTN_W:], ln_gain[l, 0][None], ln_bias[l, 0][None], rw_hilo, rbias, alpha)

        counts = cnt[0, :N_EXPERTS].astype(I32)
        padded = (counts + tb - 1) // tb * tb
        pad_end = jnp.cumsum(padded)
        pad_start = pad_end - padded
        ex = info_t[0:2].astype(I32)
        start_of = jnp.zeros_like(ex)
        for e in range(N_EXPERTS):
            start_of = jnp.where(ex == e, pad_start[e], start_of)
        dest_kn = start_of + info_t[4:6].astype(I32)
        blk_row0 = jnp.arange(n_blocks, dtype=I32) * tb
        block_expert = jnp.minimum(jnp.sum((pad_end[None, :] <= blk_row0[:, None]).astype(I32), axis=1),
                                   N_EXPERTS - 1)
        block_valid = jnp.clip(pad_start[block_expert] + counts[block_expert] - blk_row0, 0, tb)
        n_used = (pad_end[-1:] // tb).astype(I32)

        xs = _sc_dispatch(xp, dest_kn, rows)
        ys = _experts(block_expert, block_valid, n_used, xs, expert_w_gate, expert_w_up, expert_w_down, l)
        y_tok = _sc_gather(ys, dest_kn.reshape(2 * n))
        pending = (y_tok, x1, info, ln_gain[l, 1][None], ln_bias[l, 1][None])
    return _combine(*pending, alpha).reshape(b, s, D_MODEL)


def kernel(x, w_in, w_out, branch_gain, sgu_gain, sgu_w, sgu_b, conv_w, ln_gain, ln_bias, router_group_w, router_group_b, router_expert_w, router_expert_b, expert_w_gate, expert_w_up, expert_w_down):
    return _forward(x, w_in, w_out, branch_gain, sgu_gain, sgu_w, sgu_b, conv_w, ln_gain, ln_bias,
                    router_group_w, router_group_b, router_expert_w, router_expert_b,
                    expert_w_gate, expert_w_up, expert_w_down)
```

```python
import functools
import math

import jax
import jax.numpy as jnp
from jax import lax
from jax.experimental import pallas as pl
from jax.experimental.pallas import tpu as pltpu
from jax.experimental.pallas import tpu_sc as plsc

F32 = jnp.float32
BF16 = jnp.bfloat16
U32 = jnp.uint32
I32 = jnp.int32

D_MODEL = 1024
HEAD_DIM = 64
ATTN_W = 512
N_HEADS = 8
SGU_W = 256
SGU_GROUPS = 4
SGU_CHUNK = 128
CONV_W = 256
REST_W = 2 * SGU_W + 3 * CONV_W
PROJ_W = 3 * ATTN_W + REST_W
DILATIONS = (16, 4, 1)
BAND = 128
ROPE_THETA = 10000.0
MOE_GROUPS = 4
EXPERTS_PER_GROUP = 8
N_EXPERTS = MOE_GROUPS * EXPERTS_PER_GROUP
D_EXPERT = 512
EPS = 1e-5
NEG = -1e30

LANES = 128
VMEM_LIMIT = 56 * 1024 * 1024

TILE = 512
PERM_D = 16
PERM_TILE = 256
ATTN_TQ = 512
OUT_TM = 512
MOE_TB = 512
ROW_TM = 512
SC_CORES = 2
SC_SUBCORES = 16
SC_WORKERS = SC_CORES * SC_SUBCORES
SC_CHUNK = 64


def _params(*sem):
    return pltpu.CompilerParams(dimension_semantics=sem, vmem_limit_bytes=VMEM_LIMIT)


def _pack_pairs(x):
    w = x.shape[1] // 2
    lo = lax.bitcast_convert_type(x[:, :w].astype(BF16).astype(F32), U32)
    hi = lax.bitcast_convert_type(x[:, w:].astype(BF16).astype(F32), U32)
    return (lo >> 16) | (hi & jnp.uint32(0xFFFF0000))


def _unpack_pairs(p):
    lo = lax.bitcast_convert_type(p << 16, F32)
    hi = lax.bitcast_convert_type(p & jnp.uint32(0xFFFF0000), F32)
    return lo, hi


def _tile_perm():
    i = jnp.arange(PERM_TILE, dtype=I32)
    tok = PERM_D * (i % (PERM_TILE // PERM_D)) + i // (PERM_TILE // PERM_D)
    p = (jnp.arange(PERM_TILE, dtype=I32)[None, :] == tok[:, None]).astype(BF16)
    return p, p.T


def _mixers(rest, prev_zz, sg, sw_ref, sb, cw_ref, bg):
    tm = rest.shape[0]
    u = rest[:, 0:SGU_W]
    z = rest[:, SGU_W:2 * SGU_W]
    gb = rest[:, 2 * SGU_W:2 * SGU_W + CONV_W]
    gc = rest[:, 2 * SGU_W + CONV_W:2 * SGU_W + 2 * CONV_W]
    hh = rest[:, 2 * SGU_W + 2 * CONV_W:]

    gdim = SGU_W // SGU_GROUPS
    ri = lax.broadcasted_iota(I32, (SGU_W, SGU_W), 0) // gdim
    ci = lax.broadcasted_iota(I32, (SGU_W, SGU_W), 1) // gdim
    avg = jnp.where(ri == ci, 1.0 / gdim, 0.0).astype(BF16)
    z = _gelu_tanh(z)
    zc = z - _split_dot(z, avg)
    var = _split_dot(zc * zc, avg)
    zn = (zc * lax.rsqrt(var + EPS) * sg).astype(BF16)

    tr = lax.broadcasted_iota(I32, (SGU_CHUNK, SGU_CHUNK), 0)
    tc = lax.broadcasted_iota(I32, (SGU_CHUNK, SGU_CHUNK), 1)
    w_cat = jnp.concatenate(
        [jnp.where(tc <= tr, sw_ref[g], 0.0).astype(BF16) for g in range(SGU_GROUPS)], axis=1)
    lane_g = lax.broadcasted_iota(I32, (SGU_CHUNK, SGU_W), 1) // gdim
    gu = _gelu_tanh(u)
    yb = []
    for c in range(tm // SGU_CHUNK):
        zch = zn[c * SGU_CHUNK:(c + 1) * SGU_CHUNK, :]
        stack = jnp.concatenate(
            [jnp.where(lane_g == g, zch, jnp.zeros_like(zch)) for g in range(SGU_GROUPS)], axis=0)
        sp = jnp.dot(w_cat, stack, preferred_element_type=F32) + sb
        yb.append(gu[c * SGU_CHUNK:(c + 1) * SGU_CHUNK, :] * sp)
    yb = jnp.concatenate(yb, axis=0)

    zz = gc * hh
    hrows = prev_zz.shape[0]
    ext = jnp.concatenate([prev_zz, zz], axis=0)
    z1 = ext[hrows - 1:hrows - 1 + tm, :]
    z2 = ext[hrows - 2:hrows - 2 + tm, :]
    yc = gb * (cw_ref[0:1, :] * z2 + cw_ref[1:2, :] * z1 + cw_ref[2:3, :] * zz)

    def rms(t, g):
        return t * lax.rsqrt(jnp.mean(t * t, axis=1, keepdims=True) + EPS) * g

    out = jnp.concatenate([rms(yb, bg[:, 0:SGU_W]).astype(BF16), rms(yc, bg[:, SGU_W:]).astype(BF16)], axis=1)
    return out, zz[tm - hrows:, :]


def _combine_tile(ya, yb, x1, info, lg, lb, alpha):
    g1 = info[:, 2:3]
    g2 = info[:, 3:4]
    a_lo, a_hi = _unpack_pairs(ya)
    b_lo, b_hi = _unpack_pairs(yb)
    half = D_MODEL // 2
    t_lo = alpha * x1[:, 0:half] + g1 * a_lo + g2 * b_lo
    t_hi = alpha * x1[:, half:] + g1 * a_hi + g2 * b_hi
    return _layer_norm(jnp.concatenate([t_lo, t_hi], axis=1), lg, lb)


def _proj_kernel(*refs, fused, alpha):
    if fused:
        (ya_ref, yb_ref, x1_ref, info_ref, lg_ref, lb_ref, w_ref, cos_ref, sin_ref, p_ref,
         sg_ref, sw_ref, sb_ref, cw_ref, bg_ref,
         x_out, q_ref, k_ref, v_ref, m_ref, q16_ref, k16_ref, v16_ref, zz_prev) = refs
        x = _combine_tile(ya_ref[...], yb_ref[...], x1_ref[...], info_ref[...], lg_ref[...], lb_ref[...], alpha)
        x_out[...] = x
    else:
        (x_ref, w_ref, cos_ref, sin_ref, p_ref, sg_ref, sw_ref, sb_ref, cw_ref, bg_ref,
         q_ref, k_ref, v_ref, m_ref, q16_ref, k16_ref, v16_ref, zz_prev) = refs
        x = x_ref[...]
    si = pl.program_id(1)

    @pl.when(si == 0)
    def _():
        zz_prev[...] = jnp.zeros_like(zz_prev)

    xb = x.astype(BF16)
    cos = cos_ref[...]
    sin = sin_ref[...]
    lane = lax.broadcasted_iota(I32, cos.shape, 1)
    first_half = (lane % HEAD_DIM) < (HEAD_DIM // 2)
    rows = PERM_TILE // PERM_D

    def store_both(val_bf, out_ref, out16_ref):
        out_ref[...] = val_bf
        for h in range(TILE // PERM_TILE):
            perm = jnp.dot(p_ref[...], val_bf[h * PERM_TILE:(h + 1) * PERM_TILE, :],
                           preferred_element_type=F32).astype(BF16)
            for r in range(PERM_D):
                out16_ref[h * rows:(h + 1) * rows, r * ATTN_W:(r + 1) * ATTN_W] = perm[r * rows:(r + 1) * rows, :]

    def rope(col0, scale):
        t = jnp.dot(xb, w_ref[:, col0:col0 + ATTN_W], preferred_element_type=F32)
        out = []
        for c in range(ATTN_W // LANES):
            tc = t[:, c * LANES:(c + 1) * LANES]
            partner = jnp.where(first_half, pltpu.roll(tc, LANES - 32, 1), pltpu.roll(tc, 32, 1))
            out.append(((tc * cos + partner * sin) * scale).astype(BF16))
        return jnp.concatenate(out, axis=1)

    rest = jnp.dot(xb, w_ref[:, 3 * ATTN_W:], preferred_element_type=F32)
    tail = [zz_prev[...]]

    def mix_chunk(c):
        rows_c = slice(c * SGU_CHUNK, (c + 1) * SGU_CHUNK)
        out, tail[0] = _mixers(rest[rows_c, :], tail[0], sg_ref[...], sw_ref, sb_ref[...], cw_ref, bg_ref[...])
        m_ref[rows_c, :] = out

    qb = rope(0, HEAD_DIM ** -0.5)
    mix_chunk(0)
    store_both(qb, q_ref, q16_ref)
    kb = rope(ATTN_W, 1.0)
    mix_chunk(1)
    store_both(kb, k_ref, k16_ref)
    vb = jnp.dot(xb, w_ref[:, 2 * ATTN_W:3 * ATTN_W], preferred_element_type=F32).astype(BF16)
    mix_chunk(2)
    store_both(vb, v_ref, v16_ref)
    mix_chunk(3)
    zz_prev[...] = tail[0]


def _proj(x_or_parts, w_bf, cos_t, sin_t, perm, mix_params, b, s, alpha):
    tm = TILE
    nt = s // tm
    fused = isinstance(x_or_parts, tuple)
    out3 = jax.ShapeDtypeStruct((b, s, ATTN_W), BF16)
    out16 = jax.ShapeDtypeStruct((b, s // PERM_D, PERM_D * ATTN_W), BF16)
    tok = lambda width: pl.BlockSpec((None, tm, width), lambda bi, si: (bi, si, 0))
    tok16 = pl.BlockSpec((None, tm // PERM_D, PERM_D * ATTN_W), lambda bi, si: (bi, si, 0))
    full = lambda shape: pl.BlockSpec(shape, lambda bi, si: (0,) * len(shape))
    flat = lambda width, off: pl.BlockSpec((tm, width), lambda bi, si: (bi * nt + si + off, 0))
    common_specs = [full((D_MODEL, PROJ_W)),
                    pl.BlockSpec((tm, LANES), lambda bi, si: (si, 0)),
                    pl.BlockSpec((tm, LANES), lambda bi, si: (si, 0)),
                    full((PERM_TILE, PERM_TILE)),
                    full((1, SGU_W)), full((SGU_GROUPS, SGU_CHUNK, SGU_CHUNK)),
                    full((SGU_CHUNK, SGU_W)), full((3, CONV_W)), full((1, SGU_W + CONV_W))]
    out_specs = [tok(ATTN_W), tok(ATTN_W), tok(ATTN_W), tok(SGU_W + CONV_W), tok16, tok16, tok16]
    out_shape = [out3, out3, out3, jax.ShapeDtypeStruct((b, s, SGU_W + CONV_W), BF16), out16, out16, out16]
    if fused:
        y_tok, x1, info, lg, lb = x_or_parts
        n = b * s
        ins = [y_tok, y_tok, x1, info, lg, lb]
        in_specs = [flat(D_MODEL // 2, 0), flat(D_MODEL // 2, n // tm), flat(D_MODEL, 0), flat(LANES, 0),
                    full((1, D_MODEL)), full((1, D_MODEL))]
        out_specs = [tok(D_MODEL)] + out_specs
        out_shape = [jax.ShapeDtypeStruct((b, s, D_MODEL), F32)] + out_shape
    else:
        ins = [x_or_parts]
        in_specs = [tok(D_MODEL)]
    return pl.pallas_call(
        functools.partial(_proj_kernel, fused=fused, alpha=alpha),
        grid=(b, nt),
        in_specs=in_specs + common_specs,
        out_specs=out_specs, out_shape=out_shape,
        scratch_shapes=[pltpu.VMEM((8, CONV_W), F32)],
        compiler_params=_params("arbitrary", "arbitrary"),
        name="proj",
    )(*ins, w_bf, cos_t, sin_t, perm, *mix_params)


def _band_bias(permuted):
    row = lax.broadcasted_iota(I32, (2 * BAND, BAND), 0)
    qry = lax.broadcasted_iota(I32, (2 * BAND, BAND), 1)
    key = row % BAND
    if permuted:
        qry = 4 * (qry % 32) + qry // 32
        key = 4 * (key % 32) + key // 32
    lo_key = jnp.where(row < BAND, qry, 0)
    hi_key = jnp.where(row < BAND, BAND - 1, qry)
    return jnp.where(jnp.logical_and(key >= lo_key, key <= hi_key), 0.0, NEG).astype(F32), row


def _attn_heads(get_q, get_k, get_v, bias, emit):
    lane_lo = lax.broadcasted_iota(I32, (BAND, LANES), 1) < HEAD_DIM
    lses = []
    for p in range(ATTN_W // LANES):
        ql, kk, vv = get_q(p), get_k(p), get_v(p)
        o_t = []
        for hh in range(2):
            qm = jnp.where(lane_lo if hh == 0 else jnp.logical_not(lane_lo), ql, jnp.zeros_like(ql))
            sc = lax.dot_general(kk, qm, (((1,), (1,)), ((), ())), preferred_element_type=F32)
            sc = sc + bias
            mx = jnp.max(sc, axis=0, keepdims=True)
            pe = jnp.exp(sc - mx)
            den = jnp.sum(pe, axis=0, keepdims=True)
            ot = lax.dot_general(vv, pe.astype(BF16), (((0,), (0,)), ((), ())),
                                 preferred_element_type=F32)
            o_t.append(ot / den)
            lses.append(mx + jnp.log(den))
        emit(p, jnp.concatenate([o_t[0][0:HEAD_DIM, :], o_t[1][HEAD_DIM:, :]], axis=0).T)
    return jnp.concatenate(lses + [jnp.zeros((BAND - N_HEADS, BAND), F32)], axis=0).T


def _three_terms(st):
    hi = st.astype(BF16).astype(F32)
    rest = st - hi
    mid = rest.astype(BF16).astype(F32)
    lo = (rest - mid).astype(BF16).astype(F32)
    return hi + pltpu.roll(mid, 8, 1) + pltpu.roll(lo, 16, 1)


def _attn16_kernel(q_ref, k_ref, v_ref, o_ref, st_ref, kbuf, vbuf, *, tq):
    i = pl.program_id(2)

    @pl.when(i == 0)
    def _():
        kbuf[0:BAND, :] = jnp.zeros((BAND, ATTN_W), BF16)
        vbuf[0:BAND, :] = jnp.zeros((BAND, ATTN_W), BF16)

    kbuf[BAND:BAND + tq, :] = k_ref[...]
    vbuf[BAND:BAND + tq, :] = v_ref[...]
    band_bias, col = _band_bias(False)

    def block(j, carry):
        r0 = pl.multiple_of(j * BAND, BAND)
        first_key = jnp.where(jnp.logical_or(j > 0, i > 0), 0, BAND)
        bias = jnp.where(col >= first_key, band_bias, NEG)

        def emit(p, o_pair):
            o_ref[pl.ds(r0, BAND), p * LANES:(p + 1) * LANES] = o_pair.astype(BF16)

        st = _attn_heads(lambda p: q_ref[pl.ds(r0, BAND), p * LANES:(p + 1) * LANES],
                         lambda p: kbuf[pl.ds(r0, 2 * BAND), p * LANES:(p + 1) * LANES],
                         lambda p: vbuf[pl.ds(r0, 2 * BAND), p * LANES:(p + 1) * LANES],
                         bias, emit)
        st_ref[pl.ds(r0, BAND), :] = _three_terms(st)
        return carry

    lax.fori_loop(0, tq // BAND, block, 0, unroll=True)
    kbuf[0:BAND, :] = kbuf[tq:tq + BAND, :]
    vbuf[0:BAND, :] = vbuf[tq:tq + BAND, :]


def _attn4_kernel(q_ref, k_ref, v_ref, o_ref, st_ref, qbuf, kbuf, vbuf, obuf, sbuf):
    i = pl.program_id(1)
    rows = TILE // PERM_D
    nres = 4

    @pl.when(i == 0)
    def _():
        kbuf[:, 0:BAND, :] = jnp.zeros((nres, BAND, ATTN_W), BF16)
        vbuf[:, 0:BAND, :] = jnp.zeros((nres, BAND, ATTN_W), BF16)

    @pl.when(i > 0)
    def _():
        kbuf[:, 0:BAND, :] = kbuf[:, BAND:2 * BAND, :]
        vbuf[:, 0:BAND, :] = vbuf[:, BAND:2 * BAND, :]

    for r4 in range(nres):
        for g in range(PERM_D // nres):
            lanes = slice((r4 + nres * g) * ATTN_W, (r4 + nres * g + 1) * ATTN_W)
            qbuf[r4, g * rows:(g + 1) * rows, :] = q_ref[:, lanes]
            kbuf[r4, BAND + g * rows:BAND + (g + 1) * rows, :] = k_ref[:, lanes]
            vbuf[r4, BAND + g * rows:BAND + (g + 1) * rows, :] = v_ref[:, lanes]

    band_bias, col = _band_bias(True)
    first_key = jnp.where(i > 0, 0, BAND)
    bias = jnp.where(col >= first_key, band_bias, NEG)

    def block(j, carry):
        def emit(p, o_pair):
            obuf[j, :, p * LANES:(p + 1) * LANES] = o_pair.astype(BF16)

        st = _attn_heads(lambda p: qbuf[j, :, p * LANES:(p + 1) * LANES],
                         lambda p: kbuf[j, :, p * LANES:(p + 1) * LANES],
                         lambda p: vbuf[j, :, p * LANES:(p + 1) * LANES],
                         bias, emit)
        sbuf[j] = _three_terms(st)
        return carry

    lax.fori_loop(0, nres, block, 0, unroll=True)

    for r4 in range(nres):
        for g in range(PERM_D // nres):
            grp = r4 + nres * g
            o_ref[:, grp * ATTN_W:(grp + 1) * ATTN_W] = obuf[r4, g * rows:(g + 1) * rows, :]
            st_ref[:, grp * LANES:(grp + 1) * LANES] = sbuf[r4, g * rows:(g + 1) * rows, :]


def _attn1_kernel(q_ref, k_ref, v_ref, o4_ref, s4_ref, o16_ref, s16_ref, pt_ref, g_ref, out_ref,
                  kbuf, vbuf, acc, o4t, o16t, stt, s1t, *, tq):
    i = pl.program_id(1)

    @pl.when(i == 0)
    def _():
        kbuf[0:BAND, :] = jnp.zeros((BAND, ATTN_W), BF16)
        vbuf[0:BAND, :] = jnp.zeros((BAND, ATTN_W), BF16)

    kbuf[BAND:BAND + tq, :] = k_ref[...]
    vbuf[BAND:BAND + tq, :] = v_ref[...]

    prow = PERM_TILE // PERM_D
    for h in range(tq // PERM_TILE):
        tok_rows = slice(h * PERM_TILE, (h + 1) * PERM_TILE)

        def rows_of(ref, w):
            return jnp.concatenate([ref[h * prow:(h + 1) * prow, r * w:(r + 1) * w] for r in range(PERM_D)], axis=0)

        o4t[tok_rows, :] = jnp.dot(pt_ref[...], rows_of(o4_ref, ATTN_W), preferred_element_type=F32).astype(BF16)
        o16t[tok_rows, :] = jnp.dot(pt_ref[...], rows_of(o16_ref, ATTN_W), preferred_element_type=F32).astype(BF16)
        terms = (rows_of(s4_ref, LANES) + pltpu.roll(rows_of(s16_ref, LANES), 32, 1)).astype(BF16)
        terms = jnp.dot(pt_ref[...], terms, preferred_element_type=F32)
        stt[tok_rows, :] = terms + pltpu.roll(terms, LANES - 8, 1) + pltpu.roll(terms, LANES - 16, 1)
    band_bias, col = _band_bias(False)

    def block(j, carry):
        r0 = pl.multiple_of(j * BAND, BAND)
        first_key = jnp.where(jnp.logical_or(j > 0, i > 0), 0, BAND)
        bias = jnp.where(col >= first_key, band_bias, NEG)

        def emit(p, o_pair):
            acc[pl.ds(r0, BAND), p * LANES:(p + 1) * LANES] = o_pair

        s1t[pl.ds(r0, BAND), :] = _attn_heads(
            lambda p: q_ref[pl.ds(r0, BAND), p * LANES:(p + 1) * LANES],
            lambda p: kbuf[pl.ds(r0, 2 * BAND), p * LANES:(p + 1) * LANES],
            lambda p: vbuf[pl.ds(r0, 2 * BAND), p * LANES:(p + 1) * LANES],
            bias, emit)
        return carry

    lax.fori_loop(0, tq // BAND, block, 0, unroll=True)
    kbuf[0:BAND, :] = kbuf[tq:tq + BAND, :]
    vbuf[0:BAND, :] = vbuf[tq:tq + BAND, :]

    head_of = lax.broadcasted_iota(I32, (LANES, ATTN_W), 1) // HEAD_DIM
    expand = jnp.where(lax.broadcasted_iota(I32, (LANES, ATTN_W), 0) == head_of, 1.0, 0.0).astype(BF16)

    def merge(j, carry):
        r0 = pl.multiple_of(j * BAND, BAND)
        is_head = lax.broadcasted_iota(I32, (BAND, LANES), 1) < N_HEADS
        l1 = s1t[pl.ds(r0, BAND), :]
        l4 = stt[pl.ds(r0, BAND), :]
        l16 = pltpu.roll(l4, LANES - 32, 1)
        top = jnp.maximum(l1, jnp.maximum(l4, l16))
        e1 = jnp.exp(l1 - top)
        e4 = jnp.exp(l4 - top)
        e16 = jnp.exp(l16 - top)
        inv = 1.0 / (e1 + e4 + e16)

        def spread(e):
            w = jnp.where(is_head, e * inv, 0.0)
            hi = w.astype(BF16)
            lo = (w - hi.astype(F32)).astype(BF16)
            return (jnp.dot(hi, expand, preferred_element_type=F32)
                    + jnp.dot(lo, expand, preferred_element_type=F32))

        y = (spread(e1) * acc[pl.ds(r0, BAND), :]
             + spread(e4) * o4t[pl.ds(r0, BAND), :].astype(F32)
             + spread(e16) * o16t[pl.ds(r0, BAND), :].astype(F32))
        scale = lax.rsqrt(jnp.mean(y * y, axis=1, keepdims=True) + EPS)
        out_ref[pl.ds(r0, BAND), :] = (y * scale * g_ref[...]).astype(BF16)
        return carry

    lax.fori_loop(0, tq // BAND, merge, 0, unroll=True)


def _attention(q, k, v, q16, k16, v16, perm_t, gain):
    b, s, _ = q.shape
    m16 = s // PERM_D
    rows = TILE // PERM_D
    o16_shape = [jax.ShapeDtypeStruct((b, m16, PERM_D * ATTN_W), BF16),
                 jax.ShapeDtypeStruct((b, m16, PERM_D * LANES), F32)]

    tq16 = min(ATTN_TQ, m16)
    blk16 = lambda w: pl.BlockSpec((None, tq16, w), lambda bi, ri, ii: (bi, ii, ri))
    o16, s16 = pl.pallas_call(
        functools.partial(_attn16_kernel, tq=tq16),
        grid=(b, PERM_D, m16 // tq16),
        in_specs=[blk16(ATTN_W)] * 3, out_specs=[blk16(ATTN_W), blk16(LANES)], out_shape=o16_shape,
        scratch_shapes=[pltpu.VMEM((tq16 + BAND, ATTN_W), BF16)] * 2,
        compiler_params=_params("arbitrary", "arbitrary", "arbitrary"),
        name="attn_d16",
    )(q16, k16, v16)

    tile = lambda w: pl.BlockSpec((None, rows, PERM_D * w), lambda bi, ii: (bi, ii, 0))
    o4, s4 = pl.pallas_call(
        _attn4_kernel,
        grid=(b, s // TILE),
        in_specs=[tile(ATTN_W)] * 3, out_specs=[tile(ATTN_W), tile(LANES)], out_shape=o16_shape,
        scratch_shapes=[pltpu.VMEM((4, BAND, ATTN_W), BF16), pltpu.VMEM((4, 2 * BAND, ATTN_W), BF16),
                        pltpu.VMEM((4, 2 * BAND, ATTN_W), BF16), pltpu.VMEM((4, BAND, ATTN_W), BF16),
                        pltpu.VMEM((4, BAND, LANES), F32)],
        compiler_params=_params("arbitrary", "arbitrary"),
        name="attn_d4",
    )(q16, k16, v16)

    tq = TILE
    tok = lambda w: pl.BlockSpec((None, tq, w), lambda bi, ii: (bi, ii, 0))
    full = lambda shape: pl.BlockSpec(shape, lambda bi, ii: (0,) * len(shape))
    return pl.pallas_call(
        functools.partial(_attn1_kernel, tq=tq),
        grid=(b, s // tq),
        in_specs=[tok(ATTN_W)] * 3 + [tile(ATTN_W), tile(LANES), tile(ATTN_W), tile(LANES),
                                      full((PERM_TILE, PERM_TILE)), full((1, ATTN_W))],
        out_specs=tok(ATTN_W),
        out_shape=jax.ShapeDtypeStruct((b, s, ATTN_W), BF16),
        scratch_shapes=[pltpu.VMEM((tq + BAND, ATTN_W), BF16), pltpu.VMEM((tq + BAND, ATTN_W), BF16),
                        pltpu.VMEM((tq, ATTN_W), F32),
                        pltpu.VMEM((tq, ATTN_W), BF16), pltpu.VMEM((tq, ATTN_W), BF16),
                        pltpu.VMEM((tq, LANES), F32), pltpu.VMEM((tq, LANES), F32)],
        compiler_params=_params("arbitrary", "arbitrary"),
        name="attn_d1",
    )(q, k, v, o4, s4, o16, s16, perm_t, gain)


def _gelu_tanh(x):
    c = math.sqrt(2.0 / math.pi)
    return x * (0.5 * (1.0 + jnp.tanh(c * (x + 0.044715 * (x * x * x)))))


def _split_dot(x, m_bf):
    hi = x.astype(BF16)
    lo = (x - hi.astype(F32)).astype(BF16)
    return (jnp.dot(hi, m_bf, preferred_element_type=F32)
            + jnp.dot(lo, m_bf, preferred_element_type=F32))


def _layer_norm(t, g, b):
    mu = jnp.mean(t, axis=1, keepdims=True)
    tc = t - mu
    var = jnp.mean(tc * tc, axis=1, keepdims=True)
    return tc * lax.rsqrt(var + EPS) * g + b


def _outproj_kernel(ma_ref, mb_ref, x_ref, wa_ref, wb_ref, lg_ref, lb_ref, rw_ref, rb_ref,
                    x1_ref, xp_ref, info_ref, infot_ref, cnt_ref, run_ref, *, tm, alpha):
    i = pl.program_id(0)

    @pl.when(i == 0)
    def _():
        run_ref[...] = jnp.zeros_like(run_ref)

    y = (jnp.dot(ma_ref[...], wa_ref[...], preferred_element_type=F32)
         + jnp.dot(mb_ref[...], wb_ref[...], preferred_element_type=F32))
    x1 = _layer_norm(alpha * x_ref[...] + y, lg_ref[...], lb_ref[...])
    x1_ref[...] = x1
    xp_ref[...] = _pack_pairs(x1)

    hi = x1.astype(BF16)
    lo = (x1 - hi.astype(F32)).astype(BF16)
    both = jnp.dot(hi, rw_ref[...], preferred_element_type=F32)
    logit = (both[:, :LANES] + both[:, LANES:]
             + jnp.dot(lo, rw_ref[:, :LANES], preferred_element_type=F32)) + rb_ref[...]
    lane = lax.broadcasted_iota(I32, (tm, LANES), 1)
    lane_f = lane.astype(F32)

    def top(mask):
        v = jnp.max(jnp.where(mask, logit, NEG), axis=1, keepdims=True)
        first = jnp.min(jnp.where(jnp.logical_and(mask, logit == v), lane_f, float(LANES)),
                        axis=1, keepdims=True)
        return v, first.astype(I32)

    is_g = lane < MOE_GROUPS
    gmax, gidx = top(is_g)
    g_p = 1.0 / jnp.sum(jnp.where(is_g, jnp.exp(logit - gmax), 0.0), axis=1, keepdims=True)
    in_grp = jnp.logical_and(lane >= MOE_GROUPS + gidx * EXPERTS_PER_GROUP,
                             lane < MOE_GROUPS + (gidx + 1) * EXPERTS_PER_GROUP)
    v1, i1 = top(in_grp)
    v2, i2 = top(jnp.logical_and(in_grp, lane != i1))
    e21 = jnp.exp(v2 - v1)
    gate1 = g_p / (1.0 + e21)
    gate2 = g_p * e21 / (1.0 + e21)
    ex1 = i1 - MOE_GROUPS
    ex2 = i2 - MOE_GROUPS

    oh1 = lane == ex1
    oh2 = lane == ex2
    oh = (oh1.astype(F32) + oh2.astype(F32))
    tr = lax.broadcasted_iota(I32, (tm, tm), 0)
    tc = lax.broadcasted_iota(I32, (tm, tm), 1)
    lower = jnp.where(tc < tr, 1.0, 0.0).astype(BF16)
    before = jnp.dot(lower, oh.astype(BF16), preferred_element_type=F32) + run_ref[0:1, :]
    rank1 = jnp.sum(jnp.where(oh1, before, 0.0), axis=1, keepdims=True)
    rank2 = jnp.sum(jnp.where(oh2, before, 0.0), axis=1, keepdims=True)
    run_new = run_ref[0:1, :] + jnp.sum(oh, axis=0, keepdims=True)
    run_ref[...] = jnp.broadcast_to(run_new, run_ref.shape)
    cnt_ref[...] = jnp.broadcast_to(run_new, cnt_ref.shape)

    info = jnp.where(lane == 0, ex1.astype(F32), 0.0)
    info = jnp.where(lane == 1, ex2.astype(F32), info)
    info = jnp.where(lane == 2, gate1, info)
    info = jnp.where(lane == 3, gate2, info)
    info = jnp.where(lane == 4, rank1, info)
    info = jnp.where(lane == 5, rank2, info)
    info_ref[...] = info
    infot_ref[...] = info.T[0:8, :]


def _outproj(ma, mbc, x, wo_a, wo_b, ln_g, ln_b, rw_hilo, rbias, alpha):
    n = x.shape[0]
    tm = OUT_TM
    tok = lambda w: pl.BlockSpec((tm, w), lambda i: (i, 0))
    full = lambda shape: pl.BlockSpec(shape, lambda i: (0,) * len(shape))
    return pl.pallas_call(
        functools.partial(_outproj_kernel, tm=tm, alpha=alpha),
        grid=(n // tm,),
        in_specs=[tok(ATTN_W), tok(SGU_W + CONV_W), tok(D_MODEL),
                  full((ATTN_W, D_MODEL)), full((SGU_W + CONV_W, D_MODEL)),
                  full((1, D_MODEL)), full((1, D_MODEL)),
                  full((D_MODEL, 2 * LANES)), full((1, LANES))],
        out_specs=[tok(D_MODEL), tok(D_MODEL // 2), tok(LANES), pl.BlockSpec((8, tm), lambda i: (0, i)),
                   full((8, LANES))],
        out_shape=[jax.ShapeDtypeStruct((n, D_MODEL), F32),
                   jax.ShapeDtypeStruct((n, D_MODEL // 2), U32),
                   jax.ShapeDtypeStruct((n, LANES), F32),
                   jax.ShapeDtypeStruct((8, n), F32),
                   jax.ShapeDtypeStruct((8, LANES), F32)],
        scratch_shapes=[pltpu.VMEM((8, LANES), F32)],
        compiler_params=_params("arbitrary"),
        name="outproj",
    )(ma, mbc, x, wo_a, wo_b, ln_g, ln_b, rw_hilo, rbias)


def _sc_mesh():
    return plsc.VectorSubcoreMesh(core_axis_name="c", subcore_axis_name="s",
                                  num_cores=SC_CORES, num_subcores=SC_SUBCORES)


def _sc_chunk(rows_per_worker):
    chunk = min(SC_CHUNK, rows_per_worker // 2)
    assert rows_per_worker % (2 * chunk) == 0 and chunk % 8 == 0, rows_per_worker
    return chunk


def _sc_dispatch(xp, dest_kn, rows):
    n, w = xp.shape
    t_per_w = n // SC_WORKERS
    chunk = _sc_chunk(t_per_w)
    nchunk = t_per_w // chunk

    def body(src_hbm, dest_hbm, out_hbm, idx_v, rows_v, lsem, ssem):
        wid = lax.axis_index("s") * SC_CORES + lax.axis_index("c")
        base = wid * t_per_w
        pltpu.sync_copy(dest_hbm.at[0, wid], idx_v.at[0])
        pltpu.sync_copy(dest_hbm.at[1, wid], idx_v.at[1])

        def load(c, slot):
            return pltpu.make_async_copy(src_hbm.at[pl.ds(base + c * chunk, chunk)], rows_v.at[slot],
                                         lsem.at[slot])

        def put(c, slot, kk):
            return pltpu.make_async_copy(rows_v.at[slot], out_hbm.at[idx_v.at[kk, c]], ssem.at[slot])

        load(0, 0).start()

        @pl.loop(0, nchunk, step=2)
        def _(c):
            for b in range(2):
                cc = c + b
                load(cc, b).wait()

                @pl.when(cc + 1 < nchunk)
                def _():
                    @pl.when(cc >= 1)
                    def _():
                        put(cc - 1, 1 - b, 0).wait()
                        put(cc - 1, 1 - b, 1).wait()
                    load(cc + 1, 1 - b).start()

                put(cc, b, 0).start()
                put(cc, b, 1).start()

        for b in range(2):
            put(nchunk - 2 + b, b, 0).wait()
            put(nchunk - 2 + b, b, 1).wait()

    call = pl.kernel(
        body, mesh=_sc_mesh(),
        out_type=jax.ShapeDtypeStruct((rows, w), U32),
        scratch_types=[pltpu.VMEM((2, nchunk, chunk), I32), pltpu.VMEM((2, chunk, w), U32),
                       pltpu.SemaphoreType.DMA((2,)), pltpu.SemaphoreType.DMA((2,))],
        name="sc_dispatch")
    return call(xp, dest_kn.reshape(2, SC_WORKERS, nchunk, chunk))


def _sc_gather(table, idx):
    b = idx.shape[0]
    w = table.shape[1]
    b_per_w = b // SC_WORKERS
    chunk = _sc_chunk(b_per_w)
    nchunk = b_per_w // chunk

    def body(table_hbm, idx_hbm, out_hbm, idx_v, rows_v, gsem, osem):
        wid = lax.axis_index("s") * SC_CORES + lax.axis_index("c")
        base = wid * b_per_w
        pltpu.sync_copy(idx_hbm.at[wid], idx_v)

        def gather(c, slot):
            return pltpu.make_async_copy(table_hbm.at[idx_v.at[c]], rows_v.at[slot], gsem.at[slot])

        def put(c, slot):
            return pltpu.make_async_copy(rows_v.at[slot], out_hbm.at[pl.ds(base + c * chunk, chunk)],
                                         osem.at[slot])

        gather(0, 0).start()

        @pl.loop(0, nchunk, step=2)
        def _(c):
            for b in range(2):
                cc = c + b
                gather(cc, b).wait()

                @pl.when(cc + 1 < nchunk)
                def _():
                    @pl.when(cc >= 1)
                    def _():
                        put(cc - 1, 1 - b).wait()
                    gather(cc + 1, 1 - b).start()

                put(cc, b).start()

        put(nchunk - 2, 0).wait()
        put(nchunk - 1, 1).wait()

    call = pl.kernel(
        body, mesh=_sc_mesh(),
        out_type=jax.ShapeDtypeStruct((b, w), table.dtype),
        scratch_types=[pltpu.VMEM((nchunk, chunk), I32), pltpu.VMEM((2, chunk, w), table.dtype),
                       pltpu.SemaphoreType.DMA((2,)), pltpu.SemaphoreType.DMA((2,))],
        name="sc_gather")
    return call(table, idx.reshape(SC_WORKERS, nchunk, chunk))


def _expert_kernel(be_ref, nv_ref, nu_ref, xs_ref, wg_ref, wu_ref, wd_ref, y_ref, wgb, wub, wdb):
    i = pl.program_id(0)
    cur = jnp.minimum(i, nu_ref[0] - 1)
    new_expert = jnp.logical_or(i == 0, be_ref[cur] != be_ref[jnp.maximum(cur - 1, 0)])

    @pl.when(jnp.logical_and(i < nu_ref[0], new_expert))
    def _():
        wgb[...] = wg_ref[...].astype(BF16)
        wub[...] = wu_ref[...].astype(BF16)
        wdb[...] = wd_ref[...].astype(BF16)

    @pl.when(i < nu_ref[0])
    def _():
        row = lax.broadcasted_iota(I32, xs_ref.shape, 0)
        lo, hi = _unpack_pairs(jnp.where(row < nv_ref[i], xs_ref[...], jnp.uint32(0)))
        xb = jnp.concatenate([lo, hi], axis=1).astype(BF16)
        g = jnp.dot(xb, wgb[...], preferred_element_type=F32)
        u = jnp.dot(xb, wub[...], preferred_element_type=F32)
        hdn = (g * (1.0 / (1.0 + jnp.exp(-g))) * u).astype(BF16)
        y_ref[...] = _pack_pairs(jnp.dot(hdn, wdb[...], preferred_element_type=F32))

    @pl.when(i >= nu_ref[0])
    def _():
        y_ref[...] = jnp.zeros_like(y_ref)


def _experts(block_expert, block_valid, n_used, xs, wg, wu, wd, layer):
    rows, w = xs.shape
    tb = MOE_TB
    blk = lambda i, be, nv, nu: (jnp.minimum(i, nu[0] - 1), 0)
    oblk = lambda i, be, nv, nu: (i, 0)
    wsel = lambda i, be, nv, nu: (layer, be[jnp.minimum(i, nu[0] - 1)], 0, 0)
    return pl.pallas_call(
        _expert_kernel,
        grid_spec=pltpu.PrefetchScalarGridSpec(
            num_scalar_prefetch=3,
            grid=(rows // tb,),
            in_specs=[pl.BlockSpec((tb, w), blk),
                      pl.BlockSpec((None, None, D_MODEL, D_EXPERT), wsel),
                      pl.BlockSpec((None, None, D_MODEL, D_EXPERT), wsel),
                      pl.BlockSpec((None, None, D_EXPERT, D_MODEL), wsel)],
            out_specs=pl.BlockSpec((tb, w), oblk),
            scratch_shapes=[pltpu.VMEM((D_MODEL, D_EXPERT), BF16), pltpu.VMEM((D_MODEL, D_EXPERT), BF16),
                            pltpu.VMEM((D_EXPERT, D_MODEL), BF16)]),
        out_shape=jax.ShapeDtypeStruct((rows, w), U32),
        compiler_params=_params("arbitrary"),
        name="experts",
    )(block_expert, block_valid, n_used, xs, wg, wu, wd)


def _combine_kernel(ya_ref, yb_ref, x_ref, info_ref, lg_ref, lb_ref, out_ref, *, alpha):
    out_ref[...] = _combine_tile(ya_ref[...], yb_ref[...], x_ref[...], info_ref[...], lg_ref[...], lb_ref[...],
                                 alpha)


def _combine(y_tok, x1, info, ln_g, ln_b, alpha):
    n = x1.shape[0]
    tm = ROW_TM
    tok = lambda w: pl.BlockSpec((tm, w), lambda i: (i, 0))
    slot = lambda k: pl.BlockSpec((tm, D_MODEL // 2), lambda i: (i + k * (n // tm), 0))
    full = lambda shape: pl.BlockSpec(shape, lambda i: (0,) * len(shape))
    return pl.pallas_call(
        functools.partial(_combine_kernel, alpha=alpha),
        grid=(n // tm,),
        in_specs=[slot(0), slot(1), tok(D_MODEL), tok(LANES), full((1, D_MODEL)), full((1, D_MODEL))],
        out_specs=tok(D_MODEL),
        out_shape=jax.ShapeDtypeStruct((n, D_MODEL), F32),
        compiler_params=_params("arbitrary"),
        name="combine",
    )(y_tok, y_tok, x1, info, ln_g, ln_b)


def _rope_tables(s):
    half = HEAD_DIM // 2
    inv_freq = ROPE_THETA ** (-jnp.arange(half, dtype=F32) / half)
    ang = jnp.arange(s, dtype=F32)[:, None] * inv_freq[None, :]
    cos = jnp.cos(ang)
    sin = jnp.sin(ang)
    cos_t = jnp.tile(cos, (1, LANES // half))
    sin_t = jnp.tile(jnp.concatenate([-sin, sin], axis=1), (1, LANES // HEAD_DIM))
    return cos_t, sin_t


def _forward(x, w_in, w_out, branch_gain, sgu_gain, sgu_w, sgu_b, conv_w, ln_gain, ln_bias,
             router_group_w, router_group_b, router_expert_w, router_expert_b,
             expert_w_gate, expert_w_up, expert_w_down):
    b, s, d_model = x.shape
    depth = w_in.shape[0]
    n = b * s
    assert d_model == D_MODEL and w_in.shape[1:] == (D_MODEL, PROJ_W), (x.shape, w_in.shape)
    assert s % (PERM_D * BAND) == 0 and n % (2 * SC_WORKERS * 8) == 0, (b, s)
    alpha = (2.0 * depth) ** 0.25
    cos_t, sin_t = _rope_tables(s)
    perm, perm_t = _tile_perm()
    tb = MOE_TB
    n_blocks = (2 * n + N_EXPERTS * (tb - 1) + tb - 1) // tb
    rows = n_blocks * tb

    pending = None
    for l in range(depth):
        g = branch_gain[l]
        bias_tile = jnp.repeat(sgu_b[l].T, SGU_W // SGU_GROUPS, axis=1)
        mix_params = (sgu_gain[l][None, :], sgu_w[l], bias_tile, conv_w[l], g[None, ATTN_W:])
        outs = _proj(x if pending is None else pending, w_in[l].astype(BF16), cos_t, sin_t, perm, mix_params,
                     b, s, alpha)
        if pending is not None:
            x, outs = outs[0], outs[1:]
        q, k, v, mbc, q16, k16, v16 = outs
        ma = _attention(q, k, v, q16, k16, v16, perm_t, g[None, :ATTN_W])

        rw = jnp.zeros((D_MODEL, LANES), F32)
        rw = rw.at[:, :MOE_GROUPS].set(router_group_w[l])
        rw = rw.at[:, MOE_GROUPS:MOE_GROUPS + N_EXPERTS].set(router_expert_w[l])
        rw_hi = rw.astype(BF16)
        rw_hilo = jnp.concatenate([rw_hi, (rw - rw_hi.astype(F32)).astype(BF16)], axis=1)
        rbias = jnp.zeros((1, LANES), F32)
        rbias = rbias.at[0, :MOE_GROUPS].set(router_group_b[l])
        rbias = rbias.at[0, MOE_GROUPS:MOE_GROUPS + N_EXPERTS].set(router_expert_b[l])
        wo = w_out[l].astype(BF16)
        x1, xp, info, info_t, cnt = _outproj(
            ma.reshape(n, ATTN_W), mbc.reshape(n, SGU_W + CONV_W), x.reshape(n, D_MODEL),
            wo[:ATTN_W], wo[ATTN_W:], ln_gain[l, 0][None], ln_bias[l, 0][None], rw_hilo, rbias, alpha)

        counts = cnt[0, :N_EXPERTS].astype(I32)
        padded = (counts + tb - 1) // tb * tb
        pad_end = jnp.cumsum(padded)
        pad_start = pad_end - padded
        ex = info_t[0:2].astype(I32)
        start_of = jnp.zeros_like(ex)
        for e in range(N_EXPERTS):
            start_of = jnp.where(ex == e, pad_start[e], start_of)
        dest_kn = start_of + info_t[4:6].astype(I32)
        blk_row0 = jnp.arange(n_blocks, dtype=I32) * tb
        block_expert = jnp.minimum(jnp.sum((pad_end[None, :] <= blk_row0[:, None]).astype(I32), axis=1),
                                   N_EXPERTS - 1)
        block_valid = jnp.clip(pad_start[block_expert] + counts[block_expert] - blk_row0, 0, tb)
        n_used = (pad_end[-1:] // tb).astype(I32)

        xs = _sc_dispatch(xp, dest_kn, rows)
        ys = _experts(block_expert, block_valid, n_used, xs, expert_w_gate, expert_w_up, expert_w_down, l)
        y_tok = _sc_gather(ys, dest_kn.reshape(2 * n))
        pending = (y_tok, x1, info, ln_gain[l, 1][None], ln_bias[l, 1][None])
    return _combine(*pending, alpha).reshape(b, s, D_MODEL)


def kernel(x, w_in, w_out, branch_gain, sgu_gain, sgu_w, sgu_b, conv_w, ln_gain, ln_bias, router_group_w, router_group_b, router_expert_w, router_expert_b, expert_w_gate, expert_w_up, expert_w_down):
    return _forward(x, w_in, w_out, branch_gain, sgu_gain, sgu_w, sgu_b, conv_w, ln_gain, ln_bias,
                    router_group_w, router_group_b, router_expert_w, router_expert_b,
                    expert_w_gate, expert_w_up, expert_w_down)
```

```python
import functools
import math

import jax
import jax.numpy as jnp
from jax import lax
from jax.experimental import pallas as pl
from jax.experimental.pallas import tpu as pltpu
from jax.experimental.pallas import tpu_sc as plsc

F32 = jnp.float32
BF16 = jnp.bfloat16
U32 = jnp.uint32
I32 = jnp.int32

D_MODEL = 1024
HEAD_DIM = 64
ATTN_W = 512
N_HEADS = 8
SGU_W = 256
SGU_GROUPS = 4
SGU_CHUNK = 128
CONV_W = 256
REST_W = 2 * SGU_W + 3 * CONV_W
PROJ_W = 3 * ATTN_W + REST_W
DILATIONS = (16, 4, 1)
BAND = 128
ROPE_THETA = 10000.0
MOE_GROUPS = 4
EXPERTS_PER_GROUP = 8
N_EXPERTS = MOE_GROUPS * EXPERTS_PER_GROUP
D_EXPERT = 512
EPS = 1e-5
NEG = -1e30

LANES = 128
VMEM_LIMIT = 56 * 1024 * 1024

TILE = 512
PERM_D = 16
PERM_TILE = 256
ATTN_TQ = 512
ATTN1_TQ = 1024
OUT_TM = 512
MOE_TB = 512
ROW_TM = 512
SC_CORES = 2
SC_SUBCORES = 16
SC_WORKERS = SC_CORES * SC_SUBCORES
SC_CHUNK = 64


def _params(*sem):
    return pltpu.CompilerParams(dimension_semantics=sem, vmem_limit_bytes=VMEM_LIMIT)


def _pack_pairs(x):
    w = x.shape[1] // 2
    lo = lax.bitcast_convert_type(x[:, :w].astype(BF16).astype(F32), U32)
    hi = lax.bitcast_convert_type(x[:, w:].astype(BF16).astype(F32), U32)
    return (lo >> 16) | (hi & jnp.uint32(0xFFFF0000))


def _unpack_pairs(p):
    lo = lax.bitcast_convert_type(p << 16, F32)
    hi = lax.bitcast_convert_type(p & jnp.uint32(0xFFFF0000), F32)
    return lo, hi


def _tile_perm():
    i = jnp.arange(PERM_TILE, dtype=I32)
    tok = PERM_D * (i % (PERM_TILE // PERM_D)) + i // (PERM_TILE // PERM_D)
    p = (jnp.arange(PERM_TILE, dtype=I32)[None, :] == tok[:, None]).astype(BF16)
    return p, p.T


def _mixers(rest, prev_zz, sg, sw_ref, sb, cw_ref, bg):
    tm = rest.shape[0]
    u = rest[:, 0:SGU_W]
    z = rest[:, SGU_W:2 * SGU_W]
    gb = rest[:, 2 * SGU_W:2 * SGU_W + CONV_W]
    gc = rest[:, 2 * SGU_W + CONV_W:2 * SGU_W + 2 * CONV_W]
    hh = rest[:, 2 * SGU_W + 2 * CONV_W:]

    gdim = SGU_W // SGU_GROUPS
    ri = lax.broadcasted_iota(I32, (SGU_W, SGU_W), 0) // gdim
    ci = lax.broadcasted_iota(I32, (SGU_W, SGU_W), 1) // gdim
    avg = jnp.where(ri == ci, 1.0 / gdim, 0.0).astype(BF16)
    z = _gelu_tanh(z)
    zc = z - _split_dot(z, avg)
    var = _split_dot(zc * zc, avg)
    zn = (zc * lax.rsqrt(var + EPS) * sg).astype(BF16)

    tr = lax.broadcasted_iota(I32, (SGU_CHUNK, SGU_CHUNK), 0)
    tc = lax.broadcasted_iota(I32, (SGU_CHUNK, SGU_CHUNK), 1)
    w_cat = jnp.concatenate(
        [jnp.where(tc <= tr, sw_ref[g], 0.0).astype(BF16) for g in range(SGU_GROUPS)], axis=1)
    lane_g = lax.broadcasted_iota(I32, (SGU_CHUNK, SGU_W), 1) // gdim
    gu = _gelu_tanh(u)
    yb = []
    for c in range(tm // SGU_CHUNK):
        zch = zn[c * SGU_CHUNK:(c + 1) * SGU_CHUNK, :]
        stack = jnp.concatenate(
            [jnp.where(lane_g == g, zch, jnp.zeros_like(zch)) for g in range(SGU_GROUPS)], axis=0)
        sp = jnp.dot(w_cat, stack, preferred_element_type=F32) + sb
        yb.append(gu[c * SGU_CHUNK:(c + 1) * SGU_CHUNK, :] * sp)
    yb = jnp.concatenate(yb, axis=0)

    zz = gc * hh
    hrows = prev_zz.shape[0]
    ext = jnp.concatenate([prev_zz, zz], axis=0)
    z1 = ext[hrows - 1:hrows - 1 + tm, :]
    z2 = ext[hrows - 2:hrows - 2 + tm, :]
    yc = gb * (cw_ref[0:1, :] * z2 + cw_ref[1:2, :] * z1 + cw_ref[2:3, :] * zz)

    def rms(t, g):
        return t * lax.rsqrt(jnp.mean(t * t, axis=1, keepdims=True) + EPS) * g

    out = jnp.concatenate([rms(yb, bg[:, 0:SGU_W]).astype(BF16), rms(yc, bg[:, SGU_W:]).astype(BF16)], axis=1)
    return out, zz[tm - hrows:, :]


def _combine_tile(ya, yb, x1, info, lg, lb, alpha):
    g1 = info[:, 2:3]
    g2 = info[:, 3:4]
    a_lo, a_hi = _unpack_pairs(ya)
    b_lo, b_hi = _unpack_pairs(yb)
    half = D_MODEL // 2
    t_lo = alpha * x1[:, 0:half] + g1 * a_lo + g2 * b_lo
    t_hi = alpha * x1[:, half:] + g1 * a_hi + g2 * b_hi
    return _layer_norm(jnp.concatenate([t_lo, t_hi], axis=1), lg, lb)


def _proj_kernel(*refs, fused, alpha):
    if fused:
        (ya_ref, yb_ref, x1_ref, info_ref, lg_ref, lb_ref, w_ref, cos_ref, sin_ref, p_ref,
         sg_ref, sw_ref, sb_ref, cw_ref, bg_ref,
         x_out, q_ref, k_ref, v_ref, m_ref, q16_ref, k16_ref, v16_ref, zz_prev) = refs
        x = _combine_tile(ya_ref[...], yb_ref[...], x1_ref[...], info_ref[...], lg_ref[...], lb_ref[...], alpha)
        x_out[...] = x
    else:
        (x_ref, w_ref, cos_ref, sin_ref, p_ref, sg_ref, sw_ref, sb_ref, cw_ref, bg_ref,
         q_ref, k_ref, v_ref, m_ref, q16_ref, k16_ref, v16_ref, zz_prev) = refs
        x = x_ref[...]
    si = pl.program_id(1)

    @pl.when(si == 0)
    def _():
        zz_prev[...] = jnp.zeros_like(zz_prev)

    xb = x.astype(BF16)
    cos = cos_ref[...]
    sin = sin_ref[...]
    lane = lax.broadcasted_iota(I32, cos.shape, 1)
    first_half = (lane % HEAD_DIM) < (HEAD_DIM // 2)
    rows = PERM_TILE // PERM_D

    def store_both(val_bf, out_ref, out16_ref):
        out_ref[...] = val_bf
        for h in range(TILE // PERM_TILE):
            perm = jnp.dot(p_ref[...], val_bf[h * PERM_TILE:(h + 1) * PERM_TILE, :],
                           preferred_element_type=F32).astype(BF16)
            for r in range(PERM_D):
                out16_ref[h * rows:(h + 1) * rows, r * ATTN_W:(r + 1) * ATTN_W] = perm[r * rows:(r + 1) * rows, :]

    def rope(col0, scale):
        t = jnp.dot(xb, w_ref[:, col0:col0 + ATTN_W], preferred_element_type=F32)
        out = []
        for c in range(ATTN_W // LANES):
            tc = t[:, c * LANES:(c + 1) * LANES]
            partner = jnp.where(first_half, pltpu.roll(tc, LANES - 32, 1), pltpu.roll(tc, 32, 1))
            out.append(((tc * cos + partner * sin) * scale).astype(BF16))
        return jnp.concatenate(out, axis=1)

    rest = jnp.dot(xb, w_ref[:, 3 * ATTN_W:], preferred_element_type=F32)
    tail = [zz_prev[...]]

    def mix_chunk(c):
        rows_c = slice(c * SGU_CHUNK, (c + 1) * SGU_CHUNK)
        out, tail[0] = _mixers(rest[rows_c, :], tail[0], sg_ref[...], sw_ref, sb_ref[...], cw_ref, bg_ref[...])
        m_ref[rows_c, :] = out

    qb = rope(0, HEAD_DIM ** -0.5)
    mix_chunk(0)
    store_both(qb, q_ref, q16_ref)
    kb = rope(ATTN_W, 1.0)
    mix_chunk(1)
    store_both(kb, k_ref, k16_ref)
    vb = jnp.dot(xb, w_ref[:, 2 * ATTN_W:3 * ATTN_W], preferred_element_type=F32).astype(BF16)
    mix_chunk(2)
    store_both(vb, v_ref, v16_ref)
    mix_chunk(3)
    zz_prev[...] = tail[0]


def _proj(x_or_parts, w_bf, cos_t, sin_t, perm, mix_params, b, s, alpha):
    tm = TILE
    nt = s // tm
    fused = isinstance(x_or_parts, tuple)
    out3 = jax.ShapeDtypeStruct((b, s, ATTN_W), BF16)
    out16 = jax.ShapeDtypeStruct((b, s // PERM_D, PERM_D * ATTN_W), BF16)
    tok = lambda width: pl.BlockSpec((None, tm, width), lambda bi, si: (bi, si, 0))
    tok16 = pl.BlockSpec((None, tm // PERM_D, PERM_D * ATTN_W), lambda bi, si: (bi, si, 0))
    full = lambda shape: pl.BlockSpec(shape, lambda bi, si: (0,) * len(shape))
    flat = lambda width, off: pl.BlockSpec((tm, width), lambda bi, si: (bi * nt + si + off, 0))
    common_specs = [full((D_MODEL, PROJ_W)),
                    pl.BlockSpec((tm, LANES), lambda bi, si: (si, 0)),
                    pl.BlockSpec((tm, LANES), lambda bi, si: (si, 0)),
                    full((PERM_TILE, PERM_TILE)),
                    full((1, SGU_W)), full((SGU_GROUPS, SGU_CHUNK, SGU_CHUNK)),
                    full((SGU_CHUNK, SGU_W)), full((3, CONV_W)), full((1, SGU_W + CONV_W))]
    out_specs = [tok(ATTN_W), tok(ATTN_W), tok(ATTN_W), tok(SGU_W + CONV_W), tok16, tok16, tok16]
    out_shape = [out3, out3, out3, jax.ShapeDtypeStruct((b, s, SGU_W + CONV_W), BF16), out16, out16, out16]
    if fused:
        y_tok, x1, info, lg, lb = x_or_parts
        n = b * s
        ins = [y_tok, y_tok, x1, info, lg, lb]
        in_specs = [flat(D_MODEL // 2, 0), flat(D_MODEL // 2, n // tm), flat(D_MODEL, 0), flat(LANES, 0),
                    full((1, D_MODEL)), full((1, D_MODEL))]
        out_specs = [tok(D_MODEL)] + out_specs
        out_shape = [jax.ShapeDtypeStruct((b, s, D_MODEL), F32)] + out_shape
    else:
        ins = [x_or_parts]
        in_specs = [tok(D_MODEL)]
    return pl.pallas_call(
        functools.partial(_proj_kernel, fused=fused, alpha=alpha),
        grid=(b, nt),
        in_specs=in_specs + common_specs,
        out_specs=out_specs, out_shape=out_shape,
        scratch_shapes=[pltpu.VMEM((8, CONV_W), F32)],
        compiler_params=_params("arbitrary", "arbitrary"),
        name="proj",
    )(*ins, w_bf, cos_t, sin_t, perm, *mix_params)


def _band_bias(permuted):
    row = lax.broadcasted_iota(I32, (2 * BAND, BAND), 0)
    qry = lax.broadcasted_iota(I32, (2 * BAND, BAND), 1)
    key = row % BAND
    if permuted:
        qry = 4 * (qry % 32) + qry // 32
        key = 4 * (key % 32) + key // 32
    lo_key = jnp.where(row < BAND, qry, 0)
    hi_key = jnp.where(row < BAND, BAND - 1, qry)
    return jnp.where(jnp.logical_and(key >= lo_key, key <= hi_key), 0.0, NEG).astype(F32), row


def _attn_heads(get_q, get_k, get_v, bias, emit):
    lane_lo = lax.broadcasted_iota(I32, (BAND, LANES), 1) < HEAD_DIM
    lses = []
    for p in range(ATTN_W // LANES):
        ql, kk, vv = get_q(p), get_k(p), get_v(p)
        o_t = []
        for hh in range(2):
            qm = jnp.where(lane_lo if hh == 0 else jnp.logical_not(lane_lo), ql, jnp.zeros_like(ql))
            sc = lax.dot_general(kk, qm, (((1,), (1,)), ((), ())), preferred_element_type=F32)
            sc = sc + bias
            mx = jnp.max(sc, axis=0, keepdims=True)
            pe = jnp.exp(sc - mx)
            den = jnp.sum(pe, axis=0, keepdims=True)
            ot = lax.dot_general(vv, pe.astype(BF16), (((0,), (0,)), ((), ())),
                                 preferred_element_type=F32)
            o_t.append(ot / den)
            lses.append(mx + jnp.log(den))
        emit(p, jnp.concatenate([o_t[0][0:HEAD_DIM, :], o_t[1][HEAD_DIM:, :]], axis=0).T)
    return jnp.concatenate(lses + [jnp.zeros((BAND - N_HEADS, BAND), F32)], axis=0).T


def _three_terms(st):
    hi = st.astype(BF16).astype(F32)
    rest = st - hi
    mid = rest.astype(BF16).astype(F32)
    lo = (rest - mid).astype(BF16).astype(F32)
    return hi + pltpu.roll(mid, 8, 1) + pltpu.roll(lo, 16, 1)


def _attn16_kernel(q_ref, k_ref, v_ref, o_ref, st_ref, kbuf, vbuf, *, tq):
    i = pl.program_id(2)

    @pl.when(i == 0)
    def _():
        kbuf[0:BAND, :] = jnp.zeros((BAND, ATTN_W), BF16)
        vbuf[0:BAND, :] = jnp.zeros((BAND, ATTN_W), BF16)

    kbuf[BAND:BAND + tq, :] = k_ref[...]
    vbuf[BAND:BAND + tq, :] = v_ref[...]
    band_bias, col = _band_bias(False)

    def block(j, carry):
        r0 = pl.multiple_of(j * BAND, BAND)
        first_key = jnp.where(jnp.logical_or(j > 0, i > 0), 0, BAND)
        bias = jnp.where(col >= first_key, band_bias, NEG)

        def emit(p, o_pair):
            o_ref[pl.ds(r0, BAND), p * LANES:(p + 1) * LANES] = o_pair.astype(BF16)

        st = _attn_heads(lambda p: q_ref[pl.ds(r0, BAND), p * LANES:(p + 1) * LANES],
                         lambda p: kbuf[pl.ds(r0, 2 * BAND), p * LANES:(p + 1) * LANES],
                         lambda p: vbuf[pl.ds(r0, 2 * BAND), p * LANES:(p + 1) * LANES],
                         bias, emit)
        st_ref[pl.ds(r0, BAND), :] = _three_terms(st)
        return carry

    lax.fori_loop(0, tq // BAND, block, 0, unroll=True)
    kbuf[0:BAND, :] = kbuf[tq:tq + BAND, :]
    vbuf[0:BAND, :] = vbuf[tq:tq + BAND, :]


def _attn4_kernel(q_ref, k_ref, v_ref, o_ref, st_ref, qbuf, kbuf, vbuf, obuf, sbuf):
    i = pl.program_id(1)
    rows = TILE // PERM_D
    nres = 4

    @pl.when(i == 0)
    def _():
        kbuf[:, 0:BAND, :] = jnp.zeros((nres, BAND, ATTN_W), BF16)
        vbuf[:, 0:BAND, :] = jnp.zeros((nres, BAND, ATTN_W), BF16)

    @pl.when(i > 0)
    def _():
        kbuf[:, 0:BAND, :] = kbuf[:, BAND:2 * BAND, :]
        vbuf[:, 0:BAND, :] = vbuf[:, BAND:2 * BAND, :]

    for r4 in range(nres):
        for g in range(PERM_D // nres):
            lanes = slice((r4 + nres * g) * ATTN_W, (r4 + nres * g + 1) * ATTN_W)
            qbuf[r4, g * rows:(g + 1) * rows, :] = q_ref[:, lanes]
            kbuf[r4, BAND + g * rows:BAND + (g + 1) * rows, :] = k_ref[:, lanes]
            vbuf[r4, BAND + g * rows:BAND + (g + 1) * rows, :] = v_ref[:, lanes]

    band_bias, col = _band_bias(True)
    first_key = jnp.where(i > 0, 0, BAND)
    bias = jnp.where(col >= first_key, band_bias, NEG)

    def block(j, carry):
        def emit(p, o_pair):
            obuf[j, :, p * LANES:(p + 1) * LANES] = o_pair.astype(BF16)

        st = _attn_heads(lambda p: qbuf[j, :, p * LANES:(p + 1) * LANES],
                         lambda p: kbuf[j, :, p * LANES:(p + 1) * LANES],
                         lambda p: vbuf[j, :, p * LANES:(p + 1) * LANES],
                         bias, emit)
        sbuf[j] = _three_terms(st)
        return carry

    lax.fori_loop(0, nres, block, 0, unroll=True)

    for r4 in range(nres):
        for g in range(PERM_D // nres):
            grp = r4 + nres * g
            o_ref[:, grp * ATTN_W:(grp + 1) * ATTN_W] = obuf[r4, g * rows:(g + 1) * rows, :]
            st_ref[:, grp * LANES:(grp + 1) * LANES] = sbuf[r4, g * rows:(g + 1) * rows, :]


def _attn1_kernel(q_ref, k_ref, v_ref, o4_ref, s4_ref, o16_ref, s16_ref, pt_ref, g_ref, out_ref,
                  kbuf, vbuf, acc, o4t, o16t, stt, s1t, *, tq):
    i = pl.program_id(1)

    @pl.when(i == 0)
    def _():
        kbuf[0:BAND, :] = jnp.zeros((BAND, ATTN_W), BF16)
        vbuf[0:BAND, :] = jnp.zeros((BAND, ATTN_W), BF16)

    kbuf[BAND:BAND + tq, :] = k_ref[...]
    vbuf[BAND:BAND + tq, :] = v_ref[...]

    prow = PERM_TILE // PERM_D
    for h in range(tq // PERM_TILE):
        tok_rows = slice(h * PERM_TILE, (h + 1) * PERM_TILE)

        def rows_of(ref, w):
            return jnp.concatenate([ref[h * prow:(h + 1) * prow, r * w:(r + 1) * w] for r in range(PERM_D)], axis=0)

        o4t[tok_rows, :] = jnp.dot(pt_ref[...], rows_of(o4_ref, ATTN_W), preferred_element_type=F32).astype(BF16)
        o16t[tok_rows, :] = jnp.dot(pt_ref[...], rows_of(o16_ref, ATTN_W), preferred_element_type=F32).astype(BF16)
        terms = (rows_of(s4_ref, LANES) + pltpu.roll(rows_of(s16_ref, LANES), 32, 1)).astype(BF16)
        terms = jnp.dot(pt_ref[...], terms, preferred_element_type=F32)
        stt[tok_rows, :] = terms + pltpu.roll(terms, LANES - 8, 1) + pltpu.roll(terms, LANES - 16, 1)
    band_bias, col = _band_bias(False)

    def block(j, carry):
        r0 = pl.multiple_of(j * BAND, BAND)
        first_key = jnp.where(jnp.logical_or(j > 0, i > 0), 0, BAND)
        bias = jnp.where(col >= first_key, band_bias, NEG)

        def emit(p, o_pair):
            acc[pl.ds(r0, BAND), p * LANES:(p + 1) * LANES] = o_pair

        s1t[pl.ds(r0, BAND), :] = _attn_heads(
            lambda p: q_ref[pl.ds(r0, BAND), p * LANES:(p + 1) * LANES],
            lambda p: kbuf[pl.ds(r0, 2 * BAND), p * LANES:(p + 1) * LANES],
            lambda p: vbuf[pl.ds(r0, 2 * BAND), p * LANES:(p + 1) * LANES],
            bias, emit)
        return carry

    lax.fori_loop(0, tq // BAND, block, 0, unroll=True)
    kbuf[0:BAND, :] = kbuf[tq:tq + BAND, :]
    vbuf[0:BAND, :] = vbuf[tq:tq + BAND, :]

    head_of = lax.broadcasted_iota(I32, (LANES, ATTN_W), 1) // HEAD_DIM
    expand = jnp.where(lax.broadcasted_iota(I32, (LANES, ATTN_W), 0) == head_of, 1.0, 0.0).astype(BF16)

    def merge(j, carry):
        r0 = pl.multiple_of(j * BAND, BAND)
        is_head = lax.broadcasted_iota(I32, (BAND, LANES), 1) < N_HEADS
        l1 = s1t[pl.ds(r0, BAND), :]
        l4 = stt[pl.ds(r0, BAND), :]
        l16 = pltpu.roll(l4, LANES - 32, 1)
        top = jnp.maximum(l1, jnp.maximum(l4, l16))
        e1 = jnp.exp(l1 - top)
        e4 = jnp.exp(l4 - top)
        e16 = jnp.exp(l16 - top)
        inv = 1.0 / (e1 + e4 + e16)

        def spread(e):
            w = jnp.where(is_head, e * inv, 0.0)
            hi = w.astype(BF16)
            lo = (w - hi.astype(F32)).astype(BF16)
            return (jnp.dot(hi, expand, preferred_element_type=F32)
                    + jnp.dot(lo, expand, preferred_element_type=F32))

        y = (spread(e1) * acc[pl.ds(r0, BAND), :]
             + spread(e4) * o4t[pl.ds(r0, BAND), :].astype(F32)
             + spread(e16) * o16t[pl.ds(r0, BAND), :].astype(F32))
        scale = lax.rsqrt(jnp.mean(y * y, axis=1, keepdims=True) + EPS)
        out_ref[pl.ds(r0, BAND), :] = (y * scale * g_ref[...]).astype(BF16)
        return carry

    lax.fori_loop(0, tq // BAND, merge, 0, unroll=True)


def _attention(q, k, v, q16, k16, v16, perm_t, gain):
    b, s, _ = q.shape
    m16 = s // PERM_D
    rows = TILE // PERM_D
    o16_shape = [jax.ShapeDtypeStruct((b, m16, PERM_D * ATTN_W), BF16),
                 jax.ShapeDtypeStruct((b, m16, PERM_D * LANES), F32)]

    tq16 = min(ATTN_TQ, m16)
    blk16 = lambda w: pl.BlockSpec((None, tq16, w), lambda bi, ri, ii: (bi, ii, ri))
    o16, s16 = pl.pallas_call(
        functools.partial(_attn16_kernel, tq=tq16),
        grid=(b, PERM_D, m16 // tq16),
        in_specs=[blk16(ATTN_W)] * 3, out_specs=[blk16(ATTN_W), blk16(LANES)], out_shape=o16_shape,
        scratch_shapes=[pltpu.VMEM((tq16 + BAND, ATTN_W), BF16)] * 2,
        compiler_params=_params("arbitrary", "arbitrary", "arbitrary"),
        name="attn_d16",
    )(q16, k16, v16)

    tile = lambda w: pl.BlockSpec((None, rows, PERM_D * w), lambda bi, ii: (bi, ii, 0))
    o4, s4 = pl.pallas_call(
        _attn4_kernel,
        grid=(b, s // TILE),
        in_specs=[tile(ATTN_W)] * 3, out_specs=[tile(ATTN_W), tile(LANES)], out_shape=o16_shape,
        scratch_shapes=[pltpu.VMEM((4, BAND, ATTN_W), BF16), pltpu.VMEM((4, 2 * BAND, ATTN_W), BF16),
                        pltpu.VMEM((4, 2 * BAND, ATTN_W), BF16), pltpu.VMEM((4, BAND, ATTN_W), BF16),
                        pltpu.VMEM((4, BAND, LANES), F32)],
        compiler_params=_params("arbitrary", "arbitrary"),
        name="attn_d4",
    )(q16, k16, v16)

    tq = min(ATTN1_TQ, s)
    tok = lambda w: pl.BlockSpec((None, tq, w), lambda bi, ii: (bi, ii, 0))
    tile = lambda w: pl.BlockSpec((None, tq // PERM_D, PERM_D * w), lambda bi, ii: (bi, ii, 0))
    full = lambda shape: pl.BlockSpec(shape, lambda bi, ii: (0,) * len(shape))
    return pl.pallas_call(
        functools.partial(_attn1_kernel, tq=tq),
        grid=(b, s // tq),
        in_specs=[tok(ATTN_W)] * 3 + [tile(ATTN_W), tile(LANES), tile(ATTN_W), tile(LANES),
                                      full((PERM_TILE, PERM_TILE)), full((1, ATTN_W))],
        out_specs=tok(ATTN_W),
        out_shape=jax.ShapeDtypeStruct((b, s, ATTN_W), BF16),
        scratch_shapes=[pltpu.VMEM((tq + BAND, ATTN_W), BF16), pltpu.VMEM((tq + BAND, ATTN_W), BF16),
                        pltpu.VMEM((tq, ATTN_W), F32),
                        pltpu.VMEM((tq, ATTN_W), BF16), pltpu.VMEM((tq, ATTN_W), BF16),
                        pltpu.VMEM((tq, LANES), F32), pltpu.VMEM((tq, LANES), F32)],
        compiler_params=_params("arbitrary", "arbitrary"),
        name="attn_d1",
    )(q, k, v, o4, s4, o16, s16, perm_t, gain)


def _gelu_tanh(x):
    c = math.sqrt(2.0 / math.pi)
    return x * (0.5 * (1.0 + jnp.tanh(c * (x + 0.044715 * (x * x * x)))))


def _split_dot(x, m_bf):
    hi = x.astype(BF16)
    lo = (x - hi.astype(F32)).astype(BF16)
    return (jnp.dot(hi, m_bf, preferred_element_type=F32)
            + jnp.dot(lo, m_bf, preferred_element_type=F32))


def _layer_norm(t, g, b):
    mu = jnp.mean(t, axis=1, keepdims=True)
    tc = t - mu
    var = jnp.mean(tc * tc, axis=1, keepdims=True)
    return tc * lax.rsqrt(var + EPS) * g + b


def _outproj_kernel(ma_ref, mb_ref, x_ref, wa_ref, wb_ref, lg_ref, lb_ref, rw_ref, rb_ref,
                    x1_ref, xp_ref, info_ref, infot_ref, cnt_ref, run_ref, *, tm, alpha):
    i = pl.program_id(0)

    @pl.when(i == 0)
    def _():
        run_ref[...] = jnp.zeros_like(run_ref)

    y = (jnp.dot(ma_ref[...], wa_ref[...], preferred_element_type=F32)
         + jnp.dot(mb_ref[...], wb_ref[...], preferred_element_type=F32))
    x1 = _layer_norm(alpha * x_ref[...] + y, lg_ref[...], lb_ref[...])
    x1_ref[...] = x1
    xp_ref[...] = _pack_pairs(x1)

    hi = x1.astype(BF16)
    lo = (x1 - hi.astype(F32)).astype(BF16)
    both = jnp.dot(hi, rw_ref[...], preferred_element_type=F32)
    logit = (both[:, :LANES] + both[:, LANES:]
             + jnp.dot(lo, rw_ref[:, :LANES], preferred_element_type=F32)) + rb_ref[...]
    lane = lax.broadcasted_iota(I32, (tm, LANES), 1)
    lane_f = lane.astype(F32)

    def top(mask):
        v = jnp.max(jnp.where(mask, logit, NEG), axis=1, keepdims=True)
        first = jnp.min(jnp.where(jnp.logical_and(mask, logit == v), lane_f, float(LANES)),
                        axis=1, keepdims=True)
        return v, first.astype(I32)

    is_g = lane < MOE_GROUPS
    gmax, gidx = top(is_g)
    g_p = 1.0 / jnp.sum(jnp.where(is_g, jnp.exp(logit - gmax), 0.0), axis=1, keepdims=True)
    in_grp = jnp.logical_and(lane >= MOE_GROUPS + gidx * EXPERTS_PER_GROUP,
                             lane < MOE_GROUPS + (gidx + 1) * EXPERTS_PER_GROUP)
    v1, i1 = top(in_grp)
    v2, i2 = top(jnp.logical_and(in_grp, lane != i1))
    e21 = jnp.exp(v2 - v1)
    gate1 = g_p / (1.0 + e21)
    gate2 = g_p * e21 / (1.0 + e21)
    ex1 = i1 - MOE_GROUPS
    ex2 = i2 - MOE_GROUPS

    oh1 = lane == ex1
    oh2 = lane == ex2
    oh = (oh1.astype(F32) + oh2.astype(F32))
    tr = lax.broadcasted_iota(I32, (tm, tm), 0)
    tc = lax.broadcasted_iota(I32, (tm, tm), 1)
    lower = jnp.where(tc < tr, 1.0, 0.0).astype(BF16)
    before = jnp.dot(lower, oh.astype(BF16), preferred_element_type=F32) + run_ref[0:1, :]
    rank1 = jnp.sum(jnp.where(oh1, before, 0.0), axis=1, keepdims=True)
    rank2 = jnp.sum(jnp.where(oh2, before, 0.0), axis=1, keepdims=True)
    run_new = run_ref[0:1, :] + jnp.sum(oh, axis=0, keepdims=True)
    run_ref[...] = jnp.broadcast_to(run_new, run_ref.shape)
    cnt_ref[...] = jnp.broadcast_to(run_new, cnt_ref.shape)

    info = jnp.where(lane == 0, ex1.astype(F32), 0.0)
    info = jnp.where(lane == 1, ex2.astype(F32), info)
    info = jnp.where(lane == 2, gate1, info)
    info = jnp.where(lane == 3, gate2, info)
    info = jnp.where(lane == 4, rank1, info)
    info = jnp.where(lane == 5, rank2, info)
    info_ref[...] = info
    infot_ref[...] = info.T[0:8, :]


def _outproj(ma, mbc, x, wo_a, wo_b, ln_g, ln_b, rw_hilo, rbias, alpha):
    n = x.shape[0]
    tm = OUT_TM
    tok = lambda w: pl.BlockSpec((tm, w), lambda i: (i, 0))
    full = lambda shape: pl.BlockSpec(shape, lambda i: (0,) * len(shape))
    return pl.pallas_call(
        functools.partial(_outproj_kernel, tm=tm, alpha=alpha),
        grid=(n // tm,),
        in_specs=[tok(ATTN_W), tok(SGU_W + CONV_W), tok(D_MODEL),
                  full((ATTN_W, D_MODEL)), full((SGU_W + CONV_W, D_MODEL)),
                  full((1, D_MODEL)), full((1, D_MODEL)),
                  full((D_MODEL, 2 * LANES)), full((1, LANES))],
        out_specs=[tok(D_MODEL), tok(D_MODEL // 2), tok(LANES), pl.BlockSpec((8, tm), lambda i: (0, i)),
                   full((8, LANES))],
        out_shape=[jax.ShapeDtypeStruct((n, D_MODEL), F32),
                   jax.ShapeDtypeStruct((n, D_MODEL // 2), U32),
                   jax.ShapeDtypeStruct((n, LANES), F32),
                   jax.ShapeDtypeStruct((8, n), F32),
                   jax.ShapeDtypeStruct((8, LANES), F32)],
        scratch_shapes=[pltpu.VMEM((8, LANES), F32)],
        compiler_params=_params("arbitrary"),
        name="outproj",
    )(ma, mbc, x, wo_a, wo_b, ln_g, ln_b, rw_hilo, rbias)


def _sc_mesh():
    return plsc.VectorSubcoreMesh(core_axis_name="c", subcore_axis_name="s",
                                  num_cores=SC_CORES, num_subcores=SC_SUBCORES)


def _sc_chunk(rows_per_worker):
    chunk = min(SC_CHUNK, rows_per_worker // 2)
    assert rows_per_worker % (2 * chunk) == 0 and chunk % 8 == 0, rows_per_worker
    return chunk


def _sc_dispatch(xp, dest_kn, rows):
    n, w = xp.shape
    t_per_w = n // SC_WORKERS
    chunk = _sc_chunk(t_per_w)
    nchunk = t_per_w // chunk

    def body(src_hbm, dest_hbm, out_hbm, idx_v, rows_v, lsem, ssem):
        wid = lax.axis_index("s") * SC_CORES + lax.axis_index("c")
        base = wid * t_per_w
        pltpu.sync_copy(dest_hbm.at[0, wid], idx_v.at[0])
        pltpu.sync_copy(dest_hbm.at[1, wid], idx_v.at[1])

        def load(c, slot):
            return pltpu.make_async_copy(src_hbm.at[pl.ds(base + c * chunk, chunk)], rows_v.at[slot],
                                         lsem.at[slot])

        def put(c, slot, kk):
            return pltpu.make_async_copy(rows_v.at[slot], out_hbm.at[idx_v.at[kk, c]], ssem.at[slot])

        load(0, 0).start()

        @pl.loop(0, nchunk, step=2)
        def _(c):
            for b in range(2):
                cc = c + b
                load(cc, b).wait()

                @pl.when(cc + 1 < nchunk)
                def _():
                    @pl.when(cc >= 1)
                    def _():
                        put(cc - 1, 1 - b, 0).wait()
                        put(cc - 1, 1 - b, 1).wait()
                    load(cc + 1, 1 - b).start()

                put(cc, b, 0).start()
                put(cc, b, 1).start()

        for b in range(2):
            put(nchunk - 2 + b, b, 0).wait()
            put(nchunk - 2 + b, b, 1).wait()

    call = pl.kernel(
        body, mesh=_sc_mesh(),
        out_type=jax.ShapeDtypeStruct((rows, w), U32),
        scratch_types=[pltpu.VMEM((2, nchunk, chunk), I32), pltpu.VMEM((2, chunk, w), U32),
                       pltpu.SemaphoreType.DMA((2,)), pltpu.SemaphoreType.DMA((2,))],
        name="sc_dispatch")
    return call(xp, dest_kn.reshape(2, SC_WORKERS, nchunk, chunk))


def _sc_gather(table, idx):
    b = idx.shape[0]
    w = table.shape[1]
    b_per_w = b // SC_WORKERS
    chunk = _sc_chunk(b_per_w)
    nchunk = b_per_w // chunk

    def body(table_hbm, idx_hbm, out_hbm, idx_v, rows_v, gsem, osem):
        wid = lax.axis_index("s") * SC_CORES + lax.axis_index("c")
        base = wid * b_per_w
        pltpu.sync_copy(idx_hbm.at[wid], idx_v)

        def gather(c, slot):
            return pltpu.make_async_copy(table_hbm.at[idx_v.at[c]], rows_v.at[slot], gsem.at[slot])

        def put(c, slot):
            return pltpu.make_async_copy(rows_v.at[slot], out_hbm.at[pl.ds(base + c * chunk, chunk)],
                                         osem.at[slot])

        gather(0, 0).start()

        @pl.loop(0, nchunk, step=2)
        def _(c):
            for b in range(2):
                cc = c + b
                gather(cc, b).wait()

                @pl.when(cc + 1 < nchunk)
                def _():
                    @pl.when(cc >= 1)
                    def _():
                        put(cc - 1, 1 - b).wait()
                    gather(cc + 1, 1 - b).start()

                put(cc, b).start()

        put(nchunk - 2, 0).wait()
        put(nchunk - 1, 1).wait()

    call = pl.kernel(
        body, mesh=_sc_mesh(),
        out_type=jax.ShapeDtypeStruct((b, w), table.dtype),
        scratch_types=[pltpu.VMEM((nchunk, chunk), I32), pltpu.VMEM((2, chunk, w), table.dtype),
                       pltpu.SemaphoreType.DMA((2,)), pltpu.SemaphoreType.DMA((2,))],
        name="sc_gather")
    return call(table, idx.reshape(SC_WORKERS, nchunk, chunk))


def _expert_kernel(be_ref, nv_ref, nu_ref, xs_ref, wg_ref, wu_ref, wd_ref, y_ref, wgb, wub, wdb):
    i = pl.program_id(0)
    cur = jnp.minimum(i, nu_ref[0] - 1)
    new_expert = jnp.logical_or(i == 0, be_ref[cur] != be_ref[jnp.maximum(cur - 1, 0)])

    @pl.when(jnp.logical_and(i < nu_ref[0], new_expert))
    def _():
        wgb[...] = wg_ref[...].astype(BF16)
        wub[...] = wu_ref[...].astype(BF16)
        wdb[...] = wd_ref[...].astype(BF16)

    @pl.when(i < nu_ref[0])
    def _():
        row = lax.broadcasted_iota(I32, xs_ref.shape, 0)
        lo, hi = _unpack_pairs(jnp.where(row < nv_ref[i], xs_ref[...], jnp.uint32(0)))
        xb = jnp.concatenate([lo, hi], axis=1).astype(BF16)
        g = jnp.dot(xb, wgb[...], preferred_element_type=F32)
        u = jnp.dot(xb, wub[...], preferred_element_type=F32)
        hdn = (g * (1.0 / (1.0 + jnp.exp(-g))) * u).astype(BF16)
        y_ref[...] = _pack_pairs(jnp.dot(hdn, wdb[...], preferred_element_type=F32))

    @pl.when(i >= nu_ref[0])
    def _():
        y_ref[...] = jnp.zeros_like(y_ref)


def _experts(block_expert, block_valid, n_used, xs, wg, wu, wd, layer):
    rows, w = xs.shape
    tb = MOE_TB
    blk = lambda i, be, nv, nu: (jnp.minimum(i, nu[0] - 1), 0)
    oblk = lambda i, be, nv, nu: (i, 0)
    wsel = lambda i, be, nv, nu: (layer, be[jnp.minimum(i, nu[0] - 1)], 0, 0)
    return pl.pallas_call(
        _expert_kernel,
        grid_spec=pltpu.PrefetchScalarGridSpec(
            num_scalar_prefetch=3,
            grid=(rows // tb,),
            in_specs=[pl.BlockSpec((tb, w), blk),
                      pl.BlockSpec((None, None, D_MODEL, D_EXPERT), wsel),
                      pl.BlockSpec((None, None, D_MODEL, D_EXPERT), wsel),
                      pl.BlockSpec((None, None, D_EXPERT, D_MODEL), wsel)],
            out_specs=pl.BlockSpec((tb, w), oblk),
            scratch_shapes=[pltpu.VMEM((D_MODEL, D_EXPERT), BF16), pltpu.VMEM((D_MODEL, D_EXPERT), BF16),
                            pltpu.VMEM((D_EXPERT, D_MODEL), BF16)]),
        out_shape=jax.ShapeDtypeStruct((rows, w), U32),
        compiler_params=_params("arbitrary"),
        name="experts",
    )(block_expert, block_valid, n_used, xs, wg, wu, wd)


def _combine_kernel(ya_ref, yb_ref, x_ref, info_ref, lg_ref, lb_ref, out_ref, *, alpha):
    out_ref[...] = _combine_tile(ya_ref[...], yb_ref[...], x_ref[...], info_ref[...], lg_ref[...], lb_ref[...],
                                 alpha)


def _combine(y_tok, x1, info, ln_g, ln_b, alpha):
    n = x1.shape[0]
    tm = ROW_TM
    tok = lambda w: pl.BlockSpec((tm, w), lambda i: (i, 0))
    slot = lambda k: pl.BlockSpec((tm, D_MODEL // 2), lambda i: (i + k * (n // tm), 0))
    full = lambda shape: pl.BlockSpec(shape, lambda i: (0,) * len(shape))
    return pl.pallas_call(
        functools.partial(_combine_kernel, alpha=alpha),
        grid=(n // tm,),
        in_specs=[slot(0), slot(1), tok(D_MODEL), tok(LANES), full((1, D_MODEL)), full((1, D_MODEL))],
        out_specs=tok(D_MODEL),
        out_shape=jax.ShapeDtypeStruct((n, D_MODEL), F32),
        compiler_params=_params("arbitrary"),
        name="combine",
    )(y_tok, y_tok, x1, info, ln_g, ln_b)


def _rope_tables(s):
    half = HEAD_DIM // 2
    inv_freq = ROPE_THETA ** (-jnp.arange(half, dtype=F32) / half)
    ang = jnp.arange(s, dtype=F32)[:, None] * inv_freq[None, :]
    cos = jnp.cos(ang)
    sin = jnp.sin(ang)
    cos_t = jnp.tile(cos, (1, LANES // half))
    sin_t = jnp.tile(jnp.concatenate([-sin, sin], axis=1), (1, LANES // HEAD_DIM))
    return cos_t, sin_t


def _forward(x, w_in, w_out, branch_gain, sgu_gain, sgu_w, sgu_b, conv_w, ln_gain, ln_bias,
             router_group_w, router_group_b, router_expert_w, router_expert_b,
             expert_w_gate, expert_w_up, expert_w_down):
    b, s, d_model = x.shape
    depth = w_in.shape[0]
    n = b * s
    assert d_model == D_MODEL and w_in.shape[1:] == (D_MODEL, PROJ_W), (x.shape, w_in.shape)
    assert s % (PERM_D * BAND) == 0 and n % (2 * SC_WORKERS * 8) == 0, (b, s)
    alpha = (2.0 * depth) ** 0.25
    cos_t, sin_t = _rope_tables(s)
    perm, perm_t = _tile_perm()
    tb = MOE_TB
    n_blocks = (2 * n + N_EXPERTS * (tb - 1) + tb - 1) // tb
    rows = n_blocks * tb

    pending = None
    for l in range(depth):
        g = branch_gain[l]
        bias_tile = jnp.repeat(sgu_b[l].T, SGU_W // SGU_GROUPS, axis=1)
        mix_params = (sgu_gain[l][None, :], sgu_w[l], bias_tile, conv_w[l], g[None, ATTN_W:])
        outs = _proj(x if pending is None else pending, w_in[l].astype(BF16), cos_t, sin_t, perm, mix_params,
                     b, s, alpha)
        if pending is not None:
            x, outs = outs[0], outs[1:]
        q, k, v, mbc, q16, k16, v16 = outs
        ma = _attention(q, k, v, q16, k16, v16, perm_t, g[None, :ATTN_W])

        rw = jnp.zeros((D_MODEL, LANES), F32)
        rw = rw.at[:, :MOE_GROUPS].set(router_group_w[l])
        rw = rw.at[:, MOE_GROUPS:MOE_GROUPS + N_EXPERTS].set(router_expert_w[l])
        rw_hi = rw.astype(BF16)
        rw_hilo = jnp.concatenate([rw_hi, (rw - rw_hi.astype(F32)).astype(BF16)], axis=1)
        rbias = jnp.zeros((1, LANES), F32)
        rbias = rbias.at[0, :MOE_GROUPS].set(router_group_b[l])
        rbias = rbias.at[0, MOE_GROUPS:MOE_GROUPS + N_EXPERTS].set(router_expert_b[l])
        wo = w_out[l].astype(BF16)
        x1, xp, info, info_t, cnt = _outproj(
            ma.reshape(n, ATTN_W), mbc.reshape(n, SGU_W + CONV_W), x.reshape(n, D_MODEL),
            wo[:ATTN_W], wo[ATTN_W:], ln_gain[l, 0][None], ln_bias[l, 0][None], rw_hilo, rbias, alpha)

        counts = cnt[0, :N_EXPERTS].astype(I32)
        padded = (counts + tb - 1) // tb * tb
        pad_end = jnp.cumsum(padded)
        pad_start = pad_end - padded
        ex = info_t[0:2].astype(I32)
        start_of = jnp.zeros_like(ex)
        for e in range(N_EXPERTS):
            start_of = jnp.where(ex == e, pad_start[e], start_of)
        dest_kn = start_of + info_t[4:6].astype(I32)
        blk_row0 = jnp.arange(n_blocks, dtype=I32) * tb
        block_expert = jnp.minimum(jnp.sum((pad_end[None, :] <= blk_row0[:, None]).astype(I32), axis=1),
                                   N_EXPERTS - 1)
        block_valid = jnp.clip(pad_start[block_expert] + counts[block_expert] - blk_row0, 0, tb)
        n_used = (pad_end[-1:] // tb).astype(I32)

        xs = _sc_dispatch(xp, dest_kn, rows)
        ys = _experts(block_expert, block_valid, n_used, xs, expert_w_gate, expert_w_up, expert_w_down, l)
        y_tok = _sc_gather(ys, dest_kn.reshape(2 * n))
        pending = (y_tok, x1, info, ln_gain[l, 1][None], ln_bias[l, 1][None])
    return _combine(*pending, alpha).reshape(b, s, D_MODEL)


def kernel(x, w_in, w_out, branch_gain, sgu_gain, sgu_w, sgu_b, conv_w, ln_gain, ln_bias, router_group_w, router_group_b, router_expert_w, router_expert_b, expert_w_gate, expert_w_up, expert_w_down):
    return _forward(x, w_in, w_out, branch_gain, sgu_gain, sgu_w, sgu_b, conv_w, ln_gain, ln_bias,
                    router_group_w, router_group_b, router_expert_w, router_expert_b,
                    expert_w_gate, expert_w_up, expert_w_down)
```

```python
import functools
import math

import jax
import jax.numpy as jnp
from jax import lax
from jax.experimental import pallas as pl
from jax.experimental.pallas import tpu as pltpu
from jax.experimental.pallas import tpu_sc as plsc

F32 = jnp.float32
BF16 = jnp.bfloat16
U32 = jnp.uint32
I32 = jnp.int32

D_MODEL = 1024
HEAD_DIM = 64
ATTN_W = 512
N_HEADS = 8
SGU_W = 256
SGU_GROUPS = 4
SGU_CHUNK = 128
CONV_W = 256
REST_W = 2 * SGU_W + 3 * CONV_W
PROJ_W = 3 * ATTN_W + REST_W
DILATIONS = (16, 4, 1)
BAND = 128
ROPE_THETA = 10000.0
MOE_GROUPS = 4
EXPERTS_PER_GROUP = 8
N_EXPERTS = MOE_GROUPS * EXPERTS_PER_GROUP
D_EXPERT = 512
EPS = 1e-5
NEG = -1e30

LANES = 128
VMEM_LIMIT = 56 * 1024 * 1024

TILE = 512
PERM_D = 16
PERM_TILE = 256
ATTN_TQ = 512
ATTN1_TQ = 2048
OUT_TM = 512
MOE_TB = 512
ROW_TM = 512
SC_CORES = 2
SC_SUBCORES = 16
SC_WORKERS = SC_CORES * SC_SUBCORES
SC_CHUNK = 64


def _params(*sem):
    return pltpu.CompilerParams(dimension_semantics=sem, vmem_limit_bytes=VMEM_LIMIT)


def _pack_pairs(x):
    w = x.shape[1] // 2
    lo = lax.bitcast_convert_type(x[:, :w].astype(BF16).astype(F32), U32)
    hi = lax.bitcast_convert_type(x[:, w:].astype(BF16).astype(F32), U32)
    return (lo >> 16) | (hi & jnp.uint32(0xFFFF0000))


def _unpack_pairs(p):
    lo = lax.bitcast_convert_type(p << 16, F32)
    hi = lax.bitcast_convert_type(p & jnp.uint32(0xFFFF0000), F32)
    return lo, hi


def _tile_perm():
    i = jnp.arange(PERM_TILE, dtype=I32)
    tok = PERM_D * (i % (PERM_TILE // PERM_D)) + i // (PERM_TILE // PERM_D)
    p = (jnp.arange(PERM_TILE, dtype=I32)[None, :] == tok[:, None]).astype(BF16)
    return p, p.T


def _mixers(rest, prev_zz, sg, sw_ref, sb, cw_ref, bg):
    tm = rest.shape[0]
    u = rest[:, 0:SGU_W]
    z = rest[:, SGU_W:2 * SGU_W]
    gb = rest[:, 2 * SGU_W:2 * SGU_W + CONV_W]
    gc = rest[:, 2 * SGU_W + CONV_W:2 * SGU_W + 2 * CONV_W]
    hh = rest[:, 2 * SGU_W + 2 * CONV_W:]

    gdim = SGU_W // SGU_GROUPS
    ri = lax.broadcasted_iota(I32, (SGU_W, SGU_W), 0) // gdim
    ci = lax.broadcasted_iota(I32, (SGU_W, SGU_W), 1) // gdim
    avg = jnp.where(ri == ci, 1.0 / gdim, 0.0).astype(BF16)
    z = _gelu_tanh(z)
    zc = z - _split_dot(z, avg)
    var = _split_dot(zc * zc, avg)
    zn = (zc * lax.rsqrt(var + EPS) * sg).astype(BF16)

    tr = lax.broadcasted_iota(I32, (SGU_CHUNK, SGU_CHUNK), 0)
    tc = lax.broadcasted_iota(I32, (SGU_CHUNK, SGU_CHUNK), 1)
    w_cat = jnp.concatenate(
        [jnp.where(tc <= tr, sw_ref[g], 0.0).astype(BF16) for g in range(SGU_GROUPS)], axis=1)
    lane_g = lax.broadcasted_iota(I32, (SGU_CHUNK, SGU_W), 1) // gdim
    gu = _gelu_tanh(u)
    yb = []
    for c in range(tm // SGU_CHUNK):
        zch = zn[c * SGU_CHUNK:(c + 1) * SGU_CHUNK, :]
        stack = jnp.concatenate(
            [jnp.where(lane_g == g, zch, jnp.zeros_like(zch)) for g in range(SGU_GROUPS)], axis=0)
        sp = jnp.dot(w_cat, stack, preferred_element_type=F32) + sb
        yb.append(gu[c * SGU_CHUNK:(c + 1) * SGU_CHUNK, :] * sp)
    yb = jnp.concatenate(yb, axis=0)

    zz = gc * hh
    hrows = prev_zz.shape[0]
    ext = jnp.concatenate([prev_zz, zz], axis=0)
    z1 = ext[hrows - 1:hrows - 1 + tm, :]
    z2 = ext[hrows - 2:hrows - 2 + tm, :]
    yc = gb * (cw_ref[0:1, :] * z2 + cw_ref[1:2, :] * z1 + cw_ref[2:3, :] * zz)

    def rms(t, g):
        return t * lax.rsqrt(jnp.mean(t * t, axis=1, keepdims=True) + EPS) * g

    out = jnp.concatenate([rms(yb, bg[:, 0:SGU_W]).astype(BF16), rms(yc, bg[:, SGU_W:]).astype(BF16)], axis=1)
    return out, zz[tm - hrows:, :]


def _combine_tile(ya, yb, x1, info, lg, lb, alpha):
    g1 = info[:, 2:3]
    g2 = info[:, 3:4]
    a_lo, a_hi = _unpack_pairs(ya)
    b_lo, b_hi = _unpack_pairs(yb)
    half = D_MODEL // 2
    t_lo = alpha * x1[:, 0:half] + g1 * a_lo + g2 * b_lo
    t_hi = alpha * x1[:, half:] + g1 * a_hi + g2 * b_hi
    return _layer_norm(jnp.concatenate([t_lo, t_hi], axis=1), lg, lb)


def _proj_kernel(*refs, fused, alpha):
    if fused:
        (ya_ref, yb_ref, x1_ref, info_ref, lg_ref, lb_ref, w_ref, cos_ref, sin_ref, p_ref,
         sg_ref, sw_ref, sb_ref, cw_ref, bg_ref,
         x_out, q_ref, k_ref, v_ref, m_ref, q16_ref, k16_ref, v16_ref, zz_prev) = refs
        x = _combine_tile(ya_ref[...], yb_ref[...], x1_ref[...], info_ref[...], lg_ref[...], lb_ref[...], alpha)
        x_out[...] = x
    else:
        (x_ref, w_ref, cos_ref, sin_ref, p_ref, sg_ref, sw_ref, sb_ref, cw_ref, bg_ref,
         q_ref, k_ref, v_ref, m_ref, q16_ref, k16_ref, v16_ref, zz_prev) = refs
        x = x_ref[...]
    si = pl.program_id(1)

    @pl.when(si == 0)
    def _():
        zz_prev[...] = jnp.zeros_like(zz_prev)

    xb = x.astype(BF16)
    cos = cos_ref[...]
    sin = sin_ref[...]
    lane = lax.broadcasted_iota(I32, cos.shape, 1)
    first_half = (lane % HEAD_DIM) < (HEAD_DIM // 2)
    rows = PERM_TILE // PERM_D

    def store_both(val_bf, out_ref, out16_ref):
        out_ref[...] = val_bf
        for h in range(TILE // PERM_TILE):
            perm = jnp.dot(p_ref[...], val_bf[h * PERM_TILE:(h + 1) * PERM_TILE, :],
                           preferred_element_type=F32).astype(BF16)
            for r in range(PERM_D):
                out16_ref[h * rows:(h + 1) * rows, r * ATTN_W:(r + 1) * ATTN_W] = perm[r * rows:(r + 1) * rows, :]

    def rope(col0, scale):
        t = jnp.dot(xb, w_ref[:, col0:col0 + ATTN_W], preferred_element_type=F32)
        out = []
        for c in range(ATTN_W // LANES):
            tc = t[:, c * LANES:(c + 1) * LANES]
            partner = jnp.where(first_half, pltpu.roll(tc, LANES - 32, 1), pltpu.roll(tc, 32, 1))
            out.append(((tc * cos + partner * sin) * scale).astype(BF16))
        return jnp.concatenate(out, axis=1)

    rest = jnp.dot(xb, w_ref[:, 3 * ATTN_W:], preferred_element_type=F32)
    tail = [zz_prev[...]]

    def mix_chunk(c):
        rows_c = slice(c * SGU_CHUNK, (c + 1) * SGU_CHUNK)
        out, tail[0] = _mixers(rest[rows_c, :], tail[0], sg_ref[...], sw_ref, sb_ref[...], cw_ref, bg_ref[...])
        m_ref[rows_c, :] = out

    qb = rope(0, HEAD_DIM ** -0.5)
    mix_chunk(0)
    store_both(qb, q_ref, q16_ref)
    kb = rope(ATTN_W, 1.0)
    mix_chunk(1)
    store_both(kb, k_ref, k16_ref)
    vb = jnp.dot(xb, w_ref[:, 2 * ATTN_W:3 * ATTN_W], preferred_element_type=F32).astype(BF16)
    mix_chunk(2)
    store_both(vb, v_ref, v16_ref)
    mix_chunk(3)
    zz_prev[...] = tail[0]


def _proj(x_or_parts, w_bf, cos_t, sin_t, perm, mix_params, b, s, alpha):
    tm = TILE
    nt = s // tm
    fused = isinstance(x_or_parts, tuple)
    out3 = jax.ShapeDtypeStruct((b, s, ATTN_W), BF16)
    out16 = jax.ShapeDtypeStruct((b, s // PERM_D, PERM_D * ATTN_W), BF16)
    tok = lambda width: pl.BlockSpec((None, tm, width), lambda bi, si: (bi, si, 0))
    tok16 = pl.BlockSpec((None, tm // PERM_D, PERM_D * ATTN_W), lambda bi, si: (bi, si, 0))
    full = lambda shape: pl.BlockSpec(shape, lambda bi, si: (0,) * len(shape))
    flat = lambda width, off: pl.BlockSpec((tm, width), lambda bi, si: (bi * nt + si + off, 0))
    common_specs = [full((D_MODEL, PROJ_W)),
                    pl.BlockSpec((tm, LANES), lambda bi, si: (si, 0)),
                    pl.BlockSpec((tm, LANES), lambda bi, si: (si, 0)),
                    full((PERM_TILE, PERM_TILE)),
                    full((1, SGU_W)), full((SGU_GROUPS, SGU_CHUNK, SGU_CHUNK)),
                    full((SGU_CHUNK, SGU_W)), full((3, CONV_W)), full((1, SGU_W + CONV_W))]
    out_specs = [tok(ATTN_W), tok(ATTN_W), tok(ATTN_W), tok(SGU_W + CONV_W), tok16, tok16, tok16]
    out_shape = [out3, out3, out3, jax.ShapeDtypeStruct((b, s, SGU_W + CONV_W), BF16), out16, out16, out16]
    if fused:
        y_tok, x1, info, lg, lb = x_or_parts
        n = b * s
        ins = [y_tok, y_tok, x1, info, lg, lb]
        in_specs = [flat(D_MODEL // 2, 0), flat(D_MODEL // 2, n // tm), flat(D_MODEL, 0), flat(LANES, 0),
                    full((1, D_MODEL)), full((1, D_MODEL))]
        out_specs = [tok(D_MODEL)] + out_specs
        out_shape = [jax.ShapeDtypeStruct((b, s, D_MODEL), F32)] + out_shape
    else:
        ins = [x_or_parts]
        in_specs = [tok(D_MODEL)]
    return pl.pallas_call(
        functools.partial(_proj_kernel, fused=fused, alpha=alpha),
        grid=(b, nt),
        in_specs=in_specs + common_specs,
        out_specs=out_specs, out_shape=out_shape,
        scratch_shapes=[pltpu.VMEM((8, CONV_W), F32)],
        compiler_params=_params("arbitrary", "arbitrary"),
        name="proj",
    )(*ins, w_bf, cos_t, sin_t, perm, *mix_params)


def _band_bias(permuted):
    row = lax.broadcasted_iota(I32, (2 * BAND, BAND), 0)
    qry = lax.broadcasted_iota(I32, (2 * BAND, BAND), 1)
    key = row % BAND
    if permuted:
        qry = 4 * (qry % 32) + qry // 32
        key = 4 * (key % 32) + key // 32
    lo_key = jnp.where(row < BAND, qry, 0)
    hi_key = jnp.where(row < BAND, BAND - 1, qry)
    return jnp.where(jnp.logical_and(key >= lo_key, key <= hi_key), 0.0, NEG).astype(F32), row


def _attn_heads(get_q, get_k, get_v, bias, emit):
    lane_lo = lax.broadcasted_iota(I32, (BAND, LANES), 1) < HEAD_DIM
    lses = []
    for p in range(ATTN_W // LANES):
        ql, kk, vv = get_q(p), get_k(p), get_v(p)
        o_t = []
        for hh in range(2):
            qm = jnp.where(lane_lo if hh == 0 else jnp.logical_not(lane_lo), ql, jnp.zeros_like(ql))
            sc = lax.dot_general(kk, qm, (((1,), (1,)), ((), ())), preferred_element_type=F32)
            sc = sc + bias
            mx = jnp.max(sc, axis=0, keepdims=True)
            pe = jnp.exp(sc - mx)
            den = jnp.sum(pe, axis=0, keepdims=True)
            ot = lax.dot_general(vv, pe.astype(BF16), (((0,), (0,)), ((), ())),
                                 preferred_element_type=F32)
            o_t.append(ot / den)
            lses.append(mx + jnp.log(den))
        emit(p, jnp.concatenate([o_t[0][0:HEAD_DIM, :], o_t[1][HEAD_DIM:, :]], axis=0).T)
    return jnp.concatenate(lses + [jnp.zeros((BAND - N_HEADS, BAND), F32)], axis=0).T


def _three_terms(st):
    hi = st.astype(BF16).astype(F32)
    rest = st - hi
    mid = rest.astype(BF16).astype(F32)
    lo = (rest - mid).astype(BF16).astype(F32)
    return hi + pltpu.roll(mid, 8, 1) + pltpu.roll(lo, 16, 1)


def _attn16_kernel(q_ref, k_ref, v_ref, o_ref, st_ref, kbuf, vbuf, *, tq):
    i = pl.program_id(2)

    @pl.when(i == 0)
    def _():
        kbuf[0:BAND, :] = jnp.zeros((BAND, ATTN_W), BF16)
        vbuf[0:BAND, :] = jnp.zeros((BAND, ATTN_W), BF16)

    kbuf[BAND:BAND + tq, :] = k_ref[...]
    vbuf[BAND:BAND + tq, :] = v_ref[...]
    band_bias, col = _band_bias(False)

    def block(j, carry):
        r0 = pl.multiple_of(j * BAND, BAND)
        first_key = jnp.where(jnp.logical_or(j > 0, i > 0), 0, BAND)
        bias = jnp.where(col >= first_key, band_bias, NEG)

        def emit(p, o_pair):
            o_ref[pl.ds(r0, BAND), p * LANES:(p + 1) * LANES] = o_pair.astype(BF16)

        st = _attn_heads(lambda p: q_ref[pl.ds(r0, BAND), p * LANES:(p + 1) * LANES],
                         lambda p: kbuf[pl.ds(r0, 2 * BAND), p * LANES:(p + 1) * LANES],
                         lambda p: vbuf[pl.ds(r0, 2 * BAND), p * LANES:(p + 1) * LANES],
                         bias, emit)
        st_ref[pl.ds(r0, BAND), :] = _three_terms(st)
        return carry

    lax.fori_loop(0, tq // BAND, block, 0, unroll=True)
    kbuf[0:BAND, :] = kbuf[tq:tq + BAND, :]
    vbuf[0:BAND, :] = vbuf[tq:tq + BAND, :]


def _attn4_kernel(q_ref, k_ref, v_ref, o_ref, st_ref, qbuf, kbuf, vbuf, obuf, sbuf):
    i = pl.program_id(1)
    rows = TILE // PERM_D
    nres = 4

    @pl.when(i == 0)
    def _():
        kbuf[:, 0:BAND, :] = jnp.zeros((nres, BAND, ATTN_W), BF16)
        vbuf[:, 0:BAND, :] = jnp.zeros((nres, BAND, ATTN_W), BF16)

    @pl.when(i > 0)
    def _():
        kbuf[:, 0:BAND, :] = kbuf[:, BAND:2 * BAND, :]
        vbuf[:, 0:BAND, :] = vbuf[:, BAND:2 * BAND, :]

    for r4 in range(nres):
        for g in range(PERM_D // nres):
            lanes = slice((r4 + nres * g) * ATTN_W, (r4 + nres * g + 1) * ATTN_W)
            qbuf[r4, g * rows:(g + 1) * rows, :] = q_ref[:, lanes]
            kbuf[r4, BAND + g * rows:BAND + (g + 1) * rows, :] = k_ref[:, lanes]
            vbuf[r4, BAND + g * rows:BAND + (g + 1) * rows, :] = v_ref[:, lanes]

    band_bias, col = _band_bias(True)
    first_key = jnp.where(i > 0, 0, BAND)
    bias = jnp.where(col >= first_key, band_bias, NEG)

    def block(j, carry):
        def emit(p, o_pair):
            obuf[j, :, p * LANES:(p + 1) * LANES] = o_pair.astype(BF16)

        st = _attn_heads(lambda p: qbuf[j, :, p * LANES:(p + 1) * LANES],
                         lambda p: kbuf[j, :, p * LANES:(p + 1) * LANES],
                         lambda p: vbuf[j, :, p * LANES:(p + 1) * LANES],
                         bias, emit)
        sbuf[j] = _three_terms(st)
        return carry

    lax.fori_loop(0, nres, block, 0, unroll=True)

    for r4 in range(nres):
        for g in range(PERM_D // nres):
            grp = r4 + nres * g
            o_ref[:, grp * ATTN_W:(grp + 1) * ATTN_W] = obuf[r4, g * rows:(g + 1) * rows, :]
            st_ref[:, grp * LANES:(grp + 1) * LANES] = sbuf[r4, g * rows:(g + 1) * rows, :]


def _attn1_kernel(q_ref, k_ref, v_ref, o4_ref, s4_ref, o16_ref, s16_ref, pt_ref, g_ref, out_ref,
                  kbuf, vbuf, acc, o4t, o16t, stt, s1t, *, tq):
    i = pl.program_id(1)

    @pl.when(i == 0)
    def _():
        kbuf[0:BAND, :] = jnp.zeros((BAND, ATTN_W), BF16)
        vbuf[0:BAND, :] = jnp.zeros((BAND, ATTN_W), BF16)

    kbuf[BAND:BAND + tq, :] = k_ref[...]
    vbuf[BAND:BAND + tq, :] = v_ref[...]

    prow = PERM_TILE // PERM_D
    for h in range(tq // PERM_TILE):
        tok_rows = slice(h * PERM_TILE, (h + 1) * PERM_TILE)

        def rows_of(ref, w):
            return jnp.concatenate([ref[h * prow:(h + 1) * prow, r * w:(r + 1) * w] for r in range(PERM_D)], axis=0)

        o4t[tok_rows, :] = jnp.dot(pt_ref[...], rows_of(o4_ref, ATTN_W), preferred_element_type=F32).astype(BF16)
        o16t[tok_rows, :] = jnp.dot(pt_ref[...], rows_of(o16_ref, ATTN_W), preferred_element_type=F32).astype(BF16)
        terms = (rows_of(s4_ref, LANES) + pltpu.roll(rows_of(s16_ref, LANES), 32, 1)).astype(BF16)
        terms = jnp.dot(pt_ref[...], terms, preferred_element_type=F32)
        stt[tok_rows, :] = terms + pltpu.roll(terms, LANES - 8, 1) + pltpu.roll(terms, LANES - 16, 1)
    band_bias, col = _band_bias(False)

    def block(j, carry):
        r0 = pl.multiple_of(j * BAND, BAND)
        first_key = jnp.where(jnp.logical_or(j > 0, i > 0), 0, BAND)
        bias = jnp.where(col >= first_key, band_bias, NEG)

        def emit(p, o_pair):
            acc[pl.ds(r0, BAND), p * LANES:(p + 1) * LANES] = o_pair

        s1t[pl.ds(r0, BAND), :] = _attn_heads(
            lambda p: q_ref[pl.ds(r0, BAND), p * LANES:(p + 1) * LANES],
            lambda p: kbuf[pl.ds(r0, 2 * BAND), p * LANES:(p + 1) * LANES],
            lambda p: vbuf[pl.ds(r0, 2 * BAND), p * LANES:(p + 1) * LANES],
            bias, emit)
        return carry

    lax.fori_loop(0, tq // BAND, block, 0, unroll=True)
    kbuf[0:BAND, :] = kbuf[tq:tq + BAND, :]
    vbuf[0:BAND, :] = vbuf[tq:tq + BAND, :]

    head_of = lax.broadcasted_iota(I32, (LANES, ATTN_W), 1) // HEAD_DIM
    expand = jnp.where(lax.broadcasted_iota(I32, (LANES, ATTN_W), 0) == head_of, 1.0, 0.0).astype(BF16)

    def merge(j, carry):
        r0 = pl.multiple_of(j * BAND, BAND)
        is_head = lax.broadcasted_iota(I32, (BAND, LANES), 1) < N_HEADS
        l1 = s1t[pl.ds(r0, BAND), :]
        l4 = stt[pl.ds(r0, BAND), :]
        l16 = pltpu.roll(l4, LANES - 32, 1)
        top = jnp.maximum(l1, jnp.maximum(l4, l16))
        e1 = jnp.exp(l1 - top)
        e4 = jnp.exp(l4 - top)
        e16 = jnp.exp(l16 - top)
        inv = 1.0 / (e1 + e4 + e16)

        def spread(e):
            w = jnp.where(is_head, e * inv, 0.0)
            hi = w.astype(BF16)
            lo = (w - hi.astype(F32)).astype(BF16)
            return (jnp.dot(hi, expand, preferred_element_type=F32)
                    + jnp.dot(lo, expand, preferred_element_type=F32))

        y = (spread(e1) * acc[pl.ds(r0, BAND), :]
             + spread(e4) * o4t[pl.ds(r0, BAND), :].astype(F32)
             + spread(e16) * o16t[pl.ds(r0, BAND), :].astype(F32))
        scale = lax.rsqrt(jnp.mean(y * y, axis=1, keepdims=True) + EPS)
        out_ref[pl.ds(r0, BAND), :] = (y * scale * g_ref[...]).astype(BF16)
        return carry

    lax.fori_loop(0, tq // BAND, merge, 0, unroll=True)


def _attention(q, k, v, q16, k16, v16, perm_t, gain):
    b, s, _ = q.shape
    m16 = s // PERM_D
    rows = TILE // PERM_D
    o16_shape = [jax.ShapeDtypeStruct((b, m16, PERM_D * ATTN_W), BF16),
                 jax.ShapeDtypeStruct((b, m16, PERM_D * LANES), F32)]

    tq16 = min(ATTN_TQ, m16)
    blk16 = lambda w: pl.BlockSpec((None, tq16, w), lambda bi, ri, ii: (bi, ii, ri))
    o16, s16 = pl.pallas_call(
        functools.partial(_attn16_kernel, tq=tq16),
        grid=(b, PERM_D, m16 // tq16),
        in_specs=[blk16(ATTN_W)] * 3, out_specs=[blk16(ATTN_W), blk16(LANES)], out_shape=o16_shape,
        scratch_shapes=[pltpu.VMEM((tq16 + BAND, ATTN_W), BF16)] * 2,
        compiler_params=_params("arbitrary", "arbitrary", "arbitrary"),
        name="attn_d16",
    )(q16, k16, v16)

    tile = lambda w: pl.BlockSpec((None, rows, PERM_D * w), lambda bi, ii: (bi, ii, 0))
    o4, s4 = pl.pallas_call(
        _attn4_kernel,
        grid=(b, s // TILE),
        in_specs=[tile(ATTN_W)] * 3, out_specs=[tile(ATTN_W), tile(LANES)], out_shape=o16_shape,
        scratch_shapes=[pltpu.VMEM((4, BAND, ATTN_W), BF16), pltpu.VMEM((4, 2 * BAND, ATTN_W), BF16),
                        pltpu.VMEM((4, 2 * BAND, ATTN_W), BF16), pltpu.VMEM((4, BAND, ATTN_W), BF16),
                        pltpu.VMEM((4, BAND, LANES), F32)],
        compiler_params=_params("arbitrary", "arbitrary"),
        name="attn_d4",
    )(q16, k16, v16)

    tq = min(ATTN1_TQ, s)
    tok = lambda w: pl.BlockSpec((None, tq, w), lambda bi, ii: (bi, ii, 0))
    tile = lambda w: pl.BlockSpec((None, tq // PERM_D, PERM_D * w), lambda bi, ii: (bi, ii, 0))
    full = lambda shape: pl.BlockSpec(shape, lambda bi, ii: (0,) * len(shape))
    return pl.pallas_call(
        functools.partial(_attn1_kernel, tq=tq),
        grid=(b, s // tq),
        in_specs=[tok(ATTN_W)] * 3 + [tile(ATTN_W), tile(LANES), tile(ATTN_W), tile(LANES),
                                      full((PERM_TILE, PERM_TILE)), full((1, ATTN_W))],
        out_specs=tok(ATTN_W),
        out_shape=jax.ShapeDtypeStruct((b, s, ATTN_W), BF16),
        scratch_shapes=[pltpu.VMEM((tq + BAND, ATTN_W), BF16), pltpu.VMEM((tq + BAND, ATTN_W), BF16),
                        pltpu.VMEM((tq, ATTN_W), F32),
                        pltpu.VMEM((tq, ATTN_W), BF16), pltpu.VMEM((tq, ATTN_W), BF16),
                        pltpu.VMEM((tq, LANES), F32), pltpu.VMEM((tq, LANES), F32)],
        compiler_params=_params("arbitrary", "arbitrary"),
        name="attn_d1",
    )(q, k, v, o4, s4, o16, s16, perm_t, gain)


def _gelu_tanh(x):
    c = math.sqrt(2.0 / math.pi)
    return x * (0.5 * (1.0 + jnp.tanh(c * (x + 0.044715 * (x * x * x)))))


def _split_dot(x, m_bf):
    hi = x.astype(BF16)
    lo = (x - hi.astype(F32)).astype(BF16)
    return (jnp.dot(hi, m_bf, preferred_element_type=F32)
            + jnp.dot(lo, m_bf, preferred_element_type=F32))


def _layer_norm(t, g, b):
    mu = jnp.mean(t, axis=1, keepdims=True)
    tc = t - mu
    var = jnp.mean(tc * tc, axis=1, keepdims=True)
    return tc * lax.rsqrt(var + EPS) * g + b


def _outproj_kernel(ma_ref, mb_ref, x_ref, wa_ref, wb_ref, lg_ref, lb_ref, rw_ref, rb_ref,
                    x1_ref, xp_ref, info_ref, infot_ref, cnt_ref, run_ref, *, tm, alpha):
    i = pl.program_id(0)

    @pl.when(i == 0)
    def _():
        run_ref[...] = jnp.zeros_like(run_ref)

    y = (jnp.dot(ma_ref[...], wa_ref[...], preferred_element_type=F32)
         + jnp.dot(mb_ref[...], wb_ref[...], preferred_element_type=F32))
    x1 = _layer_norm(alpha * x_ref[...] + y, lg_ref[...], lb_ref[...])
    x1_ref[...] = x1
    xp_ref[...] = _pack_pairs(x1)

    hi = x1.astype(BF16)
    lo = (x1 - hi.astype(F32)).astype(BF16)
    both = jnp.dot(hi, rw_ref[...], preferred_element_type=F32)
    logit = (both[:, :LANES] + both[:, LANES:]
             + jnp.dot(lo, rw_ref[:, :LANES], preferred_element_type=F32)) + rb_ref[...]
    lane = lax.broadcasted_iota(I32, (tm, LANES), 1)
    lane_f = lane.astype(F32)

    def top(mask):
        v = jnp.max(jnp.where(mask, logit, NEG), axis=1, keepdims=True)
        first = jnp.min(jnp.where(jnp.logical_and(mask, logit == v), lane_f, float(LANES)),
                        axis=1, keepdims=True)
        return v, first.astype(I32)

    is_g = lane < MOE_GROUPS
    gmax, gidx = top(is_g)
    g_p = 1.0 / jnp.sum(jnp.where(is_g, jnp.exp(logit - gmax), 0.0), axis=1, keepdims=True)
    in_grp = jnp.logical_and(lane >= MOE_GROUPS + gidx * EXPERTS_PER_GROUP,
                             lane < MOE_GROUPS + (gidx + 1) * EXPERTS_PER_GROUP)
    v1, i1 = top(in_grp)
    v2, i2 = top(jnp.logical_and(in_grp, lane != i1))
    e21 = jnp.exp(v2 - v1)
    gate1 = g_p / (1.0 + e21)
    gate2 = g_p * e21 / (1.0 + e21)
    ex1 = i1 - MOE_GROUPS
    ex2 = i2 - MOE_GROUPS

    oh1 = lane == ex1
    oh2 = lane == ex2
    oh = (oh1.astype(F32) + oh2.astype(F32))
    tr = lax.broadcasted_iota(I32, (tm, tm), 0)
    tc = lax.broadcasted_iota(I32, (tm, tm), 1)
    lower = jnp.where(tc < tr, 1.0, 0.0).astype(BF16)
    before = jnp.dot(lower, oh.astype(BF16), preferred_element_type=F32) + run_ref[0:1, :]
    rank1 = jnp.sum(jnp.where(oh1, before, 0.0), axis=1, keepdims=True)
    rank2 = jnp.sum(jnp.where(oh2, before, 0.0), axis=1, keepdims=True)
    run_new = run_ref[0:1, :] + jnp.sum(oh, axis=0, keepdims=True)
    run_ref[...] = jnp.broadcast_to(run_new, run_ref.shape)
    cnt_ref[...] = jnp.broadcast_to(run_new, cnt_ref.shape)

    info = jnp.where(lane == 0, ex1.astype(F32), 0.0)
    info = jnp.where(lane == 1, ex2.astype(F32), info)
    info = jnp.where(lane == 2, gate1, info)
    info = jnp.where(lane == 3, gate2, info)
    info = jnp.where(lane == 4, rank1, info)
    info = jnp.where(lane == 5, rank2, info)
    info_ref[...] = info
    infot_ref[...] = info.T[0:8, :]


def _outproj(ma, mbc, x, wo_a, wo_b, ln_g, ln_b, rw_hilo, rbias, alpha):
    n = x.shape[0]
    tm = OUT_TM
    tok = lambda w: pl.BlockSpec((tm, w), lambda i: (i, 0))
    full = lambda shape: pl.BlockSpec(shape, lambda i: (0,) * len(shape))
    return pl.pallas_call(
        functools.partial(_outproj_kernel, tm=tm, alpha=alpha),
        grid=(n // tm,),
        in_specs=[tok(ATTN_W), tok(SGU_W + CONV_W), tok(D_MODEL),
                  full((ATTN_W, D_MODEL)), full((SGU_W + CONV_W, D_MODEL)),
                  full((1, D_MODEL)), full((1, D_MODEL)),
                  full((D_MODEL, 2 * LANES)), full((1, LANES))],
        out_specs=[tok(D_MODEL), tok(D_MODEL // 2), tok(LANES), pl.BlockSpec((8, tm), lambda i: (0, i)),
                   full((8, LANES))],
        out_shape=[jax.ShapeDtypeStruct((n, D_MODEL), F32),
                   jax.ShapeDtypeStruct((n, D_MODEL // 2), U32),
                   jax.ShapeDtypeStruct((n, LANES), F32),
                   jax.ShapeDtypeStruct((8, n), F32),
                   jax.ShapeDtypeStruct((8, LANES), F32)],
        scratch_shapes=[pltpu.VMEM((8, LANES), F32)],
        compiler_params=_params("arbitrary"),
        name="outproj",
    )(ma, mbc, x, wo_a, wo_b, ln_g, ln_b, rw_hilo, rbias)


def _sc_mesh():
    return plsc.VectorSubcoreMesh(core_axis_name="c", subcore_axis_name="s",
                                  num_cores=SC_CORES, num_subcores=SC_SUBCORES)


def _sc_chunk(rows_per_worker):
    chunk = min(SC_CHUNK, rows_per_worker // 2)
    assert rows_per_worker % (2 * chunk) == 0 and chunk % 8 == 0, rows_per_worker
    return chunk


def _sc_dispatch(xp, dest_kn, rows):
    n, w = xp.shape
    t_per_w = n // SC_WORKERS
    chunk = _sc_chunk(t_per_w)
    nchunk = t_per_w // chunk

    def body(src_hbm, dest_hbm, out_hbm, idx_v, rows_v, lsem, ssem):
        wid = lax.axis_index("s") * SC_CORES + lax.axis_index("c")
        base = wid * t_per_w
        pltpu.sync_copy(dest_hbm.at[0, wid], idx_v.at[0])
        pltpu.sync_copy(dest_hbm.at[1, wid], idx_v.at[1])

        def load(c, slot):
            return pltpu.make_async_copy(src_hbm.at[pl.ds(base + c * chunk, chunk)], rows_v.at[slot],
                                         lsem.at[slot])

        def put(c, slot, kk):
            return pltpu.make_async_copy(rows_v.at[slot], out_hbm.at[idx_v.at[kk, c]], ssem.at[slot])

        load(0, 0).start()

        @pl.loop(0, nchunk, step=2)
        def _(c):
            for b in range(2):
                cc = c + b
                load(cc, b).wait()

                @pl.when(cc + 1 < nchunk)
                def _():
                    @pl.when(cc >= 1)
                    def _():
                        put(cc - 1, 1 - b, 0).wait()
                        put(cc - 1, 1 - b, 1).wait()
                    load(cc + 1, 1 - b).start()

                put(cc, b, 0).start()
                put(cc, b, 1).start()

        for b in range(2):
            put(nchunk - 2 + b, b, 0).wait()
            put(nchunk - 2 + b, b, 1).wait()

    call = pl.kernel(
        body, mesh=_sc_mesh(),
        out_type=jax.ShapeDtypeStruct((rows, w), U32),
        scratch_types=[pltpu.VMEM((2, nchunk, chunk), I32), pltpu.VMEM((2, chunk, w), U32),
                       pltpu.SemaphoreType.DMA((2,)), pltpu.SemaphoreType.DMA((2,))],
        name="sc_dispatch")
    return call(xp, dest_kn.reshape(2, SC_WORKERS, nchunk, chunk))


def _sc_gather(table, idx):
    b = idx.shape[0]
    w = table.shape[1]
    b_per_w = b // SC_WORKERS
    chunk = _sc_chunk(b_per_w)
    nchunk = b_per_w // chunk

    def body(table_hbm, idx_hbm, out_hbm, idx_v, rows_v, gsem, osem):
        wid = lax.axis_index("s") * SC_CORES + lax.axis_index("c")
        base = wid * b_per_w
        pltpu.sync_copy(idx_hbm.at[wid], idx_v)

        def gather(c, slot):
            return pltpu.make_async_copy(table_hbm.at[idx_v.at[c]], rows_v.at[slot], gsem.at[slot])

        def put(c, slot):
            return pltpu.make_async_copy(rows_v.at[slot], out_hbm.at[pl.ds(base + c * chunk, chunk)],
                                         osem.at[slot])

        gather(0, 0).start()

        @pl.loop(0, nchunk, step=2)
        def _(c):
            for b in range(2):
                cc = c + b
                gather(cc, b).wait()

                @pl.when(cc + 1 < nchunk)
                def _():
                    @pl.when(cc >= 1)
                    def _():
                        put(cc - 1, 1 - b).wait()
                    gather(cc + 1, 1 - b).start()

                put(cc, b).start()

        put(nchunk - 2, 0).wait()
        put(nchunk - 1, 1).wait()

    call = pl.kernel(
        body, mesh=_sc_mesh(),
        out_type=jax.ShapeDtypeStruct((b, w), table.dtype),
        scratch_types=[pltpu.VMEM((nchunk, chunk), I32), pltpu.VMEM((2, chunk, w), table.dtype),
                       pltpu.SemaphoreType.DMA((2,)), pltpu.SemaphoreType.DMA((2,))],
        name="sc_gather")
    return call(table, idx.reshape(SC_WORKERS, nchunk, chunk))


def _expert_kernel(be_ref, nv_ref, nu_ref, xs_ref, wg_ref, wu_ref, wd_ref, y_ref, wgb, wub, wdb):
    i = pl.program_id(0)
    cur = jnp.minimum(i, nu_ref[0] - 1)
    new_expert = jnp.logical_or(i == 0, be_ref[cur] != be_ref[jnp.maximum(cur - 1, 0)])

    @pl.when(jnp.logical_and(i < nu_ref[0], new_expert))
    def _():
        wgb[...] = wg_ref[...].astype(BF16)
        wub[...] = wu_ref[...].astype(BF16)
        wdb[...] = wd_ref[...].astype(BF16)

    @pl.when(i < nu_ref[0])
    def _():
        row = lax.broadcasted_iota(I32, xs_ref.shape, 0)
        lo, hi = _unpack_pairs(jnp.where(row < nv_ref[i], xs_ref[...], jnp.uint32(0)))
        xb = jnp.concatenate([lo, hi], axis=1).astype(BF16)
        g = jnp.dot(xb, wgb[...], preferred_element_type=F32)
        u = jnp.dot(xb, wub[...], preferred_element_type=F32)
        hdn = (g * (1.0 / (1.0 + jnp.exp(-g))) * u).astype(BF16)
        y_ref[...] = _pack_pairs(jnp.dot(hdn, wdb[...], preferred_element_type=F32))

    @pl.when(i >= nu_ref[0])
    def _():
        y_ref[...] = jnp.zeros_like(y_ref)


def _experts(block_expert, block_valid, n_used, xs, wg, wu, wd, layer):
    rows, w = xs.shape
    tb = MOE_TB
    blk = lambda i, be, nv, nu: (jnp.minimum(i, nu[0] - 1), 0)
    oblk = lambda i, be, nv, nu: (i, 0)
    wsel = lambda i, be, nv, nu: (layer, be[jnp.minimum(i, nu[0] - 1)], 0, 0)
    return pl.pallas_call(
        _expert_kernel,
        grid_spec=pltpu.PrefetchScalarGridSpec(
            num_scalar_prefetch=3,
            grid=(rows // tb,),
            in_specs=[pl.BlockSpec((tb, w), blk),
                      pl.BlockSpec((None, None, D_MODEL, D_EXPERT), wsel),
                      pl.BlockSpec((None, None, D_MODEL, D_EXPERT), wsel),
                      pl.BlockSpec((None, None, D_EXPERT, D_MODEL), wsel)],
            out_specs=pl.BlockSpec((tb, w), oblk),
            scratch_shapes=[pltpu.VMEM((D_MODEL, D_EXPERT), BF16), pltpu.VMEM((D_MODEL, D_EXPERT), BF16),
                            pltpu.VMEM((D_EXPERT, D_MODEL), BF16)]),
        out_shape=jax.ShapeDtypeStruct((rows, w), U32),
        compiler_params=_params("arbitrary"),
        name="experts",
    )(block_expert, block_valid, n_used, xs, wg, wu, wd)


def _combine_kernel(ya_ref, yb_ref, x_ref, info_ref, lg_ref, lb_ref, out_ref, *, alpha):
    out_ref[...] = _combine_tile(ya_ref[...], yb_ref[...], x_ref[...], info_ref[...], lg_ref[...], lb_ref[...],
                                 alpha)


def _combine(y_tok, x1, info, ln_g, ln_b, alpha):
    n = x1.shape[0]
    tm = ROW_TM
    tok = lambda w: pl.BlockSpec((tm, w), lambda i: (i, 0))
    slot = lambda k: pl.BlockSpec((tm, D_MODEL // 2), lambda i: (i + k * (n // tm), 0))
    full = lambda shape: pl.BlockSpec(shape, lambda i: (0,) * len(shape))
    return pl.pallas_call(
        functools.partial(_combine_kernel, alpha=alpha),
        grid=(n // tm,),
        in_specs=[slot(0), slot(1), tok(D_MODEL), tok(LANES), full((1, D_MODEL)), full((1, D_MODEL))],
        out_specs=tok(D_MODEL),
        out_shape=jax.ShapeDtypeStruct((n, D_MODEL), F32),
        compiler_params=_params("arbitrary"),
        name="combine",
    )(y_tok, y_tok, x1, info, ln_g, ln_b)


def _rope_tables(s):
    half = HEAD_DIM // 2
    inv_freq = ROPE_THETA ** (-jnp.arange(half, dtype=F32) / half)
    ang = jnp.arange(s, dtype=F32)[:, None] * inv_freq[None, :]
    cos = jnp.cos(ang)
    sin = jnp.sin(ang)
    cos_t = jnp.tile(cos, (1, LANES // half))
    sin_t = jnp.tile(jnp.concatenate([-sin, sin], axis=1), (1, LANES // HEAD_DIM))
    return cos_t, sin_t


def _forward(x, w_in, w_out, branch_gain, sgu_gain, sgu_w, sgu_b, conv_w, ln_gain, ln_bias,
             router_group_w, router_group_b, router_expert_w, router_expert_b,
             expert_w_gate, expert_w_up, expert_w_down):
    b, s, d_model = x.shape
    depth = w_in.shape[0]
    n = b * s
    assert d_model == D_MODEL and w_in.shape[1:] == (D_MODEL, PROJ_W), (x.shape, w_in.shape)
    assert s % (PERM_D * BAND) == 0 and n % (2 * SC_WORKERS * 8) == 0, (b, s)
    alpha = (2.0 * depth) ** 0.25
    cos_t, sin_t = _rope_tables(s)
    perm, perm_t = _tile_perm()
    tb = MOE_TB
    n_blocks = (2 * n + N_EXPERTS * (tb - 1) + tb - 1) // tb
    rows = n_blocks * tb

    pending = None
    for l in range(depth):
        g = branch_gain[l]
        bias_tile = jnp.repeat(sgu_b[l].T, SGU_W // SGU_GROUPS, axis=1)
        mix_params = (sgu_gain[l][None, :], sgu_w[l], bias_tile, conv_w[l], g[None, ATTN_W:])
        outs = _proj(x if pending is None else pending, w_in[l].astype(BF16), cos_t, sin_t, perm, mix_params,
                     b, s, alpha)
        if pending is not None:
            x, outs = outs[0], outs[1:]
        q, k, v, mbc, q16, k16, v16 = outs
        ma = _attention(q, k, v, q16, k16, v16, perm_t, g[None, :ATTN_W])

        rw = jnp.zeros((D_MODEL, LANES), F32)
        rw = rw.at[:, :MOE_GROUPS].set(router_group_w[l])
        rw = rw.at[:, MOE_GROUPS:MOE_GROUPS + N_EXPERTS].set(router_expert_w[l])
        rw_hi = rw.astype(BF16)
        rw_hilo = jnp.concatenate([rw_hi, (rw - rw_hi.astype(F32)).astype(BF16)], axis=1)
        rbias = jnp.zeros((1, LANES), F32)
        rbias = rbias.at[0, :MOE_GROUPS].set(router_group_b[l])
        rbias = rbias.at[0, MOE_GROUPS:MOE_GROUPS + N_EXPERTS].set(router_expert_b[l])
        wo = w_out[l].astype(BF16)
        x1, xp, info, info_t, cnt = _outproj(
            ma.reshape(n, ATTN_W), mbc.reshape(n, SGU_W + CONV_W), x.reshape(n, D_MODEL),
            wo[:ATTN_W], wo[ATTN_W:], ln_gain[l, 0][None], ln_bias[l, 0][None], rw_hilo, rbias, alpha)

        counts = cnt[0, :N_EXPERTS].astype(I32)
        padded = (counts + tb - 1) // tb * tb
        pad_end = jnp.cumsum(padded)
        pad_start = pad_end - padded
        ex = info_t[0:2].astype(I32)
        start_of = jnp.zeros_like(ex)
        for e in range(N_EXPERTS):
            start_of = jnp.where(ex == e, pad_start[e], start_of)
        dest_kn = start_of + info_t[4:6].astype(I32)
        blk_row0 = jnp.arange(n_blocks, dtype=I32) * tb
        block_expert = jnp.minimum(jnp.sum((pad_end[None, :] <= blk_row0[:, None]).astype(I32), axis=1),
                                   N_EXPERTS - 1)
        block_valid = jnp.clip(pad_start[block_expert] + counts[block_expert] - blk_row0, 0, tb)
        n_used = (pad_end[-1:] // tb).astype(I32)

        xs = _sc_dispatch(xp, dest_kn, rows)
        ys = _experts(block_expert, block_valid, n_used, xs, expert_w_gate, expert_w_up, expert_w_down, l)
        y_tok = _sc_gather(ys, dest_kn.reshape(2 * n))
        pending = (y_tok, x1, info, ln_gain[l, 1][None], ln_bias[l, 1][None])
    return _combine(*pending, alpha).reshape(b, s, D_MODEL)


def kernel(x, w_in, w_out, branch_gain, sgu_gain, sgu_w, sgu_b, conv_w, ln_gain, ln_bias, router_group_w, router_group_b, router_expert_w, router_expert_b, expert_w_gate, expert_w_up, expert_w_down):
    return _forward(x, w_in, w_out, branch_gain, sgu_gain, sgu_w, sgu_b, conv_w, ln_gain, ln_bias,
                    router_group_w, router_group_b, router_expert_w, router_expert_b,
                    expert_w_gate, expert_w_up, expert_w_down)
```

```python
import functools
import math

import jax
import jax.numpy as jnp
from jax import lax
from jax.experimental import pallas as pl
from jax.experimental.pallas import tpu as pltpu
from jax.experimental.pallas import tpu_sc as plsc

F32 = jnp.float32
BF16 = jnp.bfloat16
U32 = jnp.uint32
I32 = jnp.int32

D_MODEL = 1024
HEAD_DIM = 64
ATTN_W = 512
N_HEADS = 8
SGU_W = 256
SGU_GROUPS = 4
SGU_CHUNK = 128
CONV_W = 256
REST_W = 2 * SGU_W + 3 * CONV_W
PROJ_W = 3 * ATTN_W + REST_W
DILATIONS = (16, 4, 1)
BAND = 128
ROPE_THETA = 10000.0
MOE_GROUPS = 4
EXPERTS_PER_GROUP = 8
N_EXPERTS = MOE_GROUPS * EXPERTS_PER_GROUP
D_EXPERT = 512
EPS = 1e-5
NEG = -1e30

LANES = 128
VMEM_LIMIT = 56 * 1024 * 1024

TILE = 512
PERM_D = 16
PERM_TILE = 256
ATTN_TQ = 512
ATTN1_TQ = 1024
OUT_TM = 512
MOE_TB = 1024
ROW_TM = 512
SC_CORES = 2
SC_SUBCORES = 16
SC_WORKERS = SC_CORES * SC_SUBCORES
SC_CHUNK = 64


def _params(*sem):
    return pltpu.CompilerParams(dimension_semantics=sem, vmem_limit_bytes=VMEM_LIMIT)


def _pack_pairs(x):
    w = x.shape[1] // 2
    lo = lax.bitcast_convert_type(x[:, :w].astype(BF16).astype(F32), U32)
    hi = lax.bitcast_convert_type(x[:, w:].astype(BF16).astype(F32), U32)
    return (lo >> 16) | (hi & jnp.uint32(0xFFFF0000))


def _unpack_pairs(p):
    lo = lax.bitcast_convert_type(p << 16, F32)
    hi = lax.bitcast_convert_type(p & jnp.uint32(0xFFFF0000), F32)
    return lo, hi


def _tile_perm():
    i = jnp.arange(PERM_TILE, dtype=I32)
    tok = PERM_D * (i % (PERM_TILE // PERM_D)) + i // (PERM_TILE // PERM_D)
    p = (jnp.arange(PERM_TILE, dtype=I32)[None, :] == tok[:, None]).astype(BF16)
    return p, p.T


def _mixers(rest, prev_zz, sg, sw_ref, sb, cw_ref, bg):
    tm = rest.shape[0]
    u = rest[:, 0:SGU_W]
    z = rest[:, SGU_W:2 * SGU_W]
    gb = rest[:, 2 * SGU_W:2 * SGU_W + CONV_W]
    gc = rest[:, 2 * SGU_W + CONV_W:2 * SGU_W + 2 * CONV_W]
    hh = rest[:, 2 * SGU_W + 2 * CONV_W:]

    gdim = SGU_W // SGU_GROUPS
    ri = lax.broadcasted_iota(I32, (SGU_W, SGU_W), 0) // gdim
    ci = lax.broadcasted_iota(I32, (SGU_W, SGU_W), 1) // gdim
    avg = jnp.where(ri == ci, 1.0 / gdim, 0.0).astype(BF16)
    z = _gelu_tanh(z)
    zc = z - _split_dot(z, avg)
    var = _split_dot(zc * zc, avg)
    zn = (zc * lax.rsqrt(var + EPS) * sg).astype(BF16)

    tr = lax.broadcasted_iota(I32, (SGU_CHUNK, SGU_CHUNK), 0)
    tc = lax.broadcasted_iota(I32, (SGU_CHUNK, SGU_CHUNK), 1)
    w_cat = jnp.concatenate(
        [jnp.where(tc <= tr, sw_ref[g], 0.0).astype(BF16) for g in range(SGU_GROUPS)], axis=1)
    lane_g = lax.broadcasted_iota(I32, (SGU_CHUNK, SGU_W), 1) // gdim
    gu = _gelu_tanh(u)
    yb = []
    for c in range(tm // SGU_CHUNK):
        zch = zn[c * SGU_CHUNK:(c + 1) * SGU_CHUNK, :]
        stack = jnp.concatenate(
            [jnp.where(lane_g == g, zch, jnp.zeros_like(zch)) for g in range(SGU_GROUPS)], axis=0)
        sp = jnp.dot(w_cat, stack, preferred_element_type=F32) + sb
        yb.append(gu[c * SGU_CHUNK:(c + 1) * SGU_CHUNK, :] * sp)
    yb = jnp.concatenate(yb, axis=0)

    zz = gc * hh
    hrows = prev_zz.shape[0]
    ext = jnp.concatenate([prev_zz, zz], axis=0)
    z1 = ext[hrows - 1:hrows - 1 + tm, :]
    z2 = ext[hrows - 2:hrows - 2 + tm, :]
    yc = gb * (cw_ref[0:1, :] * z2 + cw_ref[1:2, :] * z1 + cw_ref[2:3, :] * zz)

    def rms(t, g):
        return t * lax.rsqrt(jnp.mean(t * t, axis=1, keepdims=True) + EPS) * g

    out = jnp.concatenate([rms(yb, bg[:, 0:SGU_W]).astype(BF16), rms(yc, bg[:, SGU_W:]).astype(BF16)], axis=1)
    return out, zz[tm - hrows:, :]


def _combine_tile(ya, yb, x1, info, lg, lb, alpha):
    g1 = info[:, 2:3]
    g2 = info[:, 3:4]
    a_lo, a_hi = _unpack_pairs(ya)
    b_lo, b_hi = _unpack_pairs(yb)
    half = D_MODEL // 2
    t_lo = alpha * x1[:, 0:half] + g1 * a_lo + g2 * b_lo
    t_hi = alpha * x1[:, half:] + g1 * a_hi + g2 * b_hi
    return _layer_norm(jnp.concatenate([t_lo, t_hi], axis=1), lg, lb)


def _proj_kernel(*refs, fused, alpha):
    if fused:
        (ya_ref, yb_ref, x1_ref, info_ref, lg_ref, lb_ref, w_ref, cos_ref, sin_ref, p_ref,
         sg_ref, sw_ref, sb_ref, cw_ref, bg_ref,
         x_out, q_ref, k_ref, v_ref, m_ref, q16_ref, k16_ref, v16_ref, zz_prev) = refs
        x = _combine_tile(ya_ref[...], yb_ref[...], x1_ref[...], info_ref[...], lg_ref[...], lb_ref[...], alpha)
        x_out[...] = x
    else:
        (x_ref, w_ref, cos_ref, sin_ref, p_ref, sg_ref, sw_ref, sb_ref, cw_ref, bg_ref,
         q_ref, k_ref, v_ref, m_ref, q16_ref, k16_ref, v16_ref, zz_prev) = refs
        x = x_ref[...]
    si = pl.program_id(1)

    @pl.when(si == 0)
    def _():
        zz_prev[...] = jnp.zeros_like(zz_prev)

    xb = x.astype(BF16)
    cos = cos_ref[...]
    sin = sin_ref[...]
    lane = lax.broadcasted_iota(I32, cos.shape, 1)
    first_half = (lane % HEAD_DIM) < (HEAD_DIM // 2)
    rows = PERM_TILE // PERM_D

    def store_both(val_bf, out_ref, out16_ref):
        out_ref[...] = val_bf
        for h in range(TILE // PERM_TILE):
            perm = jnp.dot(p_ref[...], val_bf[h * PERM_TILE:(h + 1) * PERM_TILE, :],
                           preferred_element_type=F32).astype(BF16)
            for r in range(PERM_D):
                out16_ref[h * rows:(h + 1) * rows, r * ATTN_W:(r + 1) * ATTN_W] = perm[r * rows:(r + 1) * rows, :]

    def rope(col0, scale):
        t = jnp.dot(xb, w_ref[:, col0:col0 + ATTN_W], preferred_element_type=F32)
        out = []
        for c in range(ATTN_W // LANES):
            tc = t[:, c * LANES:(c + 1) * LANES]
            partner = jnp.where(first_half, pltpu.roll(tc, LANES - 32, 1), pltpu.roll(tc, 32, 1))
            out.append(((tc * cos + partner * sin) * scale).astype(BF16))
        return jnp.concatenate(out, axis=1)

    rest = jnp.dot(xb, w_ref[:, 3 * ATTN_W:], preferred_element_type=F32)
    tail = [zz_prev[...]]

    def mix_chunk(c):
        rows_c = slice(c * SGU_CHUNK, (c + 1) * SGU_CHUNK)
        out, tail[0] = _mixers(rest[rows_c, :], tail[0], sg_ref[...], sw_ref, sb_ref[...], cw_ref, bg_ref[...])
        m_ref[rows_c, :] = out

    qb = rope(0, HEAD_DIM ** -0.5)
    mix_chunk(0)
    store_both(qb, q_ref, q16_ref)
    kb = rope(ATTN_W, 1.0)
    mix_chunk(1)
    store_both(kb, k_ref, k16_ref)
    vb = jnp.dot(xb, w_ref[:, 2 * ATTN_W:3 * ATTN_W], preferred_element_type=F32).astype(BF16)
    mix_chunk(2)
    store_both(vb, v_ref, v16_ref)
    mix_chunk(3)
    zz_prev[...] = tail[0]


def _proj(x_or_parts, w_bf, cos_t, sin_t, perm, mix_params, b, s, alpha):
    tm = TILE
    nt = s // tm
    fused = isinstance(x_or_parts, tuple)
    out3 = jax.ShapeDtypeStruct((b, s, ATTN_W), BF16)
    out16 = jax.ShapeDtypeStruct((b, s // PERM_D, PERM_D * ATTN_W), BF16)
    tok = lambda width: pl.BlockSpec((None, tm, width), lambda bi, si: (bi, si, 0))
    tok16 = pl.BlockSpec((None, tm // PERM_D, PERM_D * ATTN_W), lambda bi, si: (bi, si, 0))
    full = lambda shape: pl.BlockSpec(shape, lambda bi, si: (0,) * len(shape))
    flat = lambda width, off: pl.BlockSpec((tm, width), lambda bi, si: (bi * nt + si + off, 0))
    common_specs = [full((D_MODEL, PROJ_W)),
                    pl.BlockSpec((tm, LANES), lambda bi, si: (si, 0)),
                    pl.BlockSpec((tm, LANES), lambda bi, si: (si, 0)),
                    full((PERM_TILE, PERM_TILE)),
                    full((1, SGU_W)), full((SGU_GROUPS, SGU_CHUNK, SGU_CHUNK)),
                    full((SGU_CHUNK, SGU_W)), full((3, CONV_W)), full((1, SGU_W + CONV_W))]
    out_specs = [tok(ATTN_W), tok(ATTN_W), tok(ATTN_W), tok(SGU_W + CONV_W), tok16, tok16, tok16]
    out_shape = [out3, out3, out3, jax.ShapeDtypeStruct((b, s, SGU_W + CONV_W), BF16), out16, out16, out16]
    if fused:
        y_tok, x1, info, lg, lb = x_or_parts
        n = b * s
        ins = [y_tok, y_tok, x1, info, lg, lb]
        in_specs = [flat(D_MODEL // 2, 0), flat(D_MODEL // 2, n // tm), flat(D_MODEL, 0), flat(LANES, 0),
                    full((1, D_MODEL)), full((1, D_MODEL))]
        out_specs = [tok(D_MODEL)] + out_specs
        out_shape = [jax.ShapeDtypeStruct((b, s, D_MODEL), F32)] + out_shape
    else:
        ins = [x_or_parts]
        in_specs = [tok(D_MODEL)]
    return pl.pallas_call(
        functools.partial(_proj_kernel, fused=fused, alpha=alpha),
        grid=(b, nt),
        in_specs=in_specs + common_specs,
        out_specs=out_specs, out_shape=out_shape,
        scratch_shapes=[pltpu.VMEM((8, CONV_W), F32)],
        compiler_params=_params("arbitrary", "arbitrary"),
        name="proj",
    )(*ins, w_bf, cos_t, sin_t, perm, *mix_params)


def _band_bias(permuted):
    row = lax.broadcasted_iota(I32, (2 * BAND, BAND), 0)
    qry = lax.broadcasted_iota(I32, (2 * BAND, BAND), 1)
    key = row % BAND
    if permuted:
        qry = 4 * (qry % 32) + qry // 32
        key = 4 * (key % 32) + key // 32
    lo_key = jnp.where(row < BAND, qry, 0)
    hi_key = jnp.where(row < BAND, BAND - 1, qry)
    return jnp.where(jnp.logical_and(key >= lo_key, key <= hi_key), 0.0, NEG).astype(F32), row


def _attn_heads(get_q, get_k, get_v, bias, emit):
    lane_lo = lax.broadcasted_iota(I32, (BAND, LANES), 1) < HEAD_DIM
    lses = []
    for p in range(ATTN_W // LANES):
        ql, kk, vv = get_q(p), get_k(p), get_v(p)
        o_t = []
        for hh in range(2):
            qm = jnp.where(lane_lo if hh == 0 else jnp.logical_not(lane_lo), ql, jnp.zeros_like(ql))
            sc = lax.dot_general(kk, qm, (((1,), (1,)), ((), ())), preferred_element_type=F32)
            sc = sc + bias
            mx = jnp.max(sc, axis=0, keepdims=True)
            pe = jnp.exp(sc - mx)
            den = jnp.sum(pe, axis=0, keepdims=True)
            ot = lax.dot_general(vv, pe.astype(BF16), (((0,), (0,)), ((), ())),
                                 preferred_element_type=F32)
            o_t.append(ot / den)
            lses.append(mx + jnp.log(den))
        emit(p, jnp.concatenate([o_t[0][0:HEAD_DIM, :], o_t[1][HEAD_DIM:, :]], axis=0).T)
    return jnp.concatenate(lses + [jnp.zeros((BAND - N_HEADS, BAND), F32)], axis=0).T


def _three_terms(st):
    hi = st.astype(BF16).astype(F32)
    rest = st - hi
    mid = rest.astype(BF16).astype(F32)
    lo = (rest - mid).astype(BF16).astype(F32)
    return hi + pltpu.roll(mid, 8, 1) + pltpu.roll(lo, 16, 1)


def _attn16_kernel(q_ref, k_ref, v_ref, o_ref, st_ref, kbuf, vbuf, *, tq):
    i = pl.program_id(2)

    @pl.when(i == 0)
    def _():
        kbuf[0:BAND, :] = jnp.zeros((BAND, ATTN_W), BF16)
        vbuf[0:BAND, :] = jnp.zeros((BAND, ATTN_W), BF16)

    kbuf[BAND:BAND + tq, :] = k_ref[...]
    vbuf[BAND:BAND + tq, :] = v_ref[...]
    band_bias, col = _band_bias(False)

    def block(j, carry):
        r0 = pl.multiple_of(j * BAND, BAND)
        first_key = jnp.where(jnp.logical_or(j > 0, i > 0), 0, BAND)
        bias = jnp.where(col >= first_key, band_bias, NEG)

        def emit(p, o_pair):
            o_ref[pl.ds(r0, BAND), p * LANES:(p + 1) * LANES] = o_pair.astype(BF16)

        st = _attn_heads(lambda p: q_ref[pl.ds(r0, BAND), p * LANES:(p + 1) * LANES],
                         lambda p: kbuf[pl.ds(r0, 2 * BAND), p * LANES:(p + 1) * LANES],
                         lambda p: vbuf[pl.ds(r0, 2 * BAND), p * LANES:(p + 1) * LANES],
                         bias, emit)
        st_ref[pl.ds(r0, BAND), :] = _three_terms(st)
        return carry

    lax.fori_loop(0, tq // BAND, block, 0, unroll=True)
    kbuf[0:BAND, :] = kbuf[tq:tq + BAND, :]
    vbuf[0:BAND, :] = vbuf[tq:tq + BAND, :]


def _attn4_kernel(q_ref, k_ref, v_ref, o_ref, st_ref, qbuf, kbuf, vbuf, obuf, sbuf):
    i = pl.program_id(1)
    rows = TILE // PERM_D
    nres = 4

    @pl.when(i == 0)
    def _():
        kbuf[:, 0:BAND, :] = jnp.zeros((nres, BAND, ATTN_W), BF16)
        vbuf[:, 0:BAND, :] = jnp.zeros((nres, BAND, ATTN_W), BF16)

    @pl.when(i > 0)
    def _():
        kbuf[:, 0:BAND, :] = kbuf[:, BAND:2 * BAND, :]
        vbuf[:, 0:BAND, :] = vbuf[:, BAND:2 * BAND, :]

    for r4 in range(nres):
        for g in range(PERM_D // nres):
            lanes = slice((r4 + nres * g) * ATTN_W, (r4 + nres * g + 1) * ATTN_W)
            qbuf[r4, g * rows:(g + 1) * rows, :] = q_ref[:, lanes]
            kbuf[r4, BAND + g * rows:BAND + (g + 1) * rows, :] = k_ref[:, lanes]
            vbuf[r4, BAND + g * rows:BAND + (g + 1) * rows, :] = v_ref[:, lanes]

    band_bias, col = _band_bias(True)
    first_key = jnp.where(i > 0, 0, BAND)
    bias = jnp.where(col >= first_key, band_bias, NEG)

    def block(j, carry):
        def emit(p, o_pair):
            obuf[j, :, p * LANES:(p + 1) * LANES] = o_pair.astype(BF16)

        st = _attn_heads(lambda p: qbuf[j, :, p * LANES:(p + 1) * LANES],
                         lambda p: kbuf[j, :, p * LANES:(p + 1) * LANES],
                         lambda p: vbuf[j, :, p * LANES:(p + 1) * LANES],
                         bias, emit)
        sbuf[j] = _three_terms(st)
        return carry

    lax.fori_loop(0, nres, block, 0, unroll=True)

    for r4 in range(nres):
        for g in range(PERM_D // nres):
            grp = r4 + nres * g
            o_ref[:, grp * ATTN_W:(grp + 1) * ATTN_W] = obuf[r4, g * rows:(g + 1) * rows, :]
            st_ref[:, grp * LANES:(grp + 1) * LANES] = sbuf[r4, g * rows:(g + 1) * rows, :]


def _attn1_kernel(q_ref, k_ref, v_ref, o4_ref, s4_ref, o16_ref, s16_ref, pt_ref, g_ref, out_ref,
                  kbuf, vbuf, acc, o4t, o16t, stt, s1t, *, tq):
    i = pl.program_id(1)

    @pl.when(i == 0)
    def _():
        kbuf[0:BAND, :] = jnp.zeros((BAND, ATTN_W), BF16)
        vbuf[0:BAND, :] = jnp.zeros((BAND, ATTN_W), BF16)

    kbuf[BAND:BAND + tq, :] = k_ref[...]
    vbuf[BAND:BAND + tq, :] = v_ref[...]

    prow = PERM_TILE // PERM_D
    for h in range(tq // PERM_TILE):
        tok_rows = slice(h * PERM_TILE, (h + 1) * PERM_TILE)

        def rows_of(ref, w):
            return jnp.concatenate([ref[h * prow:(h + 1) * prow, r * w:(r + 1) * w] for r in range(PERM_D)], axis=0)

        o4t[tok_rows, :] = jnp.dot(pt_ref[...], rows_of(o4_ref, ATTN_W), preferred_element_type=F32).astype(BF16)
        o16t[tok_rows, :] = jnp.dot(pt_ref[...], rows_of(o16_ref, ATTN_W), preferred_element_type=F32).astype(BF16)
        terms = (rows_of(s4_ref, LANES) + pltpu.roll(rows_of(s16_ref, LANES), 32, 1)).astype(BF16)
        terms = jnp.dot(pt_ref[...], terms, preferred_element_type=F32)
        stt[tok_rows, :] = terms + pltpu.roll(terms, LANES - 8, 1) + pltpu.roll(terms, LANES - 16, 1)
    band_bias, col = _band_bias(False)

    def block(j, carry):
        r0 = pl.multiple_of(j * BAND, BAND)
        first_key = jnp.where(jnp.logical_or(j > 0, i > 0), 0, BAND)
        bias = jnp.where(col >= first_key, band_bias, NEG)

        def emit(p, o_pair):
            acc[pl.ds(r0, BAND), p * LANES:(p + 1) * LANES] = o_pair

        s1t[pl.ds(r0, BAND), :] = _attn_heads(
            lambda p: q_ref[pl.ds(r0, BAND), p * LANES:(p + 1) * LANES],
            lambda p: kbuf[pl.ds(r0, 2 * BAND), p * LANES:(p + 1) * LANES],
            lambda p: vbuf[pl.ds(r0, 2 * BAND), p * LANES:(p + 1) * LANES],
            bias, emit)
        return carry

    lax.fori_loop(0, tq // BAND, block, 0, unroll=True)
    kbuf[0:BAND, :] = kbuf[tq:tq + BAND, :]
    vbuf[0:BAND, :] = vbuf[tq:tq + BAND, :]

    head_of = lax.broadcasted_iota(I32, (LANES, ATTN_W), 1) // HEAD_DIM
    expand = jnp.where(lax.broadcasted_iota(I32, (LANES, ATTN_W), 0) == head_of, 1.0, 0.0).astype(BF16)

    def merge(j, carry):
        r0 = pl.multiple_of(j * BAND, BAND)
        is_head = lax.broadcasted_iota(I32, (BAND, LANES), 1) < N_HEADS
        l1 = s1t[pl.ds(r0, BAND), :]
        l4 = stt[pl.ds(r0, BAND), :]
        l16 = pltpu.roll(l4, LANES - 32, 1)
        top = jnp.maximum(l1, jnp.maximum(l4, l16))
        e1 = jnp.exp(l1 - top)
        e4 = jnp.exp(l4 - top)
        e16 = jnp.exp(l16 - top)
        inv = 1.0 / (e1 + e4 + e16)

        def spread(e):
            w = jnp.where(is_head, e * inv, 0.0)
            hi = w.astype(BF16)
            lo = (w - hi.astype(F32)).astype(BF16)
            return (jnp.dot(hi, expand, preferred_element_type=F32)
                    + jnp.dot(lo, expand, preferred_element_type=F32))

        y = (spread(e1) * acc[pl.ds(r0, BAND), :]
             + spread(e4) * o4t[pl.ds(r0, BAND), :].astype(F32)
             + spread(e16) * o16t[pl.ds(r0, BAND), :].astype(F32))
        scale = lax.rsqrt(jnp.mean(y * y, axis=1, keepdims=True) + EPS)
        out_ref[pl.ds(r0, BAND), :] = (y * scale * g_ref[...]).astype(BF16)
        return carry

    lax.fori_loop(0, tq // BAND, merge, 0, unroll=True)


def _attention(q, k, v, q16, k16, v16, perm_t, gain):
    b, s, _ = q.shape
    m16 = s // PERM_D
    rows = TILE // PERM_D
    o16_shape = [jax.ShapeDtypeStruct((b, m16, PERM_D * ATTN_W), BF16),
                 jax.ShapeDtypeStruct((b, m16, PERM_D * LANES), F32)]

    tq16 = min(ATTN_TQ, m16)
    blk16 = lambda w: pl.BlockSpec((None, tq16, w), lambda bi, ri, ii: (bi, ii, ri))
    o16, s16 = pl.pallas_call(
        functools.partial(_attn16_kernel, tq=tq16),
        grid=(b, PERM_D, m16 // tq16),
        in_specs=[blk16(ATTN_W)] * 3, out_specs=[blk16(ATTN_W), blk16(LANES)], out_shape=o16_shape,
        scratch_shapes=[pltpu.VMEM((tq16 + BAND, ATTN_W), BF16)] * 2,
        compiler_params=_params("arbitrary", "arbitrary", "arbitrary"),
        name="attn_d16",
    )(q16, k16, v16)

    tile = lambda w: pl.BlockSpec((None, rows, PERM_D * w), lambda bi, ii: (bi, ii, 0))
    o4, s4 = pl.pallas_call(
        _attn4_kernel,
        grid=(b, s // TILE),
        in_specs=[tile(ATTN_W)] * 3, out_specs=[tile(ATTN_W), tile(LANES)], out_shape=o16_shape,
        scratch_shapes=[pltpu.VMEM((4, BAND, ATTN_W), BF16), pltpu.VMEM((4, 2 * BAND, ATTN_W), BF16),
                        pltpu.VMEM((4, 2 * BAND, ATTN_W), BF16), pltpu.VMEM((4, BAND, ATTN_W), BF16),
                        pltpu.VMEM((4, BAND, LANES), F32)],
        compiler_params=_params("arbitrary", "arbitrary"),
        name="attn_d4",
    )(q16, k16, v16)

    tq = min(ATTN1_TQ, s)
    tok = lambda w: pl.BlockSpec((None, tq, w), lambda bi, ii: (bi, ii, 0))
    tile = lambda w: pl.BlockSpec((None, tq // PERM_D, PERM_D * w), lambda bi, ii: (bi, ii, 0))
    full = lambda shape: pl.BlockSpec(shape, lambda bi, ii: (0,) * len(shape))
    return pl.pallas_call(
        functools.partial(_attn1_kernel, tq=tq),
        grid=(b, s // tq),
        in_specs=[tok(ATTN_W)] * 3 + [tile(ATTN_W), tile(LANES), tile(ATTN_W), tile(LANES),
                                      full((PERM_TILE, PERM_TILE)), full((1, ATTN_W))],
        out_specs=tok(ATTN_W),
        out_shape=jax.ShapeDtypeStruct((b, s, ATTN_W), BF16),
        scratch_shapes=[pltpu.VMEM((tq + BAND, ATTN_W), BF16), pltpu.VMEM((tq + BAND, ATTN_W), BF16),
                        pltpu.VMEM((tq, ATTN_W), F32),
                        pltpu.VMEM((tq, ATTN_W), BF16), pltpu.VMEM((tq, ATTN_W), BF16),
                        pltpu.VMEM((tq, LANES), F32), pltpu.VMEM((tq, LANES), F32)],
        compiler_params=_params("arbitrary", "arbitrary"),
        name="attn_d1",
    )(q, k, v, o4, s4, o16, s16, perm_t, gain)


def _gelu_tanh(x):
    c = math.sqrt(2.0 / math.pi)
    return x * (0.5 * (1.0 + jnp.tanh(c * (x + 0.044715 * (x * x * x)))))


def _split_dot(x, m_bf):
    hi = x.astype(BF16)
    lo = (x - hi.astype(F32)).astype(BF16)
    return (jnp.dot(hi, m_bf, preferred_element_type=F32)
            + jnp.dot(lo, m_bf, preferred_element_type=F32))


def _layer_norm(t, g, b):
    mu = jnp.mean(t, axis=1, keepdims=True)
    tc = t - mu
    var = jnp.mean(tc * tc, axis=1, keepdims=True)
    return tc * lax.rsqrt(var + EPS) * g + b


def _outproj_kernel(ma_ref, mb_ref, x_ref, wa_ref, wb_ref, lg_ref, lb_ref, rw_ref, rb_ref,
                    x1_ref, xp_ref, info_ref, infot_ref, cnt_ref, run_ref, *, tm, alpha):
    i = pl.program_id(0)

    @pl.when(i == 0)
    def _():
        run_ref[...] = jnp.zeros_like(run_ref)

    y = (jnp.dot(ma_ref[...], wa_ref[...], preferred_element_type=F32)
         + jnp.dot(mb_ref[...], wb_ref[...], preferred_element_type=F32))
    x1 = _layer_norm(alpha * x_ref[...] + y, lg_ref[...], lb_ref[...])
    x1_ref[...] = x1
    xp_ref[...] = _pack_pairs(x1)

    hi = x1.astype(BF16)
    lo = (x1 - hi.astype(F32)).astype(BF16)
    both = jnp.dot(hi, rw_ref[...], preferred_element_type=F32)
    logit = (both[:, :LANES] + both[:, LANES:]
             + jnp.dot(lo, rw_ref[:, :LANES], preferred_element_type=F32)) + rb_ref[...]
    lane = lax.broadcasted_iota(I32, (tm, LANES), 1)
    lane_f = lane.astype(F32)

    def top(mask):
        v = jnp.max(jnp.where(mask, logit, NEG), axis=1, keepdims=True)
        first = jnp.min(jnp.where(jnp.logical_and(mask, logit == v), lane_f, float(LANES)),
                        axis=1, keepdims=True)
        return v, first.astype(I32)

    is_g = lane < MOE_GROUPS
    gmax, gidx = top(is_g)
    g_p = 1.0 / jnp.sum(jnp.where(is_g, jnp.exp(logit - gmax), 0.0), axis=1, keepdims=True)
    in_grp = jnp.logical_and(lane >= MOE_GROUPS + gidx * EXPERTS_PER_GROUP,
                             lane < MOE_GROUPS + (gidx + 1) * EXPERTS_PER_GROUP)
    v1, i1 = top(in_grp)
    v2, i2 = top(jnp.logical_and(in_grp, lane != i1))
    e21 = jnp.exp(v2 - v1)
    gate1 = g_p / (1.0 + e21)
    gate2 = g_p * e21 / (1.0 + e21)
    ex1 = i1 - MOE_GROUPS
    ex2 = i2 - MOE_GROUPS

    oh1 = lane == ex1
    oh2 = lane == ex2
    oh = (oh1.astype(F32) + oh2.astype(F32))
    tr = lax.broadcasted_iota(I32, (tm, tm), 0)
    tc = lax.broadcasted_iota(I32, (tm, tm), 1)
    lower = jnp.where(tc < tr, 1.0, 0.0).astype(BF16)
    before = jnp.dot(lower, oh.astype(BF16), preferred_element_type=F32) + run_ref[0:1, :]
    rank1 = jnp.sum(jnp.where(oh1, before, 0.0), axis=1, keepdims=True)
    rank2 = jnp.sum(jnp.where(oh2, before, 0.0), axis=1, keepdims=True)
    run_new = run_ref[0:1, :] + jnp.sum(oh, axis=0, keepdims=True)
    run_ref[...] = jnp.broadcast_to(run_new, run_ref.shape)
    cnt_ref[...] = jnp.broadcast_to(run_new, cnt_ref.shape)

    info = jnp.where(lane == 0, ex1.astype(F32), 0.0)
    info = jnp.where(lane == 1, ex2.astype(F32), info)
    info = jnp.where(lane == 2, gate1, info)
    info = jnp.where(lane == 3, gate2, info)
    info = jnp.where(lane == 4, rank1, info)
    info = jnp.where(lane == 5, rank2, info)
    info_ref[...] = info
    infot_ref[...] = info.T[0:8, :]


def _outproj(ma, mbc, x, wo_a, wo_b, ln_g, ln_b, rw_hilo, rbias, alpha):
    n = x.shape[0]
    tm = OUT_TM
    tok = lambda w: pl.BlockSpec((tm, w), lambda i: (i, 0))
    full = lambda shape: pl.BlockSpec(shape, lambda i: (0,) * len(shape))
    return pl.pallas_call(
        functools.partial(_outproj_kernel, tm=tm, alpha=alpha),
        grid=(n // tm,),
        in_specs=[tok(ATTN_W), tok(SGU_W + CONV_W), tok(D_MODEL),
                  full((ATTN_W, D_MODEL)), full((SGU_W + CONV_W, D_MODEL)),
                  full((1, D_MODEL)), full((1, D_MODEL)),
                  full((D_MODEL, 2 * LANES)), full((1, LANES))],
        out_specs=[tok(D_MODEL), tok(D_MODEL // 2), tok(LANES), pl.BlockSpec((8, tm), lambda i: (0, i)),
                   full((8, LANES))],
        out_shape=[jax.ShapeDtypeStruct((n, D_MODEL), F32),
                   jax.ShapeDtypeStruct((n, D_MODEL // 2), U32),
                   jax.ShapeDtypeStruct((n, LANES), F32),
                   jax.ShapeDtypeStruct((8, n), F32),
                   jax.ShapeDtypeStruct((8, LANES), F32)],
        scratch_shapes=[pltpu.VMEM((8, LANES), F32)],
        compiler_params=_params("arbitrary"),
        name="outproj",
    )(ma, mbc, x, wo_a, wo_b, ln_g, ln_b, rw_hilo, rbias)


def _sc_mesh():
    return plsc.VectorSubcoreMesh(core_axis_name="c", subcore_axis_name="s",
                                  num_cores=SC_CORES, num_subcores=SC_SUBCORES)


def _sc_chunk(rows_per_worker):
    chunk = min(SC_CHUNK, rows_per_worker // 2)
    assert rows_per_worker % (2 * chunk) == 0 and chunk % 8 == 0, rows_per_worker
    return chunk


def _sc_dispatch(xp, dest_kn, rows):
    n, w = xp.shape
    t_per_w = n // SC_WORKERS
    chunk = _sc_chunk(t_per_w)
    nchunk = t_per_w // chunk

    def body(src_hbm, dest_hbm, out_hbm, idx_v, rows_v, lsem, ssem):
        wid = lax.axis_index("s") * SC_CORES + lax.axis_index("c")
        base = wid * t_per_w
        pltpu.sync_copy(dest_hbm.at[0, wid], idx_v.at[0])
        pltpu.sync_copy(dest_hbm.at[1, wid], idx_v.at[1])

        def load(c, slot):
            return pltpu.make_async_copy(src_hbm.at[pl.ds(base + c * chunk, chunk)], rows_v.at[slot],
                                         lsem.at[slot])

        def put(c, slot, kk):
            return pltpu.make_async_copy(rows_v.at[slot], out_hbm.at[idx_v.at[kk, c]], ssem.at[slot])

        load(0, 0).start()

        @pl.loop(0, nchunk, step=2)
        def _(c):
            for b in range(2):
                cc = c + b
                load(cc, b).wait()

                @pl.when(cc + 1 < nchunk)
                def _():
                    @pl.when(cc >= 1)
                    def _():
                        put(cc - 1, 1 - b, 0).wait()
                        put(cc - 1, 1 - b, 1).wait()
                    load(cc + 1, 1 - b).start()

                put(cc, b, 0).start()
                put(cc, b, 1).start()

        for b in range(2):
            put(nchunk - 2 + b, b, 0).wait()
            put(nchunk - 2 + b, b, 1).wait()

    call = pl.kernel(
        body, mesh=_sc_mesh(),
        out_type=jax.ShapeDtypeStruct((rows, w), U32),
        scratch_types=[pltpu.VMEM((2, nchunk, chunk), I32), pltpu.VMEM((2, chunk, w), U32),
                       pltpu.SemaphoreType.DMA((2,)), pltpu.SemaphoreType.DMA((2,))],
        name="sc_dispatch")
    return call(xp, dest_kn.reshape(2, SC_WORKERS, nchunk, chunk))


def _sc_gather(table, idx):
    b = idx.shape[0]
    w = table.shape[1]
    b_per_w = b // SC_WORKERS
    chunk = _sc_chunk(b_per_w)
    nchunk = b_per_w // chunk

    def body(table_hbm, idx_hbm, out_hbm, idx_v, rows_v, gsem, osem):
        wid = lax.axis_index("s") * SC_CORES + lax.axis_index("c")
        base = wid * b_per_w
        pltpu.sync_copy(idx_hbm.at[wid], idx_v)

        def gather(c, slot):
            return pltpu.make_async_copy(table_hbm.at[idx_v.at[c]], rows_v.at[slot], gsem.at[slot])

        def put(c, slot):
            return pltpu.make_async_copy(rows_v.at[slot], out_hbm.at[pl.ds(base + c * chunk, chunk)],
                                         osem.at[slot])

        gather(0, 0).start()

        @pl.loop(0, nchunk, step=2)
        def _(c):
            for b in range(2):
                cc = c + b
                gather(cc, b).wait()

                @pl.when(cc + 1 < nchunk)
                def _():
                    @pl.when(cc >= 1)
                    def _():
                        put(cc - 1, 1 - b).wait()
                    gather(cc + 1, 1 - b).start()

                put(cc, b).start()

        put(nchunk - 2, 0).wait()
        put(nchunk - 1, 1).wait()

    call = pl.kernel(
        body, mesh=_sc_mesh(),
        out_type=jax.ShapeDtypeStruct((b, w), table.dtype),
        scratch_types=[pltpu.VMEM((nchunk, chunk), I32), pltpu.VMEM((2, chunk, w), table.dtype),
                       pltpu.SemaphoreType.DMA((2,)), pltpu.SemaphoreType.DMA((2,))],
        name="sc_gather")
    return call(table, idx.reshape(SC_WORKERS, nchunk, chunk))


def _expert_kernel(be_ref, nv_ref, nu_ref, xs_ref, wg_ref, wu_ref, wd_ref, y_ref, wgb, wub, wdb):
    i = pl.program_id(0)
    cur = jnp.minimum(i, nu_ref[0] - 1)
    new_expert = jnp.logical_or(i == 0, be_ref[cur] != be_ref[jnp.maximum(cur - 1, 0)])

    @pl.when(jnp.logical_and(i < nu_ref[0], new_expert))
    def _():
        wgb[...] = wg_ref[...].astype(BF16)
        wub[...] = wu_ref[...].astype(BF16)
        wdb[...] = wd_ref[...].astype(BF16)

    @pl.when(i < nu_ref[0])
    def _():
        row = lax.broadcasted_iota(I32, xs_ref.shape, 0)
        lo, hi = _unpack_pairs(jnp.where(row < nv_ref[i], xs_ref[...], jnp.uint32(0)))
        xb = jnp.concatenate([lo, hi], axis=1).astype(BF16)
        g = jnp.dot(xb, wgb[...], preferred_element_type=F32)
        u = jnp.dot(xb, wub[...], preferred_element_type=F32)
        hdn = (g * (1.0 / (1.0 + jnp.exp(-g))) * u).astype(BF16)
        y_ref[...] = _pack_pairs(jnp.dot(hdn, wdb[...], preferred_element_type=F32))

    @pl.when(i >= nu_ref[0])
    def _():
        y_ref[...] = jnp.zeros_like(y_ref)


def _experts(block_expert, block_valid, n_used, xs, wg, wu, wd, layer):
    rows, w = xs.shape
    tb = MOE_TB
    blk = lambda i, be, nv, nu: (jnp.minimum(i, nu[0] - 1), 0)
    oblk = lambda i, be, nv, nu: (i, 0)
    wsel = lambda i, be, nv, nu: (layer, be[jnp.minimum(i, nu[0] - 1)], 0, 0)
    return pl.pallas_call(
        _expert_kernel,
        grid_spec=pltpu.PrefetchScalarGridSpec(
            num_scalar_prefetch=3,
            grid=(rows // tb,),
            in_specs=[pl.BlockSpec((tb, w), blk),
                      pl.BlockSpec((None, None, D_MODEL, D_EXPERT), wsel),
                      pl.BlockSpec((None, None, D_MODEL, D_EXPERT), wsel),
                      pl.BlockSpec((None, None, D_EXPERT, D_MODEL), wsel)],
            out_specs=pl.BlockSpec((tb, w), oblk),
            scratch_shapes=[pltpu.VMEM((D_MODEL, D_EXPERT), BF16), pltpu.VMEM((D_MODEL, D_EXPERT), BF16),
                            pltpu.VMEM((D_EXPERT, D_MODEL), BF16)]),
        out_shape=jax.ShapeDtypeStruct((rows, w), U32),
        compiler_params=_params("arbitrary"),
        name="experts",
    )(block_expert, block_valid, n_used, xs, wg, wu, wd)


def _combine_kernel(ya_ref, yb_ref, x_ref, info_ref, lg_ref, lb_ref, out_ref, *, alpha):
    out_ref[...] = _combine_tile(ya_ref[...], yb_ref[...], x_ref[...], info_ref[...], lg_ref[...], lb_ref[...],
                                 alpha)


def _combine(y_tok, x1, info, ln_g, ln_b, alpha):
    n = x1.shape[0]
    tm = ROW_TM
    tok = lambda w: pl.BlockSpec((tm, w), lambda i: (i, 0))
    slot = lambda k: pl.BlockSpec((tm, D_MODEL // 2), lambda i: (i + k * (n // tm), 0))
    full = lambda shape: pl.BlockSpec(shape, lambda i: (0,) * len(shape))
    return pl.pallas_call(
        functools.partial(_combine_kernel, alpha=alpha),
        grid=(n // tm,),
        in_specs=[slot(0), slot(1), tok(D_MODEL), tok(LANES), full((1, D_MODEL)), full((1, D_MODEL))],
        out_specs=tok(D_MODEL),
        out_shape=jax.ShapeDtypeStruct((n, D_MODEL), F32),
        compiler_params=_params("arbitrary"),
        name="combine",
    )(y_tok, y_tok, x1, info, ln_g, ln_b)


def _rope_tables(s):
    half = HEAD_DIM // 2
    inv_freq = ROPE_THETA ** (-jnp.arange(half, dtype=F32) / half)
    ang = jnp.arange(s, dtype=F32)[:, None] * inv_freq[None, :]
    cos = jnp.cos(ang)
    sin = jnp.sin(ang)
    cos_t = jnp.tile(cos, (1, LANES // half))
    sin_t = jnp.tile(jnp.concatenate([-sin, sin], axis=1), (1, LANES // HEAD_DIM))
    return cos_t, sin_t


def _forward(x, w_in, w_out, branch_gain, sgu_gain, sgu_w, sgu_b, conv_w, ln_gain, ln_bias,
             router_group_w, router_group_b, router_expert_w, router_expert_b,
             expert_w_gate, expert_w_up, expert_w_down):
    b, s, d_model = x.shape
    depth = w_in.shape[0]
    n = b * s
    assert d_model == D_MODEL and w_in.shape[1:] == (D_MODEL, PROJ_W), (x.shape, w_in.shape)
    assert s % (PERM_D * BAND) == 0 and n % (2 * SC_WORKERS * 8) == 0, (b, s)
    alpha = (2.0 * depth) ** 0.25
    cos_t, sin_t = _rope_tables(s)
    perm, perm_t = _tile_perm()
    tb = MOE_TB
    n_blocks = (2 * n + N_EXPERTS * (tb - 1) + tb - 1) // tb
    rows = n_blocks * tb

    pending = None
    for l in range(depth):
        g = branch_gain[l]
        bias_tile = jnp.repeat(sgu_b[l].T, SGU_W // SGU_GROUPS, axis=1)
        mix_params = (sgu_gain[l][None, :], sgu_w[l], bias_tile, conv_w[l], g[None, ATTN_W:])
        outs = _proj(x if pending is None else pending, w_in[l].astype(BF16), cos_t, sin_t, perm, mix_params,
                     b, s, alpha)
        if pending is not None:
            x, outs = outs[0], outs[1:]
        q, k, v, mbc, q16, k16, v16 = outs
        ma = _attention(q, k, v, q16, k16, v16, perm_t, g[None, :ATTN_W])

        rw = jnp.zeros((D_MODEL, LANES), F32)
        rw = rw.at[:, :MOE_GROUPS].set(router_group_w[l])
        rw = rw.at[:, MOE_GROUPS:MOE_GROUPS + N_EXPERTS].set(router_expert_w[l])
        rw_hi = rw.astype(BF16)
        rw_hilo = jnp.concatenate([rw_hi, (rw - rw_hi.astype(F32)).astype(BF16)], axis=1)
        rbias = jnp.zeros((1, LANES), F32)
        rbias = rbias.at[0, :MOE_GROUPS].set(router_group_b[l])
        rbias = rbias.at[0, MOE_GROUPS:MOE_GROUPS + N_EXPERTS].set(router_expert_b[l])
        wo = w_out[l].astype(BF16)
        x1, xp, info, info_t, cnt = _outproj(
            ma.reshape(n, ATTN_W), mbc.reshape(n, SGU_W + CONV_W), x.reshape(n, D_MODEL),
            wo[:ATTN_W], wo[ATTN_W:], ln_gain[l, 0][None], ln_bias[l, 0][None], rw_hilo, rbias, alpha)

        counts = cnt[0, :N_EXPERTS].astype(I32)
        padded = (counts + tb - 1) // tb * tb
        pad_end = jnp.cumsum(padded)
        pad_start = pad_end - padded
        ex = info_t[0:2].astype(I32)
        start_of = jnp.zeros_like(ex)
        for e in range(N_EXPERTS):
            start_of = jnp.where(ex == e, pad_start[e], start_of)
        dest_kn = start_of + info_t[4:6].astype(I32)
        blk_row0 = jnp.arange(n_blocks, dtype=I32) * tb
        block_expert = jnp.minimum(jnp.sum((pad_end[None, :] <= blk_row0[:, None]).astype(I32), axis=1),
                                   N_EXPERTS - 1)
        block_valid = jnp.clip(pad_start[block_expert] + counts[block_expert] - blk_row0, 0, tb)
        n_used = (pad_end[-1:] // tb).astype(I32)

        xs = _sc_dispatch(xp, dest_kn, rows)
        ys = _experts(block_expert, block_valid, n_used, xs, expert_w_gate, expert_w_up, expert_w_down, l)
        y_tok = _sc_gather(ys, dest_kn.reshape(2 * n))
        pending = (y_tok, x1, info, ln_gain[l, 1][None], ln_bias[l, 1][None])
    return _combine(*pending, alpha).reshape(b, s, D_MODEL)


def kernel(x, w_in, w_out, branch_gain, sgu_gain, sgu_w, sgu_b, conv_w, ln_gain, ln_bias, router_group_w, router_group_b, router_expert_w, router_expert_b, expert_w_gate, expert_w_up, expert_w_down):
    return _forward(x, w_in, w_out, branch_gain, sgu_gain, sgu_w, sgu_b, conv_w, ln_gain, ln_bias,
                    router_group_w, router_group_b, router_expert_w, router_expert_b,
                    expert_w_gate, expert_w_up, expert_w_down)
```

```python
import functools
import math

import jax
import jax.numpy as jnp
from jax import lax
from jax.experimental import pallas as pl
from jax.experimental.pallas import tpu as pltpu
from jax.experimental.pallas import tpu_sc as plsc

F32 = jnp.float32
BF16 = jnp.bfloat16
U32 = jnp.uint32
I32 = jnp.int32

D_MODEL = 1024
HEAD_DIM = 64
ATTN_W = 512
N_HEADS = 8
SGU_W = 256
SGU_GROUPS = 4
SGU_CHUNK = 128
CONV_W = 256
REST_W = 2 * SGU_W + 3 * CONV_W
PROJ_W = 3 * ATTN_W + REST_W
DILATIONS = (16, 4, 1)
BAND = 128
ROPE_THETA = 10000.0
MOE_GROUPS = 4
EXPERTS_PER_GROUP = 8
N_EXPERTS = MOE_GROUPS * EXPERTS_PER_GROUP
D_EXPERT = 512
EPS = 1e-5
NEG = -1e30

LANES = 128
VMEM_LIMIT = 56 * 1024 * 1024

TILE = 512
PERM_D = 16
PERM_TILE = 256
ATTN_TQ = 512
ATTN1_TQ = 1024
OUT_TM = 1024
OUT_SUB = 512
MOE_TB = 1024
ROW_TM = 512
SC_CORES = 2
SC_SUBCORES = 16
SC_WORKERS = SC_CORES * SC_SUBCORES
SC_CHUNK = 64


def _params(*sem):
    return pltpu.CompilerParams(dimension_semantics=sem, vmem_limit_bytes=VMEM_LIMIT)


def _pack_pairs(x):
    w = x.shape[1] // 2
    lo = lax.bitcast_convert_type(x[:, :w].astype(BF16).astype(F32), U32)
    hi = lax.bitcast_convert_type(x[:, w:].astype(BF16).astype(F32), U32)
    return (lo >> 16) | (hi & jnp.uint32(0xFFFF0000))


def _unpack_pairs(p):
    lo = lax.bitcast_convert_type(p << 16, F32)
    hi = lax.bitcast_convert_type(p & jnp.uint32(0xFFFF0000), F32)
    return lo, hi


def _tile_perm():
    i = jnp.arange(PERM_TILE, dtype=I32)
    tok = PERM_D * (i % (PERM_TILE // PERM_D)) + i // (PERM_TILE // PERM_D)
    p = (jnp.arange(PERM_TILE, dtype=I32)[None, :] == tok[:, None]).astype(BF16)
    return p, p.T


def _mixers(rest, prev_zz, sg, sw_ref, sb, cw_ref, bg):
    tm = rest.shape[0]
    u = rest[:, 0:SGU_W]
    z = rest[:, SGU_W:2 * SGU_W]
    gb = rest[:, 2 * SGU_W:2 * SGU_W + CONV_W]
    gc = rest[:, 2 * SGU_W + CONV_W:2 * SGU_W + 2 * CONV_W]
    hh = rest[:, 2 * SGU_W + 2 * CONV_W:]

    gdim = SGU_W // SGU_GROUPS
    ri = lax.broadcasted_iota(I32, (SGU_W, SGU_W), 0) // gdim
    ci = lax.broadcasted_iota(I32, (SGU_W, SGU_W), 1) // gdim
    avg = jnp.where(ri == ci, 1.0 / gdim, 0.0).astype(BF16)
    z = _gelu_tanh(z)
    zc = z - _split_dot(z, avg)
    var = _split_dot(zc * zc, avg)
    zn = (zc * lax.rsqrt(var + EPS) * sg).astype(BF16)

    tr = lax.broadcasted_iota(I32, (SGU_CHUNK, SGU_CHUNK), 0)
    tc = lax.broadcasted_iota(I32, (SGU_CHUNK, SGU_CHUNK), 1)
    w_cat = jnp.concatenate(
        [jnp.where(tc <= tr, sw_ref[g], 0.0).astype(BF16) for g in range(SGU_GROUPS)], axis=1)
    lane_g = lax.broadcasted_iota(I32, (SGU_CHUNK, SGU_W), 1) // gdim
    gu = _gelu_tanh(u)
    yb = []
    for c in range(tm // SGU_CHUNK):
        zch = zn[c * SGU_CHUNK:(c + 1) * SGU_CHUNK, :]
        stack = jnp.concatenate(
            [jnp.where(lane_g == g, zch, jnp.zeros_like(zch)) for g in range(SGU_GROUPS)], axis=0)
        sp = jnp.dot(w_cat, stack, preferred_element_type=F32) + sb
        yb.append(gu[c * SGU_CHUNK:(c + 1) * SGU_CHUNK, :] * sp)
    yb = jnp.concatenate(yb, axis=0)

    zz = gc * hh
    hrows = prev_zz.shape[0]
    ext = jnp.concatenate([prev_zz, zz], axis=0)
    z1 = ext[hrows - 1:hrows - 1 + tm, :]
    z2 = ext[hrows - 2:hrows - 2 + tm, :]
    yc = gb * (cw_ref[0:1, :] * z2 + cw_ref[1:2, :] * z1 + cw_ref[2:3, :] * zz)

    def rms(t, g):
        return t * lax.rsqrt(jnp.mean(t * t, axis=1, keepdims=True) + EPS) * g

    out = jnp.concatenate([rms(yb, bg[:, 0:SGU_W]).astype(BF16), rms(yc, bg[:, SGU_W:]).astype(BF16)], axis=1)
    return out, zz[tm - hrows:, :]


def _combine_tile(ya, yb, x1, info, lg, lb, alpha):
    g1 = info[:, 2:3]
    g2 = info[:, 3:4]
    a_lo, a_hi = _unpack_pairs(ya)
    b_lo, b_hi = _unpack_pairs(yb)
    half = D_MODEL // 2
    t_lo = alpha * x1[:, 0:half] + g1 * a_lo + g2 * b_lo
    t_hi = alpha * x1[:, half:] + g1 * a_hi + g2 * b_hi
    return _layer_norm(jnp.concatenate([t_lo, t_hi], axis=1), lg, lb)


def _proj_kernel(*refs, fused, alpha):
    if fused:
        (ya_ref, yb_ref, x1_ref, info_ref, lg_ref, lb_ref, w_ref, cos_ref, sin_ref, p_ref,
         sg_ref, sw_ref, sb_ref, cw_ref, bg_ref,
         x_out, q_ref, k_ref, v_ref, m_ref, q16_ref, k16_ref, v16_ref, zz_prev) = refs
        x = _combine_tile(ya_ref[...], yb_ref[...], x1_ref[...], info_ref[...], lg_ref[...], lb_ref[...], alpha)
        x_out[...] = x
    else:
        (x_ref, w_ref, cos_ref, sin_ref, p_ref, sg_ref, sw_ref, sb_ref, cw_ref, bg_ref,
         q_ref, k_ref, v_ref, m_ref, q16_ref, k16_ref, v16_ref, zz_prev) = refs
        x = x_ref[...]
    si = pl.program_id(1)

    @pl.when(si == 0)
    def _():
        zz_prev[...] = jnp.zeros_like(zz_prev)

    xb = x.astype(BF16)
    cos = cos_ref[...]
    sin = sin_ref[...]
    lane = lax.broadcasted_iota(I32, cos.shape, 1)
    first_half = (lane % HEAD_DIM) < (HEAD_DIM // 2)
    rows = PERM_TILE // PERM_D

    def store_both(val_bf, out_ref, out16_ref):
        out_ref[...] = val_bf
        for h in range(TILE // PERM_TILE):
            perm = jnp.dot(p_ref[...], val_bf[h * PERM_TILE:(h + 1) * PERM_TILE, :],
                           preferred_element_type=F32).astype(BF16)
            for r in range(PERM_D):
                out16_ref[h * rows:(h + 1) * rows, r * ATTN_W:(r + 1) * ATTN_W] = perm[r * rows:(r + 1) * rows, :]

    def rope(col0, scale):
        t = jnp.dot(xb, w_ref[:, col0:col0 + ATTN_W], preferred_element_type=F32)
        out = []
        for c in range(ATTN_W // LANES):
            tc = t[:, c * LANES:(c + 1) * LANES]
            partner = jnp.where(first_half, pltpu.roll(tc, LANES - 32, 1), pltpu.roll(tc, 32, 1))
            out.append(((tc * cos + partner * sin) * scale).astype(BF16))
        return jnp.concatenate(out, axis=1)

    rest = jnp.dot(xb, w_ref[:, 3 * ATTN_W:], preferred_element_type=F32)
    tail = [zz_prev[...]]

    def mix_chunk(c):
        rows_c = slice(c * SGU_CHUNK, (c + 1) * SGU_CHUNK)
        out, tail[0] = _mixers(rest[rows_c, :], tail[0], sg_ref[...], sw_ref, sb_ref[...], cw_ref, bg_ref[...])
        m_ref[rows_c, :] = out

    qb = rope(0, HEAD_DIM ** -0.5)
    mix_chunk(0)
    store_both(qb, q_ref, q16_ref)
    kb = rope(ATTN_W, 1.0)
    mix_chunk(1)
    store_both(kb, k_ref, k16_ref)
    vb = jnp.dot(xb, w_ref[:, 2 * ATTN_W:3 * ATTN_W], preferred_element_type=F32).astype(BF16)
    mix_chunk(2)
    store_both(vb, v_ref, v16_ref)
    mix_chunk(3)
    zz_prev[...] = tail[0]


def _proj(x_or_parts, w_bf, cos_t, sin_t, perm, mix_params, b, s, alpha):
    tm = TILE
    nt = s // tm
    fused = isinstance(x_or_parts, tuple)
    out3 = jax.ShapeDtypeStruct((b, s, ATTN_W), BF16)
    out16 = jax.ShapeDtypeStruct((b, s // PERM_D, PERM_D * ATTN_W), BF16)
    tok = lambda width: pl.BlockSpec((None, tm, width), lambda bi, si: (bi, si, 0))
    tok16 = pl.BlockSpec((None, tm // PERM_D, PERM_D * ATTN_W), lambda bi, si: (bi, si, 0))
    full = lambda shape: pl.BlockSpec(shape, lambda bi, si: (0,) * len(shape))
    flat = lambda width, off: pl.BlockSpec((tm, width), lambda bi, si: (bi * nt + si + off, 0))
    common_specs = [full((D_MODEL, PROJ_W)),
                    pl.BlockSpec((tm, LANES), lambda bi, si: (si, 0)),
                    pl.BlockSpec((tm, LANES), lambda bi, si: (si, 0)),
                    full((PERM_TILE, PERM_TILE)),
                    full((1, SGU_W)), full((SGU_GROUPS, SGU_CHUNK, SGU_CHUNK)),
                    full((SGU_CHUNK, SGU_W)), full((3, CONV_W)), full((1, SGU_W + CONV_W))]
    out_specs = [tok(ATTN_W), tok(ATTN_W), tok(ATTN_W), tok(SGU_W + CONV_W), tok16, tok16, tok16]
    out_shape = [out3, out3, out3, jax.ShapeDtypeStruct((b, s, SGU_W + CONV_W), BF16), out16, out16, out16]
    if fused:
        y_tok, x1, info, lg, lb = x_or_parts
        n = b * s
        ins = [y_tok, y_tok, x1, info, lg, lb]
        in_specs = [flat(D_MODEL // 2, 0), flat(D_MODEL // 2, n // tm), flat(D_MODEL, 0), flat(LANES, 0),
                    full((1, D_MODEL)), full((1, D_MODEL))]
        out_specs = [tok(D_MODEL)] + out_specs
        out_shape = [jax.ShapeDtypeStruct((b, s, D_MODEL), F32)] + out_shape
    else:
        ins = [x_or_parts]
        in_specs = [tok(D_MODEL)]
    return pl.pallas_call(
        functools.partial(_proj_kernel, fused=fused, alpha=alpha),
        grid=(b, nt),
        in_specs=in_specs + common_specs,
        out_specs=out_specs, out_shape=out_shape,
        scratch_shapes=[pltpu.VMEM((8, CONV_W), F32)],
        compiler_params=_params("arbitrary", "arbitrary"),
        name="proj",
    )(*ins, w_bf, cos_t, sin_t, perm, *mix_params)


def _band_bias(permuted):
    row = lax.broadcasted_iota(I32, (2 * BAND, BAND), 0)
    qry = lax.broadcasted_iota(I32, (2 * BAND, BAND), 1)
    key = row % BAND
    if permuted:
        qry = 4 * (qry % 32) + qry // 32
        key = 4 * (key % 32) + key // 32
    lo_key = jnp.where(row < BAND, qry, 0)
    hi_key = jnp.where(row < BAND, BAND - 1, qry)
    return jnp.where(jnp.logical_and(key >= lo_key, key <= hi_key), 0.0, NEG).astype(F32), row


def _attn_heads(get_q, get_k, get_v, bias, emit):
    lane_lo = lax.broadcasted_iota(I32, (BAND, LANES), 1) < HEAD_DIM
    lses = []
    for p in range(ATTN_W // LANES):
        ql, kk, vv = get_q(p), get_k(p), get_v(p)
        o_t = []
        for hh in range(2):
            qm = jnp.where(lane_lo if hh == 0 else jnp.logical_not(lane_lo), ql, jnp.zeros_like(ql))
            sc = lax.dot_general(kk, qm, (((1,), (1,)), ((), ())), preferred_element_type=F32)
            sc = sc + bias
            mx = jnp.max(sc, axis=0, keepdims=True)
            pe = jnp.exp(sc - mx)
            den = jnp.sum(pe, axis=0, keepdims=True)
            ot = lax.dot_general(vv, pe.astype(BF16), (((0,), (0,)), ((), ())),
                                 preferred_element_type=F32)
            o_t.append(ot / den)
            lses.append(mx + jnp.log(den))
        emit(p, jnp.concatenate([o_t[0][0:HEAD_DIM, :], o_t[1][HEAD_DIM:, :]], axis=0).T)
    return jnp.concatenate(lses + [jnp.zeros((BAND - N_HEADS, BAND), F32)], axis=0).T


def _three_terms(st):
    hi = st.astype(BF16).astype(F32)
    rest = st - hi
    mid = rest.astype(BF16).astype(F32)
    lo = (rest - mid).astype(BF16).astype(F32)
    return hi + pltpu.roll(mid, 8, 1) + pltpu.roll(lo, 16, 1)


def _attn16_kernel(q_ref, k_ref, v_ref, o_ref, st_ref, kbuf, vbuf, *, tq):
    i = pl.program_id(2)

    @pl.when(i == 0)
    def _():
        kbuf[0:BAND, :] = jnp.zeros((BAND, ATTN_W), BF16)
        vbuf[0:BAND, :] = jnp.zeros((BAND, ATTN_W), BF16)

    kbuf[BAND:BAND + tq, :] = k_ref[...]
    vbuf[BAND:BAND + tq, :] = v_ref[...]
    band_bias, col = _band_bias(False)

    def block(j, carry):
        r0 = pl.multiple_of(j * BAND, BAND)
        first_key = jnp.where(jnp.logical_or(j > 0, i > 0), 0, BAND)
        bias = jnp.where(col >= first_key, band_bias, NEG)

        def emit(p, o_pair):
            o_ref[pl.ds(r0, BAND), p * LANES:(p + 1) * LANES] = o_pair.astype(BF16)

        st = _attn_heads(lambda p: q_ref[pl.ds(r0, BAND), p * LANES:(p + 1) * LANES],
                         lambda p: kbuf[pl.ds(r0, 2 * BAND), p * LANES:(p + 1) * LANES],
                         lambda p: vbuf[pl.ds(r0, 2 * BAND), p * LANES:(p + 1) * LANES],
                         bias, emit)
        st_ref[pl.ds(r0, BAND), :] = _three_terms(st)
        return carry

    lax.fori_loop(0, tq // BAND, block, 0, unroll=True)
    kbuf[0:BAND, :] = kbuf[tq:tq + BAND, :]
    vbuf[0:BAND, :] = vbuf[tq:tq + BAND, :]


def _attn4_kernel(q_ref, k_ref, v_ref, o_ref, st_ref, qbuf, kbuf, vbuf, obuf, sbuf):
    i = pl.program_id(1)
    rows = TILE // PERM_D
    nres = 4

    @pl.when(i == 0)
    def _():
        kbuf[:, 0:BAND, :] = jnp.zeros((nres, BAND, ATTN_W), BF16)
        vbuf[:, 0:BAND, :] = jnp.zeros((nres, BAND, ATTN_W), BF16)

    @pl.when(i > 0)
    def _():
        kbuf[:, 0:BAND, :] = kbuf[:, BAND:2 * BAND, :]
        vbuf[:, 0:BAND, :] = vbuf[:, BAND:2 * BAND, :]

    for r4 in range(nres):
        for g in range(PERM_D // nres):
            lanes = slice((r4 + nres * g) * ATTN_W, (r4 + nres * g + 1) * ATTN_W)
            qbuf[r4, g * rows:(g + 1) * rows, :] = q_ref[:, lanes]
            kbuf[r4, BAND + g * rows:BAND + (g + 1) * rows, :] = k_ref[:, lanes]
            vbuf[r4, BAND + g * rows:BAND + (g + 1) * rows, :] = v_ref[:, lanes]

    band_bias, col = _band_bias(True)
    first_key = jnp.where(i > 0, 0, BAND)
    bias = jnp.where(col >= first_key, band_bias, NEG)

    def block(j, carry):
        def emit(p, o_pair):
            obuf[j, :, p * LANES:(p + 1) * LANES] = o_pair.astype(BF16)

        st = _attn_heads(lambda p: qbuf[j, :, p * LANES:(p + 1) * LANES],
                         lambda p: kbuf[j, :, p * LANES:(p + 1) * LANES],
                         lambda p: vbuf[j, :, p * LANES:(p + 1) * LANES],
                         bias, emit)
        sbuf[j] = _three_terms(st)
        return carry

    lax.fori_loop(0, nres, block, 0, unroll=True)

    for r4 in range(nres):
        for g in range(PERM_D // nres):
            grp = r4 + nres * g
            o_ref[:, grp * ATTN_W:(grp + 1) * ATTN_W] = obuf[r4, g * rows:(g + 1) * rows, :]
            st_ref[:, grp * LANES:(grp + 1) * LANES] = sbuf[r4, g * rows:(g + 1) * rows, :]


def _attn1_kernel(q_ref, k_ref, v_ref, o4_ref, s4_ref, o16_ref, s16_ref, pt_ref, g_ref, out_ref,
                  kbuf, vbuf, acc, o4t, o16t, stt, s1t, *, tq):
    i = pl.program_id(1)

    @pl.when(i == 0)
    def _():
        kbuf[0:BAND, :] = jnp.zeros((BAND, ATTN_W), BF16)
        vbuf[0:BAND, :] = jnp.zeros((BAND, ATTN_W), BF16)

    kbuf[BAND:BAND + tq, :] = k_ref[...]
    vbuf[BAND:BAND + tq, :] = v_ref[...]

    prow = PERM_TILE // PERM_D
    for h in range(tq // PERM_TILE):
        tok_rows = slice(h * PERM_TILE, (h + 1) * PERM_TILE)

        def rows_of(ref, w):
            return jnp.concatenate([ref[h * prow:(h + 1) * prow, r * w:(r + 1) * w] for r in range(PERM_D)], axis=0)

        o4t[tok_rows, :] = jnp.dot(pt_ref[...], rows_of(o4_ref, ATTN_W), preferred_element_type=F32).astype(BF16)
        o16t[tok_rows, :] = jnp.dot(pt_ref[...], rows_of(o16_ref, ATTN_W), preferred_element_type=F32).astype(BF16)
        terms = (rows_of(s4_ref, LANES) + pltpu.roll(rows_of(s16_ref, LANES), 32, 1)).astype(BF16)
        terms = jnp.dot(pt_ref[...], terms, preferred_element_type=F32)
        stt[tok_rows, :] = terms + pltpu.roll(terms, LANES - 8, 1) + pltpu.roll(terms, LANES - 16, 1)
    band_bias, col = _band_bias(False)

    def block(j, carry):
        r0 = pl.multiple_of(j * BAND, BAND)
        first_key = jnp.where(jnp.logical_or(j > 0, i > 0), 0, BAND)
        bias = jnp.where(col >= first_key, band_bias, NEG)

        def emit(p, o_pair):
            acc[pl.ds(r0, BAND), p * LANES:(p + 1) * LANES] = o_pair

        s1t[pl.ds(r0, BAND), :] = _attn_heads(
            lambda p: q_ref[pl.ds(r0, BAND), p * LANES:(p + 1) * LANES],
            lambda p: kbuf[pl.ds(r0, 2 * BAND), p * LANES:(p + 1) * LANES],
            lambda p: vbuf[pl.ds(r0, 2 * BAND), p * LANES:(p + 1) * LANES],
            bias, emit)
        return carry

    lax.fori_loop(0, tq // BAND, block, 0, unroll=True)
    kbuf[0:BAND, :] = kbuf[tq:tq + BAND, :]
    vbuf[0:BAND, :] = vbuf[tq:tq + BAND, :]

    head_of = lax.broadcasted_iota(I32, (LANES, ATTN_W), 1) // HEAD_DIM
    expand = jnp.where(lax.broadcasted_iota(I32, (LANES, ATTN_W), 0) == head_of, 1.0, 0.0).astype(BF16)

    def merge(j, carry):
        r0 = pl.multiple_of(j * BAND, BAND)
        is_head = lax.broadcasted_iota(I32, (BAND, LANES), 1) < N_HEADS
        l1 = s1t[pl.ds(r0, BAND), :]
        l4 = stt[pl.ds(r0, BAND), :]
        l16 = pltpu.roll(l4, LANES - 32, 1)
        top = jnp.maximum(l1, jnp.maximum(l4, l16))
        e1 = jnp.exp(l1 - top)
        e4 = jnp.exp(l4 - top)
        e16 = jnp.exp(l16 - top)
        inv = 1.0 / (e1 + e4 + e16)

        def spread(e):
            w = jnp.where(is_head, e * inv, 0.0)
            hi = w.astype(BF16)
            lo = (w - hi.astype(F32)).astype(BF16)
            return (jnp.dot(hi, expand, preferred_element_type=F32)
                    + jnp.dot(lo, expand, preferred_element_type=F32))

        y = (spread(e1) * acc[pl.ds(r0, BAND), :]
             + spread(e4) * o4t[pl.ds(r0, BAND), :].astype(F32)
             + spread(e16) * o16t[pl.ds(r0, BAND), :].astype(F32))
        scale = lax.rsqrt(jnp.mean(y * y, axis=1, keepdims=True) + EPS)
        out_ref[pl.ds(r0, BAND), :] = (y * scale * g_ref[...]).astype(BF16)
        return carry

    lax.fori_loop(0, tq // BAND, merge, 0, unroll=True)


def _attention(q, k, v, q16, k16, v16, perm_t, gain):
    b, s, _ = q.shape
    m16 = s // PERM_D
    rows = TILE // PERM_D
    o16_shape = [jax.ShapeDtypeStruct((b, m16, PERM_D * ATTN_W), BF16),
                 jax.ShapeDtypeStruct((b, m16, PERM_D * LANES), F32)]

    tq16 = min(ATTN_TQ, m16)
    blk16 = lambda w: pl.BlockSpec((None, tq16, w), lambda bi, ri, ii: (bi, ii, ri))
    o16, s16 = pl.pallas_call(
        functools.partial(_attn16_kernel, tq=tq16),
        grid=(b, PERM_D, m16 // tq16),
        in_specs=[blk16(ATTN_W)] * 3, out_specs=[blk16(ATTN_W), blk16(LANES)], out_shape=o16_shape,
        scratch_shapes=[pltpu.VMEM((tq16 + BAND, ATTN_W), BF16)] * 2,
        compiler_params=_params("arbitrary", "arbitrary", "arbitrary"),
        name="attn_d16",
    )(q16, k16, v16)

    tile = lambda w: pl.BlockSpec((None, rows, PERM_D * w), lambda bi, ii: (bi, ii, 0))
    o4, s4 = pl.pallas_call(
        _attn4_kernel,
        grid=(b, s // TILE),
        in_specs=[tile(ATTN_W)] * 3, out_specs=[tile(ATTN_W), tile(LANES)], out_shape=o16_shape,
        scratch_shapes=[pltpu.VMEM((4, BAND, ATTN_W), BF16), pltpu.VMEM((4, 2 * BAND, ATTN_W), BF16),
                        pltpu.VMEM((4, 2 * BAND, ATTN_W), BF16), pltpu.VMEM((4, BAND, ATTN_W), BF16),
                        pltpu.VMEM((4, BAND, LANES), F32)],
        compiler_params=_params("arbitrary", "arbitrary"),
        name="attn_d4",
    )(q16, k16, v16)

    tq = min(ATTN1_TQ, s)
    tok = lambda w: pl.BlockSpec((None, tq, w), lambda bi, ii: (bi, ii, 0))
    tile = lambda w: pl.BlockSpec((None, tq // PERM_D, PERM_D * w), lambda bi, ii: (bi, ii, 0))
    full = lambda shape: pl.BlockSpec(shape, lambda bi, ii: (0,) * len(shape))
    return pl.pallas_call(
        functools.partial(_attn1_kernel, tq=tq),
        grid=(b, s // tq),
        in_specs=[tok(ATTN_W)] * 3 + [tile(ATTN_W), tile(LANES), tile(ATTN_W), tile(LANES),
                                      full((PERM_TILE, PERM_TILE)), full((1, ATTN_W))],
        out_specs=tok(ATTN_W),
        out_shape=jax.ShapeDtypeStruct((b, s, ATTN_W), BF16),
        scratch_shapes=[pltpu.VMEM((tq + BAND, ATTN_W), BF16), pltpu.VMEM((tq + BAND, ATTN_W), BF16),
                        pltpu.VMEM((tq, ATTN_W), F32),
                        pltpu.VMEM((tq, ATTN_W), BF16), pltpu.VMEM((tq, ATTN_W), BF16),
                        pltpu.VMEM((tq, LANES), F32), pltpu.VMEM((tq, LANES), F32)],
        compiler_params=_params("arbitrary", "arbitrary"),
        name="attn_d1",
    )(q, k, v, o4, s4, o16, s16, perm_t, gain)


def _gelu_tanh(x):
    c = math.sqrt(2.0 / math.pi)
    return x * (0.5 * (1.0 + jnp.tanh(c * (x + 0.044715 * (x * x * x)))))


def _split_dot(x, m_bf):
    hi = x.astype(BF16)
    lo = (x - hi.astype(F32)).astype(BF16)
    return (jnp.dot(hi, m_bf, preferred_element_type=F32)
            + jnp.dot(lo, m_bf, preferred_element_type=F32))


def _layer_norm(t, g, b):
    mu = jnp.mean(t, axis=1, keepdims=True)
    tc = t - mu
    var = jnp.mean(tc * tc, axis=1, keepdims=True)
    return tc * lax.rsqrt(var + EPS) * g + b


def _outproj_kernel(ma_ref, mb_ref, x_ref, wa_ref, wb_ref, lg_ref, lb_ref, rw_ref, rb_ref,
                    x1_ref, xp_ref, info_ref, infot_ref, cnt_ref, run_ref, *, tm, sub, alpha):
    @pl.when(pl.program_id(0) == 0)
    def _():
        run_ref[...] = jnp.zeros_like(run_ref)

    for h in range(tm // sub):
        rows = pl.ds(h * sub, sub)
        _outproj_tile(ma_ref.at[rows], mb_ref.at[rows], x_ref.at[rows], wa_ref, wb_ref, lg_ref, lb_ref,
                      rw_ref, rb_ref, x1_ref.at[rows], xp_ref.at[rows], info_ref.at[rows],
                      infot_ref.at[:, rows], cnt_ref, run_ref, tm=sub, alpha=alpha)


def _outproj_tile(ma_ref, mb_ref, x_ref, wa_ref, wb_ref, lg_ref, lb_ref, rw_ref, rb_ref,
                  x1_ref, xp_ref, info_ref, infot_ref, cnt_ref, run_ref, *, tm, alpha):
    y = (jnp.dot(ma_ref[...], wa_ref[...], preferred_element_type=F32)
         + jnp.dot(mb_ref[...], wb_ref[...], preferred_element_type=F32))
    x1 = _layer_norm(alpha * x_ref[...] + y, lg_ref[...], lb_ref[...])
    x1_ref[...] = x1
    xp_ref[...] = _pack_pairs(x1)

    hi = x1.astype(BF16)
    lo = (x1 - hi.astype(F32)).astype(BF16)
    both = jnp.dot(hi, rw_ref[...], preferred_element_type=F32)
    logit = (both[:, :LANES] + both[:, LANES:]
             + jnp.dot(lo, rw_ref[:, :LANES], preferred_element_type=F32)) + rb_ref[...]
    lane = lax.broadcasted_iota(I32, (tm, LANES), 1)
    lane_f = lane.astype(F32)

    def top(mask):
        v = jnp.max(jnp.where(mask, logit, NEG), axis=1, keepdims=True)
        first = jnp.min(jnp.where(jnp.logical_and(mask, logit == v), lane_f, float(LANES)),
                        axis=1, keepdims=True)
        return v, first.astype(I32)

    is_g = lane < MOE_GROUPS
    gmax, gidx = top(is_g)
    g_p = 1.0 / jnp.sum(jnp.where(is_g, jnp.exp(logit - gmax), 0.0), axis=1, keepdims=True)
    in_grp = jnp.logical_and(lane >= MOE_GROUPS + gidx * EXPERTS_PER_GROUP,
                             lane < MOE_GROUPS + (gidx + 1) * EXPERTS_PER_GROUP)
    v1, i1 = top(in_grp)
    v2, i2 = top(jnp.logical_and(in_grp, lane != i1))
    e21 = jnp.exp(v2 - v1)
    gate1 = g_p / (1.0 + e21)
    gate2 = g_p * e21 / (1.0 + e21)
    ex1 = i1 - MOE_GROUPS
    ex2 = i2 - MOE_GROUPS

    oh1 = lane == ex1
    oh2 = lane == ex2
    oh = (oh1.astype(F32) + oh2.astype(F32))
    tr = lax.broadcasted_iota(I32, (tm, tm), 0)
    tc = lax.broadcasted_iota(I32, (tm, tm), 1)
    lower = jnp.where(tc < tr, 1.0, 0.0).astype(BF16)
    before = jnp.dot(lower, oh.astype(BF16), preferred_element_type=F32) + run_ref[0:1, :]
    rank1 = jnp.sum(jnp.where(oh1, before, 0.0), axis=1, keepdims=True)
    rank2 = jnp.sum(jnp.where(oh2, before, 0.0), axis=1, keepdims=True)
    run_new = run_ref[0:1, :] + jnp.sum(oh, axis=0, keepdims=True)
    run_ref[...] = jnp.broadcast_to(run_new, run_ref.shape)
    cnt_ref[...] = jnp.broadcast_to(run_new, cnt_ref.shape)

    info = jnp.where(lane == 0, ex1.astype(F32), 0.0)
    info = jnp.where(lane == 1, ex2.astype(F32), info)
    info = jnp.where(lane == 2, gate1, info)
    info = jnp.where(lane == 3, gate2, info)
    info = jnp.where(lane == 4, rank1, info)
    info = jnp.where(lane == 5, rank2, info)
    info_ref[...] = info
    infot_ref[...] = info.T[0:8, :]


def _outproj(ma, mbc, x, wo_a, wo_b, ln_g, ln_b, rw_hilo, rbias, alpha):
    n = x.shape[0]
    tm = OUT_TM
    tok = lambda w: pl.BlockSpec((tm, w), lambda i: (i, 0))
    full = lambda shape: pl.BlockSpec(shape, lambda i: (0,) * len(shape))
    return pl.pallas_call(
        functools.partial(_outproj_kernel, tm=tm, sub=OUT_SUB, alpha=alpha),
        grid=(n // tm,),
        in_specs=[tok(ATTN_W), tok(SGU_W + CONV_W), tok(D_MODEL),
                  full((ATTN_W, D_MODEL)), full((SGU_W + CONV_W, D_MODEL)),
                  full((1, D_MODEL)), full((1, D_MODEL)),
                  full((D_MODEL, 2 * LANES)), full((1, LANES))],
        out_specs=[tok(D_MODEL), tok(D_MODEL // 2), tok(LANES), pl.BlockSpec((8, tm), lambda i: (0, i)),
                   full((8, LANES))],
        out_shape=[jax.ShapeDtypeStruct((n, D_MODEL), F32),
                   jax.ShapeDtypeStruct((n, D_MODEL // 2), U32),
                   jax.ShapeDtypeStruct((n, LANES), F32),
                   jax.ShapeDtypeStruct((8, n), F32),
                   jax.ShapeDtypeStruct((8, LANES), F32)],
        scratch_shapes=[pltpu.VMEM((8, LANES), F32)],
        compiler_params=_params("arbitrary"),
        name="outproj",
    )(ma, mbc, x, wo_a, wo_b, ln_g, ln_b, rw_hilo, rbias)


def _sc_mesh():
    return plsc.VectorSubcoreMesh(core_axis_name="c", subcore_axis_name="s",
                                  num_cores=SC_CORES, num_subcores=SC_SUBCORES)


def _sc_chunk(rows_per_worker):
    chunk = min(SC_CHUNK, rows_per_worker // 2)
    assert rows_per_worker % (2 * chunk) == 0 and chunk % 8 == 0, rows_per_worker
    return chunk


def _sc_dispatch(xp, dest_kn, rows):
    n, w = xp.shape
    t_per_w = n // SC_WORKERS
    chunk = _sc_chunk(t_per_w)
    nchunk = t_per_w // chunk

    def body(src_hbm, dest_hbm, out_hbm, idx_v, rows_v, lsem, ssem):
        wid = lax.axis_index("s") * SC_CORES + lax.axis_index("c")
        base = wid * t_per_w
        pltpu.sync_copy(dest_hbm.at[0, wid], idx_v.at[0])
        pltpu.sync_copy(dest_hbm.at[1, wid], idx_v.at[1])

        def load(c, slot):
            return pltpu.make_async_copy(src_hbm.at[pl.ds(base + c * chunk, chunk)], rows_v.at[slot],
                                         lsem.at[slot])

        def put(c, slot, kk):
            return pltpu.make_async_copy(rows_v.at[slot], out_hbm.at[idx_v.at[kk, c]], ssem.at[slot])

        load(0, 0).start()

        @pl.loop(0, nchunk, step=2)
        def _(c):
            for b in range(2):
                cc = c + b
                load(cc, b).wait()

                @pl.when(cc + 1 < nchunk)
                def _():
                    @pl.when(cc >= 1)
                    def _():
                        put(cc - 1, 1 - b, 0).wait()
                        put(cc - 1, 1 - b, 1).wait()
                    load(cc + 1, 1 - b).start()

                put(cc, b, 0).start()
                put(cc, b, 1).start()

        for b in range(2):
            put(nchunk - 2 + b, b, 0).wait()
            put(nchunk - 2 + b, b, 1).wait()

    call = pl.kernel(
        body, mesh=_sc_mesh(),
        out_type=jax.ShapeDtypeStruct((rows, w), U32),
        scratch_types=[pltpu.VMEM((2, nchunk, chunk), I32), pltpu.VMEM((2, chunk, w), U32),
                       pltpu.SemaphoreType.DMA((2,)), pltpu.SemaphoreType.DMA((2,))],
        name="sc_dispatch")
    return call(xp, dest_kn.reshape(2, SC_WORKERS, nchunk, chunk))


def _sc_gather(table, idx):
    b = idx.shape[0]
    w = table.shape[1]
    b_per_w = b // SC_WORKERS
    chunk = _sc_chunk(b_per_w)
    nchunk = b_per_w // chunk

    def body(table_hbm, idx_hbm, out_hbm, idx_v, rows_v, gsem, osem):
        wid = lax.axis_index("s") * SC_CORES + lax.axis_index("c")
        base = wid * b_per_w
        pltpu.sync_copy(idx_hbm.at[wid], idx_v)

        def gather(c, slot):
            return pltpu.make_async_copy(table_hbm.at[idx_v.at[c]], rows_v.at[slot], gsem.at[slot])

        def put(c, slot):
            return pltpu.make_async_copy(rows_v.at[slot], out_hbm.at[pl.ds(base + c * chunk, chunk)],
                                         osem.at[slot])

        gather(0, 0).start()

        @pl.loop(0, nchunk, step=2)
        def _(c):
            for b in range(2):
                cc = c + b
                gather(cc, b).wait()

                @pl.when(cc + 1 < nchunk)
                def _():
                    @pl.when(cc >= 1)
                    def _():
                        put(cc - 1, 1 - b).wait()
                    gather(cc + 1, 1 - b).start()

                put(cc, b).start()

        put(nchunk - 2, 0).wait()
        put(nchunk - 1, 1).wait()

    call = pl.kernel(
        body, mesh=_sc_mesh(),
        out_type=jax.ShapeDtypeStruct((b, w), table.dtype),
        scratch_types=[pltpu.VMEM((nchunk, chunk), I32), pltpu.VMEM((2, chunk, w), table.dtype),
                       pltpu.SemaphoreType.DMA((2,)), pltpu.SemaphoreType.DMA((2,))],
        name="sc_gather")
    return call(table, idx.reshape(SC_WORKERS, nchunk, chunk))


def _expert_kernel(be_ref, nv_ref, nu_ref, xs_ref, wg_ref, wu_ref, wd_ref, y_ref, wgb, wub, wdb):
    i = pl.program_id(0)
    cur = jnp.minimum(i, nu_ref[0] - 1)
    new_expert = jnp.logical_or(i == 0, be_ref[cur] != be_ref[jnp.maximum(cur - 1, 0)])

    @pl.when(jnp.logical_and(i < nu_ref[0], new_expert))
    def _():
        wgb[...] = wg_ref[...].astype(BF16)
        wub[...] = wu_ref[...].astype(BF16)
        wdb[...] = wd_ref[...].astype(BF16)

    @pl.when(i < nu_ref[0])
    def _():
        row = lax.broadcasted_iota(I32, xs_ref.shape, 0)
        lo, hi = _unpack_pairs(jnp.where(row < nv_ref[i], xs_ref[...], jnp.uint32(0)))
        xb = jnp.concatenate([lo, hi], axis=1).astype(BF16)
        g = jnp.dot(xb, wgb[...], preferred_element_type=F32)
        u = jnp.dot(xb, wub[...], preferred_element_type=F32)
        hdn = (g * (1.0 / (1.0 + jnp.exp(-g))) * u).astype(BF16)
        y_ref[...] = _pack_pairs(jnp.dot(hdn, wdb[...], preferred_element_type=F32))

    @pl.when(i >= nu_ref[0])
    def _():
        y_ref[...] = jnp.zeros_like(y_ref)


def _experts(block_expert, block_valid, n_used, xs, wg, wu, wd, layer):
    rows, w = xs.shape
    tb = MOE_TB
    blk = lambda i, be, nv, nu: (jnp.minimum(i, nu[0] - 1), 0)
    oblk = lambda i, be, nv, nu: (i, 0)
    wsel = lambda i, be, nv, nu: (layer, be[jnp.minimum(i, nu[0] - 1)], 0, 0)
    return pl.pallas_call(
        _expert_kernel,
        grid_spec=pltpu.PrefetchScalarGridSpec(
            num_scalar_prefetch=3,
            grid=(rows // tb,),
            in_specs=[pl.BlockSpec((tb, w), blk),
                      pl.BlockSpec((None, None, D_MODEL, D_EXPERT), wsel),
                      pl.BlockSpec((None, None, D_MODEL, D_EXPERT), wsel),
                      pl.BlockSpec((None, None, D_EXPERT, D_MODEL), wsel)],
            out_specs=pl.BlockSpec((tb, w), oblk),
            scratch_shapes=[pltpu.VMEM((D_MODEL, D_EXPERT), BF16), pltpu.VMEM((D_MODEL, D_EXPERT), BF16),
                            pltpu.VMEM((D_EXPERT, D_MODEL), BF16)]),
        out_shape=jax.ShapeDtypeStruct((rows, w), U32),
        compiler_params=_params("arbitrary"),
        name="experts",
    )(block_expert, block_valid, n_used, xs, wg, wu, wd)


def _combine_kernel(ya_ref, yb_ref, x_ref, info_ref, lg_ref, lb_ref, out_ref, *, alpha):
    out_ref[...] = _combine_tile(ya_ref[...], yb_ref[...], x_ref[...], info_ref[...], lg_ref[...], lb_ref[...],
                                 alpha)


def _combine(y_tok, x1, info, ln_g, ln_b, alpha):
    n = x1.shape[0]
    tm = ROW_TM
    tok = lambda w: pl.BlockSpec((tm, w), lambda i: (i, 0))
    slot = lambda k: pl.BlockSpec((tm, D_MODEL // 2), lambda i: (i + k * (n // tm), 0))
    full = lambda shape: pl.BlockSpec(shape, lambda i: (0,) * len(shape))
    return pl.pallas_call(
        functools.partial(_combine_kernel, alpha=alpha),
        grid=(n // tm,),
        in_specs=[slot(0), slot(1), tok(D_MODEL), tok(LANES), full((1, D_MODEL)), full((1, D_MODEL))],
        out_specs=tok(D_MODEL),
        out_shape=jax.ShapeDtypeStruct((n, D_MODEL), F32),
        compiler_params=_params("arbitrary"),
        name="combine",
    )(y_tok, y_tok, x1, info, ln_g, ln_b)


def _rope_tables(s):
    half = HEAD_DIM // 2
    inv_freq = ROPE_THETA ** (-jnp.arange(half, dtype=F32) / half)
    ang = jnp.arange(s, dtype=F32)[:, None] * inv_freq[None, :]
    cos = jnp.cos(ang)
    sin = jnp.sin(ang)
    cos_t = jnp.tile(cos, (1, LANES // half))
    sin_t = jnp.tile(jnp.concatenate([-sin, sin], axis=1), (1, LANES // HEAD_DIM))
    return cos_t, sin_t


def _forward(x, w_in, w_out, branch_gain, sgu_gain, sgu_w, sgu_b, conv_w, ln_gain, ln_bias,
             router_group_w, router_group_b, router_expert_w, router_expert_b,
             expert_w_gate, expert_w_up, expert_w_down):
    b, s, d_model = x.shape
    depth = w_in.shape[0]
    n = b * s
    assert d_model == D_MODEL and w_in.shape[1:] == (D_MODEL, PROJ_W), (x.shape, w_in.shape)
    assert s % (PERM_D * BAND) == 0 and n % (2 * SC_WORKERS * 8) == 0, (b, s)
    alpha = (2.0 * depth) ** 0.25
    cos_t, sin_t = _rope_tables(s)
    perm, perm_t = _tile_perm()
    tb = MOE_TB
    n_blocks = (2 * n + N_EXPERTS * (tb - 1) + tb - 1) // tb
    rows = n_blocks * tb

    pending = None
    for l in range(depth):
        g = branch_gain[l]
        bias_tile = jnp.repeat(sgu_b[l].T, SGU_W // SGU_GROUPS, axis=1)
        mix_params = (sgu_gain[l][None, :], sgu_w[l], bias_tile, conv_w[l], g[None, ATTN_W:])
        outs = _proj(x if pending is None else pending, w_in[l].astype(BF16), cos_t, sin_t, perm, mix_params,
                     b, s, alpha)
        if pending is not None:
            x, outs = outs[0], outs[1:]
        q, k, v, mbc, q16, k16, v16 = outs
        ma = _attention(q, k, v, q16, k16, v16, perm_t, g[None, :ATTN_W])

        rw = jnp.zeros((D_MODEL, LANES), F32)
        rw = rw.at[:, :MOE_GROUPS].set(router_group_w[l])
        rw = rw.at[:, MOE_GROUPS:MOE_GROUPS + N_EXPERTS].set(router_expert_w[l])
        rw_hi = rw.astype(BF16)
        rw_hilo = jnp.concatenate([rw_hi, (rw - rw_hi.astype(F32)).astype(BF16)], axis=1)
        rbias = jnp.zeros((1, LANES), F32)
        rbias = rbias.at[0, :MOE_GROUPS].set(router_group_b[l])
        rbias = rbias.at[0, MOE_GROUPS:MOE_GROUPS + N_EXPERTS].set(router_expert_b[l])
        wo = w_out[l].astype(BF16)
        x1, xp, info, info_t, cnt = _outproj(
            ma.reshape(n, ATTN_W), mbc.reshape(n, SGU_W + CONV_W), x.reshape(n, D_MODEL),
            wo[:ATTN_W], wo[ATTN_W:], ln_gain[l, 0][None], ln_bias[l, 0][None], rw_hilo, rbias, alpha)

        counts = cnt[0, :N_EXPERTS].astype(I32)
        padded = (counts + tb - 1) // tb * tb
        pad_end = jnp.cumsum(padded)
        pad_start = pad_end - padded
        ex = info_t[0:2].astype(I32)
        start_of = jnp.zeros_like(ex)
        for e in range(N_EXPERTS):
            start_of = jnp.where(ex == e, pad_start[e], start_of)
        dest_kn = start_of + info_t[4:6].astype(I32)
        blk_row0 = jnp.arange(n_blocks, dtype=I32) * tb
        block_expert = jnp.minimum(jnp.sum((pad_end[None, :] <= blk_row0[:, None]).astype(I32), axis=1),
                                   N_EXPERTS - 1)
        block_valid = jnp.clip(pad_start[block_expert] + counts[block_expert] - blk_row0, 0, tb)
        n_used = (pad_end[-1:] // tb).astype(I32)

        xs = _sc_dispatch(xp, dest_kn, rows)
        ys = _experts(block_expert, block_valid, n_used, xs, expert_w_gate, expert_w_up, expert_w_down, l)
        y_tok = _sc_gather(ys, dest_kn.reshape(2 * n))
        pending = (y_tok, x1, info, ln_gain[l, 1][None], ln_bias[l, 1][None])
    return _combine(*pending, alpha).reshape(b, s, D_MODEL)


def kernel(x, w_in, w_out, branch_gain, sgu_gain, sgu_w, sgu_b, conv_w, ln_gain, ln_bias, router_group_w, router_group_b, router_expert_w, router_expert_b, expert_w_gate, expert_w_up, expert_w_down):
    return _forward(x, w_in, w_out, branch_gain, sgu_gain, sgu_w, sgu_b, conv_w, ln_gain, ln_bias,
                    router_group_w, router_group_b, router_expert_w, router_expert_b,
                    expert_w_gate, expert_w_up, expert_w_down)
```

```python
import functools
import math

import jax
import jax.numpy as jnp
from jax import lax
from jax.experimental import pallas as pl
from jax.experimental.pallas import tpu as pltpu
from jax.experimental.pallas import tpu_sc as plsc

F32 = jnp.float32
BF16 = jnp.bfloat16
U32 = jnp.uint32
I32 = jnp.int32

D_MODEL = 1024
HEAD_DIM = 64
ATTN_W = 512
N_HEADS = 8
SGU_W = 256
SGU_GROUPS = 4
SGU_CHUNK = 128
CONV_W = 256
REST_W = 2 * SGU_W + 3 * CONV_W
PROJ_W = 3 * ATTN_W + REST_W
DILATIONS = (16, 4, 1)
BAND = 128
ROPE_THETA = 10000.0
MOE_GROUPS = 4
EXPERTS_PER_GROUP = 8
N_EXPERTS = MOE_GROUPS * EXPERTS_PER_GROUP
D_EXPERT = 512
EPS = 1e-5
NEG = -1e30

LANES = 128
VMEM_LIMIT = 56 * 1024 * 1024

TILE = 512
PERM_D = 16
PERM_TILE = 256
ATTN_TQ = 512
D16_GROUP = 2
ATTN1_TQ = 1024
OUT_TM = 1024
OUT_SUB = 512
MOE_TB = 1024
ROW_TM = 512
SC_CORES = 2
SC_SUBCORES = 16
SC_WORKERS = SC_CORES * SC_SUBCORES
SC_CHUNK = 64


def _params(*sem):
    return pltpu.CompilerParams(dimension_semantics=sem, vmem_limit_bytes=VMEM_LIMIT)


def _pack_pairs(x):
    w = x.shape[1] // 2
    lo = lax.bitcast_convert_type(x[:, :w].astype(BF16).astype(F32), U32)
    hi = lax.bitcast_convert_type(x[:, w:].astype(BF16).astype(F32), U32)
    return (lo >> 16) | (hi & jnp.uint32(0xFFFF0000))


def _unpack_pairs(p):
    lo = lax.bitcast_convert_type(p << 16, F32)
    hi = lax.bitcast_convert_type(p & jnp.uint32(0xFFFF0000), F32)
    return lo, hi


def _tile_perm():
    i = jnp.arange(PERM_TILE, dtype=I32)
    tok = PERM_D * (i % (PERM_TILE // PERM_D)) + i // (PERM_TILE // PERM_D)
    p = (jnp.arange(PERM_TILE, dtype=I32)[None, :] == tok[:, None]).astype(BF16)
    return p, p.T


def _mixers(rest, prev_zz, sg, sw_ref, sb, cw_ref, bg):
    tm = rest.shape[0]
    u = rest[:, 0:SGU_W]
    z = rest[:, SGU_W:2 * SGU_W]
    gb = rest[:, 2 * SGU_W:2 * SGU_W + CONV_W]
    gc = rest[:, 2 * SGU_W + CONV_W:2 * SGU_W + 2 * CONV_W]
    hh = rest[:, 2 * SGU_W + 2 * CONV_W:]

    gdim = SGU_W // SGU_GROUPS
    ri = lax.broadcasted_iota(I32, (SGU_W, SGU_W), 0) // gdim
    ci = lax.broadcasted_iota(I32, (SGU_W, SGU_W), 1) // gdim
    avg = jnp.where(ri == ci, 1.0 / gdim, 0.0).astype(BF16)
    z = _gelu_tanh(z)
    zc = z - _split_dot(z, avg)
    var = _split_dot(zc * zc, avg)
    zn = (zc * lax.rsqrt(var + EPS) * sg).astype(BF16)

    tr = lax.broadcasted_iota(I32, (SGU_CHUNK, SGU_CHUNK), 0)
    tc = lax.broadcasted_iota(I32, (SGU_CHUNK, SGU_CHUNK), 1)
    w_cat = jnp.concatenate(
        [jnp.where(tc <= tr, sw_ref[g], 0.0).astype(BF16) for g in range(SGU_GROUPS)], axis=1)
    lane_g = lax.broadcasted_iota(I32, (SGU_CHUNK, SGU_W), 1) // gdim
    gu = _gelu_tanh(u)
    yb = []
    for c in range(tm // SGU_CHUNK):
        zch = zn[c * SGU_CHUNK:(c + 1) * SGU_CHUNK, :]
        stack = jnp.concatenate(
            [jnp.where(lane_g == g, zch, jnp.zeros_like(zch)) for g in range(SGU_GROUPS)], axis=0)
        sp = jnp.dot(w_cat, stack, preferred_element_type=F32) + sb
        yb.append(gu[c * SGU_CHUNK:(c + 1) * SGU_CHUNK, :] * sp)
    yb = jnp.concatenate(yb, axis=0)

    zz = gc * hh
    hrows = prev_zz.shape[0]
    ext = jnp.concatenate([prev_zz, zz], axis=0)
    z1 = ext[hrows - 1:hrows - 1 + tm, :]
    z2 = ext[hrows - 2:hrows - 2 + tm, :]
    yc = gb * (cw_ref[0:1, :] * z2 + cw_ref[1:2, :] * z1 + cw_ref[2:3, :] * zz)

    def rms(t, g):
        return t * lax.rsqrt(jnp.mean(t * t, axis=1, keepdims=True) + EPS) * g

    out = jnp.concatenate([rms(yb, bg[:, 0:SGU_W]).astype(BF16), rms(yc, bg[:, SGU_W:]).astype(BF16)], axis=1)
    return out, zz[tm - hrows:, :]


def _combine_tile(ya, yb, x1, info, lg, lb, alpha):
    g1 = info[:, 2:3]
    g2 = info[:, 3:4]
    a_lo, a_hi = _unpack_pairs(ya)
    b_lo, b_hi = _unpack_pairs(yb)
    half = D_MODEL // 2
    t_lo = alpha * x1[:, 0:half] + g1 * a_lo + g2 * b_lo
    t_hi = alpha * x1[:, half:] + g1 * a_hi + g2 * b_hi
    return _layer_norm(jnp.concatenate([t_lo, t_hi], axis=1), lg, lb)


def _proj_kernel(*refs, fused, alpha):
    if fused:
        (ya_ref, yb_ref, x1_ref, info_ref, lg_ref, lb_ref, w_ref, cos_ref, sin_ref, p_ref,
         sg_ref, sw_ref, sb_ref, cw_ref, bg_ref,
         x_out, q_ref, k_ref, v_ref, m_ref, q16_ref, k16_ref, v16_ref, zz_prev) = refs
        x = _combine_tile(ya_ref[...], yb_ref[...], x1_ref[...], info_ref[...], lg_ref[...], lb_ref[...], alpha)
        x_out[...] = x
    else:
        (x_ref, w_ref, cos_ref, sin_ref, p_ref, sg_ref, sw_ref, sb_ref, cw_ref, bg_ref,
         q_ref, k_ref, v_ref, m_ref, q16_ref, k16_ref, v16_ref, zz_prev) = refs
        x = x_ref[...]
    si = pl.program_id(1)

    @pl.when(si == 0)
    def _():
        zz_prev[...] = jnp.zeros_like(zz_prev)

    xb = x.astype(BF16)
    cos = cos_ref[...]
    sin = sin_ref[...]
    lane = lax.broadcasted_iota(I32, cos.shape, 1)
    first_half = (lane % HEAD_DIM) < (HEAD_DIM // 2)
    rows = PERM_TILE // PERM_D

    def store_both(val_bf, out_ref, out16_ref):
        out_ref[...] = val_bf
        for h in range(TILE // PERM_TILE):
            perm = jnp.dot(p_ref[...], val_bf[h * PERM_TILE:(h + 1) * PERM_TILE, :],
                           preferred_element_type=F32).astype(BF16)
            for r in range(PERM_D):
                out16_ref[h * rows:(h + 1) * rows, r * ATTN_W:(r + 1) * ATTN_W] = perm[r * rows:(r + 1) * rows, :]

    def rope(col0, scale):
        t = jnp.dot(xb, w_ref[:, col0:col0 + ATTN_W], preferred_element_type=F32)
        out = []
        for c in range(ATTN_W // LANES):
            tc = t[:, c * LANES:(c + 1) * LANES]
            partner = jnp.where(first_half, pltpu.roll(tc, LANES - 32, 1), pltpu.roll(tc, 32, 1))
            out.append(((tc * cos + partner * sin) * scale).astype(BF16))
        return jnp.concatenate(out, axis=1)

    rest = jnp.dot(xb, w_ref[:, 3 * ATTN_W:], preferred_element_type=F32)
    tail = [zz_prev[...]]

    def mix_chunk(c):
        rows_c = slice(c * SGU_CHUNK, (c + 1) * SGU_CHUNK)
        out, tail[0] = _mixers(rest[rows_c, :], tail[0], sg_ref[...], sw_ref, sb_ref[...], cw_ref, bg_ref[...])
        m_ref[rows_c, :] = out

    qb = rope(0, HEAD_DIM ** -0.5)
    mix_chunk(0)
    store_both(qb, q_ref, q16_ref)
    kb = rope(ATTN_W, 1.0)
    mix_chunk(1)
    store_both(kb, k_ref, k16_ref)
    vb = jnp.dot(xb, w_ref[:, 2 * ATTN_W:3 * ATTN_W], preferred_element_type=F32).astype(BF16)
    mix_chunk(2)
    store_both(vb, v_ref, v16_ref)
    mix_chunk(3)
    zz_prev[...] = tail[0]


def _proj(x_or_parts, w_bf, cos_t, sin_t, perm, mix_params, b, s, alpha):
    tm = TILE
    nt = s // tm
    fused = isinstance(x_or_parts, tuple)
    out3 = jax.ShapeDtypeStruct((b, s, ATTN_W), BF16)
    out16 = jax.ShapeDtypeStruct((b, s // PERM_D, PERM_D * ATTN_W), BF16)
    tok = lambda width: pl.BlockSpec((None, tm, width), lambda bi, si: (bi, si, 0))
    tok16 = pl.BlockSpec((None, tm // PERM_D, PERM_D * ATTN_W), lambda bi, si: (bi, si, 0))
    full = lambda shape: pl.BlockSpec(shape, lambda bi, si: (0,) * len(shape))
    flat = lambda width, off: pl.BlockSpec((tm, width), lambda bi, si: (bi * nt + si + off, 0))
    common_specs = [full((D_MODEL, PROJ_W)),
                    pl.BlockSpec((tm, LANES), lambda bi, si: (si, 0)),
                    pl.BlockSpec((tm, LANES), lambda bi, si: (si, 0)),
                    full((PERM_TILE, PERM_TILE)),
                    full((1, SGU_W)), full((SGU_GROUPS, SGU_CHUNK, SGU_CHUNK)),
                    full((SGU_CHUNK, SGU_W)), full((3, CONV_W)), full((1, SGU_W + CONV_W))]
    out_specs = [tok(ATTN_W), tok(ATTN_W), tok(ATTN_W), tok(SGU_W + CONV_W), tok16, tok16, tok16]
    out_shape = [out3, out3, out3, jax.ShapeDtypeStruct((b, s, SGU_W + CONV_W), BF16), out16, out16, out16]
    if fused:
        y_tok, x1, info, lg, lb = x_or_parts
        n = b * s
        ins = [y_tok, y_tok, x1, info, lg, lb]
        in_specs = [flat(D_MODEL // 2, 0), flat(D_MODEL // 2, n // tm), flat(D_MODEL, 0), flat(LANES, 0),
                    full((1, D_MODEL)), full((1, D_MODEL))]
        out_specs = [tok(D_MODEL)] + out_specs
        out_shape = [jax.ShapeDtypeStruct((b, s, D_MODEL), F32)] + out_shape
    else:
        ins = [x_or_parts]
        in_specs = [tok(D_MODEL)]
    return pl.pallas_call(
        functools.partial(_proj_kernel, fused=fused, alpha=alpha),
        grid=(b, nt),
        in_specs=in_specs + common_specs,
        out_specs=out_specs, out_shape=out_shape,
        scratch_shapes=[pltpu.VMEM((8, CONV_W), F32)],
        compiler_params=_params("arbitrary", "arbitrary"),
        name="proj",
    )(*ins, w_bf, cos_t, sin_t, perm, *mix_params)


def _band_bias(permuted):
    row = lax.broadcasted_iota(I32, (2 * BAND, BAND), 0)
    qry = lax.broadcasted_iota(I32, (2 * BAND, BAND), 1)
    key = row % BAND
    if permuted:
        qry = 4 * (qry % 32) + qry // 32
        key = 4 * (key % 32) + key // 32
    lo_key = jnp.where(row < BAND, qry, 0)
    hi_key = jnp.where(row < BAND, BAND - 1, qry)
    return jnp.where(jnp.logical_and(key >= lo_key, key <= hi_key), 0.0, NEG).astype(F32), row


def _attn_heads(get_q, get_k, get_v, bias, emit):
    lane_lo = lax.broadcasted_iota(I32, (BAND, LANES), 1) < HEAD_DIM
    lses = []
    for p in range(ATTN_W // LANES):
        ql, kk, vv = get_q(p), get_k(p), get_v(p)
        o_t = []
        for hh in range(2):
            qm = jnp.where(lane_lo if hh == 0 else jnp.logical_not(lane_lo), ql, jnp.zeros_like(ql))
            sc = lax.dot_general(kk, qm, (((1,), (1,)), ((), ())), preferred_element_type=F32)
            sc = sc + bias
            mx = jnp.max(sc, axis=0, keepdims=True)
            pe = jnp.exp(sc - mx)
            den = jnp.sum(pe, axis=0, keepdims=True)
            ot = lax.dot_general(vv, pe.astype(BF16), (((0,), (0,)), ((), ())),
                                 preferred_element_type=F32)
            o_t.append(ot / den)
            lses.append(mx + jnp.log(den))
        emit(p, jnp.concatenate([o_t[0][0:HEAD_DIM, :], o_t[1][HEAD_DIM:, :]], axis=0).T)
    return jnp.concatenate(lses + [jnp.zeros((BAND - N_HEADS, BAND), F32)], axis=0).T


def _three_terms(st):
    hi = st.astype(BF16).astype(F32)
    rest = st - hi
    mid = rest.astype(BF16).astype(F32)
    lo = (rest - mid).astype(BF16).astype(F32)
    return hi + pltpu.roll(mid, 8, 1) + pltpu.roll(lo, 16, 1)


def _attn16_kernel(q_ref, k_ref, v_ref, o_ref, st_ref, kbuf, vbuf, *, tq):
    for g in range(D16_GROUP):
        wide = pl.ds(g * ATTN_W, ATTN_W)
        _attn16_tile(q_ref.at[:, wide], k_ref.at[:, wide], v_ref.at[:, wide], o_ref.at[:, wide],
                     st_ref.at[:, pl.ds(g * LANES, LANES)], kbuf.at[g], vbuf.at[g], tq=tq)


def _attn16_tile(q_ref, k_ref, v_ref, o_ref, st_ref, kbuf, vbuf, *, tq):
    i = pl.program_id(2)

    @pl.when(i == 0)
    def _():
        kbuf[0:BAND, :] = jnp.zeros((BAND, ATTN_W), BF16)
        vbuf[0:BAND, :] = jnp.zeros((BAND, ATTN_W), BF16)

    kbuf[BAND:BAND + tq, :] = k_ref[...]
    vbuf[BAND:BAND + tq, :] = v_ref[...]
    band_bias, col = _band_bias(False)

    def block(j, carry):
        r0 = pl.multiple_of(j * BAND, BAND)
        first_key = jnp.where(jnp.logical_or(j > 0, i > 0), 0, BAND)
        bias = jnp.where(col >= first_key, band_bias, NEG)

        def emit(p, o_pair):
            o_ref[pl.ds(r0, BAND), p * LANES:(p + 1) * LANES] = o_pair.astype(BF16)

        st = _attn_heads(lambda p: q_ref[pl.ds(r0, BAND), p * LANES:(p + 1) * LANES],
                         lambda p: kbuf[pl.ds(r0, 2 * BAND), p * LANES:(p + 1) * LANES],
                         lambda p: vbuf[pl.ds(r0, 2 * BAND), p * LANES:(p + 1) * LANES],
                         bias, emit)
        st_ref[pl.ds(r0, BAND), :] = _three_terms(st)
        return carry

    lax.fori_loop(0, tq // BAND, block, 0, unroll=True)
    kbuf[0:BAND, :] = kbuf[tq:tq + BAND, :]
    vbuf[0:BAND, :] = vbuf[tq:tq + BAND, :]


def _attn4_kernel(q_ref, k_ref, v_ref, o_ref, st_ref, qbuf, kbuf, vbuf, obuf, sbuf):
    i = pl.program_id(1)
    rows = TILE // PERM_D
    nres = 4

    @pl.when(i == 0)
    def _():
        kbuf[:, 0:BAND, :] = jnp.zeros((nres, BAND, ATTN_W), BF16)
        vbuf[:, 0:BAND, :] = jnp.zeros((nres, BAND, ATTN_W), BF16)

    @pl.when(i > 0)
    def _():
        kbuf[:, 0:BAND, :] = kbuf[:, BAND:2 * BAND, :]
        vbuf[:, 0:BAND, :] = vbuf[:, BAND:2 * BAND, :]

    for r4 in range(nres):
        for g in range(PERM_D // nres):
            lanes = slice((r4 + nres * g) * ATTN_W, (r4 + nres * g + 1) * ATTN_W)
            qbuf[r4, g * rows:(g + 1) * rows, :] = q_ref[:, lanes]
            kbuf[r4, BAND + g * rows:BAND + (g + 1) * rows, :] = k_ref[:, lanes]
            vbuf[r4, BAND + g * rows:BAND + (g + 1) * rows, :] = v_ref[:, lanes]

    band_bias, col = _band_bias(True)
    first_key = jnp.where(i > 0, 0, BAND)
    bias = jnp.where(col >= first_key, band_bias, NEG)

    def block(j, carry):
        def emit(p, o_pair):
            obuf[j, :, p * LANES:(p + 1) * LANES] = o_pair.astype(BF16)

        st = _attn_heads(lambda p: qbuf[j, :, p * LANES:(p + 1) * LANES],
                         lambda p: kbuf[j, :, p * LANES:(p + 1) * LANES],
                         lambda p: vbuf[j, :, p * LANES:(p + 1) * LANES],
                         bias, emit)
        sbuf[j] = _three_terms(st)
        return carry

    lax.fori_loop(0, nres, block, 0, unroll=True)

    for r4 in range(nres):
        for g in range(PERM_D // nres):
            grp = r4 + nres * g
            o_ref[:, grp * ATTN_W:(grp + 1) * ATTN_W] = obuf[r4, g * rows:(g + 1) * rows, :]
            st_ref[:, grp * LANES:(grp + 1) * LANES] = sbuf[r4, g * rows:(g + 1) * rows, :]


def _attn1_kernel(q_ref, k_ref, v_ref, o4_ref, s4_ref, o16_ref, s16_ref, pt_ref, g_ref, out_ref,
                  kbuf, vbuf, acc, o4t, o16t, stt, s1t, *, tq):
    i = pl.program_id(1)

    @pl.when(i == 0)
    def _():
        kbuf[0:BAND, :] = jnp.zeros((BAND, ATTN_W), BF16)
        vbuf[0:BAND, :] = jnp.zeros((BAND, ATTN_W), BF16)

    kbuf[BAND:BAND + tq, :] = k_ref[...]
    vbuf[BAND:BAND + tq, :] = v_ref[...]

    prow = PERM_TILE // PERM_D
    for h in range(tq // PERM_TILE):
        tok_rows = slice(h * PERM_TILE, (h + 1) * PERM_TILE)

        def rows_of(ref, w):
            return jnp.concatenate([ref[h * prow:(h + 1) * prow, r * w:(r + 1) * w] for r in range(PERM_D)], axis=0)

        o4t[tok_rows, :] = jnp.dot(pt_ref[...], rows_of(o4_ref, ATTN_W), preferred_element_type=F32).astype(BF16)
        o16t[tok_rows, :] = jnp.dot(pt_ref[...], rows_of(o16_ref, ATTN_W), preferred_element_type=F32).astype(BF16)
        terms = (rows_of(s4_ref, LANES) + pltpu.roll(rows_of(s16_ref, LANES), 32, 1)).astype(BF16)
        terms = jnp.dot(pt_ref[...], terms, preferred_element_type=F32)
        stt[tok_rows, :] = terms + pltpu.roll(terms, LANES - 8, 1) + pltpu.roll(terms, LANES - 16, 1)
    band_bias, col = _band_bias(False)

    def block(j, carry):
        r0 = pl.multiple_of(j * BAND, BAND)
        first_key = jnp.where(jnp.logical_or(j > 0, i > 0), 0, BAND)
        bias = jnp.where(col >= first_key, band_bias, NEG)

        def emit(p, o_pair):
            acc[pl.ds(r0, BAND), p * LANES:(p + 1) * LANES] = o_pair

        s1t[pl.ds(r0, BAND), :] = _attn_heads(
            lambda p: q_ref[pl.ds(r0, BAND), p * LANES:(p + 1) * LANES],
            lambda p: kbuf[pl.ds(r0, 2 * BAND), p * LANES:(p + 1) * LANES],
            lambda p: vbuf[pl.ds(r0, 2 * BAND), p * LANES:(p + 1) * LANES],
            bias, emit)
        return carry

    lax.fori_loop(0, tq // BAND, block, 0, unroll=True)
    kbuf[0:BAND, :] = kbuf[tq:tq + BAND, :]
    vbuf[0:BAND, :] = vbuf[tq:tq + BAND, :]

    head_of = lax.broadcasted_iota(I32, (LANES, ATTN_W), 1) // HEAD_DIM
    expand = jnp.where(lax.broadcasted_iota(I32, (LANES, ATTN_W), 0) == head_of, 1.0, 0.0).astype(BF16)

    def merge(j, carry):
        r0 = pl.multiple_of(j * BAND, BAND)
        is_head = lax.broadcasted_iota(I32, (BAND, LANES), 1) < N_HEADS
        l1 = s1t[pl.ds(r0, BAND), :]
        l4 = stt[pl.ds(r0, BAND), :]
        l16 = pltpu.roll(l4, LANES - 32, 1)
        top = jnp.maximum(l1, jnp.maximum(l4, l16))
        e1 = jnp.exp(l1 - top)
        e4 = jnp.exp(l4 - top)
        e16 = jnp.exp(l16 - top)
        inv = 1.0 / (e1 + e4 + e16)

        def spread(e):
            w = jnp.where(is_head, e * inv, 0.0)
            hi = w.astype(BF16)
            lo = (w - hi.astype(F32)).astype(BF16)
            return (jnp.dot(hi, expand, preferred_element_type=F32)
                    + jnp.dot(lo, expand, preferred_element_type=F32))

        y = (spread(e1) * acc[pl.ds(r0, BAND), :]
             + spread(e4) * o4t[pl.ds(r0, BAND), :].astype(F32)
             + spread(e16) * o16t[pl.ds(r0, BAND), :].astype(F32))
        scale = lax.rsqrt(jnp.mean(y * y, axis=1, keepdims=True) + EPS)
        out_ref[pl.ds(r0, BAND), :] = (y * scale * g_ref[...]).astype(BF16)
        return carry

    lax.fori_loop(0, tq // BAND, merge, 0, unroll=True)


def _attention(q, k, v, q16, k16, v16, perm_t, gain):
    b, s, _ = q.shape
    m16 = s // PERM_D
    rows = TILE // PERM_D
    o16_shape = [jax.ShapeDtypeStruct((b, m16, PERM_D * ATTN_W), BF16),
                 jax.ShapeDtypeStruct((b, m16, PERM_D * LANES), F32)]

    tq16 = min(ATTN_TQ, m16)
    blk16 = lambda w: pl.BlockSpec((None, tq16, D16_GROUP * w), lambda bi, ri, ii: (bi, ii, ri))
    o16, s16 = pl.pallas_call(
        functools.partial(_attn16_kernel, tq=tq16),
        grid=(b, PERM_D // D16_GROUP, m16 // tq16),
        in_specs=[blk16(ATTN_W)] * 3, out_specs=[blk16(ATTN_W), blk16(LANES)], out_shape=o16_shape,
        scratch_shapes=[pltpu.VMEM((D16_GROUP, tq16 + BAND, ATTN_W), BF16)] * 2,
        compiler_params=_params("arbitrary", "arbitrary", "arbitrary"),
        name="attn_d16",
    )(q16, k16, v16)

    tile = lambda w: pl.BlockSpec((None, rows, PERM_D * w), lambda bi, ii: (bi, ii, 0))
    o4, s4 = pl.pallas_call(
        _attn4_kernel,
        grid=(b, s // TILE),
        in_specs=[tile(ATTN_W)] * 3, out_specs=[tile(ATTN_W), tile(LANES)], out_shape=o16_shape,
        scratch_shapes=[pltpu.VMEM((4, BAND, ATTN_W), BF16), pltpu.VMEM((4, 2 * BAND, ATTN_W), BF16),
                        pltpu.VMEM((4, 2 * BAND, ATTN_W), BF16), pltpu.VMEM((4, BAND, ATTN_W), BF16),
                        pltpu.VMEM((4, BAND, LANES), F32)],
        compiler_params=_params("arbitrary", "arbitrary"),
        name="attn_d4",
    )(q16, k16, v16)

    tq = min(ATTN1_TQ, s)
    tok = lambda w: pl.BlockSpec((None, tq, w), lambda bi, ii: (bi, ii, 0))
    tile = lambda w: pl.BlockSpec((None, tq // PERM_D, PERM_D * w), lambda bi, ii: (bi, ii, 0))
    full = lambda shape: pl.BlockSpec(shape, lambda bi, ii: (0,) * len(shape))
    return pl.pallas_call(
        functools.partial(_attn1_kernel, tq=tq),
        grid=(b, s // tq),
        in_specs=[tok(ATTN_W)] * 3 + [tile(ATTN_W), tile(LANES), tile(ATTN_W), tile(LANES),
                                      full((PERM_TILE, PERM_TILE)), full((1, ATTN_W))],
        out_specs=tok(ATTN_W),
        out_shape=jax.ShapeDtypeStruct((b, s, ATTN_W), BF16),
        scratch_shapes=[pltpu.VMEM((tq + BAND, ATTN_W), BF16), pltpu.VMEM((tq + BAND, ATTN_W), BF16),
                        pltpu.VMEM((tq, ATTN_W), F32),
                        pltpu.VMEM((tq, ATTN_W), BF16), pltpu.VMEM((tq, ATTN_W), BF16),
                        pltpu.VMEM((tq, LANES), F32), pltpu.VMEM((tq, LANES), F32)],
        compiler_params=_params("arbitrary", "arbitrary"),
        name="attn_d1",
    )(q, k, v, o4, s4, o16, s16, perm_t, gain)


def _gelu_tanh(x):
    c = math.sqrt(2.0 / math.pi)
    return x * (0.5 * (1.0 + jnp.tanh(c * (x + 0.044715 * (x * x * x)))))


def _split_dot(x, m_bf):
    hi = x.astype(BF16)
    lo = (x - hi.astype(F32)).astype(BF16)
    return (jnp.dot(hi, m_bf, preferred_element_type=F32)
            + jnp.dot(lo, m_bf, preferred_element_type=F32))


def _layer_norm(t, g, b):
    mu = jnp.mean(t, axis=1, keepdims=True)
    tc = t - mu
    var = jnp.mean(tc * tc, axis=1, keepdims=True)
    return tc * lax.rsqrt(var + EPS) * g + b


def _outproj_kernel(ma_ref, mb_ref, x_ref, wa_ref, wb_ref, lg_ref, lb_ref, rw_ref, rb_ref,
                    x1_ref, xp_ref, info_ref, infot_ref, cnt_ref, run_ref, *, tm, sub, alpha):
    @pl.when(pl.program_id(0) == 0)
    def _():
        run_ref[...] = jnp.zeros_like(run_ref)

    for h in range(tm // sub):
        rows = pl.ds(h * sub, sub)
        _outproj_tile(ma_ref.at[rows], mb_ref.at[rows], x_ref.at[rows], wa_ref, wb_ref, lg_ref, lb_ref,
                      rw_ref, rb_ref, x1_ref.at[rows], xp_ref.at[rows], info_ref.at[rows],
                      infot_ref.at[:, rows], cnt_ref, run_ref, tm=sub, alpha=alpha)


def _outproj_tile(ma_ref, mb_ref, x_ref, wa_ref, wb_ref, lg_ref, lb_ref, rw_ref, rb_ref,
                  x1_ref, xp_ref, info_ref, infot_ref, cnt_ref, run_ref, *, tm, alpha):
    y = (jnp.dot(ma_ref[...], wa_ref[...], preferred_element_type=F32)
         + jnp.dot(mb_ref[...], wb_ref[...], preferred_element_type=F32))
    x1 = _layer_norm(alpha * x_ref[...] + y, lg_ref[...], lb_ref[...])
    x1_ref[...] = x1
    xp_ref[...] = _pack_pairs(x1)

    hi = x1.astype(BF16)
    lo = (x1 - hi.astype(F32)).astype(BF16)
    both = jnp.dot(hi, rw_ref[...], preferred_element_type=F32)
    logit = (both[:, :LANES] + both[:, LANES:]
             + jnp.dot(lo, rw_ref[:, :LANES], preferred_element_type=F32)) + rb_ref[...]
    lane = lax.broadcasted_iota(I32, (tm, LANES), 1)
    lane_f = lane.astype(F32)

    def top(mask):
        v = jnp.max(jnp.where(mask, logit, NEG), axis=1, keepdims=True)
        first = jnp.min(jnp.where(jnp.logical_and(mask, logit == v), lane_f, float(LANES)),
                        axis=1, keepdims=True)
        return v, first.astype(I32)

    is_g = lane < MOE_GROUPS
    gmax, gidx = top(is_g)
    g_p = 1.0 / jnp.sum(jnp.where(is_g, jnp.exp(logit - gmax), 0.0), axis=1, keepdims=True)
    in_grp = jnp.logical_and(lane >= MOE_GROUPS + gidx * EXPERTS_PER_GROUP,
                             lane < MOE_GROUPS + (gidx + 1) * EXPERTS_PER_GROUP)
    v1, i1 = top(in_grp)
    v2, i2 = top(jnp.logical_and(in_grp, lane != i1))
    e21 = jnp.exp(v2 - v1)
    gate1 = g_p / (1.0 + e21)
    gate2 = g_p * e21 / (1.0 + e21)
    ex1 = i1 - MOE_GROUPS
    ex2 = i2 - MOE_GROUPS

    oh1 = lane == ex1
    oh2 = lane == ex2
    oh = (oh1.astype(F32) + oh2.astype(F32))
    tr = lax.broadcasted_iota(I32, (tm, tm), 0)
    tc = lax.broadcasted_iota(I32, (tm, tm), 1)
    lower = jnp.where(tc < tr, 1.0, 0.0).astype(BF16)
    before = jnp.dot(lower, oh.astype(BF16), preferred_element_type=F32) + run_ref[0:1, :]
    rank1 = jnp.sum(jnp.where(oh1, before, 0.0), axis=1, keepdims=True)
    rank2 = jnp.sum(jnp.where(oh2, before, 0.0), axis=1, keepdims=True)
    run_new = run_ref[0:1, :] + jnp.sum(oh, axis=0, keepdims=True)
    run_ref[...] = jnp.broadcast_to(run_new, run_ref.shape)
    cnt_ref[...] = jnp.broadcast_to(run_new, cnt_ref.shape)

    info = jnp.where(lane == 0, ex1.astype(F32), 0.0)
    info = jnp.where(lane == 1, ex2.astype(F32), info)
    info = jnp.where(lane == 2, gate1, info)
    info = jnp.where(lane == 3, gate2, info)
    info = jnp.where(lane == 4, rank1, info)
    info = jnp.where(lane == 5, rank2, info)
    info_ref[...] = info
    infot_ref[...] = info.T[0:8, :]


def _outproj(ma, mbc, x, wo_a, wo_b, ln_g, ln_b, rw_hilo, rbias, alpha):
    n = x.shape[0]
    tm = OUT_TM
    tok = lambda w: pl.BlockSpec((tm, w), lambda i: (i, 0))
    full = lambda shape: pl.BlockSpec(shape, lambda i: (0,) * len(shape))
    return pl.pallas_call(
        functools.partial(_outproj_kernel, tm=tm, sub=OUT_SUB, alpha=alpha),
        grid=(n // tm,),
        in_specs=[tok(ATTN_W), tok(SGU_W + CONV_W), tok(D_MODEL),
                  full((ATTN_W, D_MODEL)), full((SGU_W + CONV_W, D_MODEL)),
                  full((1, D_MODEL)), full((1, D_MODEL)),
                  full((D_MODEL, 2 * LANES)), full((1, LANES))],
        out_specs=[tok(D_MODEL), tok(D_MODEL // 2), tok(LANES), pl.BlockSpec((8, tm), lambda i: (0, i)),
                   full((8, LANES))],
        out_shape=[jax.ShapeDtypeStruct((n, D_MODEL), F32),
                   jax.ShapeDtypeStruct((n, D_MODEL // 2), U32),
                   jax.ShapeDtypeStruct((n, LANES), F32),
                   jax.ShapeDtypeStruct((8, n), F32),
                   jax.ShapeDtypeStruct((8, LANES), F32)],
        scratch_shapes=[pltpu.VMEM((8, LANES), F32)],
        compiler_params=_params("arbitrary"),
        name="outproj",
    )(ma, mbc, x, wo_a, wo_b, ln_g, ln_b, rw_hilo, rbias)


def _sc_mesh():
    return plsc.VectorSubcoreMesh(core_axis_name="c", subcore_axis_name="s",
                                  num_cores=SC_CORES, num_subcores=SC_SUBCORES)


def _sc_chunk(rows_per_worker):
    chunk = min(SC_CHUNK, rows_per_worker // 2)
    assert rows_per_worker % (2 * chunk) == 0 and chunk % 8 == 0, rows_per_worker
    return chunk


def _sc_dispatch(xp, dest_kn, rows):
    n, w = xp.shape
    t_per_w = n // SC_WORKERS
    chunk = _sc_chunk(t_per_w)
    nchunk = t_per_w // chunk

    def body(src_hbm, dest_hbm, out_hbm, idx_v, rows_v, lsem, ssem):
        wid = lax.axis_index("s") * SC_CORES + lax.axis_index("c")
        base = wid * t_per_w
        pltpu.sync_copy(dest_hbm.at[0, wid], idx_v.at[0])
        pltpu.sync_copy(dest_hbm.at[1, wid], idx_v.at[1])

        def load(c, slot):
            return pltpu.make_async_copy(src_hbm.at[pl.ds(base + c * chunk, chunk)], rows_v.at[slot],
                                         lsem.at[slot])

        def put(c, slot, kk):
            return pltpu.make_async_copy(rows_v.at[slot], out_hbm.at[idx_v.at[kk, c]], ssem.at[slot])

        load(0, 0).start()

        @pl.loop(0, nchunk, step=2)
        def _(c):
            for b in range(2):
                cc = c + b
                load(cc, b).wait()

                @pl.when(cc + 1 < nchunk)
                def _():
                    @pl.when(cc >= 1)
                    def _():
                        put(cc - 1, 1 - b, 0).wait()
                        put(cc - 1, 1 - b, 1).wait()
                    load(cc + 1, 1 - b).start()

                put(cc, b, 0).start()
                put(cc, b, 1).start()

        for b in range(2):
            put(nchunk - 2 + b, b, 0).wait()
            put(nchunk - 2 + b, b, 1).wait()

    call = pl.kernel(
        body, mesh=_sc_mesh(),
        out_type=jax.ShapeDtypeStruct((rows, w), U32),
        scratch_types=[pltpu.VMEM((2, nchunk, chunk), I32), pltpu.VMEM((2, chunk, w), U32),
                       pltpu.SemaphoreType.DMA((2,)), pltpu.SemaphoreType.DMA((2,))],
        name="sc_dispatch")
    return call(xp, dest_kn.reshape(2, SC_WORKERS, nchunk, chunk))


def _sc_gather(table, idx):
    b = idx.shape[0]
    w = table.shape[1]
    b_per_w = b // SC_WORKERS
    chunk = _sc_chunk(b_per_w)
    nchunk = b_per_w // chunk

    def body(table_hbm, idx_hbm, out_hbm, idx_v, rows_v, gsem, osem):
        wid = lax.axis_index("s") * SC_CORES + lax.axis_index("c")
        base = wid * b_per_w
        pltpu.sync_copy(idx_hbm.at[wid], idx_v)

        def gather(c, slot):
            return pltpu.make_async_copy(table_hbm.at[idx_v.at[c]], rows_v.at[slot], gsem.at[slot])

        def put(c, slot):
            return pltpu.make_async_copy(rows_v.at[slot], out_hbm.at[pl.ds(base + c * chunk, chunk)],
                                         osem.at[slot])

        gather(0, 0).start()

        @pl.loop(0, nchunk, step=2)
        def _(c):
            for b in range(2):
                cc = c + b
                gather(cc, b).wait()

                @pl.when(cc + 1 < nchunk)
                def _():
                    @pl.when(cc >= 1)
                    def _():
                        put(cc - 1, 1 - b).wait()
                    gather(cc + 1, 1 - b).start()

                put(cc, b).start()

        put(nchunk - 2, 0).wait()
        put(nchunk - 1, 1).wait()

    call = pl.kernel(
        body, mesh=_sc_mesh(),
        out_type=jax.ShapeDtypeStruct((b, w), table.dtype),
        scratch_types=[pltpu.VMEM((nchunk, chunk), I32), pltpu.VMEM((2, chunk, w), table.dtype),
                       pltpu.SemaphoreType.DMA((2,)), pltpu.SemaphoreType.DMA((2,))],
        name="sc_gather")
    return call(table, idx.reshape(SC_WORKERS, nchunk, chunk))


def _expert_kernel(be_ref, nv_ref, nu_ref, xs_ref, wg_ref, wu_ref, wd_ref, y_ref, wgb, wub, wdb):
    i = pl.program_id(0)
    cur = jnp.minimum(i, nu_ref[0] - 1)
    new_expert = jnp.logical_or(i == 0, be_ref[cur] != be_ref[jnp.maximum(cur - 1, 0)])

    @pl.when(jnp.logical_and(i < nu_ref[0], new_expert))
    def _():
        wgb[...] = wg_ref[...].astype(BF16)
        wub[...] = wu_ref[...].astype(BF16)
        wdb[...] = wd_ref[...].astype(BF16)

    @pl.when(i < nu_ref[0])
    def _():
        row = lax.broadcasted_iota(I32, xs_ref.shape, 0)
        lo, hi = _unpack_pairs(jnp.where(row < nv_ref[i], xs_ref[...], jnp.uint32(0)))
        xb = jnp.concatenate([lo, hi], axis=1).astype(BF16)
        g = jnp.dot(xb, wgb[...], preferred_element_type=F32)
        u = jnp.dot(xb, wub[...], preferred_element_type=F32)
        hdn = (g * (1.0 / (1.0 + jnp.exp(-g))) * u).astype(BF16)
        y_ref[...] = _pack_pairs(jnp.dot(hdn, wdb[...], preferred_element_type=F32))

    @pl.when(i >= nu_ref[0])
    def _():
        y_ref[...] = jnp.zeros_like(y_ref)


def _experts(block_expert, block_valid, n_used, xs, wg, wu, wd, layer):
    rows, w = xs.shape
    tb = MOE_TB
    blk = lambda i, be, nv, nu: (jnp.minimum(i, nu[0] - 1), 0)
    oblk = lambda i, be, nv, nu: (i, 0)
    wsel = lambda i, be, nv, nu: (layer, be[jnp.minimum(i, nu[0] - 1)], 0, 0)
    return pl.pallas_call(
        _expert_kernel,
        grid_spec=pltpu.PrefetchScalarGridSpec(
            num_scalar_prefetch=3,
            grid=(rows // tb,),
            in_specs=[pl.BlockSpec((tb, w), blk),
                      pl.BlockSpec((None, None, D_MODEL, D_EXPERT), wsel),
                      pl.BlockSpec((None, None, D_MODEL, D_EXPERT), wsel),
                      pl.BlockSpec((None, None, D_EXPERT, D_MODEL), wsel)],
            out_specs=pl.BlockSpec((tb, w), oblk),
            scratch_shapes=[pltpu.VMEM((D_MODEL, D_EXPERT), BF16), pltpu.VMEM((D_MODEL, D_EXPERT), BF16),
                            pltpu.VMEM((D_EXPERT, D_MODEL), BF16)]),
        out_shape=jax.ShapeDtypeStruct((rows, w), U32),
        compiler_params=_params("arbitrary"),
        name="experts",
    )(block_expert, block_valid, n_used, xs, wg, wu, wd)


def _combine_kernel(ya_ref, yb_ref, x_ref, info_ref, lg_ref, lb_ref, out_ref, *, alpha):
    out_ref[...] = _combine_tile(ya_ref[...], yb_ref[...], x_ref[...], info_ref[...], lg_ref[...], lb_ref[...],
                                 alpha)


def _combine(y_tok, x1, info, ln_g, ln_b, alpha):
    n = x1.shape[0]
    tm = ROW_TM
    tok = lambda w: pl.BlockSpec((tm, w), lambda i: (i, 0))
    slot = lambda k: pl.BlockSpec((tm, D_MODEL // 2), lambda i: (i + k * (n // tm), 0))
    full = lambda shape: pl.BlockSpec(shape, lambda i: (0,) * len(shape))
    return pl.pallas_call(
        functools.partial(_combine_kernel, alpha=alpha),
        grid=(n // tm,),
        in_specs=[slot(0), slot(1), tok(D_MODEL), tok(LANES), full((1, D_MODEL)), full((1, D_MODEL))],
        out_specs=tok(D_MODEL),
        out_shape=jax.ShapeDtypeStruct((n, D_MODEL), F32),
        compiler_params=_params("arbitrary"),
        name="combine",
    )(y_tok, y_tok, x1, info, ln_g, ln_b)


def _rope_tables(s):
    half = HEAD_DIM // 2
    inv_freq = ROPE_THETA ** (-jnp.arange(half, dtype=F32) / half)
    ang = jnp.arange(s, dtype=F32)[:, None] * inv_freq[None, :]
    cos = jnp.cos(ang)
    sin = jnp.sin(ang)
    cos_t = jnp.tile(cos, (1, LANES // half))
    sin_t = jnp.tile(jnp.concatenate([-sin, sin], axis=1), (1, LANES // HEAD_DIM))
    return cos_t, sin_t


def _forward(x, w_in, w_out, branch_gain, sgu_gain, sgu_w, sgu_b, conv_w, ln_gain, ln_bias,
             router_group_w, router_group_b, router_expert_w, router_expert_b,
             expert_w_gate, expert_w_up, expert_w_down):
    b, s, d_model = x.shape
    depth = w_in.shape[0]
    n = b * s
    assert d_model == D_MODEL and w_in.shape[1:] == (D_MODEL, PROJ_W), (x.shape, w_in.shape)
    assert s % (PERM_D * BAND) == 0 and n % (2 * SC_WORKERS * 8) == 0, (b, s)
    alpha = (2.0 * depth) ** 0.25
    cos_t, sin_t = _rope_tables(s)
    perm, perm_t = _tile_perm()
    tb = MOE_TB
    n_blocks = (2 * n + N_EXPERTS * (tb - 1) + tb - 1) // tb
    rows = n_blocks * tb

    pending = None
    for l in range(depth):
        g = branch_gain[l]
        bias_tile = jnp.repeat(sgu_b[l].T, SGU_W // SGU_GROUPS, axis=1)
        mix_params = (sgu_gain[l][None, :], sgu_w[l], bias_tile, conv_w[l], g[None, ATTN_W:])
        outs = _proj(x if pending is None else pending, w_in[l].astype(BF16), cos_t, sin_t, perm, mix_params,
                     b, s, alpha)
        if pending is not None:
            x, outs = outs[0], outs[1:]
        q, k, v, mbc, q16, k16, v16 = outs
        ma = _attention(q, k, v, q16, k16, v16, perm_t, g[None, :ATTN_W])

        rw = jnp.zeros((D_MODEL, LANES), F32)
        rw = rw.at[:, :MOE_GROUPS].set(router_group_w[l])
        rw = rw.at[:, MOE_GROUPS:MOE_GROUPS + N_EXPERTS].set(router_expert_w[l])
        rw_hi = rw.astype(BF16)
        rw_hilo = jnp.concatenate([rw_hi, (rw - rw_hi.astype(F32)).astype(BF16)], axis=1)
        rbias = jnp.zeros((1, LANES), F32)
        rbias = rbias.at[0, :MOE_GROUPS].set(router_group_b[l])
        rbias = rbias.at[0, MOE_GROUPS:MOE_GROUPS + N_EXPERTS].set(router_expert_b[l])
        wo = w_out[l].astype(BF16)
        x1, xp, info, info_t, cnt = _outproj(
            ma.reshape(n, ATTN_W), mbc.reshape(n, SGU_W + CONV_W), x.reshape(n, D_MODEL),
            wo[:ATTN_W], wo[ATTN_W:], ln_gain[l, 0][None], ln_bias[l, 0][None], rw_hilo, rbias, alpha)

        counts = cnt[0, :N_EXPERTS].astype(I32)
        padded = (counts + tb - 1) // tb * tb
        pad_end = jnp.cumsum(padded)
        pad_start = pad_end - padded
        ex = info_t[0:2].astype(I32)
        start_of = jnp.zeros_like(ex)
        for e in range(N_EXPERTS):
            start_of = jnp.where(ex == e, pad_start[e], start_of)
        dest_kn = start_of + info_t[4:6].astype(I32)
        blk_row0 = jnp.arange(n_blocks, dtype=I32) * tb
        block_expert = jnp.minimum(jnp.sum((pad_end[None, :] <= blk_row0[:, None]).astype(I32), axis=1),
                                   N_EXPERTS - 1)
        block_valid = jnp.clip(pad_start[block_expert] + counts[block_expert] - blk_row0, 0, tb)
        n_used = (pad_end[-1:] // tb).astype(I32)

        xs = _sc_dispatch(xp, dest_kn, rows)
        ys = _experts(block_expert, block_valid, n_used, xs, expert_w_gate, expert_w_up, expert_w_down, l)
        y_tok = _sc_gather(ys, dest_kn.reshape(2 * n))
        pending = (y_tok, x1, info, ln_gain[l, 1][None], ln_bias[l, 1][None])
    return _combine(*pending, alpha).reshape(b, s, D_MODEL)


def kernel(x, w_in, w_out, branch_gain, sgu_gain, sgu_w, sgu_b, conv_w, ln_gain, ln_bias, router_group_w, router_group_b, router_expert_w, router_expert_b, expert_w_gate, expert_w_up, expert_w_down):
    return _forward(x, w_in, w_out, branch_gain, sgu_gain, sgu_w, sgu_b, conv_w, ln_gain, ln_bias,
                    router_group_w, router_group_b, router_expert_w, router_expert_b,
                    expert_w_gate, expert_w_up, expert_w_down)
```
